```python
import jax, jax.numpy as jnp
from jax import lax
import numpy as np

D_MODEL = 1024
BATCH = 2
SEQ = 8192
DEPTH = 2
DEC_BATCH = 128
DEC_SEQ = 4
PAST_LEN = 16384
PAGE_SIZE = 128

N_EVEN = (DEPTH + 1) // 2
N_ODD = DEPTH // 2

H_A = 8
Q_LORA = 384
KV_LORA = 256
QK_NOPE = 64
QK_ROPE = 32
V_HEAD = 64
W_A = H_A * V_HEAD
ROPE_THETA = 10000.0
MLA_SCALE = (QK_NOPE + QK_ROPE) ** -0.5
ATTN_BLOCK = 128
W_B = 512
NB_B = 8
BD_B = W_B // NB_B
CONV_W = 4
LRU_C = 8.0
H_C = 4
DH_C = 128
W_C = H_C * DH_C
MLSTM_CHUNK = 128
G_D = 4
W_D = 512
CHUNK_D = 128
N_MEM = 256
H_M = 4
DH_M = 64
W_M = H_M * DH_M

NORM_EPS = 1e-6
DEEPNORM_ALPHA = (2 * DEPTH) ** 0.25
DEEPNORM_BETA = (8 * DEPTH) ** -0.25

EVEN_SPLITS = (Q_LORA, KV_LORA, QK_ROPE, W_A, W_B, W_B, W_M, W_M)
IN_EVEN = Q_LORA + KV_LORA + QK_ROPE + W_A + 2 * W_B + 2 * W_M
OUT_EVEN = W_A + W_B + W_M
ODD_SPLITS = (W_C, W_C, W_C, H_C, H_C, W_C, W_C, W_D, W_D, W_D, W_M, W_M)
IN_ODD = 5 * W_C + 2 * H_C + 3 * W_D + 2 * W_M
OUT_ODD = W_C + W_D + W_M

kernel_name = 'hybrid_mla_rglru_mlstm_chunkmlp_step'


def split_cols(z, sizes):
    cuts = [int(c) for c in np.cumsum(sizes)[:-1]]
    return jnp.split(z, cuts, axis=-1)


def rms_norm(x, g):
    xf = x.astype(jnp.float32)
    y = xf * lax.rsqrt(jnp.mean(xf * xf, axis=-1, keepdims=True) + NORM_EPS)
    return (y * g.astype(jnp.float32)).astype(x.dtype)


def layer_norm(x, g, b):
    xf = x.astype(jnp.float32)
    mu = jnp.mean(xf, axis=-1, keepdims=True)
    var = jnp.mean(jnp.square(xf - mu), axis=-1, keepdims=True)
    y = (xf - mu) * lax.rsqrt(var + NORM_EPS) * g.astype(jnp.float32) + b.astype(jnp.float32)
    return y.astype(x.dtype)


def rope(x, pos):
    half = x.shape[-1] // 2
    inv = ROPE_THETA ** (-jnp.arange(half, dtype=jnp.float32) / half)
    ang = pos.astype(jnp.float32)[:, None] * inv[None, :]
    cos = jnp.cos(ang)[:, None, :]
    sin = jnp.sin(ang)[:, None, :]
    xf = x.astype(jnp.float32)
    x1, x2 = xf[..., :half], xf[..., half:]
    return jnp.concatenate([x1 * cos - x2 * sin, x1 * sin + x2 * cos], axis=-1).astype(x.dtype)


def _mla_scores(q_lat, q_rope, lat, krope):
    s = jnp.einsum('bhqc,bkc->bhqk', q_lat, lat) + jnp.einsum('bhqr,bkr->bhqk', q_rope, krope)
    return s.astype(jnp.float32) * MLA_SCALE


def _softmax_accumulate(carry, s, vals):
    m, l, acc = carry
    m_new = jnp.maximum(m, jnp.max(s, axis=-1))
    corr = jnp.exp(m - m_new)
    p = jnp.exp(s - m_new[..., None])
    l_new = l * corr + jnp.sum(p, axis=-1)
    acc_new = acc * corr[..., None] + jnp.einsum('bhqk,bkc->bhqc', p, vals.astype(jnp.float32))
    return (m_new, l_new, acc_new)


def mla_prompt_attend(q_lat, q_rope, latent, k_rope):
    B, H, T, C = q_lat.shape
    nb = T // ATTN_BLOCK
    ql_b = jnp.moveaxis(q_lat.reshape(B, H, nb, ATTN_BLOCK, C), 2, 0)
    qr_b = jnp.moveaxis(q_rope.reshape(B, H, nb, ATTN_BLOCK, QK_ROPE), 2, 0)
    kpos = jnp.arange(T)

    def one_block(args):
        ql, qr, i = args
        s = _mla_scores(ql, qr, latent, k_rope)
        qpos = i * ATTN_BLOCK + jnp.arange(ATTN_BLOCK)
        s = jnp.where(kpos[None, :] <= qpos[:, None], s, -jnp.inf)
        p = jax.nn.softmax(s, axis=-1).astype(latent.dtype)
        return jnp.einsum('bhqk,bkc->bhqc', p, latent)

    o = lax.map(one_block, (ql_b, qr_b, jnp.arange(nb)))
    return jnp.moveaxis(o, 0, 2).reshape(B, H, T, C)


def mla_sample_attend(q_lat, q_rope, latent, k_rope, lat_pool, rope_pool, page_table, e):
    B, H, T, C = q_lat.shape

    def page_step(carry, pt):
        lat = lat_pool[e, pt]
        kr = rope_pool[e, pt]
        s = _mla_scores(q_lat, q_rope, lat, kr)
        return _softmax_accumulate(carry, s, lat), None

    init = (jnp.full((B, H, T), -jnp.inf, jnp.float32),
            jnp.zeros((B, H, T), jnp.float32),
            jnp.zeros((B, H, T, C), jnp.float32))
    carry, _ = lax.scan(page_step, init, page_table.T)
    s_self = _mla_scores(q_lat, q_rope, latent, k_rope)
    causal = jnp.tril(jnp.ones((T, T), dtype=bool))
    s_self = jnp.where(causal, s_self, -jnp.inf)
    m, l, acc = _softmax_accumulate(carry, s_self, latent)
    return (acc / l[..., None]).astype(q_lat.dtype)


def _linear_combine(e1, e2):
    a1, b1 = e1
    a2, b2 = e2
    return (a1 * a2, a2 * b1 + b2)


def rglru(xb, buf, h0, conv_w, conv_b, wa, ba, wx, bx, lam):
    B, T, _ = xb.shape
    xp = jnp.concatenate([buf.astype(xb.dtype), xb], axis=1)
    xc = conv_b + xp[:, CONV_W - 1:] * conv_w[CONV_W - 1]
    for j in range(CONV_W - 1):
        xc = xc + xp[:, j:j + T] * conv_w[j]
    xg = xc.reshape(B, T, NB_B, BD_B)
    r = jax.nn.sigmoid((jnp.einsum('btnd,nde->btne', xg, wa).reshape(B, T, W_B) + ba).astype(jnp.float32))
    ig = jax.nn.sigmoid((jnp.einsum('btnd,nde->btne', xg, wx).reshape(B, T, W_B) + bx).astype(jnp.float32))
    log_a = -LRU_C * r * jax.nn.softplus(-lam.astype(jnp.float32))
    a = jnp.exp(log_a)
    b = jnp.sqrt(-jnp.expm1(2.0 * log_a)) * (ig * xc.astype(jnp.float32))
    a_cum, h_zero = lax.associative_scan(_linear_combine, (a, b), axis=1)
    h = a_cum * h0.astype(jnp.float32)[:, None, :] + h_zero
    return h.astype(xb.dtype), h[:, -1], xp[:, -(CONV_W - 1):]


def mlstm(q, k, v, ig, lf, c0, n0, m0):
    B, T, H, Dh = q.shape
    L = MLSTM_CHUNK if T % MLSTM_CHUNK == 0 else T
    nc = T // L
    f32 = jnp.float32

    def to_chunks(a):
        return jnp.swapaxes(a.astype(f32).reshape((B, nc, L) + a.shape[2:]), 0, 1)

    tril = jnp.tril(jnp.ones((L, L), dtype=bool))

    def step(carry, blk):
        c, n, m = carry
        qc, kc, vc, ic, fc = blk
        bcum = jnp.swapaxes(jnp.cumsum(fc, axis=1), 1, 2)
        ih = jnp.swapaxes(ic, 1, 2)
        dmat = jnp.where(tril, bcum[..., :, None] - bcum[..., None, :] + ih[..., None, :], -jnp.inf)
        inter = bcum + m[..., None]
        mt = jnp.maximum(inter, jnp.max(dmat, axis=-1))
        w = jnp.exp(dmat - mt[..., None])
        sc = jnp.einsum('blhd,bshd->bhls', qc, kc) * w
        g_inter = jnp.exp(inter - mt)
        num = jnp.einsum('bhls,bshv->bhlv', sc, vc) + g_inter[..., None] * jnp.einsum('blhd,bhdv->bhlv', qc, c)
        den = jnp.sum(sc, axis=-1) + g_inter * jnp.einsum('blhd,bhd->bhl', qc, n)
        h = num / jnp.maximum(jnp.abs(den), jnp.exp(-mt))[..., None]
        b_end = bcum[..., -1]
        g = b_end[..., None] - bcum + ih
        m_new = jnp.maximum(b_end + m, jnp.max(g, axis=-1))
        decay = jnp.exp(b_end + m - m_new)
        wg = jnp.exp(g - m_new[..., None])
        c_new = decay[..., None, None] * c + jnp.einsum('bhs,bshd,bshv->bhdv', wg, kc, vc)
        n_new = decay[..., None] * n + jnp.einsum('bhs,bshd->bhd', wg, kc)
        return (c_new, n_new, m_new), jnp.swapaxes(h, 1, 2)

    xs = (to_chunks(q), to_chunks(k), to_chunks(v), to_chunks(ig), to_chunks(lf))
    (c, n, m), hs = lax.scan(step, (c0.astype(f32), n0.astype(f32), m0.astype(f32)), xs)
    h = jnp.swapaxes(hs, 0, 1).reshape(B, T, H, Dh)
    return h, c, n, m


def chunk_mlp(u, v, ln_g_d, ln_b_d, sg_w, sg_b):
    B, T, _ = v.shape
    vn = layer_norm(v, ln_g_d, ln_b_d)
    L = CHUNK_D if T % CHUNK_D == 0 else T
    nc = T // L
    w = jnp.where(jnp.tril(jnp.ones((L, L), dtype=bool)), sg_w[:, :L, :L], 0.0)
    vg = vn.reshape(B, nc, L, G_D, W_D // G_D)
    s = jnp.einsum('gts,bcsgd->bctgd', w.astype(vn.dtype), vg) + sg_b[:, :L].T[:, :, None]
    return u * s.reshape(B, T, W_D).astype(u.dtype), vn


def mem_kv(mem, w_mk, w_mv):
    B = mem.shape[0]
    return ((mem @ w_mk).reshape(B, N_MEM, H_M, DH_M), (mem @ w_mv).reshape(B, N_MEM, H_M, DH_M))


def mem_attend(qm, mk, mv):
    B, T, _ = qm.shape
    q = qm.reshape(B, T, H_M, DH_M)
    s = jnp.einsum('bthd,bmhd->bhtm', q, mk).astype(jnp.float32) * (DH_M ** -0.5)
    p = jax.nn.softmax(s, axis=-1).astype(mv.dtype)
    return jnp.einsum('bhtm,bmhd->bthd', p, mv).reshape(B, T, W_M)


def even_sublayer(x, pos, mem_k, mem_v, lru_h0, lru_buf, paged, e,
                  w_in, q_norm, kv_norm, w_uq, w_uk, w_uv,
                  conv_w, conv_b, wa, ba, wx, bx, lam, w_out):
    B, T, _ = x.shape
    c_q, c_kv, kr_raw, g_a, x_b, g_b, q_m, g_m = split_cols(x @ w_in, EVEN_SPLITS)
    q = (rms_norm(c_q, q_norm) @ w_uq).reshape(B, T, H_A, QK_NOPE + QK_ROPE)
    q_lat = jnp.einsum('bthn,chn->bhtc', q[..., :QK_NOPE], w_uk)
    q_rope = jnp.transpose(rope(q[..., QK_NOPE:], pos), (0, 2, 1, 3))
    latent = rms_norm(c_kv, kv_norm)
    k_rope = rope(kr_raw[:, :, None, :], pos)[:, :, 0, :]
    if paged is None:
        o_lat = mla_prompt_attend(q_lat, q_rope, latent, k_rope)
    else:
        lat_pool, rope_pool, page_table = paged
        o_lat = mla_sample_attend(q_lat, q_rope, latent, k_rope, lat_pool, rope_pool, page_table, e)
    y_a = jnp.einsum('bhtc,chv->bthv', o_lat, w_uv).reshape(B, T, W_A)
    h_b, h_last, new_buf = rglru(x_b, lru_buf, lru_h0, conv_w, conv_b, wa, ba, wx, bx, lam)
    y_m = mem_attend(q_m, mem_k, mem_v)
    mixed = jnp.concatenate([y_a * jax.nn.silu(g_a), h_b * jax.nn.silu(g_b), y_m * jax.nn.silu(g_m)], axis=-1)
    return mixed @ w_out, latent, k_rope, h_last, new_buf


def odd_sublayer(x, mem_k, mem_v, c0, n0, m0, w_in, b_if, ln_g_d, ln_b_d, sg_w, sg_b, w_out):
    B, T, _ = x.shape
    q, k, v, i_pre, f_pre, o_pre, g_c, u_d, v_d, g_d, q_m, g_m = split_cols(x @ w_in, ODD_SPLITS)

    def heads(a):
        return a.reshape(B, T, H_C, DH_C)

    ig = (i_pre + b_if[:H_C]).astype(jnp.float32)
    lf = jax.nn.log_sigmoid((f_pre + b_if[H_C:]).astype(jnp.float32))
    h, c, n, m = mlstm(heads(q), heads(k) * (DH_C ** -0.5), heads(v), ig, lf, c0, n0, m0)
    y_c = jax.nn.sigmoid(o_pre) * h.reshape(B, T, W_C).astype(x.dtype)
    y_d, vn = chunk_mlp(u_d, v_d, ln_g_d, ln_b_d, sg_w, sg_b)
    y_m = mem_attend(q_m, mem_k, mem_v)
    mixed = jnp.concatenate([y_c * jax.nn.silu(g_c), y_d * jax.nn.silu(g_d), y_m * jax.nn.silu(g_m)], axis=-1)
    return mixed @ w_out, vn, c, n, m


def setup_inputs(seed: int = 0) -> dict:
    key = jax.random.key(seed)
    keys = jax.random.split(key, 40)
    f32 = jnp.float32

    def nrm(i, shape, scale=1.0):
        return jax.random.normal(keys[i], shape, f32) * scale

    n_pages = PAST_LEN // PAGE_SIZE
    n_used = DEC_BATCH * n_pages
    n_phys = n_used + n_used // 4
    page_table = jax.random.permutation(keys[0], n_phys)[:n_used].reshape(DEC_BATCH, n_pages).astype(jnp.int32)
    lam_p = jax.random.uniform(keys[1], (N_EVEN, W_B), f32, 0.9, 0.999)
    lru_lambda = jnp.log(lam_p) - jnp.log1p(-lam_p)
    b_if = jnp.concatenate([nrm(2, (N_ODD, H_C), 0.1),
                            jnp.linspace(3.0, 6.0, H_C, dtype=f32)[None, :] + nrm(3, (N_ODD, H_C), 0.1)], axis=-1)
    return {
        'x_prompt': nrm(4, (BATCH, SEQ, D_MODEL)),
        'x_sample': nrm(5, (DEC_BATCH, DEC_SEQ, D_MODEL)),
        'cache_mla_latent': nrm(6, (N_EVEN, n_phys, PAGE_SIZE, KV_LORA)),
        'cache_mla_krope': nrm(7, (N_EVEN, n_phys, PAGE_SIZE, QK_ROPE)),
        'state_lru_h': nrm(8, (N_EVEN, DEC_BATCH, W_B), 0.5),
        'state_lru_conv': nrm(9, (N_EVEN, DEC_BATCH, CONV_W - 1, W_B)),
        'state_mlstm_c': nrm(10, (N_ODD, DEC_BATCH, H_C, DH_C, DH_C), DH_C ** -0.5),
        'state_mlstm_n': nrm(11, (N_ODD, DEC_BATCH, H_C, DH_C)),
        'state_mlstm_m': nrm(12, (N_ODD, DEC_BATCH, H_C)),
        'cache_mem_k': nrm(13, (DEPTH, DEC_BATCH, N_MEM, H_M, DH_M)),
        'cache_mem_v': nrm(14, (DEPTH, DEC_BATCH, N_MEM, H_M, DH_M)),
        'page_table': page_table,
        'mem_prompt': nrm(15, (BATCH, N_MEM, D_MODEL)),
        'w_in_even': nrm(16, (N_EVEN, D_MODEL, IN_EVEN), D_MODEL ** -0.5),
        'mla_q_norm': 1.0 + nrm(17, (N_EVEN, Q_LORA), 0.02),
        'mla_kv_norm': 1.0 + nrm(18, (N_EVEN, KV_LORA), 0.02),
        'w_uq': nrm(19, (N_EVEN, Q_LORA, H_A * (QK_NOPE + QK_ROPE)), Q_LORA ** -0.5),
        'w_uk': nrm(20, (N_EVEN, KV_LORA, H_A, QK_NOPE), KV_LORA ** -0.5),
        'w_uv': nrm(21, (N_EVEN, KV_LORA, H_A, V_HEAD), KV_LORA ** -0.5),
        'lru_conv_w': nrm(22, (N_EVEN, CONV_W, W_B), CONV_W ** -0.5),
        'lru_conv_b': nrm(23, (N_EVEN, W_B), 0.01),
        'lru_wa': nrm(24, (N_EVEN, NB_B, BD_B, BD_B), BD_B ** -0.5),
        'lru_ba': nrm(25, (N_EVEN, W_B), 0.01),
        'lru_wx': nrm(26, (N_EVEN, NB_B, BD_B, BD_B), BD_B ** -0.5),
        'lru_bx': nrm(27, (N_EVEN, W_B), 0.01),
        'lru_lambda': lru_lambda,
        'w_out_even': nrm(28, (N_EVEN, OUT_EVEN, D_MODEL), DEEPNORM_BETA * OUT_EVEN ** -0.5),
        'w_in_odd': nrm(29, (N_ODD, D_MODEL, IN_ODD), D_MODEL ** -0.5),
        'mlstm_b_if': b_if,
        'sg_ln_g': 1.0 + nrm(30, (N_ODD, W_D), 0.02),
        'sg_ln_b': nrm(31, (N_ODD, W_D), 0.01),
        'sg_w': nrm(32, (N_ODD, G_D, CHUNK_D, CHUNK_D), CHUNK_D ** -0.5),
        'sg_b': 1.0 + nrm(33, (N_ODD, G_D, CHUNK_D), 0.1),
        'w_out_odd': nrm(34, (N_ODD, OUT_ODD, D_MODEL), DEEPNORM_BETA * OUT_ODD ** -0.5),
        'w_mem_k': nrm(35, (DEPTH, D_MODEL, W_M), D_MODEL ** -0.5),
        'w_mem_v': nrm(36, (DEPTH, D_MODEL, W_M), D_MODEL ** -0.5),
        'ln_g': 1.0 + nrm(37, (DEPTH, D_MODEL), 0.02),
        'ln_b': nrm(38, (DEPTH, D_MODEL), 0.01),
    }


def reference(x_prompt, x_sample, cache_mla_latent, cache_mla_krope, state_lru_h, state_lru_conv,
              state_mlstm_c, state_mlstm_n, state_mlstm_m, cache_mem_k, cache_mem_v, page_table,
              mem_prompt, w_in_even, mla_q_norm, mla_kv_norm, w_uq, w_uk, w_uv,
              lru_conv_w, lru_conv_b, lru_wa, lru_ba, lru_wx, lru_bx, lru_lambda, w_out_even,
              w_in_odd, mlstm_b_if, sg_ln_g, sg_ln_b, sg_w, sg_b, w_out_odd,
              w_mem_k, w_mem_v, ln_g, ln_b):
    f32 = jnp.float32
    Bp, Tp, _ = x_prompt.shape
    past_len = page_table.shape[1] * PAGE_SIZE
    pos_p = jnp.arange(Tp, dtype=f32)
    pos_s = past_len + jnp.arange(x_sample.shape[1], dtype=f32)

    h0_p = jnp.zeros((Bp, W_B), f32)
    buf0_p = jnp.zeros((Bp, CONV_W - 1, W_B), x_prompt.dtype)
    c0_p = jnp.zeros((Bp, H_C, DH_C, DH_C), f32)
    n0_p = jnp.zeros((Bp, H_C, DH_C), f32)
    m0_p = jnp.zeros((Bp, H_C), f32)

    lat_p, kr_p, h_p, conv_p, c_p, n_p, m_p, mk_p, mv_p = [], [], [], [], [], [], [], [], []
    lat_s, kr_s, h_s, conv_s, c_s, n_s, m_s, v_s = [], [], [], [], [], [], [], []

    xp, xs = x_prompt, x_sample
    for l in range(DEPTH):
        mk_l, mv_l = mem_kv(mem_prompt, w_mem_k[l], w_mem_v[l])
        mk_p.append(mk_l)
        mv_p.append(mv_l)
        if l % 2 == 0:
            e = l // 2
            ew = (w_in_even[e], mla_q_norm[e], mla_kv_norm[e], w_uq[e], w_uk[e], w_uv[e],
                  lru_conv_w[e], lru_conv_b[e], lru_wa[e], lru_ba[e], lru_wx[e], lru_bx[e],
                  lru_lambda[e], w_out_even[e])
            yp, la, kr, hl, cb = even_sublayer(xp, pos_p, mk_l, mv_l, h0_p, buf0_p, None, e, *ew)
            lat_p.append(la); kr_p.append(kr); h_p.append(hl); conv_p.append(cb)
            ys, la, kr, hl, cb = even_sublayer(xs, pos_s, cache_mem_k[l], cache_mem_v[l],
                                               state_lru_h[e], state_lru_conv[e],
                                               (cache_mla_latent, cache_mla_krope, page_table), e, *ew)
            lat_s.append(la); kr_s.append(kr); h_s.append(hl); conv_s.append(cb)
        else:
            o = l // 2
            ow = (w_in_odd[o], mlstm_b_if[o], sg_ln_g[o], sg_ln_b[o], sg_w[o], sg_b[o], w_out_odd[o])
            yp, _, cc, nn, mm = odd_sublayer(xp, mk_l, mv_l, c0_p, n0_p, m0_p, *ow)
            c_p.append(cc); n_p.append(nn); m_p.append(mm)
            ys, vn, cc, nn, mm = odd_sublayer(xs, cache_mem_k[l], cache_mem_v[l],
                                              state_mlstm_c[o], state_mlstm_n[o], state_mlstm_m[o], *ow)
            c_s.append(cc); n_s.append(nn); m_s.append(mm); v_s.append(vn)
        xp = layer_norm(DEEPNORM_ALPHA * xp + yp, ln_g[l], ln_b[l])
        xs = layer_norm(DEEPNORM_ALPHA * xs + ys, ln_g[l], ln_b[l])

    new_mla_latent_p = jnp.stack(lat_p)
    new_mla_krope_p = jnp.stack(kr_p)
    new_lru_h_p = jnp.stack(h_p)
    new_lru_conv_p = jnp.stack(conv_p)
    new_mlstm_c_p = jnp.stack(c_p)
    new_mlstm_n_p = jnp.stack(n_p)
    new_mlstm_m_p = jnp.stack(m_p)
    new_mem_k_p = jnp.stack(mk_p)
    new_mem_v_p = jnp.stack(mv_p)
    new_mla_latent_s = jnp.stack(lat_s)
    new_mla_krope_s = jnp.stack(kr_s)
    new_lru_h_s = jnp.stack(h_s)
    new_lru_conv_s = jnp.stack(conv_s)
    new_mlstm_c_s = jnp.stack(c_s)
    new_mlstm_n_s = jnp.stack(n_s)
    new_mlstm_m_s = jnp.stack(m_s)
    new_chunk_v_s = jnp.stack(v_s)
    return (xp, xs, new_mla_latent_p, new_mla_krope_p, new_lru_h_p, new_lru_conv_p,
            new_mlstm_c_p, new_mlstm_n_p, new_mlstm_m_p, new_mem_k_p, new_mem_v_p,
            new_mla_latent_s, new_mla_krope_s, new_lru_h_s, new_lru_conv_s,
            new_mlstm_c_s, new_mlstm_n_s, new_mlstm_m_s, new_chunk_v_s)
```

```python
import functools

import jax
import jax.numpy as jnp
import numpy as np
from jax import lax
from jax.experimental import pallas as pl
from jax.experimental.pallas import tpu as pltpu

D_MODEL = 1024
DEPTH = 2
PAGE_SIZE = 128
H_A = 8
Q_LORA = 384
KV_LORA = 256
QK_NOPE = 64
QK_ROPE = 32
V_HEAD = 64
W_A = H_A * V_HEAD
ROPE_THETA = 10000.0
MLA_SCALE = (QK_NOPE + QK_ROPE) ** -0.5
ATTN_BLOCK = 128
W_B = 512
NB_B = 8
BD_B = W_B // NB_B
CONV_W = 4
LRU_C = 8.0
H_C = 4
DH_C = 128
W_C = H_C * DH_C
MLSTM_CHUNK = 128
G_D = 4
W_D = 512
CHUNK_D = 128
N_MEM = 256
H_M = 4
DH_M = 64
W_M = H_M * DH_M
NORM_EPS = 1e-6
DEEPNORM_ALPHA = (2 * DEPTH) ** 0.25

EVEN_SPLITS = (Q_LORA, KV_LORA, QK_ROPE, W_A, W_B, W_B, W_M, W_M)
ODD_SPLITS = (W_C, W_C, W_C, H_C, H_C, W_C, W_C, W_D, W_D, W_D, W_M, W_M)

LANES = 128
VMEM_LIMIT = 48 * 1024 * 1024


def _split_cols(z, sizes):
    cuts = [int(c) for c in np.cumsum(sizes)[:-1]]
    return jnp.split(z, cuts, axis=-1)


def _mm_body(x_ref, w_ref, o_ref):
    o_ref[...] = jnp.dot(x_ref[...].astype(jnp.bfloat16), w_ref[...],
                         preferred_element_type=jnp.float32)


def _row_tile(m, n):
    tm = 512
    while tm > 8 and (tm * n * 4 * 2 > 10 * 1024 * 1024 or m % tm):
        tm //= 2
    return tm


def _matmul(x, w):
    m, k = x.shape
    n = w.shape[1]
    n_pad = -n % LANES
    wb = w.astype(jnp.bfloat16)
    if n_pad:
        wb = jnp.pad(wb, ((0, 0), (0, n_pad)))
    np_ = n + n_pad
    tm = _row_tile(m, np_)
    out = pl.pallas_call(
        _mm_body,
        grid=(m // tm,),
        in_specs=[pl.BlockSpec((tm, k), lambda i: (i, 0)),
                  pl.BlockSpec((k, np_), lambda i: (0, 0))],
        out_specs=pl.BlockSpec((tm, np_), lambda i: (i, 0)),
        out_shape=jax.ShapeDtypeStruct((m, np_), jnp.float32),
        compiler_params=pltpu.CompilerParams(dimension_semantics=("arbitrary",),
                                             vmem_limit_bytes=VMEM_LIMIT),
        name="row_matmul",
    )(x, wb)
    return out[:, :n] if n_pad else out


def _proj(x, w):
    lead = x.shape[:-1]
    return _matmul(x.reshape(-1, x.shape[-1]), w).reshape(lead + (w.shape[1],))


def _rms_norm(x, g):
    return x * lax.rsqrt(jnp.mean(x * x, axis=-1, keepdims=True) + NORM_EPS) * g


def _layer_norm(x, g, b):
    mu = jnp.mean(x, axis=-1, keepdims=True)
    var = jnp.mean(jnp.square(x - mu), axis=-1, keepdims=True)
    return (x - mu) * lax.rsqrt(var + NORM_EPS) * g + b


def _rope(x, pos):
    half = x.shape[-1] // 2
    inv = ROPE_THETA ** (-jnp.arange(half, dtype=jnp.float32) / half)
    ang = pos.astype(jnp.float32)[:, None] * inv[None, :]
    cos = jnp.cos(ang)[:, None, :]
    sin = jnp.sin(ang)[:, None, :]
    x1, x2 = x[..., :half], x[..., half:]
    return jnp.concatenate([x1 * cos - x2 * sin, x1 * sin + x2 * cos], axis=-1)


def _mla_scores(q_lat, q_rope, lat, krope):
    s = jnp.einsum('bhqc,bkc->bhqk', q_lat, lat) + jnp.einsum('bhqr,bkr->bhqk', q_rope, krope)
    return s * MLA_SCALE


def _softmax_accumulate(carry, s, vals):
    m, l, acc = carry
    m_new = jnp.maximum(m, jnp.max(s, axis=-1))
    corr = jnp.exp(m - m_new)
    p = jnp.exp(s - m_new[..., None])
    l_new = l * corr + jnp.sum(p, axis=-1)
    acc_new = acc * corr[..., None] + jnp.einsum('bhqk,bkc->bhqc', p, vals)
    return (m_new, l_new, acc_new)


def _mla_prompt_attend(q_lat, q_rope, latent, k_rope):
    B, H, T, C = q_lat.shape
    nb = T // ATTN_BLOCK
    ql_b = jnp.moveaxis(q_lat.reshape(B, H, nb, ATTN_BLOCK, C), 2, 0)
    qr_b = jnp.moveaxis(q_rope.reshape(B, H, nb, ATTN_BLOCK, QK_ROPE), 2, 0)
    kpos = jnp.arange(T)

    def one_block(args):
        ql, qr, i = args
        s = _mla_scores(ql, qr, latent, k_rope)
        qpos = i * ATTN_BLOCK + jnp.arange(ATTN_BLOCK)
        s = jnp.where(kpos[None, :] <= qpos[:, None], s, -jnp.inf)
        p = jax.nn.softmax(s, axis=-1)
        return jnp.einsum('bhqk,bkc->bhqc', p, latent)

    o = lax.map(one_block, (ql_b, qr_b, jnp.arange(nb)))
    return jnp.moveaxis(o, 0, 2).reshape(B, H, T, C)


def _mla_sample_attend(q_lat, q_rope, latent, k_rope, lat_pool, rope_pool, page_table, e):
    B, H, T, C = q_lat.shape

    def page_step(carry, pt):
        lat = lat_pool[e, pt]
        kr = rope_pool[e, pt]
        s = _mla_scores(q_lat, q_rope, lat, kr)
        return _softmax_accumulate(carry, s, lat), None

    init = (jnp.full((B, H, T), -jnp.inf, jnp.float32),
            jnp.zeros((B, H, T), jnp.float32),
            jnp.zeros((B, H, T, C), jnp.float32))
    carry, _ = lax.scan(page_step, init, page_table.T)
    s_self = _mla_scores(q_lat, q_rope, latent, k_rope)
    causal = jnp.tril(jnp.ones((T, T), dtype=bool))
    s_self = jnp.where(causal, s_self, -jnp.inf)
    m, l, acc = _softmax_accumulate(carry, s_self, latent)
    return acc / l[..., None]


def _linear_combine(e1, e2):
    a1, b1 = e1
    a2, b2 = e2
    return (a1 * a2, a2 * b1 + b2)


def _rglru(xb, buf, h0, conv_w, conv_b, wa, ba, wx, bx, lam):
    B, T, _ = xb.shape
    xp = jnp.concatenate([buf, xb], axis=1)
    xc = conv_b + xp[:, CONV_W - 1:] * conv_w[CONV_W - 1]
    for j in range(CONV_W - 1):
        xc = xc + xp[:, j:j + T] * conv_w[j]
    xg = xc.reshape(B, T, NB_B, BD_B)
    r = jax.nn.sigmoid(jnp.einsum('btnd,nde->btne', xg, wa).reshape(B, T, W_B) + ba)
    ig = jax.nn.sigmoid(jnp.einsum('btnd,nde->btne', xg, wx).reshape(B, T, W_B) + bx)
    log_a = -LRU_C * r * jax.nn.softplus(-lam)
    a = jnp.exp(log_a)
    b = jnp.sqrt(-jnp.expm1(2.0 * log_a)) * (ig * xc)
    a_cum, h_zero = lax.associative_scan(_linear_combine, (a, b), axis=1)
    h = a_cum * h0[:, None, :] + h_zero
    return h, h[:, -1], xp[:, -(CONV_W - 1):]


def _mlstm(q, k, v, ig, lf, c0, n0, m0):
    B, T, H, Dh = q.shape
    L = MLSTM_CHUNK if T % MLSTM_CHUNK == 0 else T
    nc = T // L

    def to_chunks(a):
        return jnp.swapaxes(a.reshape((B, nc, L) + a.shape[2:]), 0, 1)

    tril = jnp.tril(jnp.ones((L, L), dtype=bool))

    def step(carry, blk):
        c, n, m = carry
        qc, kc, vc, ic, fc = blk
        bcum = jnp.swapaxes(jnp.cumsum(fc, axis=1), 1, 2)
        ih = jnp.swapaxes(ic, 1, 2)
        dmat = jnp.where(tril, bcum[..., :, None] - bcum[..., None, :] + ih[..., None, :], -jnp.inf)
        inter = bcum + m[..., None]
        mt = jnp.maximum(inter, jnp.max(dmat, axis=-1))
        w = jnp.exp(dmat - mt[..., None])
        sc = jnp.einsum('blhd,bshd->bhls', qc, kc) * w
        g_inter = jnp.exp(inter - mt)
        num = jnp.einsum('bhls,bshv->bhlv', sc, vc) + g_inter[..., None] * jnp.einsum('blhd,bhdv->bhlv', qc, c)
        den = jnp.sum(sc, axis=-1) + g_inter * jnp.einsum('blhd,bhd->bhl', qc, n)
        h = num / jnp.maximum(jnp.abs(den), jnp.exp(-mt))[..., None]
        b_end = bcum[..., -1]
        g = b_end[..., None] - bcum + ih
        m_new = jnp.maximum(b_end + m, jnp.max(g, axis=-1))
        decay = jnp.exp(b_end + m - m_new)
        wg = jnp.exp(g - m_new[..., None])
        c_new = decay[..., None, None] * c + jnp.einsum('bhs,bshd,bshv->bhdv', wg, kc, vc)
        n_new = decay[..., None] * n + jnp.einsum('bhs,bshd->bhd', wg, kc)
        return (c_new, n_new, m_new), jnp.swapaxes(h, 1, 2)

    xs = (to_chunks(q), to_chunks(k), to_chunks(v), to_chunks(ig), to_chunks(lf))
    (c, n, m), hs = lax.scan(step, (c0, n0, m0), xs)
    h = jnp.swapaxes(hs, 0, 1).reshape(B, T, H, Dh)
    return h, c, n, m


def _chunk_mlp(u, v, ln_g_d, ln_b_d, sg_w, sg_b):
    B, T, _ = v.shape
    vn = _layer_norm(v, ln_g_d, ln_b_d)
    L = CHUNK_D if T % CHUNK_D == 0 else T
    nc = T // L
    w = jnp.where(jnp.tril(jnp.ones((L, L), dtype=bool)), sg_w[:, :L, :L], 0.0)
    vg = vn.reshape(B, nc, L, G_D, W_D // G_D)
    s = jnp.einsum('gts,bcsgd->bctgd', w, vg) + sg_b[:, :L].T[:, :, None]
    return u * s.reshape(B, T, W_D), vn


def _mem_kv(mem, w_mk, w_mv):
    B = mem.shape[0]
    kv = _proj(mem, jnp.concatenate([w_mk, w_mv], axis=1))
    return (kv[..., :W_M].reshape(B, N_MEM, H_M, DH_M), kv[..., W_M:].reshape(B, N_MEM, H_M, DH_M))


def _mem_attend(qm, mk, mv):
    B, T, _ = qm.shape
    q = qm.reshape(B, T, H_M, DH_M)
    s = jnp.einsum('bthd,bmhd->bhtm', q, mk) * (DH_M ** -0.5)
    p = jax.nn.softmax(s, axis=-1)
    return jnp.einsum('bhtm,bmhd->bthd', p, mv).reshape(B, T, W_M)


def _even_sublayer(x, pos, mem_k, mem_v, lru_h0, lru_buf, paged, e,
                   w_in, q_norm, kv_norm, w_uq, w_uk, w_uv,
                   conv_w, conv_b, wa, ba, wx, bx, lam, w_out):
    B, T, _ = x.shape
    c_q, c_kv, kr_raw, g_a, x_b, g_b, q_m, g_m = _split_cols(_proj(x, w_in), EVEN_SPLITS)
    q = _proj(_rms_norm(c_q, q_norm), w_uq).reshape(B, T, H_A, QK_NOPE + QK_ROPE)
    q_lat = jnp.einsum('bthn,chn->bhtc', q[..., :QK_NOPE], w_uk)
    q_rope = jnp.transpose(_rope(q[..., QK_NOPE:], pos), (0, 2, 1, 3))
    latent = _rms_norm(c_kv, kv_norm)
    k_rope = _rope(kr_raw[:, :, None, :], pos)[:, :, 0, :]
    if paged is None:
        o_lat = _mla_prompt_attend(q_lat, q_rope, latent, k_rope)
    else:
        lat_pool, rope_pool, page_table = paged
        o_lat = _mla_sample_attend(q_lat, q_rope, latent, k_rope, lat_pool, rope_pool, page_table, e)
    y_a = jnp.einsum('bhtc,chv->bthv', o_lat, w_uv).reshape(B, T, W_A)
    h_b, h_last, new_buf = _rglru(x_b, lru_buf, lru_h0, conv_w, conv_b, wa, ba, wx, bx, lam)
    y_m = _mem_attend(q_m, mem_k, mem_v)
    mixed = jnp.concatenate([y_a * jax.nn.silu(g_a), h_b * jax.nn.silu(g_b), y_m * jax.nn.silu(g_m)], axis=-1)
    return _proj(mixed, w_out), latent, k_rope, h_last, new_buf


def _odd_sublayer(x, mem_k, mem_v, c0, n0, m0, w_in, b_if, ln_g_d, ln_b_d, sg_w, sg_b, w_out):
    B, T, _ = x.shape
    q, k, v, i_pre, f_pre, o_pre, g_c, u_d, v_d, g_d, q_m, g_m = _split_cols(_proj(x, w_in), ODD_SPLITS)

    def heads(a):
        return a.reshape(B, T, H_C, DH_C)

    ig = i_pre + b_if[:H_C]
    lf = jax.nn.log_sigmoid(f_pre + b_if[H_C:])
    h, c, n, m = _mlstm(heads(q), heads(k) * (DH_C ** -0.5), heads(v), ig, lf, c0, n0, m0)
    y_c = jax.nn.sigmoid(o_pre) * h.reshape(B, T, W_C)
    y_d, vn = _chunk_mlp(u_d, v_d, ln_g_d, ln_b_d, sg_w, sg_b)
    y_m = _mem_attend(q_m, mem_k, mem_v)
    mixed = jnp.concatenate([y_c * jax.nn.silu(g_c), y_d * jax.nn.silu(g_d), y_m * jax.nn.silu(g_m)], axis=-1)
    return _proj(mixed, w_out), vn, c, n, m


def kernel(x_prompt, x_sample, cache_mla_latent, cache_mla_krope, state_lru_h, state_lru_conv,
           state_mlstm_c, state_mlstm_n, state_mlstm_m, cache_mem_k, cache_mem_v, page_table,
           mem_prompt, w_in_even, mla_q_norm, mla_kv_norm, w_uq, w_uk, w_uv,
           lru_conv_w, lru_conv_b, lru_wa, lru_ba, lru_wx, lru_bx, lru_lambda, w_out_even,
           w_in_odd, mlstm_b_if, sg_ln_g, sg_ln_b, sg_w, sg_b, w_out_odd,
           w_mem_k, w_mem_v, ln_g, ln_b):
    f32 = jnp.float32
    Bp, Tp, _ = x_prompt.shape
    past_len = page_table.shape[1] * PAGE_SIZE
    pos_p = jnp.arange(Tp, dtype=f32)
    pos_s = past_len + jnp.arange(x_sample.shape[1], dtype=f32)

    h0_p = jnp.zeros((Bp, W_B), f32)
    buf0_p = jnp.zeros((Bp, CONV_W - 1, W_B), f32)
    c0_p = jnp.zeros((Bp, H_C, DH_C, DH_C), f32)
    n0_p = jnp.zeros((Bp, H_C, DH_C), f32)
    m0_p = jnp.zeros((Bp, H_C), f32)

    lat_p, kr_p, h_p, conv_p, c_p, n_p, m_p, mk_p, mv_p = [], [], [], [], [], [], [], [], []
    lat_s, kr_s, h_s, conv_s, c_s, n_s, m_s, v_s = [], [], [], [], [], [], [], []

    xp, xs = x_prompt, x_sample
    for l in range(DEPTH):
        mk_l, mv_l = _mem_kv(mem_prompt, w_mem_k[l], w_mem_v[l])
        mk_p.append(mk_l)
        mv_p.append(mv_l)
        if l % 2 == 0:
            e = l // 2
            ew = (w_in_even[e], mla_q_norm[e], mla_kv_norm[e], w_uq[e], w_uk[e], w_uv[e],
                  lru_conv_w[e], lru_conv_b[e], lru_wa[e], lru_ba[e], lru_wx[e], lru_bx[e],
                  lru_lambda[e], w_out_even[e])
            yp, la, kr, hl, cb = _even_sublayer(xp, pos_p, mk_l, mv_l, h0_p, buf0_p, None, e, *ew)
            lat_p.append(la); kr_p.append(kr); h_p.append(hl); conv_p.append(cb)
            ys, la, kr, hl, cb = _even_sublayer(xs, pos_s, cache_mem_k[l], cache_mem_v[l],
                                                state_lru_h[e], state_lru_conv[e],
                                                (cache_mla_latent, cache_mla_krope, page_table), e, *ew)
            lat_s.append(la); kr_s.append(kr); h_s.append(hl); conv_s.append(cb)
        else:
            o = l // 2
            ow = (w_in_odd[o], mlstm_b_if[o], sg_ln_g[o], sg_ln_b[o], sg_w[o], sg_b[o], w_out_odd[o])
            yp, _, cc, nn, mm = _odd_sublayer(xp, mk_l, mv_l, c0_p, n0_p, m0_p, *ow)
            c_p.append(cc); n_p.append(nn); m_p.append(mm)
            ys, vn, cc, nn, mm = _odd_sublayer(xs, cache_mem_k[l], cache_mem_v[l],
                                               state_mlstm_c[o], state_mlstm_n[o], state_mlstm_m[o], *ow)
            c_s.append(cc); n_s.append(nn); m_s.append(mm); v_s.append(vn)
        xp = _layer_norm(DEEPNORM_ALPHA * xp + yp, ln_g[l], ln_b[l])
        xs = _layer_norm(DEEPNORM_ALPHA * xs + ys, ln_g[l], ln_b[l])

    return (xp, xs, jnp.stack(lat_p), jnp.stack(kr_p), jnp.stack(h_p), jnp.stack(conv_p),
            jnp.stack(c_p), jnp.stack(n_p), jnp.stack(m_p), jnp.stack(mk_p), jnp.stack(mv_p),
            jnp.stack(lat_s), jnp.stack(kr_s), jnp.stack(h_s), jnp.stack(conv_s),
            jnp.stack(c_s), jnp.stack(n_s), jnp.stack(m_s), jnp.stack(v_s))
```

```python
import functools

import jax
import jax.numpy as jnp
import numpy as np
from jax import lax
from jax.experimental import pallas as pl
from jax.experimental.pallas import tpu as pltpu

D_MODEL = 1024
DEPTH = 2
PAGE_SIZE = 128
H_A = 8
Q_LORA = 384
KV_LORA = 256
QK_NOPE = 64
QK_ROPE = 32
ROPE_HALF = QK_ROPE // 2
V_HEAD = 64
W_A = H_A * V_HEAD
ROPE_THETA = 10000.0
MLA_SCALE = (QK_NOPE + QK_ROPE) ** -0.5
W_B = 512
NB_B = 8
BD_B = W_B // NB_B
CONV_W = 4
LRU_C = 8.0
H_C = 4
DH_C = 128
W_C = H_C * DH_C
MLSTM_CHUNK = 128
G_D = 4
W_D = 512
CHUNK_D = 128
N_MEM = 256
H_M = 4
DH_M = 64
W_M = H_M * DH_M
NORM_EPS = 1e-6
DEEPNORM_ALPHA = (2 * DEPTH) ** 0.25

EVEN_SPLITS = (Q_LORA, KV_LORA, QK_ROPE, W_A, W_B, W_B, W_M, W_M)
ODD_SPLITS = (W_C, W_C, W_C, H_C, H_C, W_C, W_C, W_D, W_D, W_D, W_M, W_M)

F32 = jnp.float32
BF16 = jnp.bfloat16
LANES = 128
SUBLANES = 8
VMEM_LIMIT = 48 * 1024 * 1024
LOG2E = 1.4426950408889634
NT_DIMS = (((1,), (1,)), ((), ()))

ZE_GA, ZE_XB, ZE_GB, ZE_QM, ZE_GM = 0, 512, 1024, 1536, 1792
ZE_GATES = 2048
ZE_CKV = 2048
ZE_CQ = ZE_CKV + KV_LORA
ZE_KR1 = ZE_CQ + Q_LORA
ZE_KR2 = ZE_KR1 + LANES
ZE_KRN = ZE_KR2 + LANES
ZE_KRS = ZE_KRN + LANES
ZE_W = ZE_KRS + LANES
QP_W = KV_LORA + 2 * LANES
QS_W = KV_LORA + LANES
UQ_NOPE_W = H_A * LANES


def _split_cols(z, sizes):
    cuts = [int(c) for c in np.cumsum(sizes)[:-1]]
    return jnp.split(z, cuts, axis=-1)


def _params(sem):
    return pltpu.CompilerParams(dimension_semantics=sem, vmem_limit_bytes=VMEM_LIMIT)


def _mm_body(x_ref, w_ref, o_ref):
    o_ref[...] = jnp.dot(x_ref[...].astype(BF16), w_ref[...], preferred_element_type=F32)


def _row_tile(m, n):
    tm = 512
    while tm > SUBLANES and (tm * n * 4 * 2 > 10 * 1024 * 1024 or m % tm):
        tm //= 2
    return tm


def _matmul(x, w):
    m, k = x.shape
    n = w.shape[1]
    n_pad = -n % LANES
    wb = w.astype(BF16)
    if n_pad:
        wb = jnp.pad(wb, ((0, 0), (0, n_pad)))
    np_ = n + n_pad
    tm = _row_tile(m, np_)
    out = pl.pallas_call(
        _mm_body,
        grid=(m // tm,),
        in_specs=[pl.BlockSpec((tm, k), lambda i: (i, 0)),
                  pl.BlockSpec((k, np_), lambda i: (0, 0))],
        out_specs=pl.BlockSpec((tm, np_), lambda i: (i, 0)),
        out_shape=jax.ShapeDtypeStruct((m, np_), F32),
        compiler_params=_params(("arbitrary",)),
        name="row_matmul",
    )(x, wb)
    return out[:, :n] if n_pad else out


def _proj(x, w):
    lead = x.shape[:-1]
    return _matmul(x.reshape(-1, x.shape[-1]), w).reshape(lead + (w.shape[1],))


def _even_weights(w_in, w_uq, w_uk, w_uv):
    c_q, c_kv, kr, g_a, x_b, g_b, q_m, g_m = _split_cols(w_in, EVEN_SPLITS)
    x1, x2 = kr[:, :ROPE_HALF], kr[:, ROPE_HALF:]

    def lane_pad(a):
        return jnp.pad(a, ((0, 0), (0, LANES - a.shape[1])))

    w_in_r = jnp.concatenate(
        [g_a, x_b, g_b, q_m, g_m, c_kv, c_q, jnp.tile(x1, (1, H_A)), jnp.tile(x2, (1, H_A)),
         lane_pad(kr), lane_pad(jnp.concatenate([x2, x1], axis=1))], axis=1).astype(BF16)
    r = w_uq.reshape(Q_LORA, H_A, QK_NOPE + QK_ROPE)
    nope = jnp.pad(r[:, :, :QK_NOPE], ((0, 0), (0, 0), (0, LANES - QK_NOPE))).reshape(Q_LORA, UQ_NOPE_W)
    r1 = r[:, :, QK_NOPE:QK_NOPE + ROPE_HALF].reshape(Q_LORA, LANES)
    r2 = r[:, :, QK_NOPE + ROPE_HALF:].reshape(Q_LORA, LANES)
    w_uq_r = jnp.concatenate([nope, r1, r2], axis=1).astype(BF16)
    w_uk_r = jnp.pad(jnp.transpose(w_uk, (1, 2, 0)), ((0, 0), (0, LANES - QK_NOPE), (0, 0))).astype(BF16)
    eye = jnp.eye(H_A, dtype=w_uv.dtype)
    w_uv_bd = jnp.einsum('chv,hg->hcgv', w_uv, eye).reshape(H_A * KV_LORA, W_A).astype(BF16)
    return w_in_r, w_uq_r, w_uk_r, w_uv_bd


def _rope_tables(pos):
    inv = ROPE_THETA ** (-jnp.arange(ROPE_HALF, dtype=F32) / ROPE_HALF)
    ang = pos.astype(F32)[:, None] * inv[None, :]
    cos, sin = jnp.cos(ang), jnp.sin(ang)
    zpad = jnp.zeros((pos.shape[0], LANES - QK_ROPE), F32)
    return (jnp.tile(cos, (1, H_A)), jnp.tile(sin, (1, H_A)),
            jnp.concatenate([cos, cos, zpad], axis=1), jnp.concatenate([-sin, sin, zpad], axis=1))


def _rms(x, g):
    return x * lax.rsqrt(jnp.mean(x * x, axis=-1, keepdims=True) + NORM_EPS) * g


def _even_in_body(x_ref, w_ref, qn_ref, kvn_ref, wuq_ref, wuk_ref, cos_ref, sin_ref, cosn_ref, sinn_ref,
                  zg_ref, lat_ref, kr_ref, kp_ref, qp_ref):
    z = jnp.dot(x_ref[...].astype(BF16), w_ref[...], preferred_element_type=F32)
    zg_ref[...] = z[:, :ZE_GATES]
    lat = _rms(z[:, ZE_CKV:ZE_CQ], kvn_ref[...])
    lat_ref[...] = lat
    cos, sin = cos_ref[...], sin_ref[...]
    kr1, kr2 = z[:, ZE_KR1:ZE_KR2], z[:, ZE_KR2:ZE_KRN]
    kp_ref[...] = jnp.concatenate([lat, kr1 * cos - kr2 * sin, kr1 * sin + kr2 * cos], axis=1).astype(BF16)
    kr_nat = z[:, ZE_KRN:ZE_KRS] * cosn_ref[...] + z[:, ZE_KRS:ZE_W] * sinn_ref[...]
    kr_ref[...] = kr_nat[:, :QK_ROPE]
    q = jnp.dot(_rms(z[:, ZE_CQ:ZE_KR1], qn_ref[...]).astype(BF16), wuq_ref[...],
                preferred_element_type=F32)
    q1, q2 = q[:, UQ_NOPE_W:UQ_NOPE_W + LANES], q[:, UQ_NOPE_W + LANES:]
    o1, o2 = q1 * cos - q2 * sin, q1 * sin + q2 * cos
    lane_head = lax.shift_right_logical(lax.broadcasted_iota(jnp.int32, o1.shape, 1), ROPE_HALF.bit_length() - 1)
    for h in range(H_A):
        ql = jnp.dot(q[:, h * LANES:(h + 1) * LANES].astype(BF16), wuk_ref[h], preferred_element_type=F32)
        own = lane_head == h
        qp_ref[h] = jnp.concatenate([ql, jnp.where(own, o1, 0.0), jnp.where(own, o2, 0.0)], axis=1).astype(BF16)


def _even_in(x2d, tables, w_in_r, q_norm, kv_norm, w_uq_r, w_uk_r):
    n = x2d.shape[0]
    tm = 256
    period = tables[0].shape[0] // tm
    row = lambda i: (i, 0)
    fixed2 = lambda i: (0, 0)
    tab = lambda i: (i % period, 0)
    return pl.pallas_call(
        _even_in_body,
        grid=(n // tm,),
        in_specs=[pl.BlockSpec((tm, D_MODEL), row),
                  pl.BlockSpec((D_MODEL, ZE_W), fixed2),
                  pl.BlockSpec((1, Q_LORA), fixed2),
                  pl.BlockSpec((1, KV_LORA), fixed2),
                  pl.BlockSpec((Q_LORA, UQ_NOPE_W + 2 * LANES), fixed2),
                  pl.BlockSpec((H_A, LANES, KV_LORA), lambda i: (0, 0, 0)),
                  pl.BlockSpec((tm, LANES), tab), pl.BlockSpec((tm, LANES), tab),
                  pl.BlockSpec((tm, LANES), tab), pl.BlockSpec((tm, LANES), tab)],
        out_specs=[pl.BlockSpec((tm, ZE_GATES), row),
                   pl.BlockSpec((tm, KV_LORA), row),
                   pl.BlockSpec((tm, QK_ROPE), row),
                   pl.BlockSpec((tm, QP_W), row),
                   pl.BlockSpec((H_A, tm, QP_W), lambda i: (0, i, 0))],
        out_shape=[jax.ShapeDtypeStruct((n, ZE_GATES), F32),
                   jax.ShapeDtypeStruct((n, KV_LORA), F32),
                   jax.ShapeDtypeStruct((n, QK_ROPE), F32),
                   jax.ShapeDtypeStruct((n, QP_W), BF16),
                   jax.ShapeDtypeStruct((H_A, n, QP_W), BF16)],
        compiler_params=_params(("arbitrary",)),
        name="even_in_proj",
    )(x2d, w_in_r, q_norm.reshape(1, -1), kv_norm.reshape(1, -1), w_uq_r, w_uk_r, *tables)


def _softmax_update(s, vals, m_sc, l_sc, acc_sc):
    m_prev = m_sc[...]
    m_new = jnp.maximum(m_prev, jnp.max(s, axis=-1, keepdims=True))
    alpha = jnp.exp2(m_prev - m_new)
    p = jnp.exp2(s - m_new)
    l_sc[...] = alpha * l_sc[...] + jnp.sum(p, axis=-1, keepdims=True)
    acc_sc[...] = alpha * acc_sc[...] + jnp.dot(p.astype(BF16), vals, preferred_element_type=F32)
    m_sc[...] = m_new


def _softmax_init(m_sc, l_sc, acc_sc):
    m_sc[...] = jnp.full(m_sc.shape, -jnp.inf, F32)
    l_sc[...] = jnp.zeros(l_sc.shape, F32)
    acc_sc[...] = jnp.zeros(acc_sc.shape, F32)


FLASH_TQ = 256
FLASH_TK = 512
FLAG_FIRST, FLAG_LAST, FLAG_DIAG = 1, 2, 4


def _flash_body(qb, kb, qo, ko, fl, q_ref, k_ref, wuv_ref, o_ref, m_sc, l_sc, acc_sc, *, tq, tk):
    i = pl.program_id(0)
    flags = fl[i]

    @pl.when((flags & FLAG_FIRST) != 0)
    def _():
        _softmax_init(m_sc, l_sc, acc_sc)

    q = q_ref[...].reshape(H_A * tq, QP_W)
    k = k_ref[...]
    s = lax.dot_general(q, k, NT_DIMS, preferred_element_type=F32) * (MLA_SCALE * LOG2E)
    vals = k[:, :KV_LORA]

    @pl.when((flags & FLAG_DIAG) != 0)
    def _():
        qpos = (lax.broadcasted_iota(jnp.int32, s.shape, 0) & (tq - 1)) + qo[i]
        kpos = lax.broadcasted_iota(jnp.int32, s.shape, 1) + ko[i]
        _softmax_update(jnp.where(kpos <= qpos, s, -jnp.inf), vals, m_sc, l_sc, acc_sc)

    @pl.when((flags & FLAG_DIAG) == 0)
    def _():
        _softmax_update(s, vals, m_sc, l_sc, acc_sc)

    @pl.when((flags & FLAG_LAST) != 0)
    def _():
        o = acc_sc[...] / l_sc[...]
        o_all = jnp.concatenate([o[h * tq:(h + 1) * tq] for h in range(H_A)], axis=1).astype(BF16)
        o_ref[...] = jnp.dot(o_all, wuv_ref[...], preferred_element_type=F32)


def _flash_steps(batch, seq, tq, tk):
    nq, nk = seq // tq, seq // tk
    qb, kb, qo, ko, fl = [], [], [], [], []
    for b in range(batch):
        for qi in range(nq):
            last = ((qi + 1) * tq - 1) // tk
            for kj in range(last + 1):
                qb.append(b * nq + qi)
                kb.append(b * nk + kj)
                qo.append(qi * tq)
                ko.append(kj * tk)
                diag = (kj + 1) * tk - 1 > qi * tq
                fl.append((FLAG_FIRST if kj == 0 else 0) | (FLAG_LAST if kj == last else 0)
                          | (FLAG_DIAG if diag else 0))
    return [np.asarray(a, np.int32) for a in (qb, kb, qo, ko, fl)]


def _mla_prompt(qp, kp, w_uv_bd, batch, seq):
    tq, tk = min(FLASH_TQ, seq), min(FLASH_TK, seq)
    assert tq & (tq - 1) == 0 and seq % tq == 0 and seq % tk == 0
    steps = _flash_steps(batch, seq, tq, tk)
    n = batch * seq
    rows = H_A * tq
    grid_spec = pltpu.PrefetchScalarGridSpec(
        num_scalar_prefetch=5,
        grid=(steps[0].shape[0],),
        in_specs=[pl.BlockSpec((H_A, tq, QP_W), lambda i, qb, kb, qo, ko, fl: (0, qb[i], 0)),
                  pl.BlockSpec((tk, QP_W), lambda i, qb, kb, qo, ko, fl: (kb[i], 0)),
                  pl.BlockSpec((H_A * KV_LORA, W_A), lambda i, qb, kb, qo, ko, fl: (0, 0))],
        out_specs=pl.BlockSpec((tq, W_A), lambda i, qb, kb, qo, ko, fl: (qb[i], 0)),
        scratch_shapes=[pltpu.VMEM((rows, 1), F32), pltpu.VMEM((rows, 1), F32),
                        pltpu.VMEM((rows, KV_LORA), F32)])
    return pl.pallas_call(
        functools.partial(_flash_body, tq=tq, tk=tk),
        grid_spec=grid_spec,
        out_shape=jax.ShapeDtypeStruct((n, W_A), F32),
        compiler_params=_params(("arbitrary",)),
        name="mla_prompt_flash",
    )(*[jnp.asarray(a) for a in steps], qp, kp, w_uv_bd)


PAGES_PER_STEP = 16


def _paged_body(pt_ref, q_ref, kself_ref, *refs, npg, t_new):
    lat_refs, kr_refs = refs[:npg], refs[npg:2 * npg]
    o_ref = refs[2 * npg]
    kp_sc, m_sc, l_sc, acc_sc = refs[2 * npg + 1:]
    j = pl.program_id(1)

    @pl.when(j == 0)
    def _():
        _softmax_init(m_sc, l_sc, acc_sc)
        kp_sc[:, KV_LORA:] = jnp.zeros((kp_sc.shape[0], QS_W - KV_LORA), BF16)

    for i in range(npg):
        rows = pl.ds(i * PAGE_SIZE, PAGE_SIZE)
        kp_sc[rows, :KV_LORA] = lat_refs[i][0, 0].astype(BF16)
        kp_sc[rows, KV_LORA:KV_LORA + QK_ROPE] = kr_refs[i][0, 0].astype(BF16)

    q = q_ref[0]

    def attend(keys, mask):
        s = lax.dot_general(q, keys, NT_DIMS, preferred_element_type=F32) * (MLA_SCALE * LOG2E)
        if mask is not None:
            s = jnp.where(mask, s, -jnp.inf)
        _softmax_update(s, keys[:, :KV_LORA], m_sc, l_sc, acc_sc)

    attend(kp_sc[...], None)

    @pl.when(j == pl.num_programs(1) - 1)
    def _():
        shape = (q.shape[0], PAGE_SIZE)
        t_row = lax.broadcasted_iota(jnp.int32, shape, 0) & (t_new - 1)
        attend(kself_ref[0], lax.broadcasted_iota(jnp.int32, shape, 1) <= t_row)
        o_ref[0] = acc_sc[...] / l_sc[...]


def _mla_sample(qs, kself, lat_pool, rope_pool, page_table, e, t_new):
    batch, rows, _ = qs.shape
    n_pages = page_table.shape[1]
    npg = min(PAGES_PER_STEP, n_pages)
    assert n_pages % npg == 0 and t_new & (t_new - 1) == 0 and t_new <= PAGE_SIZE
    chunks = n_pages // npg

    def page_map(i):
        return lambda b, j, pt: (e, pt[b * n_pages + j * npg + i], 0, 0)

    in_specs = [pl.BlockSpec((1, rows, QS_W), lambda b, j, pt: (b, 0, 0)),
                pl.BlockSpec((1, PAGE_SIZE, QS_W), lambda b, j, pt: (b, 0, 0))]
    in_specs += [pl.BlockSpec((1, 1, PAGE_SIZE, KV_LORA), page_map(i)) for i in range(npg)]
    in_specs += [pl.BlockSpec((1, 1, PAGE_SIZE, QK_ROPE), page_map(i)) for i in range(npg)]
    grid_spec = pltpu.PrefetchScalarGridSpec(
        num_scalar_prefetch=1,
        grid=(batch, chunks),
        in_specs=in_specs,
        out_specs=pl.BlockSpec((1, rows, KV_LORA), lambda b, j, pt: (b, 0, 0)),
        scratch_shapes=[pltpu.VMEM((npg * PAGE_SIZE, QS_W), BF16),
                        pltpu.VMEM((rows, 1), F32), pltpu.VMEM((rows, 1), F32),
                        pltpu.VMEM((rows, KV_LORA), F32)])
    return pl.pallas_call(
        functools.partial(_paged_body, npg=npg, t_new=t_new),
        grid_spec=grid_spec,
        out_shape=jax.ShapeDtypeStruct((batch, rows, KV_LORA), F32),
        compiler_params=_params(("arbitrary", "arbitrary")),
        name="mla_sample_paged",
    )(page_table.reshape(-1), qs, kself, *([lat_pool] * npg), *([rope_pool] * npg))


LRU_TC = 256


def _block_diag(w):
    nb, d, e = w.shape
    return jnp.einsum('nde,nm->ndme', w, jnp.eye(nb, dtype=w.dtype)).reshape(nb * d, nb * e)


def _lru_coeffs(xc, wa, wx, ba, bx, lam):
    xb = xc.astype(BF16)
    r = jax.nn.sigmoid(jnp.dot(xb, wa, preferred_element_type=F32) + ba)
    ig = jax.nn.sigmoid(jnp.dot(xb, wx, preferred_element_type=F32) + bx)
    neg = -lam
    softplus = jnp.maximum(neg, 0.0) + jnp.log1p(jnp.exp(-jnp.abs(neg)))
    log_a = -LRU_C * r * softplus
    a = jnp.exp(log_a)
    t = jnp.tanh(log_a)
    b = jnp.sqrt(-2.0 * t / (1.0 - t)) * (ig * xc)
    return a, b


def _lru_body(x_ref, h0_ref, buf_ref, cw_ref, cb_ref, wa_ref, wx_ref, ba_ref, bx_ref, lam_ref,
              h_ref, hl_ref, tail_ref, xbuf, hc, *, tc):
    c = pl.program_id(1)

    @pl.when(c == 0)
    def _():
        xbuf[0:SUBLANES] = buf_ref[0]
        hc[...] = h0_ref[0]

    x = x_ref[...]
    xbuf[SUBLANES:SUBLANES + tc] = x
    cw = cw_ref[...]
    xc = cb_ref[...] + x * cw[CONV_W - 1:CONV_W]
    for j in range(CONV_W - 1):
        xc = xc + xbuf[pl.ds(SUBLANES - (CONV_W - 1) + j, tc), :] * cw[j:j + 1]
    xbuf[0:SUBLANES] = x[tc - SUBLANES:tc]
    a, b = _lru_coeffs(xc, wa_ref[...], wx_ref[...], ba_ref[...], bx_ref[...], lam_ref[...])
    row = lax.broadcasted_iota(jnp.int32, a.shape, 0)
    d = 1
    while d < tc:
        keep = row >= d
        a_sh = jnp.where(keep, pltpu.roll(a, d, 0), 1.0)
        b_sh = jnp.where(keep, pltpu.roll(b, d, 0), 0.0)
        b = a * b_sh + b
        a = a * a_sh
        d *= 2
    h = a * hc[...] + b
    h_ref[...] = h
    hc[...] = h[tc - 1:tc]

    @pl.when(c == pl.num_programs(1) - 1)
    def _():
        hl_ref[0] = h[tc - 1:tc]
        tail_ref[0] = x[tc - SUBLANES:tc]


def _rglru_seq(zg, batch, seq, h0, buf, conv_w, conv_b, wa_bd, wx_bd, ba, bx, lam):
    tc = min(LRU_TC, seq)
    assert seq % tc == 0 and tc >= SUBLANES
    nc = seq // tc
    buf8 = jnp.pad(buf, ((0, 0), (SUBLANES - (CONV_W - 1), 0), (0, 0)))
    vec = lambda a: a.reshape(1, W_B)
    fixed = lambda b, c: (0, 0)
    per_b = lambda b, c: (b, 0, 0)
    h, hl, tail = pl.pallas_call(
        functools.partial(_lru_body, tc=tc),
        grid=(batch, nc),
        in_specs=[pl.BlockSpec((tc, W_B), lambda b, c: (b * nc + c, ZE_XB // W_B)),
                  pl.BlockSpec((1, 1, W_B), per_b),
                  pl.BlockSpec((1, SUBLANES, W_B), per_b),
                  pl.BlockSpec((CONV_W, W_B), fixed),
                  pl.BlockSpec((1, W_B), fixed),
                  pl.BlockSpec((W_B, W_B), fixed), pl.BlockSpec((W_B, W_B), fixed),
                  pl.BlockSpec((1, W_B), fixed), pl.BlockSpec((1, W_B), fixed), pl.BlockSpec((1, W_B), fixed)],
        out_specs=[pl.BlockSpec((tc, W_B), lambda b, c: (b * nc + c, 0)),
                   pl.BlockSpec((1, 1, W_B), per_b),
                   pl.BlockSpec((1, SUBLANES, W_B), per_b)],
        out_shape=[jax.ShapeDtypeStruct((batch * seq, W_B), F32),
                   jax.ShapeDtypeStruct((batch, 1, W_B), F32),
                   jax.ShapeDtypeStruct((batch, SUBLANES, W_B), F32)],
        scratch_shapes=[pltpu.VMEM((SUBLANES + tc, W_B), F32), pltpu.VMEM((1, W_B), F32)],
        compiler_params=_params(("arbitrary", "arbitrary")),
        name="rglru_seq",
    )(zg, h0.reshape(batch, 1, W_B), buf8, conv_w, vec(conv_b), wa_bd, wx_bd, vec(ba), vec(bx), vec(lam))
    return h, hl[:, 0], tail[:, SUBLANES - (CONV_W - 1):]


def _out_body(v1_ref, v2_ref, v3_ref, g1_ref, g2_ref, g3_ref, x_ref, w_ref, lg_ref, lb_ref, o_ref):
    def gated(v_ref, g_ref):
        g = g_ref[...]
        return (v_ref[...] * (g * jax.nn.sigmoid(g))).astype(BF16)

    mixed = jnp.concatenate([gated(v1_ref, g1_ref), gated(v2_ref, g2_ref), gated(v3_ref, g3_ref)], axis=1)
    u = DEEPNORM_ALPHA * x_ref[...] + jnp.dot(mixed, w_ref[...], preferred_element_type=F32)
    mu = jnp.mean(u, axis=-1, keepdims=True)
    var = jnp.mean(jnp.square(u - mu), axis=-1, keepdims=True)
    o_ref[...] = (u - mu) * lax.rsqrt(var + NORM_EPS) * lg_ref[...] + lb_ref[...]


def _out_proj_norm(vals, z, gate_cols, x2d, w_out, ln_g, ln_b):
    n = x2d.shape[0]
    tm = 256
    widths = [v.shape[1] for v in vals]
    row = lambda i: (i, 0)
    fixed = lambda i: (0, 0)
    in_specs = [pl.BlockSpec((tm, w), row) for w in widths]
    for w, off in zip(widths, gate_cols):
        assert off % w == 0
        in_specs.append(pl.BlockSpec((tm, w), functools.partial(lambda i, cb: (i, cb), cb=off // w)))
    in_specs += [pl.BlockSpec((tm, D_MODEL), row),
                 pl.BlockSpec((sum(widths), D_MODEL), fixed),
                 pl.BlockSpec((1, D_MODEL), fixed), pl.BlockSpec((1, D_MODEL), fixed)]
    return pl.pallas_call(
        _out_body,
        grid=(n // tm,),
        in_specs=in_specs,
        out_specs=pl.BlockSpec((tm, D_MODEL), row),
        out_shape=jax.ShapeDtypeStruct((n, D_MODEL), F32),
        compiler_params=_params(("arbitrary",)),
        name="out_proj_norm",
    )(*vals, z, z, z, x2d, w_out.astype(BF16), ln_g.reshape(1, -1), ln_b.reshape(1, -1))


def _layer_norm(x, g, b):
    mu = jnp.mean(x, axis=-1, keepdims=True)
    var = jnp.mean(jnp.square(x - mu), axis=-1, keepdims=True)
    return (x - mu) * lax.rsqrt(var + NORM_EPS) * g + b


def _linear_combine(e1, e2):
    a1, b1 = e1
    a2, b2 = e2
    return (a1 * a2, a2 * b1 + b2)


def _rglru_short(xb, buf, h0, conv_w, conv_b, wa, ba, wx, bx, lam):
    B, T, _ = xb.shape
    xp = jnp.concatenate([buf, xb], axis=1)
    xc = conv_b + xp[:, CONV_W - 1:] * conv_w[CONV_W - 1]
    for j in range(CONV_W - 1):
        xc = xc + xp[:, j:j + T] * conv_w[j]
    xg = xc.reshape(B, T, NB_B, BD_B)
    r = jax.nn.sigmoid(jnp.einsum('btnd,nde->btne', xg, wa).reshape(B, T, W_B) + ba)
    ig = jax.nn.sigmoid(jnp.einsum('btnd,nde->btne', xg, wx).reshape(B, T, W_B) + bx)
    log_a = -LRU_C * r * jax.nn.softplus(-lam)
    a = jnp.exp(log_a)
    b = jnp.sqrt(-jnp.expm1(2.0 * log_a)) * (ig * xc)
    a_cum, h_zero = lax.associative_scan(_linear_combine, (a, b), axis=1)
    h = a_cum * h0[:, None, :] + h_zero
    return h, h[:, -1], xp[:, -(CONV_W - 1):]


def _mlstm(q, k, v, ig, lf, c0, n0, m0):
    B, T, H, Dh = q.shape
    L = MLSTM_CHUNK if T % MLSTM_CHUNK == 0 else T
    nc = T // L

    def to_chunks(a):
        return jnp.swapaxes(a.reshape((B, nc, L) + a.shape[2:]), 0, 1)

    tril = jnp.tril(jnp.ones((L, L), dtype=bool))

    def step(carry, blk):
        c, n, m = carry
        qc, kc, vc, ic, fc = blk
        bcum = jnp.swapaxes(jnp.cumsum(fc, axis=1), 1, 2)
        ih = jnp.swapaxes(ic, 1, 2)
        dmat = jnp.where(tril, bcum[..., :, None] - bcum[..., None, :] + ih[..., None, :], -jnp.inf)
        inter = bcum + m[..., None]
        mt = jnp.maximum(inter, jnp.max(dmat, axis=-1))
        w = jnp.exp(dmat - mt[..., None])
        sc = jnp.einsum('blhd,bshd->bhls', qc, kc) * w
        g_inter = jnp.exp(inter - mt)
        num = jnp.einsum('bhls,bshv->bhlv', sc, vc) + g_inter[..., None] * jnp.einsum('blhd,bhdv->bhlv', qc, c)
        den = jnp.sum(sc, axis=-1) + g_inter * jnp.einsum('blhd,bhd->bhl', qc, n)
        h = num / jnp.maximum(jnp.abs(den), jnp.exp(-mt))[..., None]
        b_end = bcum[..., -1]
        g = b_end[..., None] - bcum + ih
        m_new = jnp.maximum(b_end + m, jnp.max(g, axis=-1))
        decay = jnp.exp(b_end + m - m_new)
        wg = jnp.exp(g - m_new[..., None])
        c_new = decay[..., None, None] * c + jnp.einsum('bhs,bshd,bshv->bhdv', wg, kc, vc)
        n_new = decay[..., None] * n + jnp.einsum('bhs,bshd->bhd', wg, kc)
        return (c_new, n_new, m_new), jnp.swapaxes(h, 1, 2)

    xs = (to_chunks(q), to_chunks(k), to_chunks(v), to_chunks(ig), to_chunks(lf))
    (c, n, m), hs = lax.scan(step, (c0, n0, m0), xs)
    h = jnp.swapaxes(hs, 0, 1).reshape(B, T, H, Dh)
    return h, c, n, m


def _chunk_mlp(u, v, ln_g_d, ln_b_d, sg_w, sg_b):
    B, T, _ = v.shape
    vn = _layer_norm(v, ln_g_d, ln_b_d)
    L = CHUNK_D if T % CHUNK_D == 0 else T
    nc = T // L
    w = jnp.where(jnp.tril(jnp.ones((L, L), dtype=bool)), sg_w[:, :L, :L], 0.0)
    vg = vn.reshape(B, nc, L, G_D, W_D // G_D)
    s = jnp.einsum('gts,bcsgd->bctgd', w, vg) + sg_b[:, :L].T[:, :, None]
    return u * s.reshape(B, T, W_D), vn


def _mem_kv(mem, w_mk, w_mv):
    B = mem.shape[0]
    kv = _proj(mem, jnp.concatenate([w_mk, w_mv], axis=1))
    return (kv[..., :W_M].reshape(B, N_MEM, H_M, DH_M), kv[..., W_M:].reshape(B, N_MEM, H_M, DH_M))


def _mem_attend(qm, mk, mv):
    B, T, _ = qm.shape
    q = qm.reshape(B, T, H_M, DH_M)
    s = jnp.einsum('bthd,bmhd->bhtm', q, mk) * (DH_M ** -0.5)
    p = jax.nn.softmax(s, axis=-1)
    return jnp.einsum('bhtm,bmhd->bthd', p, mv).reshape(B, T, W_M)


def _sample_queries(qp, batch, t_new):
    n = qp.shape[1]
    r1 = jnp.stack([qp[h, :, KV_LORA + h * ROPE_HALF:KV_LORA + (h + 1) * ROPE_HALF] for h in range(H_A)])
    r2 = jnp.stack([qp[h, :, KV_LORA + LANES + h * ROPE_HALF:KV_LORA + LANES + (h + 1) * ROPE_HALF]
                    for h in range(H_A)])
    q = jnp.concatenate([qp[:, :, :KV_LORA], r1, r2, jnp.zeros((H_A, n, LANES - QK_ROPE), qp.dtype)], axis=-1)
    q = q.reshape(H_A, batch, t_new, QS_W).transpose(1, 0, 2, 3)
    return q.reshape(batch, H_A * t_new, QS_W)


def _sample_self_keys(latent, k_rope, batch, t_new):
    k = jnp.concatenate([latent, k_rope, jnp.zeros((latent.shape[0], LANES - QK_ROPE), latent.dtype)], axis=-1)
    k = k.astype(BF16).reshape(batch, t_new, QS_W)
    return jnp.pad(k, ((0, 0), (0, PAGE_SIZE - t_new), (0, 0)))


def _even_layer(x2d, batch, seq, tables, mem_k, mem_v, lru_h0, lru_buf, paged, e, weights,
                q_norm, kv_norm, conv_w, conv_b, wa, ba, wx, bx, lam, w_out, ln_g, ln_b):
    w_in_r, w_uq_r, w_uk_r, w_uv_bd = weights
    zg, latent, k_rope, kp, qp = _even_in(x2d, tables, w_in_r, q_norm, kv_norm, w_uq_r, w_uk_r)
    if paged is None:
        y_a = _mla_prompt(qp, kp, w_uv_bd, batch, seq)
        h_b, h_last, new_buf = _rglru_seq(zg, batch, seq, lru_h0, lru_buf, conv_w, conv_b,
                                          _block_diag(wa).astype(BF16), _block_diag(wx).astype(BF16), ba, bx, lam)
    else:
        lat_pool, rope_pool, page_table = paged
        o = _mla_sample(_sample_queries(qp, batch, seq), _sample_self_keys(latent, k_rope, batch, seq),
                        lat_pool, rope_pool, page_table, e, seq)
        o = o.reshape(batch, H_A, seq, KV_LORA).transpose(0, 2, 1, 3).reshape(batch * seq, H_A * KV_LORA)
        y_a = _matmul(o, w_uv_bd)
        x_b = zg[:, ZE_XB:ZE_XB + W_B].reshape(batch, seq, W_B)
        h_b, h_last, new_buf = _rglru_short(x_b, lru_buf, lru_h0, conv_w, conv_b, wa, ba, wx, bx, lam)
        h_b = h_b.reshape(batch * seq, W_B)
    q_m = zg[:, ZE_QM:ZE_QM + W_M].reshape(batch, seq, W_M)
    y_m = _mem_attend(q_m, mem_k, mem_v).reshape(batch * seq, W_M)
    x_new = _out_proj_norm([y_a, h_b, y_m], zg, (ZE_GA, ZE_GB, ZE_GM), x2d, w_out, ln_g, ln_b)
    return (x_new, latent.reshape(batch, seq, KV_LORA), k_rope.reshape(batch, seq, QK_ROPE), h_last, new_buf)


ZO_Q, ZO_K, ZO_V, ZO_O, ZO_GC, ZO_U, ZO_VD, ZO_GD = (i * 512 for i in range(8))
ZO_QM, ZO_GM, ZO_IF = 4096, 4352, 4608


def _odd_weights(w_in):
    q, k, v, i_pre, f_pre, o_pre, g_c, u_d, v_d, g_d, q_m, g_m = _split_cols(w_in, ODD_SPLITS)
    gates = jnp.pad(jnp.concatenate([i_pre, f_pre], axis=1), ((0, 0), (0, LANES - 2 * H_C)))
    return jnp.concatenate([q, k, v, o_pre, g_c, u_d, v_d, g_d, q_m, g_m, gates], axis=1)


def _odd_layer(x2d, batch, seq, mem_k, mem_v, c0, n0, m0, w_in_r, b_if, ln_g_d, ln_b_d, sg_w, sg_b,
               w_out, ln_g, ln_b):
    z = _matmul(x2d, w_in_r)

    def seg(off, w):
        return z[:, off:off + w].reshape(batch, seq, w)

    def heads(a):
        return a.reshape(batch, seq, H_C, DH_C)

    ig = seg(ZO_IF, H_C) + b_if[:H_C]
    lf = jax.nn.log_sigmoid(seg(ZO_IF + H_C, H_C) + b_if[H_C:])
    h, c, n, m = _mlstm(heads(seg(ZO_Q, W_C)), heads(seg(ZO_K, W_C)) * (DH_C ** -0.5), heads(seg(ZO_V, W_C)),
                        ig, lf, c0, n0, m0)
    y_c = jax.nn.sigmoid(seg(ZO_O, W_C)) * h.reshape(batch, seq, W_C)
    y_d, vn = _chunk_mlp(seg(ZO_U, W_D), seg(ZO_VD, W_D), ln_g_d, ln_b_d, sg_w, sg_b)
    y_m = _mem_attend(seg(ZO_QM, W_M), mem_k, mem_v)
    flat = lambda a: a.reshape(batch * seq, a.shape[-1])
    x_new = _out_proj_norm([flat(y_c), flat(y_d), flat(y_m)], z, (ZO_GC, ZO_GD, ZO_GM), x2d, w_out, ln_g, ln_b)
    return x_new, vn, c, n, m


def kernel(x_prompt, x_sample, cache_mla_latent, cache_mla_krope, state_lru_h, state_lru_conv,
           state_mlstm_c, state_mlstm_n, state_mlstm_m, cache_mem_k, cache_mem_v, page_table,
           mem_prompt, w_in_even, mla_q_norm, mla_kv_norm, w_uq, w_uk, w_uv,
           lru_conv_w, lru_conv_b, lru_wa, lru_ba, lru_wx, lru_bx, lru_lambda, w_out_even,
           w_in_odd, mlstm_b_if, sg_ln_g, sg_ln_b, sg_w, sg_b, w_out_odd,
           w_mem_k, w_mem_v, ln_g, ln_b):
    Bp, Tp, _ = x_prompt.shape
    Bs, Ts, _ = x_sample.shape
    past_len = page_table.shape[1] * PAGE_SIZE
    tables_p = _rope_tables(jnp.arange(Tp, dtype=F32))
    tables_s = tuple(jnp.tile(t, (Bs, 1)) for t in _rope_tables(past_len + jnp.arange(Ts, dtype=F32)))

    h0_p = jnp.zeros((Bp, W_B), F32)
    buf0_p = jnp.zeros((Bp, CONV_W - 1, W_B), F32)
    c0_p = jnp.zeros((Bp, H_C, DH_C, DH_C), F32)
    n0_p = jnp.zeros((Bp, H_C, DH_C), F32)
    m0_p = jnp.zeros((Bp, H_C), F32)

    lat_p, kr_p, h_p, conv_p, c_p, n_p, m_p, mk_p, mv_p = [], [], [], [], [], [], [], [], []
    lat_s, kr_s, h_s, conv_s, c_s, n_s, m_s, v_s = [], [], [], [], [], [], [], []

    xp = x_prompt.reshape(Bp * Tp, D_MODEL)
    xs = x_sample.reshape(Bs * Ts, D_MODEL)
    for l in range(DEPTH):
        mk_l, mv_l = _mem_kv(mem_prompt, w_mem_k[l], w_mem_v[l])
        mk_p.append(mk_l)
        mv_p.append(mv_l)
        if l % 2 == 0:
            e = l // 2
            weights = _even_weights(w_in_even[e], w_uq[e], w_uk[e], w_uv[e])
            rest = (mla_q_norm[e], mla_kv_norm[e], lru_conv_w[e], lru_conv_b[e], lru_wa[e], lru_ba[e],
                    lru_wx[e], lru_bx[e], lru_lambda[e], w_out_even[e], ln_g[l], ln_b[l])
            xp, la, kr, hl, cb = _even_layer(xp, Bp, Tp, tables_p, mk_l, mv_l, h0_p, buf0_p, None, e,
                                             weights, *rest)
            lat_p.append(la); kr_p.append(kr); h_p.append(hl); conv_p.append(cb)
            xs, la, kr, hl, cb = _even_layer(xs, Bs, Ts, tables_s, cache_mem_k[l], cache_mem_v[l],
                                             state_lru_h[e], state_lru_conv[e],
                                             (cache_mla_latent, cache_mla_krope, page_table), e, weights, *rest)
            lat_s.append(la); kr_s.append(kr); h_s.append(hl); conv_s.append(cb)
        else:
            o = l // 2
            ow = (_odd_weights(w_in_odd[o]), mlstm_b_if[o], sg_ln_g[o], sg_ln_b[o], sg_w[o], sg_b[o],
                  w_out_odd[o], ln_g[l], ln_b[l])
            xp, _, cc, nn, mm = _odd_layer(xp, Bp, Tp, mk_l, mv_l, c0_p, n0_p, m0_p, *ow)
            c_p.append(cc); n_p.append(nn); m_p.append(mm)
            xs, vn, cc, nn, mm = _odd_layer(xs, Bs, Ts, cache_mem_k[l], cache_mem_v[l],
                                            state_mlstm_c[o], state_mlstm_n[o], state_mlstm_m[o], *ow)
            c_s.append(cc); n_s.append(nn); m_s.append(mm); v_s.append(vn)

    return (xp.reshape(Bp, Tp, D_MODEL), xs.reshape(Bs, Ts, D_MODEL),
            jnp.stack(lat_p), jnp.stack(kr_p), jnp.stack(h_p), jnp.stack(conv_p),
            jnp.stack(c_p), jnp.stack(n_p), jnp.stack(m_p), jnp.stack(mk_p), jnp.stack(mv_p),
            jnp.stack(lat_s), jnp.stack(kr_s), jnp.stack(h_s), jnp.stack(conv_s),
            jnp.stack(c_s), jnp.stack(n_s), jnp.stack(m_s), jnp.stack(v_s))
```

```python
import functools

import jax
import jax.numpy as jnp
import numpy as np
from jax import lax
from jax.experimental import pallas as pl
from jax.experimental.pallas import tpu as pltpu

D_MODEL = 1024
DEPTH = 2
PAGE_SIZE = 128
H_A = 8
Q_LORA = 384
KV_LORA = 256
QK_NOPE = 64
QK_ROPE = 32
ROPE_HALF = QK_ROPE // 2
V_HEAD = 64
W_A = H_A * V_HEAD
ROPE_THETA = 10000.0
MLA_SCALE = (QK_NOPE + QK_ROPE) ** -0.5
W_B = 512
NB_B = 8
BD_B = W_B // NB_B
CONV_W = 4
LRU_C = 8.0
H_C = 4
DH_C = 128
W_C = H_C * DH_C
MLSTM_CHUNK = 128
G_D = 4
W_D = 512
CHUNK_D = 128
N_MEM = 256
H_M = 4
DH_M = 64
W_M = H_M * DH_M
NORM_EPS = 1e-6
DEEPNORM_ALPHA = (2 * DEPTH) ** 0.25

EVEN_SPLITS = (Q_LORA, KV_LORA, QK_ROPE, W_A, W_B, W_B, W_M, W_M)
ODD_SPLITS = (W_C, W_C, W_C, H_C, H_C, W_C, W_C, W_D, W_D, W_D, W_M, W_M)

F32 = jnp.float32
BF16 = jnp.bfloat16
LANES = 128
SUBLANES = 8
VMEM_LIMIT = 48 * 1024 * 1024
LOG2E = 1.4426950408889634
NT_DIMS = (((1,), (1,)), ((), ()))

ZE_GA, ZE_XB, ZE_GB, ZE_QM, ZE_GM = 0, 512, 1024, 1536, 1792
ZE_GATES = 2048
ZE_CKV = 2048
ZE_CQ = ZE_CKV + KV_LORA
ZE_KR1 = ZE_CQ + Q_LORA
ZE_KR2 = ZE_KR1 + LANES
ZE_KRN = ZE_KR2 + LANES
ZE_KRS = ZE_KRN + LANES
ZE_W = ZE_KRS + LANES
QP_W = KV_LORA + 2 * LANES
UQ_NOPE_W = H_A * LANES
ZO_Q, ZO_K, ZO_V, ZO_O, ZO_GC, ZO_U, ZO_VD, ZO_GD = (i * 512 for i in range(8))
ZO_QM, ZO_GM, ZO_IF = 4096, 4352, 4608
ZO_W = ZO_IF + LANES


def _split_cols(z, sizes):
    cuts = [int(c) for c in np.cumsum(sizes)[:-1]]
    return jnp.split(z, cuts, axis=-1)


def _params(sem):
    return pltpu.CompilerParams(dimension_semantics=sem, vmem_limit_bytes=VMEM_LIMIT)


def _mm_body(x_ref, w_ref, o_ref):
    o_ref[...] = jnp.dot(x_ref[...].astype(BF16), w_ref[...], preferred_element_type=F32)


def _row_tile(m, n):
    tm = 512
    while tm > SUBLANES and (tm * n * 4 * 2 > 10 * 1024 * 1024 or m % tm):
        tm //= 2
    return tm


def _matmul(x, w):
    m, k = x.shape
    n = w.shape[1]
    n_pad = -n % LANES
    wb = w.astype(BF16)
    if n_pad:
        wb = jnp.pad(wb, ((0, 0), (0, n_pad)))
    np_ = n + n_pad
    tm = _row_tile(m, np_)
    out = pl.pallas_call(
        _mm_body,
        grid=(m // tm,),
        in_specs=[pl.BlockSpec((tm, k), lambda i: (i, 0)),
                  pl.BlockSpec((k, np_), lambda i: (0, 0))],
        out_specs=pl.BlockSpec((tm, np_), lambda i: (i, 0)),
        out_shape=jax.ShapeDtypeStruct((m, np_), F32),
        compiler_params=_params(("arbitrary",)),
        name="row_matmul",
    )(x, wb)
    return out[:, :n] if n_pad else out


def _proj(x, w):
    lead = x.shape[:-1]
    return _matmul(x.reshape(-1, x.shape[-1]), w).reshape(lead + (w.shape[1],))


def _even_weights(w_in, w_uq, w_uk, w_uv):
    c_q, c_kv, kr, g_a, x_b, g_b, q_m, g_m = _split_cols(w_in, EVEN_SPLITS)
    x1, x2 = kr[:, :ROPE_HALF], kr[:, ROPE_HALF:]

    def lane_pad(a):
        return jnp.pad(a, ((0, 0), (0, LANES - a.shape[1])))

    w_in_r = jnp.concatenate(
        [g_a, x_b, g_b, q_m, g_m, c_kv, c_q, jnp.tile(x1, (1, H_A)), jnp.tile(x2, (1, H_A)),
         lane_pad(kr), lane_pad(jnp.concatenate([x2, x1], axis=1))], axis=1).astype(BF16)
    r = w_uq.reshape(Q_LORA, H_A, QK_NOPE + QK_ROPE)
    nope = jnp.pad(r[:, :, :QK_NOPE], ((0, 0), (0, 0), (0, LANES - QK_NOPE))).reshape(Q_LORA, UQ_NOPE_W)
    r1 = r[:, :, QK_NOPE:QK_NOPE + ROPE_HALF].reshape(Q_LORA, LANES)
    r2 = r[:, :, QK_NOPE + ROPE_HALF:].reshape(Q_LORA, LANES)
    w_uq_r = jnp.concatenate([nope, r1, r2], axis=1).astype(BF16)
    w_uk_r = jnp.pad(jnp.transpose(w_uk, (1, 2, 0)), ((0, 0), (0, LANES - QK_NOPE), (0, 0))).astype(BF16)
    eye = jnp.eye(H_A, dtype=w_uv.dtype)
    w_uv_bd = jnp.einsum('chv,hg->hcgv', w_uv, eye).reshape(H_A * KV_LORA, W_A).astype(BF16)
    return w_in_r, w_uq_r, w_uk_r, w_uv_bd


def _rope_tables(pos):
    inv = ROPE_THETA ** (-jnp.arange(ROPE_HALF, dtype=F32) / ROPE_HALF)
    ang = pos.astype(F32)[:, None] * inv[None, :]
    cos, sin = jnp.cos(ang), jnp.sin(ang)
    zpad = jnp.zeros((pos.shape[0], LANES - QK_ROPE), F32)
    return (jnp.tile(cos, (1, H_A)), jnp.tile(sin, (1, H_A)),
            jnp.concatenate([cos, cos, zpad], axis=1), jnp.concatenate([-sin, sin, zpad], axis=1))


def _rms(x, g):
    return x * lax.rsqrt(jnp.mean(x * x, axis=-1, keepdims=True) + NORM_EPS) * g


def _even_in_body(x_ref, w_ref, qn_ref, kvn_ref, wuq_ref, wuk_ref, cos_ref, sin_ref, cosn_ref, sinn_ref,
                  zg_ref, lat_ref, kr_ref, kp_ref, qp_ref):
    z = jnp.dot(x_ref[...].astype(BF16), w_ref[...], preferred_element_type=F32)
    zg_ref[...] = z[:, :ZE_GATES]
    lat = _rms(z[:, ZE_CKV:ZE_CQ], kvn_ref[...])
    lat_ref[...] = lat
    cos, sin = cos_ref[...], sin_ref[...]
    kr1, kr2 = z[:, ZE_KR1:ZE_KR2], z[:, ZE_KR2:ZE_KRN]
    kp_ref[...] = jnp.concatenate([lat, kr1 * cos - kr2 * sin, kr1 * sin + kr2 * cos], axis=1).astype(BF16)
    kr_nat = z[:, ZE_KRN:ZE_KRS] * cosn_ref[...] + z[:, ZE_KRS:ZE_W] * sinn_ref[...]
    kr_ref[...] = kr_nat[:, :QK_ROPE]
    q = jnp.dot(_rms(z[:, ZE_CQ:ZE_KR1], qn_ref[...]).astype(BF16), wuq_ref[...],
                preferred_element_type=F32)
    q1, q2 = q[:, UQ_NOPE_W:UQ_NOPE_W + LANES], q[:, UQ_NOPE_W + LANES:]
    o1, o2 = q1 * cos - q2 * sin, q1 * sin + q2 * cos
    lane_head = lax.shift_right_logical(lax.broadcasted_iota(jnp.int32, o1.shape, 1), ROPE_HALF.bit_length() - 1)
    for h in range(H_A):
        ql = jnp.dot(q[:, h * LANES:(h + 1) * LANES].astype(BF16), wuk_ref[h], preferred_element_type=F32)
        own = lane_head == h
        qp_ref[h] = jnp.concatenate([ql, jnp.where(own, o1, 0.0), jnp.where(own, o2, 0.0)], axis=1).astype(BF16)


def _even_in(x2d, tables, w_in_r, q_norm, kv_norm, w_uq_r, w_uk_r):
    n = x2d.shape[0]
    tm = min(256, n)
    period = tables[0].shape[0] // tm
    row = lambda i: (i, 0)
    fixed2 = lambda i: (0, 0)
    tab = lambda i: (i % period, 0)
    return pl.pallas_call(
        _even_in_body,
        grid=(n // tm,),
        in_specs=[pl.BlockSpec((tm, D_MODEL), row),
                  pl.BlockSpec((D_MODEL, ZE_W), fixed2),
                  pl.BlockSpec((1, Q_LORA), fixed2),
                  pl.BlockSpec((1, KV_LORA), fixed2),
                  pl.BlockSpec((Q_LORA, UQ_NOPE_W + 2 * LANES), fixed2),
                  pl.BlockSpec((H_A, LANES, KV_LORA), lambda i: (0, 0, 0)),
                  pl.BlockSpec((tm, LANES), tab), pl.BlockSpec((tm, LANES), tab),
                  pl.BlockSpec((tm, LANES), tab), pl.BlockSpec((tm, LANES), tab)],
        out_specs=[pl.BlockSpec((tm, ZE_GATES), row),
                   pl.BlockSpec((tm, KV_LORA), row),
                   pl.BlockSpec((tm, QK_ROPE), row),
                   pl.BlockSpec((tm, QP_W), row),
                   pl.BlockSpec((H_A, tm, QP_W), lambda i: (0, i, 0))],
        out_shape=[jax.ShapeDtypeStruct((n, ZE_GATES), F32),
                   jax.ShapeDtypeStruct((n, KV_LORA), F32),
                   jax.ShapeDtypeStruct((n, QK_ROPE), F32),
                   jax.ShapeDtypeStruct((n, QP_W), BF16),
                   jax.ShapeDtypeStruct((H_A, n, QP_W), BF16)],
        compiler_params=_params(("arbitrary",)),
        name="even_in_proj",
    )(x2d, w_in_r, q_norm.reshape(1, -1), kv_norm.reshape(1, -1), w_uq_r, w_uk_r, *tables)


def _softmax_update(s, vals, m_sc, l_sc, acc_sc):
    m_prev = m_sc[...]
    m_new = jnp.maximum(m_prev, jnp.max(s, axis=-1, keepdims=True))
    alpha = jnp.exp2(m_prev - m_new)
    p = jnp.exp2(s - m_new)
    l_sc[...] = alpha * l_sc[...] + jnp.sum(p, axis=-1, keepdims=True)
    acc_sc[...] = alpha * acc_sc[...] + jnp.dot(p.astype(BF16), vals, preferred_element_type=F32)
    m_sc[...] = m_new


def _softmax_init(m_sc, l_sc, acc_sc):
    m_sc[...] = jnp.full(m_sc.shape, -jnp.inf, F32)
    l_sc[...] = jnp.zeros(l_sc.shape, F32)
    acc_sc[...] = jnp.zeros(acc_sc.shape, F32)


FLASH_TQ = 256
FLASH_TK = 512
FLAG_FIRST, FLAG_LAST, FLAG_DIAG = 1, 2, 4


def _flash_body(qb, kb, qo, ko, fl, q_ref, k_ref, wuv_ref, o_ref, m_sc, l_sc, acc_sc, *, tq, tk):
    i = pl.program_id(0)
    flags = fl[i]

    @pl.when((flags & FLAG_FIRST) != 0)
    def _():
        _softmax_init(m_sc, l_sc, acc_sc)

    q = q_ref[...].reshape(H_A * tq, QP_W)
    k = k_ref[...]
    s = lax.dot_general(q, k, NT_DIMS, preferred_element_type=F32) * (MLA_SCALE * LOG2E)
    vals = k[:, :KV_LORA]

    @pl.when((flags & FLAG_DIAG) != 0)
    def _():
        qpos = (lax.broadcasted_iota(jnp.int32, s.shape, 0) & (tq - 1)) + qo[i]
        kpos = lax.broadcasted_iota(jnp.int32, s.shape, 1) + ko[i]
        _softmax_update(jnp.where(kpos <= qpos, s, -jnp.inf), vals, m_sc, l_sc, acc_sc)

    @pl.when((flags & FLAG_DIAG) == 0)
    def _():
        _softmax_update(s, vals, m_sc, l_sc, acc_sc)

    @pl.when((flags & FLAG_LAST) != 0)
    def _():
        o = acc_sc[...] / l_sc[...]
        o_all = jnp.concatenate([o[h * tq:(h + 1) * tq] for h in range(H_A)], axis=1).astype(BF16)
        o_ref[...] = jnp.dot(o_all, wuv_ref[...], preferred_element_type=F32)


def _flash_steps(batch, seq, tq, tk):
    nq, nk = seq // tq, seq // tk
    qb, kb, qo, ko, fl = [], [], [], [], []
    for b in range(batch):
        for qi in range(nq):
            last = ((qi + 1) * tq - 1) // tk
            for kj in range(last + 1):
                qb.append(b * nq + qi)
                kb.append(b * nk + kj)
                qo.append(qi * tq)
                ko.append(kj * tk)
                diag = (kj + 1) * tk - 1 > qi * tq
                fl.append((FLAG_FIRST if kj == 0 else 0) | (FLAG_LAST if kj == last else 0)
                          | (FLAG_DIAG if diag else 0))
    return [np.asarray(a, np.int32) for a in (qb, kb, qo, ko, fl)]


def _mla_prompt(qp, kp, w_uv_bd, batch, seq):
    tq, tk = min(FLASH_TQ, seq), min(FLASH_TK, seq)
    assert tq & (tq - 1) == 0 and seq % tq == 0 and seq % tk == 0
    steps = _flash_steps(batch, seq, tq, tk)
    n = batch * seq
    rows = H_A * tq
    grid_spec = pltpu.PrefetchScalarGridSpec(
        num_scalar_prefetch=5,
        grid=(steps[0].shape[0],),
        in_specs=[pl.BlockSpec((H_A, tq, QP_W), lambda i, qb, kb, qo, ko, fl: (0, qb[i], 0)),
                  pl.BlockSpec((tk, QP_W), lambda i, qb, kb, qo, ko, fl: (kb[i], 0)),
                  pl.BlockSpec((H_A * KV_LORA, W_A), lambda i, qb, kb, qo, ko, fl: (0, 0))],
        out_specs=pl.BlockSpec((tq, W_A), lambda i, qb, kb, qo, ko, fl: (qb[i], 0)),
        scratch_shapes=[pltpu.VMEM((rows, 1), F32), pltpu.VMEM((rows, 1), F32),
                        pltpu.VMEM((rows, KV_LORA), F32)])
    return pl.pallas_call(
        functools.partial(_flash_body, tq=tq, tk=tk),
        grid_spec=grid_spec,
        out_shape=jax.ShapeDtypeStruct((n, W_A), F32),
        compiler_params=_params(("arbitrary",)),
        name="mla_prompt_flash",
    )(*[jnp.asarray(a) for a in steps], qp, kp, w_uv_bd)


PAGES_PER_STEP = 64


def _paged_body(pt_ref, ql_ref, qr_ref, kself_ref, krself_ref, *refs, npg, t_new):
    lat_refs, kr_refs = refs[:npg], refs[npg:2 * npg]
    o_ref = refs[2 * npg]
    lat_sc, kr_sc, m_sc, l_sc, acc_sc = refs[2 * npg + 1:]
    j = pl.program_id(1)

    @pl.when(j == 0)
    def _():
        _softmax_init(m_sc, l_sc, acc_sc)
        kr_sc[QK_ROPE:, :] = jnp.zeros((LANES - QK_ROPE, kr_sc.shape[1]), BF16)

    for i in range(npg):
        lat_sc[i * PAGE_SIZE:(i + 1) * PAGE_SIZE, :] = lat_refs[i][0, 0].astype(BF16)
        kr_sc[:QK_ROPE, i * PAGE_SIZE:(i + 1) * PAGE_SIZE] = kr_refs[i][0, 0].astype(BF16)

    ql, qr = ql_ref[0], qr_ref[0]

    def attend(lat, kr_t, mask):
        s = (lax.dot_general(ql, lat, NT_DIMS, preferred_element_type=F32)
             + jnp.dot(qr, kr_t, preferred_element_type=F32)) * (MLA_SCALE * LOG2E)
        if mask is not None:
            s = jnp.where(mask, s, -jnp.inf)
        _softmax_update(s, lat, m_sc, l_sc, acc_sc)

    attend(lat_sc[...], kr_sc[...], None)

    @pl.when(j == pl.num_programs(1) - 1)
    def _():
        shape = (ql.shape[0], PAGE_SIZE)
        t_row = lax.broadcasted_iota(jnp.int32, shape, 0) & (t_new - 1)
        attend(kself_ref[0], krself_ref[0], lax.broadcasted_iota(jnp.int32, shape, 1) <= t_row)
        o_ref[0] = acc_sc[...] / l_sc[...]


def _mla_sample(ql, qr, kself, krself, lat_pool, rope_pool_t, page_table, e, t_new):
    batch, rows, _ = ql.shape
    n_pages = page_table.shape[1]
    npg = min(PAGES_PER_STEP, n_pages)
    assert n_pages % npg == 0 and t_new & (t_new - 1) == 0 and t_new <= PAGE_SIZE
    chunks = n_pages // npg

    def page_map(i):
        return lambda b, j, pt: (e, pt[b * n_pages + j * npg + i], 0, 0)

    per_b = lambda b, j, pt: (b, 0, 0)
    in_specs = [pl.BlockSpec((1, rows, KV_LORA), per_b), pl.BlockSpec((1, rows, LANES), per_b),
                pl.BlockSpec((1, PAGE_SIZE, KV_LORA), per_b), pl.BlockSpec((1, LANES, PAGE_SIZE), per_b)]
    in_specs += [pl.BlockSpec((1, 1, PAGE_SIZE, KV_LORA), page_map(i)) for i in range(npg)]
    in_specs += [pl.BlockSpec((1, 1, QK_ROPE, PAGE_SIZE), page_map(i)) for i in range(npg)]
    grid_spec = pltpu.PrefetchScalarGridSpec(
        num_scalar_prefetch=1,
        grid=(batch, chunks),
        in_specs=in_specs,
        out_specs=pl.BlockSpec((1, rows, KV_LORA), per_b),
        scratch_shapes=[pltpu.VMEM((npg * PAGE_SIZE, KV_LORA), BF16),
                        pltpu.VMEM((LANES, npg * PAGE_SIZE), BF16),
                        pltpu.VMEM((rows, 1), F32), pltpu.VMEM((rows, 1), F32),
                        pltpu.VMEM((rows, KV_LORA), F32)])
    return pl.pallas_call(
        functools.partial(_paged_body, npg=npg, t_new=t_new),
        grid_spec=grid_spec,
        out_shape=jax.ShapeDtypeStruct((batch, rows, KV_LORA), F32),
        compiler_params=_params(("arbitrary", "arbitrary")),
        name="mla_sample_paged",
    )(page_table.reshape(-1), ql, qr, kself, krself, *([lat_pool] * npg), *([rope_pool_t] * npg))


LRU_TC = 256


def _block_diag(w):
    nb, d, e = w.shape
    return jnp.einsum('nde,nm->ndme', w, jnp.eye(nb, dtype=w.dtype)).reshape(nb * d, nb * e)


def _lru_coeffs(xc, wa, wx, ba, bx, lam):
    xb = xc.astype(BF16)
    r = jax.nn.sigmoid(jnp.dot(xb, wa, preferred_element_type=F32) + ba)
    ig = jax.nn.sigmoid(jnp.dot(xb, wx, preferred_element_type=F32) + bx)
    neg = -lam
    softplus = jnp.maximum(neg, 0.0) + jnp.log1p(jnp.exp(-jnp.abs(neg)))
    log_a = -LRU_C * r * softplus
    a = jnp.exp(log_a)
    t = jnp.tanh(log_a)
    b = jnp.sqrt(-2.0 * t / (1.0 - t)) * (ig * xc)
    return a, b


def _lru_body(x_ref, h0_ref, buf_ref, cw_ref, cb_ref, wa_ref, wx_ref, ba_ref, bx_ref, lam_ref,
              h_ref, hl_ref, tail_ref, xbuf, hc, *, tc):
    c = pl.program_id(1)

    @pl.when(c == 0)
    def _():
        xbuf[0:SUBLANES] = buf_ref[0]
        hc[...] = h0_ref[0]

    x = x_ref[...]
    xbuf[SUBLANES:SUBLANES + tc] = x
    cw = cw_ref[...]
    xc = cb_ref[...] + x * cw[CONV_W - 1:CONV_W]
    for j in range(CONV_W - 1):
        xc = xc + xbuf[pl.ds(SUBLANES - (CONV_W - 1) + j, tc), :] * cw[j:j + 1]
    xbuf[0:SUBLANES] = x[tc - SUBLANES:tc]
    a, b = _lru_coeffs(xc, wa_ref[...], wx_ref[...], ba_ref[...], bx_ref[...], lam_ref[...])
    row = lax.broadcasted_iota(jnp.int32, a.shape, 0)
    d = 1
    while d < tc:
        keep = row >= d
        a_sh = jnp.where(keep, pltpu.roll(a, d, 0), 1.0)
        b_sh = jnp.where(keep, pltpu.roll(b, d, 0), 0.0)
        b = a * b_sh + b
        a = a * a_sh
        d *= 2
    h = a * hc[...] + b
    h_ref[...] = h
    hc[...] = h[tc - 1:tc]

    @pl.when(c == pl.num_programs(1) - 1)
    def _():
        hl_ref[0] = h[tc - 1:tc]
        tail_ref[0] = x[tc - SUBLANES:tc]


def _rglru_seq(zg, batch, seq, h0, buf, conv_w, conv_b, wa_bd, wx_bd, ba, bx, lam):
    tc = min(LRU_TC, seq)
    assert seq % tc == 0 and tc >= SUBLANES
    nc = seq // tc
    buf8 = jnp.pad(buf, ((0, 0), (SUBLANES - (CONV_W - 1), 0), (0, 0)))
    vec = lambda a: a.reshape(1, W_B)
    fixed = lambda b, c: (0, 0)
    per_b = lambda b, c: (b, 0, 0)
    h, hl, tail = pl.pallas_call(
        functools.partial(_lru_body, tc=tc),
        grid=(batch, nc),
        in_specs=[pl.BlockSpec((tc, W_B), lambda b, c: (b * nc + c, ZE_XB // W_B)),
                  pl.BlockSpec((1, 1, W_B), per_b),
                  pl.BlockSpec((1, SUBLANES, W_B), per_b),
                  pl.BlockSpec((CONV_W, W_B), fixed),
                  pl.BlockSpec((1, W_B), fixed),
                  pl.BlockSpec((W_B, W_B), fixed), pl.BlockSpec((W_B, W_B), fixed),
                  pl.BlockSpec((1, W_B), fixed), pl.BlockSpec((1, W_B), fixed), pl.BlockSpec((1, W_B), fixed)],
        out_specs=[pl.BlockSpec((tc, W_B), lambda b, c: (b * nc + c, 0)),
                   pl.BlockSpec((1, 1, W_B), per_b),
                   pl.BlockSpec((1, SUBLANES, W_B), per_b)],
        out_shape=[jax.ShapeDtypeStruct((batch * seq, W_B), F32),
                   jax.ShapeDtypeStruct((batch, 1, W_B), F32),
                   jax.ShapeDtypeStruct((batch, SUBLANES, W_B), F32)],
        scratch_shapes=[pltpu.VMEM((SUBLANES + tc, W_B), F32), pltpu.VMEM((1, W_B), F32)],
        compiler_params=_params(("arbitrary", "arbitrary")),
        name="rglru_seq",
    )(zg, h0.reshape(batch, 1, W_B), buf8, conv_w, vec(conv_b), wa_bd, wx_bd, vec(ba), vec(bx), vec(lam))
    return h, hl[:, 0], tail[:, SUBLANES - (CONV_W - 1):]


def _out_body(v1_ref, v2_ref, v3_ref, g1_ref, g2_ref, g3_ref, x_ref, w_ref, lg_ref, lb_ref, o_ref):
    def gated(v_ref, g_ref):
        g = g_ref[...]
        return (v_ref[...] * (g * jax.nn.sigmoid(g))).astype(BF16)

    mixed = jnp.concatenate([gated(v1_ref, g1_ref), gated(v2_ref, g2_ref), gated(v3_ref, g3_ref)], axis=1)
    u = DEEPNORM_ALPHA * x_ref[...] + jnp.dot(mixed, w_ref[...], preferred_element_type=F32)
    mu = jnp.mean(u, axis=-1, keepdims=True)
    var = jnp.mean(jnp.square(u - mu), axis=-1, keepdims=True)
    o_ref[...] = (u - mu) * lax.rsqrt(var + NORM_EPS) * lg_ref[...] + lb_ref[...]


def _out_proj_norm(vals, z, gate_cols, x2d, w_out, ln_g, ln_b):
    n = x2d.shape[0]
    tm = min(256, n)
    widths = [v.shape[1] for v in vals]
    row = lambda i: (i, 0)
    fixed = lambda i: (0, 0)
    in_specs = [pl.BlockSpec((tm, w), row) for w in widths]
    for w, off in zip(widths, gate_cols):
        assert off % w == 0
        in_specs.append(pl.BlockSpec((tm, w), functools.partial(lambda i, cb: (i, cb), cb=off // w)))
    in_specs += [pl.BlockSpec((tm, D_MODEL), row),
                 pl.BlockSpec((sum(widths), D_MODEL), fixed),
                 pl.BlockSpec((1, D_MODEL), fixed), pl.BlockSpec((1, D_MODEL), fixed)]
    return pl.pallas_call(
        _out_body,
        grid=(n // tm,),
        in_specs=in_specs,
        out_specs=pl.BlockSpec((tm, D_MODEL), row),
        out_shape=jax.ShapeDtypeStruct((n, D_MODEL), F32),
        compiler_params=_params(("arbitrary",)),
        name="out_proj_norm",
    )(*vals, z, z, z, x2d, w_out.astype(BF16), ln_g.reshape(1, -1), ln_b.reshape(1, -1))


def _mem_attend_rows(q, k, v):
    kb = k.astype(BF16)
    q_head = lax.shift_right_logical(lax.broadcasted_iota(jnp.int32, q.shape, 1), DH_M.bit_length() - 1)
    v_head = lax.shift_right_logical(lax.broadcasted_iota(jnp.int32, v.shape, 1), DH_M.bit_length() - 1)
    out = jnp.zeros(q.shape, F32)
    for h in range(H_M):
        qh = jnp.where(q_head == h, q, 0.0).astype(BF16)
        s = lax.dot_general(qh, kb, NT_DIMS, preferred_element_type=F32) * (DH_M ** -0.5)
        p = jnp.exp(s - jnp.max(s, axis=-1, keepdims=True))
        p = p / jnp.sum(p, axis=-1, keepdims=True)
        vh = jnp.where(v_head == h, v, 0.0).astype(BF16)
        out = out + jnp.dot(p.astype(BF16), vh, preferred_element_type=F32)
    return out


def _mem_body(q_ref, k_ref, v_ref, o_ref):
    for i in range(k_ref.shape[0]):
        q = q_ref[i] if len(q_ref.shape) == 3 else q_ref[...]
        y = _mem_attend_rows(q, k_ref[i], v_ref[i])
        if len(o_ref.shape) == 3:
            o_ref[i] = y
        else:
            o_ref[...] = y


def _mem_attend_long(z, col, batch, seq, mem_k, mem_v):
    tm = min(512, seq)
    nt = seq // tm
    kv = lambda b, i: (b, 0, 0)
    return pl.pallas_call(
        _mem_body,
        grid=(batch, nt),
        in_specs=[pl.BlockSpec((tm, W_M), lambda b, i: (b * nt + i, col // W_M)),
                  pl.BlockSpec((1, N_MEM, W_M), kv), pl.BlockSpec((1, N_MEM, W_M), kv)],
        out_specs=pl.BlockSpec((tm, W_M), lambda b, i: (b * nt + i, 0)),
        out_shape=jax.ShapeDtypeStruct((batch * seq, W_M), F32),
        compiler_params=_params(("arbitrary", "arbitrary")),
        name="mem_attend_long",
    )(z, mem_k.reshape(batch, N_MEM, W_M), mem_v.reshape(batch, N_MEM, W_M))


MEM_SHORT_NB = 8


def _mem_attend_short(z, col, batch, seq, mem_k, mem_v):
    assert seq <= SUBLANES
    nb = min(MEM_SHORT_NB, batch)
    q = jnp.pad(z[:, col:col + W_M].reshape(batch, seq, W_M), ((0, 0), (0, SUBLANES - seq), (0, 0)))
    blk = lambda i: (i, 0, 0)
    y = pl.pallas_call(
        _mem_body,
        grid=(batch // nb,),
        in_specs=[pl.BlockSpec((nb, SUBLANES, W_M), blk),
                  pl.BlockSpec((nb, N_MEM, W_M), blk), pl.BlockSpec((nb, N_MEM, W_M), blk)],
        out_specs=pl.BlockSpec((nb, SUBLANES, W_M), blk),
        out_shape=jax.ShapeDtypeStruct((batch, SUBLANES, W_M), F32),
        compiler_params=_params(("arbitrary",)),
        name="mem_attend_short",
    )(q, mem_k.reshape(batch, N_MEM, W_M), mem_v.reshape(batch, N_MEM, W_M))
    return y[:, :seq].reshape(batch * seq, W_M)


def _chunk_mlp_body(u_ref, v_ref, g_ref, b_ref, w_ref, bias_ref, y_ref, vn_ref, *, rows, chunks):
    gw = W_D // G_D
    for c in range(chunks):
        sl = pl.ds(c * rows, rows)
        v = v_ref[sl, :]
        mu = jnp.mean(v, axis=-1, keepdims=True)
        var = jnp.mean(jnp.square(v - mu), axis=-1, keepdims=True)
        vn = (v - mu) * lax.rsqrt(var + NORM_EPS) * g_ref[...] + b_ref[...]
        vn_ref[sl, :] = vn
        vb = vn.astype(BF16)
        s = jnp.concatenate([jnp.dot(w_ref[g], vb[:, g * gw:(g + 1) * gw], preferred_element_type=F32)
                             for g in range(G_D)], axis=1)
        y_ref[sl, :] = u_ref[sl, :] * (s + bias_ref[...])


def _chunk_mlp_call(z, rows, ln_g_d, ln_b_d, w_mix, bias):
    n = z.shape[0]
    chunks = max(1, min(4, n // rows))
    tm = rows * chunks
    row = lambda i: (i, 0)
    fixed = lambda i: (0, 0)
    return pl.pallas_call(
        functools.partial(_chunk_mlp_body, rows=rows, chunks=chunks),
        grid=(n // tm,),
        in_specs=[pl.BlockSpec((tm, W_D), lambda i: (i, ZO_U // W_D)),
                  pl.BlockSpec((tm, W_D), lambda i: (i, ZO_VD // W_D)),
                  pl.BlockSpec((1, W_D), fixed), pl.BlockSpec((1, W_D), fixed),
                  pl.BlockSpec((G_D, rows, rows), lambda i: (0, 0, 0)),
                  pl.BlockSpec((rows, W_D), fixed)],
        out_specs=[pl.BlockSpec((tm, W_D), row), pl.BlockSpec((tm, W_D), row)],
        out_shape=[jax.ShapeDtypeStruct((n, W_D), F32), jax.ShapeDtypeStruct((n, W_D), F32)],
        compiler_params=_params(("arbitrary",)),
        name="chunk_mlp",
    )(z, z, ln_g_d.reshape(1, -1), ln_b_d.reshape(1, -1), w_mix, bias)


def _chunk_mlp_weights(sg_w, sg_b, batch, seq):
    L = CHUNK_D if seq % CHUNK_D == 0 else seq
    w = jnp.where(jnp.tril(jnp.ones((L, L), dtype=bool)), sg_w[:, :L, :L], 0.0)
    bias = jnp.repeat(sg_b[:, :L].T, W_D // G_D, axis=1)
    if L == seq and L != CHUNK_D:
        eye = jnp.eye(batch, dtype=w.dtype)
        w = jnp.einsum('gts,bc->gbtcs', w, eye).reshape(G_D, batch * L, batch * L)
        bias = jnp.tile(bias, (batch, 1))
    return w.astype(BF16), bias


def _cumsum_rows(x):
    row = lax.broadcasted_iota(jnp.int32, x.shape, 0)
    d = 1
    while d < x.shape[0]:
        x = x + jnp.where(row >= d, pltpu.roll(x, d, 0), 0.0)
        d *= 2
    return x


def _mlstm_body(q_ref, k_ref, v_ref, o_ref, gt_ref, bif_ref, c0_ref, n0_ref, m0_ref,
                y_ref, c_ref, n_ref, m_ref, c_sc, n_sc, m_sc, *stage, nb, n_valid):
    L = MLSTM_CHUNK
    step = pl.program_id(1)

    @pl.when(step == 0)
    def _():
        c_sc[...] = c0_ref[...]
        n_sc[...] = n0_ref[...]
        m_sc[...] = m0_ref[...]
        for st in stage:
            st[...] = jnp.zeros(st.shape, F32)

    row = lax.broadcasted_iota(jnp.int32, (L, LANES), 0)
    lane = lax.broadcasted_iota(jnp.int32, (L, LANES), 1)
    tril = lax.broadcasted_iota(jnp.int32, (L, L), 0) >= lax.broadcasted_iota(jnp.int32, (L, L), 1)
    for b in range(nb):
        if stage:
            rows_in = q_ref.shape[1]
            bufs = []
            for st, ref in zip(stage, (q_ref, k_ref, v_ref, o_ref, gt_ref)):
                st[b, 0:rows_in, :] = ref[b]
                bufs.append(st[b])
            q, k, v, o_pre, gts = bufs
        else:
            q, k, v, o_pre, gts = q_ref[b], k_ref[b], v_ref[b], o_ref[b], gt_ref[b]
        gz = gts + bif_ref[...]
        lf = jnp.minimum(gz, 0.0) - jnp.log1p(jnp.exp(-jnp.abs(gz)))
        ig = gz
        if n_valid < L:
            ig = jnp.where(row < n_valid, ig, -jnp.inf)
            lf = jnp.where(row < n_valid, lf, 0.0)
        a = jnp.where(lane < H_C, ig, _cumsum_rows(lf))
        a_t = a.T
        ys = []
        for h in range(H_C):
            hs = slice(h * DH_C, (h + 1) * DH_C)
            ig_row, bc_row = a_t[h:h + 1, :], a_t[H_C + h:H_C + h + 1, :]
            ig_col, bc_col = a[:, h:h + 1], a[:, H_C + h:H_C + h + 1]
            m_old = m_sc[b, h][:, 0:1]
            dmat = jnp.where(tril, bc_col - bc_row + ig_row, -jnp.inf)
            inter = bc_col + m_old
            mt = jnp.maximum(inter, jnp.max(dmat, axis=-1, keepdims=True))
            w = jnp.exp(dmat - mt)
            qh = q[:, hs].astype(BF16)
            kh = k[:, hs] * (DH_C ** -0.5)
            vh = v[:, hs].astype(BF16)
            sc = lax.dot_general(qh, kh.astype(BF16), NT_DIMS, preferred_element_type=F32) * w
            g_inter = jnp.exp(inter - mt)
            c_old = c_sc[b, h]
            n_old = n_sc[b, h]
            num = (jnp.dot(sc.astype(BF16), vh, preferred_element_type=F32)
                   + g_inter * jnp.dot(qh, c_old.astype(BF16), preferred_element_type=F32))
            den = (jnp.sum(sc, axis=-1, keepdims=True)
                   + g_inter * jnp.sum(q[:, hs] * n_old, axis=-1, keepdims=True))
            hh = num / jnp.maximum(jnp.abs(den), jnp.exp(-mt))
            ys.append(jax.nn.sigmoid(o_pre[:, hs]) * hh)
            b_end = bc_col[L - 1:L, :]
            g_col = b_end - bc_col + ig_col
            m_new = jnp.maximum(b_end + m_old, jnp.max(g_col, axis=0, keepdims=True))
            decay = jnp.exp(b_end + m_old - m_new)
            kw = kh * jnp.exp(g_col - m_new)
            c_sc[b, h] = decay * c_old + jnp.dot(kw.T.astype(BF16), vh, preferred_element_type=F32)
            n_sc[b, h] = decay * n_old + jnp.sum(kw, axis=0, keepdims=True)
            m_sc[b, h] = jnp.broadcast_to(m_new, (1, LANES))
        y = jnp.concatenate(ys, axis=1)
        y_ref[b] = y[0:y_ref.shape[1]]

    @pl.when(step == pl.num_programs(1) - 1)
    def _():
        c_ref[...] = c_sc[...]
        n_ref[...] = n_sc[...]
        m_ref[...] = m_sc[...]


MLSTM_NB = 2


def _mlstm_call(z, batch, seq, b_if, c0, n0, m0):
    L = MLSTM_CHUNK
    nb = min(MLSTM_NB, batch)
    long = seq % L == 0
    if long:
        rows, nc, n_valid = L, seq // L, L
        z3 = z.reshape(batch, seq, z.shape[1])
    else:
        assert seq <= SUBLANES
        rows, nc, n_valid = SUBLANES, 1, seq
        z3 = jnp.pad(z.reshape(batch, seq, z.shape[1]), ((0, 0), (0, SUBLANES - seq), (0, 0)))
    bif = jnp.pad(b_if, (0, LANES - 2 * H_C)).reshape(1, LANES)
    m0r = jnp.broadcast_to(m0[:, :, None, None], (batch, H_C, 1, LANES))

    def col(off, w):
        return pl.BlockSpec((nb, rows, w), functools.partial(lambda g, c, cb: (g, c, cb), cb=off // w))

    st4 = lambda g, c: (g, 0, 0, 0)
    in_specs = [col(ZO_Q, W_C), col(ZO_K, W_C), col(ZO_V, W_C), col(ZO_O, W_C), col(ZO_IF, LANES),
                pl.BlockSpec((1, LANES), lambda g, c: (0, 0)),
                pl.BlockSpec((nb, H_C, DH_C, DH_C), st4),
                pl.BlockSpec((nb, H_C, 1, DH_C), st4),
                pl.BlockSpec((nb, H_C, 1, LANES), st4)]
    out_specs = [pl.BlockSpec((nb, rows, W_C), lambda g, c: (g, c, 0)),
                 pl.BlockSpec((nb, H_C, DH_C, DH_C), st4),
                 pl.BlockSpec((nb, H_C, 1, DH_C), st4),
                 pl.BlockSpec((nb, H_C, 1, LANES), st4)]
    scratch = [pltpu.VMEM((nb, H_C, DH_C, DH_C), F32), pltpu.VMEM((nb, H_C, 1, DH_C), F32),
               pltpu.VMEM((nb, H_C, 1, LANES), F32)]
    if not long:
        scratch += [pltpu.VMEM((nb, L, W_C), F32)] * 4 + [pltpu.VMEM((nb, L, LANES), F32)]
    y, c, n, m = pl.pallas_call(
        functools.partial(_mlstm_body, nb=nb, n_valid=n_valid),
        grid=(batch // nb, nc),
        in_specs=in_specs,
        out_specs=out_specs,
        out_shape=[jax.ShapeDtypeStruct((batch, rows * nc, W_C), F32),
                   jax.ShapeDtypeStruct((batch, H_C, DH_C, DH_C), F32),
                   jax.ShapeDtypeStruct((batch, H_C, 1, DH_C), F32),
                   jax.ShapeDtypeStruct((batch, H_C, 1, LANES), F32)],
        scratch_shapes=scratch,
        compiler_params=_params(("arbitrary", "arbitrary")),
        name="mlstm_chunks",
    )(z3, z3, z3, z3, z3, bif, c0, n0[:, :, None, :], m0r)
    return y[:, :seq].reshape(batch * seq, W_C), c, n[:, :, 0], m[:, :, 0, 0]


def _lru_short_body(x_ref, h0_ref, buf_ref, cw_ref, cb_ref, wa_ref, wx_ref, ba_ref, bx_ref, lam_ref,
                    h_ref, *, batch, seq):
    cw = cw_ref[...]
    xp = jnp.concatenate([buf_ref[...], x_ref[...]], axis=0)
    xc = cb_ref[...] + xp[(CONV_W - 1) * batch:] * cw[CONV_W - 1:CONV_W]
    for j in range(CONV_W - 1):
        xc = xc + xp[j * batch:(j + seq) * batch] * cw[j:j + 1]
    a, b = _lru_coeffs(xc, wa_ref[...], wx_ref[...], ba_ref[...], bx_ref[...], lam_ref[...])
    h = h0_ref[...]
    for t in range(seq):
        sl = slice(t * batch, (t + 1) * batch)
        h = a[sl] * h + b[sl]
        h_ref[sl, :] = h


def _rglru_short(zg, batch, seq, h0, buf, conv_w, conv_b, wa_bd, wx_bd, ba, bx, lam):
    assert batch % SUBLANES == 0
    x = zg[:, ZE_XB:ZE_XB + W_B].reshape(batch, seq, W_B)
    xp = jnp.concatenate([buf, x], axis=1)
    x_tm = jnp.swapaxes(x, 0, 1).reshape(seq * batch, W_B)
    buf_tm = jnp.swapaxes(buf, 0, 1).reshape((CONV_W - 1) * batch, W_B)
    vec = lambda a: a.reshape(1, W_B)
    h_tm = pl.pallas_call(
        functools.partial(_lru_short_body, batch=batch, seq=seq),
        out_shape=jax.ShapeDtypeStruct((seq * batch, W_B), F32),
        compiler_params=pltpu.CompilerParams(vmem_limit_bytes=VMEM_LIMIT),
        name="rglru_short",
    )(x_tm, h0, buf_tm, conv_w, vec(conv_b), wa_bd, wx_bd, vec(ba), vec(bx), vec(lam))
    h = jnp.swapaxes(h_tm.reshape(seq, batch, W_B), 0, 1)
    return h.reshape(batch * seq, W_B), h[:, -1], xp[:, -(CONV_W - 1):]


def _mem_kv(mem, w_mk, w_mv):
    B = mem.shape[0]
    kv = _proj(mem, jnp.concatenate([w_mk, w_mv], axis=1))
    return (kv[..., :W_M].reshape(B, N_MEM, H_M, DH_M), kv[..., W_M:].reshape(B, N_MEM, H_M, DH_M))


def _sample_queries(qp, batch, t_new):
    n = qp.shape[1]
    r1 = jnp.stack([qp[h, :, KV_LORA + h * ROPE_HALF:KV_LORA + (h + 1) * ROPE_HALF] for h in range(H_A)])
    r2 = jnp.stack([qp[h, :, KV_LORA + LANES + h * ROPE_HALF:KV_LORA + LANES + (h + 1) * ROPE_HALF]
                    for h in range(H_A)])
    qr = jnp.concatenate([r1, r2, jnp.zeros((H_A, n, LANES - QK_ROPE), qp.dtype)], axis=-1)

    def rows(a):
        w = a.shape[-1]
        return a.reshape(H_A, batch, t_new, w).transpose(1, 0, 2, 3).reshape(batch, H_A * t_new, w)

    return rows(qp[:, :, :KV_LORA]), rows(qr)


def _sample_self_keys(latent, k_rope, batch, t_new):
    kl = jnp.pad(latent.astype(BF16).reshape(batch, t_new, KV_LORA), ((0, 0), (0, PAGE_SIZE - t_new), (0, 0)))
    kr_t = jnp.swapaxes(k_rope.astype(BF16).reshape(batch, t_new, QK_ROPE), 1, 2)
    return kl, jnp.pad(kr_t, ((0, 0), (0, LANES - QK_ROPE), (0, PAGE_SIZE - t_new)))


def _even_layer(x2d, batch, seq, tables, mem_k, mem_v, lru_h0, lru_buf, paged, e, weights,
                q_norm, kv_norm, conv_w, conv_b, wa, ba, wx, bx, lam, w_out, ln_g, ln_b):
    w_in_r, w_uq_r, w_uk_r, w_uv_bd = weights
    zg, latent, k_rope, kp, qp = _even_in(x2d, tables, w_in_r, q_norm, kv_norm, w_uq_r, w_uk_r)
    lru_args = (lru_h0, lru_buf, conv_w, conv_b, _block_diag(wa).astype(BF16), _block_diag(wx).astype(BF16),
                ba, bx, lam)
    if paged is None:
        y_a = _mla_prompt(qp, kp, w_uv_bd, batch, seq)
        h_b, h_last, new_buf = _rglru_seq(zg, batch, seq, *lru_args)
        y_m = _mem_attend_long(zg, ZE_QM, batch, seq, mem_k, mem_v)
    else:
        lat_pool, rope_pool, page_table = paged
        o = _mla_sample(*_sample_queries(qp, batch, seq), *_sample_self_keys(latent, k_rope, batch, seq),
                        lat_pool, jnp.swapaxes(rope_pool, 2, 3), page_table, e, seq)
        o = o.reshape(batch, H_A, seq, KV_LORA).transpose(0, 2, 1, 3).reshape(batch * seq, H_A * KV_LORA)
        y_a = _matmul(o, w_uv_bd)
        h_b, h_last, new_buf = _rglru_short(zg, batch, seq, *lru_args)
        y_m = _mem_attend_short(zg, ZE_QM, batch, seq, mem_k, mem_v)
    x_new = _out_proj_norm([y_a, h_b, y_m], zg, (ZE_GA, ZE_GB, ZE_GM), x2d, w_out, ln_g, ln_b)
    return (x_new, latent.reshape(batch, seq, KV_LORA), k_rope.reshape(batch, seq, QK_ROPE), h_last, new_buf)


def _odd_weights(w_in):
    q, k, v, i_pre, f_pre, o_pre, g_c, u_d, v_d, g_d, q_m, g_m = _split_cols(w_in, ODD_SPLITS)
    gates = jnp.pad(jnp.concatenate([i_pre, f_pre], axis=1), ((0, 0), (0, LANES - 2 * H_C)))
    return jnp.concatenate([q, k, v, o_pre, g_c, u_d, v_d, g_d, q_m, g_m, gates], axis=1)


def _odd_layer(x2d, batch, seq, mem_k, mem_v, c0, n0, m0, w_in_r, b_if, ln_g_d, ln_b_d, sg_w, sg_b,
               w_out, ln_g, ln_b):
    z = _matmul(x2d, w_in_r)
    y_c, c, n, m = _mlstm_call(z, batch, seq, b_if, c0, n0, m0)
    w_mix, bias = _chunk_mlp_weights(sg_w, sg_b, batch, seq)
    y_d, vn = _chunk_mlp_call(z, w_mix.shape[1], ln_g_d, ln_b_d, w_mix, bias)
    attend = _mem_attend_long if seq > SUBLANES else _mem_attend_short
    y_m = attend(z, ZO_QM, batch, seq, mem_k, mem_v)
    x_new = _out_proj_norm([y_c, y_d, y_m], z, (ZO_GC, ZO_GD, ZO_GM), x2d, w_out, ln_g, ln_b)
    return x_new, vn.reshape(batch, seq, W_D), c, n, m


def kernel(x_prompt, x_sample, cache_mla_latent, cache_mla_krope, state_lru_h, state_lru_conv,
           state_mlstm_c, state_mlstm_n, state_mlstm_m, cache_mem_k, cache_mem_v, page_table,
           mem_prompt, w_in_even, mla_q_norm, mla_kv_norm, w_uq, w_uk, w_uv,
           lru_conv_w, lru_conv_b, lru_wa, lru_ba, lru_wx, lru_bx, lru_lambda, w_out_even,
           w_in_odd, mlstm_b_if, sg_ln_g, sg_ln_b, sg_w, sg_b, w_out_odd,
           w_mem_k, w_mem_v, ln_g, ln_b):
    Bp, Tp, _ = x_prompt.shape
    Bs, Ts, _ = x_sample.shape
    past_len = page_table.shape[1] * PAGE_SIZE
    tables_p = _rope_tables(jnp.arange(Tp, dtype=F32))
    tables_s = tuple(jnp.tile(t, (Bs, 1)) for t in _rope_tables(past_len + jnp.arange(Ts, dtype=F32)))

    h0_p = jnp.zeros((Bp, W_B), F32)
    buf0_p = jnp.zeros((Bp, CONV_W - 1, W_B), F32)
    c0_p = jnp.zeros((Bp, H_C, DH_C, DH_C), F32)
    n0_p = jnp.zeros((Bp, H_C, DH_C), F32)
    m0_p = jnp.zeros((Bp, H_C), F32)

    lat_p, kr_p, h_p, conv_p, c_p, n_p, m_p, mk_p, mv_p = [], [], [], [], [], [], [], [], []
    lat_s, kr_s, h_s, conv_s, c_s, n_s, m_s, v_s = [], [], [], [], [], [], [], []

    xp = x_prompt.reshape(Bp * Tp, D_MODEL)
    xs = x_sample.reshape(Bs * Ts, D_MODEL)
    for l in range(DEPTH):
        mk_l, mv_l = _mem_kv(mem_prompt, w_mem_k[l], w_mem_v[l])
        mk_p.append(mk_l)
        mv_p.append(mv_l)
        if l % 2 == 0:
            e = l // 2
            weights = _even_weights(w_in_even[e], w_uq[e], w_uk[e], w_uv[e])
            rest = (mla_q_norm[e], mla_kv_norm[e], lru_conv_w[e], lru_conv_b[e], lru_wa[e], lru_ba[e],
                    lru_wx[e], lru_bx[e], lru_lambda[e], w_out_even[e], ln_g[l], ln_b[l])
            xp, la, kr, hl, cb = _even_layer(xp, Bp, Tp, tables_p, mk_l, mv_l, h0_p, buf0_p, None, e,
                                             weights, *rest)
            lat_p.append(la); kr_p.append(kr); h_p.append(hl); conv_p.append(cb)
            xs, la, kr, hl, cb = _even_layer(xs, Bs, Ts, tables_s, cache_mem_k[l], cache_mem_v[l],
                                             state_lru_h[e], state_lru_conv[e],
                                             (cache_mla_latent, cache_mla_krope, page_table), e, weights, *rest)
            lat_s.append(la); kr_s.append(kr); h_s.append(hl); conv_s.append(cb)
        else:
            o = l // 2
            ow = (_odd_weights(w_in_odd[o]), mlstm_b_if[o], sg_ln_g[o], sg_ln_b[o], sg_w[o], sg_b[o],
                  w_out_odd[o], ln_g[l], ln_b[l])
            xp, _, cc, nn, mm = _odd_layer(xp, Bp, Tp, mk_l, mv_l, c0_p, n0_p, m0_p, *ow)
            c_p.append(cc); n_p.append(nn); m_p.append(mm)
            xs, vn, cc, nn, mm = _odd_layer(xs, Bs, Ts, cache_mem_k[l], cache_mem_v[l],
                                            state_mlstm_c[o], state_mlstm_n[o], state_mlstm_m[o], *ow)
            c_s.append(cc); n_s.append(nn); m_s.append(mm); v_s.append(vn)

    return (xp.reshape(Bp, Tp, D_MODEL), xs.reshape(Bs, Ts, D_MODEL),
            jnp.stack(lat_p), jnp.stack(kr_p), jnp.stack(h_p), jnp.stack(conv_p),
            jnp.stack(c_p), jnp.stack(n_p), jnp.stack(m_p), jnp.stack(mk_p), jnp.stack(mv_p),
            jnp.stack(lat_s), jnp.stack(kr_s), jnp.stack(h_s), jnp.stack(conv_s),
            jnp.stack(c_s), jnp.stack(n_s), jnp.stack(m_s), jnp.stack(v_s))
```

```python
import functools

import jax
import jax.numpy as jnp
import numpy as np
from jax import lax
from jax.experimental import pallas as pl
from jax.experimental.pallas import tpu as pltpu

D_MODEL = 1024
DEPTH = 2
PAGE_SIZE = 128
H_A = 8
Q_LORA = 384
KV_LORA = 256
QK_NOPE = 64
QK_ROPE = 32
ROPE_HALF = QK_ROPE // 2
V_HEAD = 64
W_A = H_A * V_HEAD
ROPE_THETA = 10000.0
MLA_SCALE = (QK_NOPE + QK_ROPE) ** -0.5
W_B = 512
NB_B = 8
BD_B = W_B // NB_B
CONV_W = 4
LRU_C = 8.0
H_C = 4
DH_C = 128
W_C = H_C * DH_C
MLSTM_CHUNK = 128
G_D = 4
W_D = 512
CHUNK_D = 128
N_MEM = 256
H_M = 4
DH_M = 64
W_M = H_M * DH_M
NORM_EPS = 1e-6
DEEPNORM_ALPHA = (2 * DEPTH) ** 0.25

EVEN_SPLITS = (Q_LORA, KV_LORA, QK_ROPE, W_A, W_B, W_B, W_M, W_M)
ODD_SPLITS = (W_C, W_C, W_C, H_C, H_C, W_C, W_C, W_D, W_D, W_D, W_M, W_M)

F32 = jnp.float32
BF16 = jnp.bfloat16
LANES = 128
SUBLANES = 8
VMEM_LIMIT = 48 * 1024 * 1024
LOG2E = 1.4426950408889634
NT_DIMS = (((1,), (1,)), ((), ()))

ZE_GA, ZE_XB, ZE_GB, ZE_QM, ZE_GM = 0, 512, 1024, 1536, 1792
ZE_GATES = 2048
ZE_CKV = 2048
ZE_CQ = ZE_CKV + KV_LORA
ZE_KR1 = ZE_CQ + Q_LORA
ZE_KR2 = ZE_KR1 + LANES
ZE_KRN = ZE_KR2 + LANES
ZE_KRS = ZE_KRN + LANES
ZE_W = ZE_KRS + LANES
QP_W = KV_LORA + 2 * LANES
UQ_NOPE_W = H_A * LANES
ZO_Q, ZO_K, ZO_V, ZO_O, ZO_GC, ZO_U, ZO_VD, ZO_GD = (i * 512 for i in range(8))
ZO_QM, ZO_GM, ZO_IF = 4096, 4352, 4608
ZO_W = ZO_IF + LANES


def _split_cols(z, sizes):
    cuts = [int(c) for c in np.cumsum(sizes)[:-1]]
    return jnp.split(z, cuts, axis=-1)


def _params(sem):
    return pltpu.CompilerParams(dimension_semantics=sem, vmem_limit_bytes=VMEM_LIMIT)


def _mm_body(x_ref, w_ref, o_ref):
    o_ref[...] = jnp.dot(x_ref[...].astype(BF16), w_ref[...], preferred_element_type=F32)


def _row_tile(m, n):
    tm = 512
    while tm > SUBLANES and (tm * n * 4 * 2 > 10 * 1024 * 1024 or m % tm):
        tm //= 2
    return tm


def _matmul(x, w):
    m, k = x.shape
    n = w.shape[1]
    n_pad = -n % LANES
    wb = w.astype(BF16)
    if n_pad:
        wb = jnp.pad(wb, ((0, 0), (0, n_pad)))
    np_ = n + n_pad
    tm = _row_tile(m, np_)
    out = pl.pallas_call(
        _mm_body,
        grid=(m // tm,),
        in_specs=[pl.BlockSpec((tm, k), lambda i: (i, 0)),
                  pl.BlockSpec((k, np_), lambda i: (0, 0))],
        out_specs=pl.BlockSpec((tm, np_), lambda i: (i, 0)),
        out_shape=jax.ShapeDtypeStruct((m, np_), F32),
        compiler_params=_params(("arbitrary",)),
        name="row_matmul",
    )(x, wb)
    return out[:, :n] if n_pad else out


def _proj(x, w):
    lead = x.shape[:-1]
    return _matmul(x.reshape(-1, x.shape[-1]), w).reshape(lead + (w.shape[1],))


def _even_weights(w_in, w_uq, w_uk, w_uv):
    c_q, c_kv, kr, g_a, x_b, g_b, q_m, g_m = _split_cols(w_in, EVEN_SPLITS)
    x1, x2 = kr[:, :ROPE_HALF], kr[:, ROPE_HALF:]

    def lane_pad(a):
        return jnp.pad(a, ((0, 0), (0, LANES - a.shape[1])))

    w_in_r = jnp.concatenate(
        [g_a, x_b, g_b, q_m, g_m, c_kv, c_q, jnp.tile(x1, (1, H_A)), jnp.tile(x2, (1, H_A)),
         lane_pad(kr), lane_pad(jnp.concatenate([x2, x1], axis=1))], axis=1).astype(BF16)
    r = w_uq.reshape(Q_LORA, H_A, QK_NOPE + QK_ROPE)
    nope = jnp.pad(r[:, :, :QK_NOPE], ((0, 0), (0, 0), (0, LANES - QK_NOPE))).reshape(Q_LORA, UQ_NOPE_W)
    r1 = r[:, :, QK_NOPE:QK_NOPE + ROPE_HALF].reshape(Q_LORA, LANES)
    r2 = r[:, :, QK_NOPE + ROPE_HALF:].reshape(Q_LORA, LANES)
    w_uq_r = jnp.concatenate([nope, r1, r2], axis=1).astype(BF16)
    w_uk_r = jnp.pad(jnp.transpose(w_uk, (1, 2, 0)), ((0, 0), (0, LANES - QK_NOPE), (0, 0))).astype(BF16)
    eye = jnp.eye(H_A, dtype=w_uv.dtype)
    w_uv_bd = jnp.einsum('chv,hg->hcgv', w_uv, eye).reshape(H_A * KV_LORA, W_A).astype(BF16)
    return w_in_r, w_uq_r, w_uk_r, w_uv_bd


def _rope_tables(pos):
    inv = ROPE_THETA ** (-jnp.arange(ROPE_HALF, dtype=F32) / ROPE_HALF)
    ang = pos.astype(F32)[:, None] * inv[None, :]
    cos, sin = jnp.cos(ang), jnp.sin(ang)
    zpad = jnp.zeros((pos.shape[0], LANES - QK_ROPE), F32)
    return (jnp.tile(cos, (1, H_A)), jnp.tile(sin, (1, H_A)),
            jnp.concatenate([cos, cos, zpad], axis=1), jnp.concatenate([-sin, sin, zpad], axis=1))


def _rms(x, g):
    return x * lax.rsqrt(jnp.mean(x * x, axis=-1, keepdims=True) + NORM_EPS) * g


def _even_in_body(x_ref, w_ref, qn_ref, kvn_ref, wuq_ref, wuk_ref, cos_ref, sin_ref, cosn_ref, sinn_ref,
                  zg_ref, lat_ref, kr_ref, kp_ref, qp_ref):
    z = jnp.dot(x_ref[...].astype(BF16), w_ref[...], preferred_element_type=F32)
    zg_ref[...] = z[:, :ZE_GATES]
    lat = _rms(z[:, ZE_CKV:ZE_CQ], kvn_ref[...])
    lat_ref[...] = lat
    cos, sin = cos_ref[...], sin_ref[...]
    kr1, kr2 = z[:, ZE_KR1:ZE_KR2], z[:, ZE_KR2:ZE_KRN]
    kp_ref[...] = jnp.concatenate([lat, kr1 * cos - kr2 * sin, kr1 * sin + kr2 * cos], axis=1).astype(BF16)
    kr_nat = z[:, ZE_KRN:ZE_KRS] * cosn_ref[...] + z[:, ZE_KRS:ZE_W] * sinn_ref[...]
    kr_ref[...] = kr_nat[:, :QK_ROPE]
    q = jnp.dot(_rms(z[:, ZE_CQ:ZE_KR1], qn_ref[...]).astype(BF16), wuq_ref[...],
                preferred_element_type=F32)
    q1, q2 = q[:, UQ_NOPE_W:UQ_NOPE_W + LANES], q[:, UQ_NOPE_W + LANES:]
    o1, o2 = q1 * cos - q2 * sin, q1 * sin + q2 * cos
    lane_head = lax.shift_right_logical(lax.broadcasted_iota(jnp.int32, o1.shape, 1), ROPE_HALF.bit_length() - 1)
    for h in range(H_A):
        ql = jnp.dot(q[:, h * LANES:(h + 1) * LANES].astype(BF16), wuk_ref[h], preferred_element_type=F32)
        own = lane_head == h
        qh = jnp.concatenate([ql, jnp.where(own, o1, 0.0), jnp.where(own, o2, 0.0)], axis=1)
        qp_ref[h] = (qh * (MLA_SCALE * LOG2E)).astype(BF16)


def _even_in(x2d, tables, w_in_r, q_norm, kv_norm, w_uq_r, w_uk_r):
    n = x2d.shape[0]
    tm = min(256, n)
    period = tables[0].shape[0] // tm
    row = lambda i: (i, 0)
    fixed2 = lambda i: (0, 0)
    tab = lambda i: (i % period, 0)
    return pl.pallas_call(
        _even_in_body,
        grid=(n // tm,),
        in_specs=[pl.BlockSpec((tm, D_MODEL), row),
                  pl.BlockSpec((D_MODEL, ZE_W), fixed2),
                  pl.BlockSpec((1, Q_LORA), fixed2),
                  pl.BlockSpec((1, KV_LORA), fixed2),
                  pl.BlockSpec((Q_LORA, UQ_NOPE_W + 2 * LANES), fixed2),
                  pl.BlockSpec((H_A, LANES, KV_LORA), lambda i: (0, 0, 0)),
                  pl.BlockSpec((tm, LANES), tab), pl.BlockSpec((tm, LANES), tab),
                  pl.BlockSpec((tm, LANES), tab), pl.BlockSpec((tm, LANES), tab)],
        out_specs=[pl.BlockSpec((tm, ZE_GATES), row),
                   pl.BlockSpec((tm, KV_LORA), row),
                   pl.BlockSpec((tm, QK_ROPE), row),
                   pl.BlockSpec((tm, QP_W), row),
                   pl.BlockSpec((H_A, tm, QP_W), lambda i: (0, i, 0))],
        out_shape=[jax.ShapeDtypeStruct((n, ZE_GATES), F32),
                   jax.ShapeDtypeStruct((n, KV_LORA), F32),
                   jax.ShapeDtypeStruct((n, QK_ROPE), F32),
                   jax.ShapeDtypeStruct((n, QP_W), BF16),
                   jax.ShapeDtypeStruct((H_A, n, QP_W), BF16)],
        compiler_params=_params(("arbitrary",)),
        name="even_in_proj",
    )(x2d, w_in_r, q_norm.reshape(1, -1), kv_norm.reshape(1, -1), w_uq_r, w_uk_r, *tables)


def _softmax_update(s, vals, m_sc, l_sc, acc_sc):
    m_prev = m_sc[...]
    m_new = jnp.maximum(m_prev, jnp.max(s, axis=-1, keepdims=True))
    alpha = jnp.exp2(m_prev - m_new)
    p = jnp.exp2(s - m_new)
    l_sc[...] = alpha * l_sc[...] + jnp.sum(p, axis=-1, keepdims=True)
    acc_sc[...] = alpha * acc_sc[...] + jnp.dot(p.astype(BF16), vals, preferred_element_type=F32)
    m_sc[...] = m_new


def _softmax_init(m_sc, l_sc, acc_sc):
    m_sc[...] = jnp.full(m_sc.shape, -jnp.inf, F32)
    l_sc[...] = jnp.zeros(l_sc.shape, F32)
    acc_sc[...] = jnp.zeros(acc_sc.shape, F32)


FLASH_TQ = 256
FLASH_TK = 512
FLAG_FIRST, FLAG_LAST, FLAG_DIAG = 1, 2, 4


def _flash_body(qb, kb, qo, ko, fl, q_ref, k_ref, wuv_ref, o_ref, m_sc, l_sc, acc_sc, *, tq, tk):
    i = pl.program_id(0)
    flags = fl[i]

    @pl.when((flags & FLAG_FIRST) != 0)
    def _():
        _softmax_init(m_sc, l_sc, acc_sc)

    k = k_ref[...]
    s = lax.dot_general(q_ref[...].reshape(H_A * tq, QP_W), k, NT_DIMS, preferred_element_type=F32)
    vals = k[:, :KV_LORA]

    @pl.when((flags & FLAG_DIAG) != 0)
    def _():
        qpos = (lax.broadcasted_iota(jnp.int32, s.shape, 0) & (tq - 1)) + qo[i]
        kpos = lax.broadcasted_iota(jnp.int32, s.shape, 1) + ko[i]
        _softmax_update(jnp.where(kpos <= qpos, s, -jnp.inf), vals, m_sc, l_sc, acc_sc)

    @pl.when((flags & FLAG_DIAG) == 0)
    def _():
        _softmax_update(s, vals, m_sc, l_sc, acc_sc)

    @pl.when((flags & FLAG_LAST) != 0)
    def _():
        o = acc_sc[...] / l_sc[...]
        o_all = jnp.concatenate([o[h * tq:(h + 1) * tq] for h in range(H_A)], axis=1).astype(BF16)
        o_ref[...] = jnp.dot(o_all, wuv_ref[...], preferred_element_type=F32)


def _flash_steps(batch, seq, tq, tk):
    nq, nk = seq // tq, seq // tk
    qb, kb, qo, ko, fl = [], [], [], [], []
    for b in range(batch):
        for qi in range(nq):
            last = ((qi + 1) * tq - 1) // tk
            for kj in range(last + 1):
                qb.append(b * nq + qi)
                kb.append(b * nk + kj)
                qo.append(qi * tq)
                ko.append(kj * tk)
                diag = (kj + 1) * tk - 1 > qi * tq
                fl.append((FLAG_FIRST if kj == 0 else 0) | (FLAG_LAST if kj == last else 0)
                          | (FLAG_DIAG if diag else 0))
    return [np.asarray(a, np.int32) for a in (qb, kb, qo, ko, fl)]


def _mla_prompt(qp, kp, w_uv_bd, batch, seq):
    tq, tk = min(FLASH_TQ, seq), min(FLASH_TK, seq)
    assert tq & (tq - 1) == 0 and seq % tq == 0 and seq % tk == 0
    steps = _flash_steps(batch, seq, tq, tk)
    n = batch * seq
    rows = H_A * tq
    grid_spec = pltpu.PrefetchScalarGridSpec(
        num_scalar_prefetch=5,
        grid=(steps[0].shape[0],),
        in_specs=[pl.BlockSpec((H_A, tq, QP_W), lambda i, qb, kb, qo, ko, fl: (0, qb[i], 0)),
                  pl.BlockSpec((tk, QP_W), lambda i, qb, kb, qo, ko, fl: (kb[i], 0)),
                  pl.BlockSpec((H_A * KV_LORA, W_A), lambda i, qb, kb, qo, ko, fl: (0, 0))],
        out_specs=pl.BlockSpec((tq, W_A), lambda i, qb, kb, qo, ko, fl: (qb[i], 0)),
        scratch_shapes=[pltpu.VMEM((rows, 1), F32), pltpu.VMEM((rows, 1), F32),
                        pltpu.VMEM((rows, KV_LORA), F32)])
    return pl.pallas_call(
        functools.partial(_flash_body, tq=tq, tk=tk),
        grid_spec=grid_spec,
        out_shape=jax.ShapeDtypeStruct((n, W_A), F32),
        compiler_params=_params(("arbitrary",)),
        name="mla_prompt_flash",
    )(*[jnp.asarray(a) for a in steps], qp, kp, w_uv_bd)


PAGES_PER_STEP = 64


def _page_copies(pt_ref, lat_hbm, kr_hbm, lat_buf, kr_buf, lat_sem, kr_sem, e, step, slot, i, npg):
    page = pt_ref[step * npg + i]
    return (pltpu.make_async_copy(lat_hbm.at[e, page], lat_buf.at[slot, i], lat_sem.at[slot]),
            pltpu.make_async_copy(kr_hbm.at[e, page], kr_buf.at[slot, i], kr_sem.at[slot]))


def _paged_body(pt_ref, ql_ref, qr_ref, kself_ref, krself_ref, lat_hbm, kr_hbm, o_ref,
                lat_buf, kr_buf, lat_sem, kr_sem, lat_sc, kr_sc, m_sc, l_sc, acc_sc, *, npg, t_new, e):
    j = pl.program_id(1)
    chunks = pl.num_programs(1)
    step = pl.program_id(0) * chunks + j
    last_step = pl.num_programs(0) * chunks - 1
    slot = step & 1
    copies = functools.partial(_page_copies, pt_ref, lat_hbm, kr_hbm, lat_buf, kr_buf, lat_sem, kr_sem, e)

    @pl.when(step == 0)
    def _():
        for i in range(npg):
            for cp in copies(0, 0, i, npg):
                cp.start()

    @pl.when(j == 0)
    def _():
        _softmax_init(m_sc, l_sc, acc_sc)
        kr_sc[QK_ROPE:, :] = jnp.zeros((LANES - QK_ROPE, kr_sc.shape[1]), BF16)

    nxt = jnp.minimum(step + 1, last_step)
    for i in range(npg):
        for cp in copies(nxt, 1 - slot, i, npg):
            cp.start()
    for i in range(npg):
        for cp in copies(step, slot, i, npg):
            cp.wait()

    for i in range(npg):
        lat_sc[i * PAGE_SIZE:(i + 1) * PAGE_SIZE, :] = lat_buf[slot, i].astype(BF16)
        kr_sc[:QK_ROPE, i * PAGE_SIZE:(i + 1) * PAGE_SIZE] = kr_buf[slot, i].astype(BF16)

    ql, qr = ql_ref[0], qr_ref[0]

    def attend(lat, kr_t, mask):
        s = (lax.dot_general(ql, lat, NT_DIMS, preferred_element_type=F32)
             + jnp.dot(qr, kr_t, preferred_element_type=F32))
        if mask is not None:
            s = jnp.where(mask, s, -jnp.inf)
        _softmax_update(s, lat, m_sc, l_sc, acc_sc)

    attend(lat_sc[...], kr_sc[...], None)

    @pl.when(j == chunks - 1)
    def _():
        shape = (ql.shape[0], PAGE_SIZE)
        t_row = lax.broadcasted_iota(jnp.int32, shape, 0) & (t_new - 1)
        attend(kself_ref[0], krself_ref[0], lax.broadcasted_iota(jnp.int32, shape, 1) <= t_row)
        o_ref[0] = acc_sc[...] / l_sc[...]

    @pl.when(step == last_step)
    def _():
        for i in range(npg):
            for cp in copies(last_step, 1 - slot, i, npg):
                cp.wait()


def _mla_sample(ql, qr, kself, krself, lat_pool, rope_pool_t, page_table, e, t_new):
    batch, rows, _ = ql.shape
    n_pages = page_table.shape[1]
    npg = min(PAGES_PER_STEP, n_pages)
    assert n_pages % npg == 0 and t_new & (t_new - 1) == 0 and t_new <= PAGE_SIZE
    chunks = n_pages // npg
    per_b = lambda b, j, pt: (b, 0, 0)
    grid_spec = pltpu.PrefetchScalarGridSpec(
        num_scalar_prefetch=1,
        grid=(batch, chunks),
        in_specs=[pl.BlockSpec((1, rows, KV_LORA), per_b), pl.BlockSpec((1, rows, LANES), per_b),
                  pl.BlockSpec((1, PAGE_SIZE, KV_LORA), per_b), pl.BlockSpec((1, LANES, PAGE_SIZE), per_b),
                  pl.BlockSpec(memory_space=pl.ANY), pl.BlockSpec(memory_space=pl.ANY)],
        out_specs=pl.BlockSpec((1, rows, KV_LORA), per_b),
        scratch_shapes=[pltpu.VMEM((2, npg, PAGE_SIZE, KV_LORA), F32),
                        pltpu.VMEM((2, npg, QK_ROPE, PAGE_SIZE), F32),
                        pltpu.SemaphoreType.DMA((2,)), pltpu.SemaphoreType.DMA((2,)),
                        pltpu.VMEM((npg * PAGE_SIZE, KV_LORA), BF16),
                        pltpu.VMEM((LANES, npg * PAGE_SIZE), BF16),
                        pltpu.VMEM((rows, 1), F32), pltpu.VMEM((rows, 1), F32),
                        pltpu.VMEM((rows, KV_LORA), F32)])
    return pl.pallas_call(
        functools.partial(_paged_body, npg=npg, t_new=t_new, e=e),
        grid_spec=grid_spec,
        out_shape=jax.ShapeDtypeStruct((batch, rows, KV_LORA), F32),
        compiler_params=_params(("arbitrary", "arbitrary")),
        name="mla_sample_paged",
    )(page_table.reshape(-1), ql, qr, kself, krself, lat_pool, rope_pool_t)


LRU_TC = 256


def _block_diag(w):
    nb, d, e = w.shape
    return jnp.einsum('nde,nm->ndme', w, jnp.eye(nb, dtype=w.dtype)).reshape(nb * d, nb * e)


def _lru_coeffs(xc, wa, wx, ba, bx, lam):
    xb = xc.astype(BF16)
    r = jax.nn.sigmoid(jnp.dot(xb, wa, preferred_element_type=F32) + ba)
    ig = jax.nn.sigmoid(jnp.dot(xb, wx, preferred_element_type=F32) + bx)
    neg = -lam
    softplus = jnp.maximum(neg, 0.0) + jnp.log1p(jnp.exp(-jnp.abs(neg)))
    log_a = -LRU_C * r * softplus
    a = jnp.exp(log_a)
    t = jnp.tanh(log_a)
    b = jnp.sqrt(-2.0 * t / (1.0 - t)) * (ig * xc)
    return a, b


def _lru_body(x_ref, h0_ref, buf_ref, cw_ref, cb_ref, wa_ref, wx_ref, ba_ref, bx_ref, lam_ref,
              h_ref, hl_ref, tail_ref, xbuf, hc, *, tc):
    c = pl.program_id(1)

    @pl.when(c == 0)
    def _():
        xbuf[0:SUBLANES] = buf_ref[0]
        hc[...] = h0_ref[0]

    x = x_ref[...]
    xbuf[SUBLANES:SUBLANES + tc] = x
    cw = cw_ref[...]
    xc = cb_ref[...] + x * cw[CONV_W - 1:CONV_W]
    for j in range(CONV_W - 1):
        xc = xc + xbuf[pl.ds(SUBLANES - (CONV_W - 1) + j, tc), :] * cw[j:j + 1]
    xbuf[0:SUBLANES] = x[tc - SUBLANES:tc]
    a, b = _lru_coeffs(xc, wa_ref[...], wx_ref[...], ba_ref[...], bx_ref[...], lam_ref[...])
    row = lax.broadcasted_iota(jnp.int32, a.shape, 0)
    d = 1
    while d < tc:
        keep = row >= d
        a_sh = jnp.where(keep, pltpu.roll(a, d, 0), 1.0)
        b_sh = jnp.where(keep, pltpu.roll(b, d, 0), 0.0)
        b = a * b_sh + b
        a = a * a_sh
        d *= 2
    h = a * hc[...] + b
    h_ref[...] = h
    hc[...] = h[tc - 1:tc]

    @pl.when(c == pl.num_programs(1) - 1)
    def _():
        hl_ref[0] = h[tc - 1:tc]
        tail_ref[0] = x[tc - SUBLANES:tc]


def _rglru_seq(zg, batch, seq, h0, buf, conv_w, conv_b, wa_bd, wx_bd, ba, bx, lam):
    tc = min(LRU_TC, seq)
    assert seq % tc == 0 and tc >= SUBLANES
    nc = seq // tc
    buf8 = jnp.pad(buf, ((0, 0), (SUBLANES - (CONV_W - 1), 0), (0, 0)))
    vec = lambda a: a.reshape(1, W_B)
    fixed = lambda b, c: (0, 0)
    per_b = lambda b, c: (b, 0, 0)
    h, hl, tail = pl.pallas_call(
        functools.partial(_lru_body, tc=tc),
        grid=(batch, nc),
        in_specs=[pl.BlockSpec((tc, W_B), lambda b, c: (b * nc + c, ZE_XB // W_B)),
                  pl.BlockSpec((1, 1, W_B), per_b),
                  pl.BlockSpec((1, SUBLANES, W_B), per_b),
                  pl.BlockSpec((CONV_W, W_B), fixed),
                  pl.BlockSpec((1, W_B), fixed),
                  pl.BlockSpec((W_B, W_B), fixed), pl.BlockSpec((W_B, W_B), fixed),
                  pl.BlockSpec((1, W_B), fixed), pl.BlockSpec((1, W_B), fixed), pl.BlockSpec((1, W_B), fixed)],
        out_specs=[pl.BlockSpec((tc, W_B), lambda b, c: (b * nc + c, 0)),
                   pl.BlockSpec((1, 1, W_B), per_b),
                   pl.BlockSpec((1, SUBLANES, W_B), per_b)],
        out_shape=[jax.ShapeDtypeStruct((batch * seq, W_B), F32),
                   jax.ShapeDtypeStruct((batch, 1, W_B), F32),
                   jax.ShapeDtypeStruct((batch, SUBLANES, W_B), F32)],
        scratch_shapes=[pltpu.VMEM((SUBLANES + tc, W_B), F32), pltpu.VMEM((1, W_B), F32)],
        compiler_params=_params(("arbitrary", "arbitrary")),
        name="rglru_seq",
    )(zg, h0.reshape(batch, 1, W_B), buf8, conv_w, vec(conv_b), wa_bd, wx_bd, vec(ba), vec(bx), vec(lam))
    return h, hl[:, 0], tail[:, SUBLANES - (CONV_W - 1):]


def _out_body(v1_ref, v2_ref, v3_ref, g1_ref, g2_ref, g3_ref, x_ref, w_ref, lg_ref, lb_ref, o_ref):
    def gated(v_ref, g_ref):
        g = g_ref[...]
        return (v_ref[...] * (g * jax.nn.sigmoid(g))).astype(BF16)

    mixed = jnp.concatenate([gated(v1_ref, g1_ref), gated(v2_ref, g2_ref), gated(v3_ref, g3_ref)], axis=1)
    u = DEEPNORM_ALPHA * x_ref[...] + jnp.dot(mixed, w_ref[...], preferred_element_type=F32)
    mu = jnp.mean(u, axis=-1, keepdims=True)
    var = jnp.mean(jnp.square(u - mu), axis=-1, keepdims=True)
    o_ref[...] = (u - mu) * lax.rsqrt(var + NORM_EPS) * lg_ref[...] + lb_ref[...]


def _out_proj_norm(vals, z, gate_cols, x2d, w_out, ln_g, ln_b):
    n = x2d.shape[0]
    tm = min(256, n)
    widths = [v.shape[1] for v in vals]
    row = lambda i: (i, 0)
    fixed = lambda i: (0, 0)
    in_specs = [pl.BlockSpec((tm, w), row) for w in widths]
    for w, off in zip(widths, gate_cols):
        assert off % w == 0
        in_specs.append(pl.BlockSpec((tm, w), functools.partial(lambda i, cb: (i, cb), cb=off // w)))
    in_specs += [pl.BlockSpec((tm, D_MODEL), row),
                 pl.BlockSpec((sum(widths), D_MODEL), fixed),
                 pl.BlockSpec((1, D_MODEL), fixed), pl.BlockSpec((1, D_MODEL), fixed)]
    return pl.pallas_call(
        _out_body,
        grid=(n // tm,),
        in_specs=in_specs,
        out_specs=pl.BlockSpec((tm, D_MODEL), row),
        out_shape=jax.ShapeDtypeStruct((n, D_MODEL), F32),
        compiler_params=_params(("arbitrary",)),
        name="out_proj_norm",
    )(*vals, z, z, z, x2d, w_out.astype(BF16), ln_g.reshape(1, -1), ln_b.reshape(1, -1))


def _mem_attend_rows(q, k_t, v_t):
    kb, vb = k_t.astype(BF16), v_t.astype(BF16)
    q_head = lax.shift_right_logical(lax.broadcasted_iota(jnp.int32, q.shape, 1), DH_M.bit_length() - 1)
    out = jnp.zeros(q.shape, F32)
    for h in range(H_M):
        own = q_head == h
        s = jnp.dot(jnp.where(own, q, 0.0).astype(BF16), kb, preferred_element_type=F32) * (DH_M ** -0.5)
        p = jnp.exp(s - jnp.max(s, axis=-1, keepdims=True))
        p = p / jnp.sum(p, axis=-1, keepdims=True)
        y = lax.dot_general(p.astype(BF16), vb, NT_DIMS, preferred_element_type=F32)
        out = out + jnp.where(own, y, 0.0)
    return out


def _mem_body(q_ref, k_ref, v_ref, o_ref):
    for i in range(k_ref.shape[1]):
        q = q_ref[i] if len(q_ref.shape) == 3 else q_ref[...]
        y = _mem_attend_rows(q, k_ref[0, i], v_ref[0, i])
        if len(o_ref.shape) == 3:
            o_ref[i] = y
        else:
            o_ref[...] = y


def _mem_transposed(mem):
    t = jnp.moveaxis(mem, -3, -1)
    t = t.reshape(t.shape[:-3] + (W_M, N_MEM))
    return t if t.ndim == 4 else t[None]


def _mem_attend_long(z, col, batch, seq, mem_kt, mem_vt, layer):
    tm = min(512, seq)
    nt = seq // tm
    kv = lambda b, i: (layer, b, 0, 0)
    return pl.pallas_call(
        _mem_body,
        grid=(batch, nt),
        in_specs=[pl.BlockSpec((tm, W_M), lambda b, i: (b * nt + i, col // W_M)),
                  pl.BlockSpec((1, 1, W_M, N_MEM), kv), pl.BlockSpec((1, 1, W_M, N_MEM), kv)],
        out_specs=pl.BlockSpec((tm, W_M), lambda b, i: (b * nt + i, 0)),
        out_shape=jax.ShapeDtypeStruct((batch * seq, W_M), F32),
        compiler_params=_params(("arbitrary", "arbitrary")),
        name="mem_attend_long",
    )(z, mem_kt, mem_vt)


MEM_SHORT_NB = 8


def _mem_attend_short(z, col, batch, seq, mem_kt, mem_vt, layer):
    assert seq <= SUBLANES
    nb = min(MEM_SHORT_NB, batch)
    q = jnp.pad(z[:, col:col + W_M].reshape(batch, seq, W_M), ((0, 0), (0, SUBLANES - seq), (0, 0)))
    blk = lambda i: (i, 0, 0)
    kv = lambda i: (layer, i, 0, 0)
    y = pl.pallas_call(
        _mem_body,
        grid=(batch // nb,),
        in_specs=[pl.BlockSpec((nb, SUBLANES, W_M), blk),
                  pl.BlockSpec((1, nb, W_M, N_MEM), kv), pl.BlockSpec((1, nb, W_M, N_MEM), kv)],
        out_specs=pl.BlockSpec((nb, SUBLANES, W_M), blk),
        out_shape=jax.ShapeDtypeStruct((batch, SUBLANES, W_M), F32),
        compiler_params=_params(("arbitrary",)),
        name="mem_attend_short",
    )(q, mem_kt, mem_vt)
    return y[:, :seq].reshape(batch * seq, W_M)


def _chunk_mlp_body(u_ref, v_ref, g_ref, b_ref, w_ref, bias_ref, y_ref, vn_ref, *, rows, chunks):
    gw = W_D // G_D
    for c in range(chunks):
        sl = pl.ds(c * rows, rows)
        v = v_ref[sl, :]
        mu = jnp.mean(v, axis=-1, keepdims=True)
        var = jnp.mean(jnp.square(v - mu), axis=-1, keepdims=True)
        vn = (v - mu) * lax.rsqrt(var + NORM_EPS) * g_ref[...] + b_ref[...]
        vn_ref[sl, :] = vn
        vb = vn.astype(BF16)
        s = jnp.concatenate([jnp.dot(w_ref[g], vb[:, g * gw:(g + 1) * gw], preferred_element_type=F32)
                             for g in range(G_D)], axis=1)
        y_ref[sl, :] = u_ref[sl, :] * (s + bias_ref[...])


def _chunk_mlp_call(z, rows, ln_g_d, ln_b_d, w_mix, bias):
    n = z.shape[0]
    chunks = max(1, min(4, n // rows))
    tm = rows * chunks
    row = lambda i: (i, 0)
    fixed = lambda i: (0, 0)
    return pl.pallas_call(
        functools.partial(_chunk_mlp_body, rows=rows, chunks=chunks),
        grid=(n // tm,),
        in_specs=[pl.BlockSpec((tm, W_D), lambda i: (i, ZO_U // W_D)),
                  pl.BlockSpec((tm, W_D), lambda i: (i, ZO_VD // W_D)),
                  pl.BlockSpec((1, W_D), fixed), pl.BlockSpec((1, W_D), fixed),
                  pl.BlockSpec((G_D, rows, rows), lambda i: (0, 0, 0)),
                  pl.BlockSpec((rows, W_D), fixed)],
        out_specs=[pl.BlockSpec((tm, W_D), row), pl.BlockSpec((tm, W_D), row)],
        out_shape=[jax.ShapeDtypeStruct((n, W_D), F32), jax.ShapeDtypeStruct((n, W_D), F32)],
        compiler_params=_params(("arbitrary",)),
        name="chunk_mlp",
    )(z, z, ln_g_d.reshape(1, -1), ln_b_d.reshape(1, -1), w_mix, bias)


def _chunk_mlp_weights(sg_w, sg_b):
    L = CHUNK_D
    w = jnp.where(jnp.tril(jnp.ones((L, L), dtype=bool)), sg_w[:, :L, :L], 0.0)
    return w.astype(BF16), jnp.repeat(sg_b[:, :L].T, W_D // G_D, axis=1)


def _chunk_mlp_short_body(u_ref, v_ref, g_ref, b_ref, w_ref, bias_ref, y_ref, vn_ref, *, batch, seq):
    v = v_ref[...]
    mu = jnp.mean(v, axis=-1, keepdims=True)
    var = jnp.mean(jnp.square(v - mu), axis=-1, keepdims=True)
    vn = (v - mu) * lax.rsqrt(var + NORM_EPS) * g_ref[...] + b_ref[...]
    vn_ref[...] = vn
    for t in range(seq):
        acc = vn[0:batch] * w_ref[t * seq:t * seq + 1, :]
        for s in range(1, t + 1):
            acc = acc + vn[s * batch:(s + 1) * batch] * w_ref[t * seq + s:t * seq + s + 1, :]
        rows = slice(t * batch, (t + 1) * batch)
        y_ref[rows, :] = u_ref[rows, :] * (acc + bias_ref[t:t + 1, :])


def _chunk_mlp_short(z, batch, seq, ln_g_d, ln_b_d, sg_w, sg_b):
    gw = W_D // G_D
    w_rows = jnp.repeat(jnp.transpose(sg_w[:, :seq, :seq], (1, 2, 0)).reshape(seq * seq, G_D), gw, axis=1)
    bias = jnp.repeat(sg_b[:, :seq].T, gw, axis=1)

    def time_major(off):
        return jnp.swapaxes(z[:, off:off + W_D].reshape(batch, seq, W_D), 0, 1).reshape(seq * batch, W_D)

    y, vn = pl.pallas_call(
        functools.partial(_chunk_mlp_short_body, batch=batch, seq=seq),
        out_shape=[jax.ShapeDtypeStruct((seq * batch, W_D), F32)] * 2,
        compiler_params=pltpu.CompilerParams(vmem_limit_bytes=VMEM_LIMIT),
        name="chunk_mlp_short",
    )(time_major(ZO_U), time_major(ZO_VD), ln_g_d.reshape(1, -1), ln_b_d.reshape(1, -1), w_rows, bias)
    back = lambda a: jnp.swapaxes(a.reshape(seq, batch, W_D), 0, 1).reshape(batch * seq, W_D)
    return back(y), back(vn)


def _cumsum_rows(x):
    row = lax.broadcasted_iota(jnp.int32, x.shape, 0)
    d = 1
    while d < x.shape[0]:
        x = x + jnp.where(row >= d, pltpu.roll(x, d, 0), 0.0)
        d *= 2
    return x


def _mlstm_body(q_ref, k_ref, v_ref, o_ref, gt_ref, bif_ref, c0_ref, n0_ref, m0_ref,
                y_ref, c_ref, n_ref, m_ref, c_sc, n_sc, m_sc, *stage, nb, n_valid):
    L = MLSTM_CHUNK
    step = pl.program_id(1)

    @pl.when(step == 0)
    def _():
        c_sc[...] = c0_ref[...]
        n_sc[...] = n0_ref[...]
        m_sc[...] = m0_ref[...]
        for st in stage:
            st[...] = jnp.zeros(st.shape, F32)

    row = lax.broadcasted_iota(jnp.int32, (L, LANES), 0)
    lane = lax.broadcasted_iota(jnp.int32, (L, LANES), 1)
    tril = lax.broadcasted_iota(jnp.int32, (L, L), 0) >= lax.broadcasted_iota(jnp.int32, (L, L), 1)
    for b in range(nb):
        if stage:
            rows_in = q_ref.shape[1]
            bufs = []
            for st, ref in zip(stage, (q_ref, k_ref, v_ref, o_ref, gt_ref)):
                st[b, 0:rows_in, :] = ref[b]
                bufs.append(st[b])
            q, k, v, o_pre, gts = bufs
        else:
            q, k, v, o_pre, gts = q_ref[b], k_ref[b], v_ref[b], o_ref[b], gt_ref[b]
        gz = gts + bif_ref[...]
        lf = jnp.minimum(gz, 0.0) - jnp.log1p(jnp.exp(-jnp.abs(gz)))
        ig = gz
        if n_valid < L:
            ig = jnp.where(row < n_valid, ig, -jnp.inf)
            lf = jnp.where(row < n_valid, lf, 0.0)
        a = jnp.where(lane < H_C, ig, _cumsum_rows(lf))
        a_t = a.T
        ys = []
        for h in range(H_C):
            hs = slice(h * DH_C, (h + 1) * DH_C)
            ig_row, bc_row = a_t[h:h + 1, :], a_t[H_C + h:H_C + h + 1, :]
            ig_col, bc_col = a[:, h:h + 1], a[:, H_C + h:H_C + h + 1]
            m_old = m_sc[b, h][:, 0:1]
            dmat = jnp.where(tril, bc_col - bc_row + ig_row, -jnp.inf)
            inter = bc_col + m_old
            mt = jnp.maximum(inter, jnp.max(dmat, axis=-1, keepdims=True))
            w = jnp.exp(dmat - mt)
            qh = q[:, hs].astype(BF16)
            kh = k[:, hs] * (DH_C ** -0.5)
            vh = v[:, hs].astype(BF16)
            sc = lax.dot_general(qh, kh.astype(BF16), NT_DIMS, preferred_element_type=F32) * w
            g_inter = jnp.exp(inter - mt)
            c_old = c_sc[b, h]
            n_old = n_sc[b, h]
            num = (jnp.dot(sc.astype(BF16), vh, preferred_element_type=F32)
                   + g_inter * jnp.dot(qh, c_old.astype(BF16), preferred_element_type=F32))
            den = (jnp.sum(sc, axis=-1, keepdims=True)
                   + g_inter * jnp.sum(q[:, hs] * n_old, axis=-1, keepdims=True))
            hh = num / jnp.maximum(jnp.abs(den), jnp.exp(-mt))
            ys.append(jax.nn.sigmoid(o_pre[:, hs]) * hh)
            b_end = bc_col[L - 1:L, :]
            g_col = b_end - bc_col + ig_col
            m_new = jnp.maximum(b_end + m_old, jnp.max(g_col, axis=0, keepdims=True))
            decay = jnp.exp(b_end + m_old - m_new)
            kw = kh * jnp.exp(g_col - m_new)
            c_sc[b, h] = decay * c_old + jnp.dot(kw.T.astype(BF16), vh, preferred_element_type=F32)
            n_sc[b, h] = decay * n_old + jnp.sum(kw, axis=0, keepdims=True)
            m_sc[b, h] = jnp.broadcast_to(m_new, (1, LANES))
        y = jnp.concatenate(ys, axis=1)
        y_ref[b] = y[0:y_ref.shape[1]]

    @pl.when(step == pl.num_programs(1) - 1)
    def _():
        c_ref[...] = c_sc[...]
        n_ref[...] = n_sc[...]
        m_ref[...] = m_sc[...]


MLSTM_NB = 2


def _mlstm_call(z, batch, seq, b_if, c0, n0, m0):
    L = MLSTM_CHUNK
    nb = min(MLSTM_NB, batch)
    long = seq % L == 0
    if long:
        rows, nc, n_valid = L, seq // L, L
        z3 = z.reshape(batch, seq, z.shape[1])
    else:
        assert seq <= SUBLANES
        rows, nc, n_valid = SUBLANES, 1, seq
        z3 = jnp.pad(z.reshape(batch, seq, z.shape[1]), ((0, 0), (0, SUBLANES - seq), (0, 0)))
    bif = jnp.pad(b_if, (0, LANES - 2 * H_C)).reshape(1, LANES)
    m0r = jnp.broadcast_to(m0[:, :, None, None], (batch, H_C, 1, LANES))

    def col(off, w):
        return pl.BlockSpec((nb, rows, w), functools.partial(lambda g, c, cb: (g, c, cb), cb=off // w))

    st4 = lambda g, c: (g, 0, 0, 0)
    in_specs = [col(ZO_Q, W_C), col(ZO_K, W_C), col(ZO_V, W_C), col(ZO_O, W_C), col(ZO_IF, LANES),
                pl.BlockSpec((1, LANES), lambda g, c: (0, 0)),
                pl.BlockSpec((nb, H_C, DH_C, DH_C), st4),
                pl.BlockSpec((nb, H_C, 1, DH_C), st4),
                pl.BlockSpec((nb, H_C, 1, LANES), st4)]
    out_specs = [pl.BlockSpec((nb, rows, W_C), lambda g, c: (g, c, 0)),
                 pl.BlockSpec((nb, H_C, DH_C, DH_C), st4),
                 pl.BlockSpec((nb, H_C, 1, DH_C), st4),
                 pl.BlockSpec((nb, H_C, 1, LANES), st4)]
    scratch = [pltpu.VMEM((nb, H_C, DH_C, DH_C), F32), pltpu.VMEM((nb, H_C, 1, DH_C), F32),
               pltpu.VMEM((nb, H_C, 1, LANES), F32)]
    if not long:
        scratch += [pltpu.VMEM((nb, L, W_C), F32)] * 4 + [pltpu.VMEM((nb, L, LANES), F32)]
    y, c, n, m = pl.pallas_call(
        functools.partial(_mlstm_body, nb=nb, n_valid=n_valid),
        grid=(batch // nb, nc),
        in_specs=in_specs,
        out_specs=out_specs,
        out_shape=[jax.ShapeDtypeStruct((batch, rows * nc, W_C), F32),
                   jax.ShapeDtypeStruct((batch, H_C, DH_C, DH_C), F32),
                   jax.ShapeDtypeStruct((batch, H_C, 1, DH_C), F32),
                   jax.ShapeDtypeStruct((batch, H_C, 1, LANES), F32)],
        scratch_shapes=scratch,
        compiler_params=_params(("arbitrary", "arbitrary")),
        name="mlstm_chunks",
    )(z3, z3, z3, z3, z3, bif, c0, n0[:, :, None, :], m0r)
    return y[:, :seq].reshape(batch * seq, W_C), c, n[:, :, 0], m[:, :, 0, 0]


def _lru_short_body(x_ref, h0_ref, buf_ref, cw_ref, cb_ref, wa_ref, wx_ref, ba_ref, bx_ref, lam_ref,
                    h_ref, *, batch, seq):
    cw = cw_ref[...]
    xp = jnp.concatenate([buf_ref[...], x_ref[...]], axis=0)
    xc = cb_ref[...] + xp[(CONV_W - 1) * batch:] * cw[CONV_W - 1:CONV_W]
    for j in range(CONV_W - 1):
        xc = xc + xp[j * batch:(j + seq) * batch] * cw[j:j + 1]
    a, b = _lru_coeffs(xc, wa_ref[...], wx_ref[...], ba_ref[...], bx_ref[...], lam_ref[...])
    h = h0_ref[...]
    for t in range(seq):
        sl = slice(t * batch, (t + 1) * batch)
        h = a[sl] * h + b[sl]
        h_ref[sl, :] = h


def _rglru_short(zg, batch, seq, h0, buf, conv_w, conv_b, wa_bd, wx_bd, ba, bx, lam):
    assert batch % SUBLANES == 0
    x = zg[:, ZE_XB:ZE_XB + W_B].reshape(batch, seq, W_B)
    xp = jnp.concatenate([buf, x], axis=1)
    x_tm = jnp.swapaxes(x, 0, 1).reshape(seq * batch, W_B)
    buf_tm = jnp.swapaxes(buf, 0, 1).reshape((CONV_W - 1) * batch, W_B)
    vec = lambda a: a.reshape(1, W_B)
    h_tm = pl.pallas_call(
        functools.partial(_lru_short_body, batch=batch, seq=seq),
        out_shape=jax.ShapeDtypeStruct((seq * batch, W_B), F32),
        compiler_params=pltpu.CompilerParams(vmem_limit_bytes=VMEM_LIMIT),
        name="rglru_short",
    )(x_tm, h0, buf_tm, conv_w, vec(conv_b), wa_bd, wx_bd, vec(ba), vec(bx), vec(lam))
    h = jnp.swapaxes(h_tm.reshape(seq, batch, W_B), 0, 1)
    return h.reshape(batch * seq, W_B), h[:, -1], xp[:, -(CONV_W - 1):]


def _mem_kv(mem, w_mk, w_mv):
    B = mem.shape[0]
    kv = _proj(mem, jnp.concatenate([w_mk, w_mv], axis=1))
    return (kv[..., :W_M].reshape(B, N_MEM, H_M, DH_M), kv[..., W_M:].reshape(B, N_MEM, H_M, DH_M))


def _sample_queries(qp, batch, t_new):
    n = qp.shape[1]
    r1 = jnp.stack([qp[h, :, KV_LORA + h * ROPE_HALF:KV_LORA + (h + 1) * ROPE_HALF] for h in range(H_A)])
    r2 = jnp.stack([qp[h, :, KV_LORA + LANES + h * ROPE_HALF:KV_LORA + LANES + (h + 1) * ROPE_HALF]
                    for h in range(H_A)])
    qr = jnp.concatenate([r1, r2, jnp.zeros((H_A, n, LANES - QK_ROPE), qp.dtype)], axis=-1)

    def rows(a):
        w = a.shape[-1]
        return a.reshape(H_A, batch, t_new, w).transpose(1, 0, 2, 3).reshape(batch, H_A * t_new, w)

    return rows(qp[:, :, :KV_LORA]), rows(qr)


def _sample_self_keys(latent, k_rope, batch, t_new):
    kl = jnp.pad(latent.astype(BF16).reshape(batch, t_new, KV_LORA), ((0, 0), (0, PAGE_SIZE - t_new), (0, 0)))
    kr_t = jnp.swapaxes(k_rope.astype(BF16).reshape(batch, t_new, QK_ROPE), 1, 2)
    return kl, jnp.pad(kr_t, ((0, 0), (0, LANES - QK_ROPE), (0, PAGE_SIZE - t_new)))


def _even_layer(x2d, batch, seq, tables, mem, lru_h0, lru_buf, paged, e, weights,
                q_norm, kv_norm, conv_w, conv_b, wa, ba, wx, bx, lam, w_out, ln_g, ln_b):
    w_in_r, w_uq_r, w_uk_r, w_uv_bd = weights
    zg, latent, k_rope, kp, qp = _even_in(x2d, tables, w_in_r, q_norm, kv_norm, w_uq_r, w_uk_r)
    lru_args = (lru_h0, lru_buf, conv_w, conv_b, _block_diag(wa).astype(BF16), _block_diag(wx).astype(BF16),
                ba, bx, lam)
    if paged is None:
        y_a = _mla_prompt(qp, kp, w_uv_bd, batch, seq)
        h_b, h_last, new_buf = _rglru_seq(zg, batch, seq, *lru_args)
        y_m = _mem_attend_long(zg, ZE_QM, batch, seq, *mem)
    else:
        lat_pool, rope_pool, page_table = paged
        o = _mla_sample(*_sample_queries(qp, batch, seq), *_sample_self_keys(latent, k_rope, batch, seq),
                        lat_pool, jnp.swapaxes(rope_pool, 2, 3), page_table, e, seq)
        o = o.reshape(batch, H_A, seq, KV_LORA).transpose(0, 2, 1, 3).reshape(batch * seq, H_A * KV_LORA)
        y_a = _matmul(o, w_uv_bd)
        h_b, h_last, new_buf = _rglru_short(zg, batch, seq, *lru_args)
        y_m = _mem_attend_short(zg, ZE_QM, batch, seq, *mem)
    x_new = _out_proj_norm([y_a, h_b, y_m], zg, (ZE_GA, ZE_GB, ZE_GM), x2d, w_out, ln_g, ln_b)
    return (x_new, latent.reshape(batch, seq, KV_LORA), k_rope.reshape(batch, seq, QK_ROPE), h_last, new_buf)


def _odd_weights(w_in):
    q, k, v, i_pre, f_pre, o_pre, g_c, u_d, v_d, g_d, q_m, g_m = _split_cols(w_in, ODD_SPLITS)
    gates = jnp.pad(jnp.concatenate([i_pre, f_pre], axis=1), ((0, 0), (0, LANES - 2 * H_C)))
    return jnp.concatenate([q, k, v, o_pre, g_c, u_d, v_d, g_d, q_m, g_m, gates], axis=1)


def _odd_layer(x2d, batch, seq, mem, c0, n0, m0, w_in_r, b_if, ln_g_d, ln_b_d, sg_w, sg_b,
               w_out, ln_g, ln_b):
    z = _matmul(x2d, w_in_r)
    y_c, c, n, m = _mlstm_call(z, batch, seq, b_if, c0, n0, m0)
    if seq % CHUNK_D == 0:
        y_d, vn = _chunk_mlp_call(z, CHUNK_D, ln_g_d, ln_b_d, *_chunk_mlp_weights(sg_w, sg_b))
        y_m = _mem_attend_long(z, ZO_QM, batch, seq, *mem)
    else:
        y_d, vn = _chunk_mlp_short(z, batch, seq, ln_g_d, ln_b_d, sg_w, sg_b)
        y_m = _mem_attend_short(z, ZO_QM, batch, seq, *mem)
    x_new = _out_proj_norm([y_c, y_d, y_m], z, (ZO_GC, ZO_GD, ZO_GM), x2d, w_out, ln_g, ln_b)
    return x_new, vn.reshape(batch, seq, W_D), c, n, m


def kernel(x_prompt, x_sample, cache_mla_latent, cache_mla_krope, state_lru_h, state_lru_conv,
           state_mlstm_c, state_mlstm_n, state_mlstm_m, cache_mem_k, cache_mem_v, page_table,
           mem_prompt, w_in_even, mla_q_norm, mla_kv_norm, w_uq, w_uk, w_uv,
           lru_conv_w, lru_conv_b, lru_wa, lru_ba, lru_wx, lru_bx, lru_lambda, w_out_even,
           w_in_odd, mlstm_b_if, sg_ln_g, sg_ln_b, sg_w, sg_b, w_out_odd,
           w_mem_k, w_mem_v, ln_g, ln_b):
    Bp, Tp, _ = x_prompt.shape
    Bs, Ts, _ = x_sample.shape
    past_len = page_table.shape[1] * PAGE_SIZE
    tables_p = _rope_tables(jnp.arange(Tp, dtype=F32))
    tables_s = tuple(jnp.tile(t, (Bs, 1)) for t in _rope_tables(past_len + jnp.arange(Ts, dtype=F32)))

    h0_p = jnp.zeros((Bp, W_B), F32)
    buf0_p = jnp.zeros((Bp, CONV_W - 1, W_B), F32)
    c0_p = jnp.zeros((Bp, H_C, DH_C, DH_C), F32)
    n0_p = jnp.zeros((Bp, H_C, DH_C), F32)
    m0_p = jnp.zeros((Bp, H_C), F32)

    lat_p, kr_p, h_p, conv_p, c_p, n_p, m_p, mk_p, mv_p = [], [], [], [], [], [], [], [], []
    lat_s, kr_s, h_s, conv_s, c_s, n_s, m_s, v_s = [], [], [], [], [], [], [], []

    xp = x_prompt.reshape(Bp * Tp, D_MODEL)
    xs = x_sample.reshape(Bs * Ts, D_MODEL)
    mem_s = (_mem_transposed(cache_mem_k), _mem_transposed(cache_mem_v))
    for l in range(DEPTH):
        mk_l, mv_l = _mem_kv(mem_prompt, w_mem_k[l], w_mem_v[l])
        mk_p.append(mk_l)
        mv_p.append(mv_l)
        mem_p = (_mem_transposed(mk_l), _mem_transposed(mv_l), 0)
        if l % 2 == 0:
            e = l // 2
            weights = _even_weights(w_in_even[e], w_uq[e], w_uk[e], w_uv[e])
            rest = (mla_q_norm[e], mla_kv_norm[e], lru_conv_w[e], lru_conv_b[e], lru_wa[e], lru_ba[e],
                    lru_wx[e], lru_bx[e], lru_lambda[e], w_out_even[e], ln_g[l], ln_b[l])
            xp, la, kr, hl, cb = _even_layer(xp, Bp, Tp, tables_p, mem_p, h0_p, buf0_p, None, e,
                                             weights, *rest)
            lat_p.append(la); kr_p.append(kr); h_p.append(hl); conv_p.append(cb)
            xs, la, kr, hl, cb = _even_layer(xs, Bs, Ts, tables_s, mem_s + (l,),
                                             state_lru_h[e], state_lru_conv[e],
                                             (cache_mla_latent, cache_mla_krope, page_table), e, weights, *rest)
            lat_s.append(la); kr_s.append(kr); h_s.append(hl); conv_s.append(cb)
        else:
            o = l // 2
            ow = (_odd_weights(w_in_odd[o]), mlstm_b_if[o], sg_ln_g[o], sg_ln_b[o], sg_w[o], sg_b[o],
                  w_out_odd[o], ln_g[l], ln_b[l])
            xp, _, cc, nn, mm = _odd_layer(xp, Bp, Tp, mem_p, c0_p, n0_p, m0_p, *ow)
            c_p.append(cc); n_p.append(nn); m_p.append(mm)
            xs, vn, cc, nn, mm = _odd_layer(xs, Bs, Ts, mem_s + (l,),
                                            state_mlstm_c[o], state_mlstm_n[o], state_mlstm_m[o], *ow)
            c_s.append(cc); n_s.append(nn); m_s.append(mm); v_s.append(vn)

    return (xp.reshape(Bp, Tp, D_MODEL), xs.reshape(Bs, Ts, D_MODEL),
            jnp.stack(lat_p), jnp.stack(kr_p), jnp.stack(h_p), jnp.stack(conv_p),
            jnp.stack(c_p), jnp.stack(n_p), jnp.stack(m_p), jnp.stack(mk_p), jnp.stack(mv_p),
            jnp.stack(lat_s), jnp.stack(kr_s), jnp.stack(h_s), jnp.stack(conv_s),
            jnp.stack(c_s), jnp.stack(n_s), jnp.stack(m_s), jnp.stack(v_s))
```

```python
import functools

import jax
import jax.numpy as jnp
import numpy as np
from jax import lax
from jax.experimental import pallas as pl
from jax.experimental.pallas import tpu as pltpu

D_MODEL = 1024
DEPTH = 2
PAGE_SIZE = 128
H_A = 8
Q_LORA = 384
KV_LORA = 256
QK_NOPE = 64
QK_ROPE = 32
ROPE_HALF = QK_ROPE // 2
V_HEAD = 64
W_A = H_A * V_HEAD
ROPE_THETA = 10000.0
MLA_SCALE = (QK_NOPE + QK_ROPE) ** -0.5
W_B = 512
NB_B = 8
BD_B = W_B // NB_B
CONV_W = 4
LRU_C = 8.0
H_C = 4
DH_C = 128
W_C = H_C * DH_C
MLSTM_CHUNK = 128
G_D = 4
W_D = 512
CHUNK_D = 128
N_MEM = 256
H_M = 4
DH_M = 64
W_M = H_M * DH_M
NORM_EPS = 1e-6
DEEPNORM_ALPHA = (2 * DEPTH) ** 0.25

EVEN_SPLITS = (Q_LORA, KV_LORA, QK_ROPE, W_A, W_B, W_B, W_M, W_M)
ODD_SPLITS = (W_C, W_C, W_C, H_C, H_C, W_C, W_C, W_D, W_D, W_D, W_M, W_M)

F32 = jnp.float32
BF16 = jnp.bfloat16
LANES = 128
SUBLANES = 8
VMEM_LIMIT = 48 * 1024 * 1024
LOG2E = 1.4426950408889634
NT_DIMS = (((1,), (1,)), ((), ()))

ZE_GA, ZE_XB, ZE_GB, ZE_QM, ZE_GM = 0, 512, 1024, 1536, 1792
ZE_GATES = 2048
ZE_CKV = 2048
ZE_CQ = ZE_CKV + KV_LORA
ZE_KR1 = ZE_CQ + Q_LORA
ZE_KR2 = ZE_KR1 + LANES
ZE_KRN = ZE_KR2 + LANES
ZE_KRS = ZE_KRN + LANES
ZE_W = ZE_KRS + LANES
QP_W = KV_LORA + 2 * LANES
UQ_NOPE_W = H_A * LANES
ZO_Q, ZO_K, ZO_V, ZO_O, ZO_GC, ZO_U, ZO_VD, ZO_GD = (i * 512 for i in range(8))
ZO_QM, ZO_GM, ZO_IF = 4096, 4352, 4608
ZO_W = ZO_IF + LANES


def _split_cols(z, sizes):
    cuts = [int(c) for c in np.cumsum(sizes)[:-1]]
    return jnp.split(z, cuts, axis=-1)


def _params(sem):
    return pltpu.CompilerParams(dimension_semantics=sem, vmem_limit_bytes=VMEM_LIMIT)


def _mm_body(x_ref, w_ref, o_ref):
    o_ref[...] = jnp.dot(x_ref[...].astype(BF16), w_ref[...], preferred_element_type=F32)


def _row_tile(m, n):
    tm = 512
    while tm > SUBLANES and (tm * n * 4 * 2 > 10 * 1024 * 1024 or m % tm):
        tm //= 2
    return tm


def _matmul(x, w):
    m, k = x.shape
    n = w.shape[1]
    n_pad = -n % LANES
    wb = w.astype(BF16)
    if n_pad:
        wb = jnp.pad(wb, ((0, 0), (0, n_pad)))
    np_ = n + n_pad
    tm = _row_tile(m, np_)
    out = pl.pallas_call(
        _mm_body,
        grid=(m // tm,),
        in_specs=[pl.BlockSpec((tm, k), lambda i: (i, 0)),
                  pl.BlockSpec((k, np_), lambda i: (0, 0))],
        out_specs=pl.BlockSpec((tm, np_), lambda i: (i, 0)),
        out_shape=jax.ShapeDtypeStruct((m, np_), F32),
        compiler_params=_params(("arbitrary",)),
        name="row_matmul",
    )(x, wb)
    return out[:, :n] if n_pad else out


def _proj(x, w):
    lead = x.shape[:-1]
    return _matmul(x.reshape(-1, x.shape[-1]), w).reshape(lead + (w.shape[1],))


def _even_weights(w_in, w_uq, w_uk, w_uv):
    c_q, c_kv, kr, g_a, x_b, g_b, q_m, g_m = _split_cols(w_in, EVEN_SPLITS)
    x1, x2 = kr[:, :ROPE_HALF], kr[:, ROPE_HALF:]

    def lane_pad(a):
        return jnp.pad(a, ((0, 0), (0, LANES - a.shape[1])))

    w_in_r = jnp.concatenate(
        [g_a, x_b, g_b, q_m, g_m, c_kv, c_q, jnp.tile(x1, (1, H_A)), jnp.tile(x2, (1, H_A)),
         lane_pad(kr), lane_pad(jnp.concatenate([x2, x1], axis=1))], axis=1).astype(BF16)
    r = w_uq.reshape(Q_LORA, H_A, QK_NOPE + QK_ROPE)
    nope = jnp.pad(r[:, :, :QK_NOPE], ((0, 0), (0, 0), (0, LANES - QK_NOPE))).reshape(Q_LORA, UQ_NOPE_W)
    r1 = r[:, :, QK_NOPE:QK_NOPE + ROPE_HALF].reshape(Q_LORA, LANES)
    r2 = r[:, :, QK_NOPE + ROPE_HALF:].reshape(Q_LORA, LANES)
    w_uq_r = jnp.concatenate([nope, r1, r2], axis=1).astype(BF16)
    w_uk_r = jnp.pad(jnp.transpose(w_uk, (1, 2, 0)), ((0, 0), (0, LANES - QK_NOPE), (0, 0))).astype(BF16)
    eye = jnp.eye(H_A, dtype=w_uv.dtype)
    w_uv_bd = jnp.einsum('chv,hg->hcgv', w_uv, eye).reshape(H_A * KV_LORA, W_A).astype(BF16)
    return w_in_r, w_uq_r, w_uk_r, w_uv_bd


def _rope_tables(pos):
    inv = ROPE_THETA ** (-jnp.arange(ROPE_HALF, dtype=F32) / ROPE_HALF)
    ang = pos.astype(F32)[:, None] * inv[None, :]
    cos, sin = jnp.cos(ang), jnp.sin(ang)
    zpad = jnp.zeros((pos.shape[0], LANES - QK_ROPE), F32)
    return (jnp.tile(cos, (1, H_A)), jnp.tile(sin, (1, H_A)),
            jnp.concatenate([cos, cos, zpad], axis=1), jnp.concatenate([-sin, sin, zpad], axis=1))


def _rms(x, g):
    return x * lax.rsqrt(jnp.mean(x * x, axis=-1, keepdims=True) + NORM_EPS) * g


def _even_in_body(x_ref, w_ref, qn_ref, kvn_ref, wuq_ref, wuk_ref, cos_ref, sin_ref, cosn_ref, sinn_ref,
                  zg_ref, lat_ref, kr_ref, kp_ref, qp_ref):
    z = jnp.dot(x_ref[...].astype(BF16), w_ref[...], preferred_element_type=F32)
    zg_ref[...] = z[:, :ZE_GATES]
    lat = _rms(z[:, ZE_CKV:ZE_CQ], kvn_ref[...])
    lat_ref[...] = lat
    cos, sin = cos_ref[...], sin_ref[...]
    kr1, kr2 = z[:, ZE_KR1:ZE_KR2], z[:, ZE_KR2:ZE_KRN]
    kp_ref[...] = jnp.concatenate([lat, kr1 * cos - kr2 * sin, kr1 * sin + kr2 * cos], axis=1).astype(BF16)
    kr_nat = z[:, ZE_KRN:ZE_KRS] * cosn_ref[...] + z[:, ZE_KRS:ZE_W] * sinn_ref[...]
    kr_ref[...] = kr_nat[:, :QK_ROPE]
    q = jnp.dot(_rms(z[:, ZE_CQ:ZE_KR1], qn_ref[...]).astype(BF16), wuq_ref[...],
                preferred_element_type=F32)
    q1, q2 = q[:, UQ_NOPE_W:UQ_NOPE_W + LANES], q[:, UQ_NOPE_W + LANES:]
    o1, o2 = q1 * cos - q2 * sin, q1 * sin + q2 * cos
    lane_head = lax.shift_right_logical(lax.broadcasted_iota(jnp.int32, o1.shape, 1), ROPE_HALF.bit_length() - 1)
    for h in range(H_A):
        ql = jnp.dot(q[:, h * LANES:(h + 1) * LANES].astype(BF16), wuk_ref[h], preferred_element_type=F32)
        own = lane_head == h
        qh = jnp.concatenate([ql, jnp.where(own, o1, 0.0), jnp.where(own, o2, 0.0)], axis=1)
        qp_ref[h] = (qh * (MLA_SCALE * LOG2E)).astype(BF16)


def _even_in(x2d, tables, w_in_r, q_norm, kv_norm, w_uq_r, w_uk_r):
    n = x2d.shape[0]
    tm = min(256, n)
    period = tables[0].shape[0] // tm
    row = lambda i: (i, 0)
    fixed2 = lambda i: (0, 0)
    tab = lambda i: (i % period, 0)
    return pl.pallas_call(
        _even_in_body,
        grid=(n // tm,),
        in_specs=[pl.BlockSpec((tm, D_MODEL), row),
                  pl.BlockSpec((D_MODEL, ZE_W), fixed2),
                  pl.BlockSpec((1, Q_LORA), fixed2),
                  pl.BlockSpec((1, KV_LORA), fixed2),
                  pl.BlockSpec((Q_LORA, UQ_NOPE_W + 2 * LANES), fixed2),
                  pl.BlockSpec((H_A, LANES, KV_LORA), lambda i: (0, 0, 0)),
                  pl.BlockSpec((tm, LANES), tab), pl.BlockSpec((tm, LANES), tab),
                  pl.BlockSpec((tm, LANES), tab), pl.BlockSpec((tm, LANES), tab)],
        out_specs=[pl.BlockSpec((tm, ZE_GATES), row),
                   pl.BlockSpec((tm, KV_LORA), row),
                   pl.BlockSpec((tm, QK_ROPE), row),
                   pl.BlockSpec((tm, QP_W), row),
                   pl.BlockSpec((H_A, tm, QP_W), lambda i: (0, i, 0))],
        out_shape=[jax.ShapeDtypeStruct((n, ZE_GATES), F32),
                   jax.ShapeDtypeStruct((n, KV_LORA), F32),
                   jax.ShapeDtypeStruct((n, QK_ROPE), F32),
                   jax.ShapeDtypeStruct((n, QP_W), BF16),
                   jax.ShapeDtypeStruct((H_A, n, QP_W), BF16)],
        compiler_params=_params(("arbitrary",)),
        name="even_in_proj",
    )(x2d, w_in_r, q_norm.reshape(1, -1), kv_norm.reshape(1, -1), w_uq_r, w_uk_r, *tables)


def _softmax_update(s, vals, m_sc, l_sc, acc_sc):
    tiles = [s[:, c * LANES:(c + 1) * LANES] for c in range(s.shape[1] // LANES)]
    m_prev = m_sc[...]
    m_new = jnp.maximum(m_prev, jnp.max(functools.reduce(jnp.maximum, tiles), axis=-1, keepdims=True))
    alpha = jnp.exp2(m_prev - m_new)
    ps = [jnp.exp2(t - m_new) for t in tiles]
    l_sc[...] = alpha * l_sc[...] + functools.reduce(jnp.add, ps)
    pv = jnp.dot(jnp.concatenate(ps, axis=1).astype(BF16), vals, preferred_element_type=F32)
    acc = acc_sc[...]
    acc_sc[...] = jnp.concatenate([acc[:, c * LANES:(c + 1) * LANES] * alpha
                                   for c in range(acc.shape[1] // LANES)], axis=1) + pv
    m_sc[...] = m_new


def _softmax_result(l_sc, acc_sc):
    return acc_sc[...] / jnp.sum(l_sc[...], axis=-1, keepdims=True)


def _softmax_init(m_sc, l_sc, acc_sc):
    m_sc[...] = jnp.full(m_sc.shape, -jnp.inf, F32)
    l_sc[...] = jnp.zeros(l_sc.shape, F32)
    acc_sc[...] = jnp.zeros(acc_sc.shape, F32)


FLASH_TQ = 256
FLASH_TK = 512
FLAG_FIRST, FLAG_LAST, FLAG_DIAG = 1, 2, 4


def _flash_body(qb, kb, qo, ko, fl, q_ref, k_ref, wuv_ref, o_ref, m_sc, l_sc, acc_sc, *, tq, tk):
    i = pl.program_id(0)
    flags = fl[i]

    @pl.when((flags & FLAG_FIRST) != 0)
    def _():
        _softmax_init(m_sc, l_sc, acc_sc)

    k = k_ref[...]
    s = lax.dot_general(q_ref[...].reshape(H_A * tq, QP_W), k, NT_DIMS, preferred_element_type=F32)
    vals = k[:, :KV_LORA]

    @pl.when((flags & FLAG_DIAG) != 0)
    def _():
        qpos = (lax.broadcasted_iota(jnp.int32, s.shape, 0) & (tq - 1)) + qo[i]
        kpos = lax.broadcasted_iota(jnp.int32, s.shape, 1) + ko[i]
        _softmax_update(jnp.where(kpos <= qpos, s, -jnp.inf), vals, m_sc, l_sc, acc_sc)

    @pl.when((flags & FLAG_DIAG) == 0)
    def _():
        _softmax_update(s, vals, m_sc, l_sc, acc_sc)

    @pl.when((flags & FLAG_LAST) != 0)
    def _():
        o = _softmax_result(l_sc, acc_sc)
        o_all = jnp.concatenate([o[h * tq:(h + 1) * tq] for h in range(H_A)], axis=1).astype(BF16)
        o_ref[...] = jnp.dot(o_all, wuv_ref[...], preferred_element_type=F32)


def _flash_steps(batch, seq, tq, tk):
    nq, nk = seq // tq, seq // tk
    qb, kb, qo, ko, fl = [], [], [], [], []
    for b in range(batch):
        for qi in range(nq):
            last = ((qi + 1) * tq - 1) // tk
            for kj in range(last + 1):
                qb.append(b * nq + qi)
                kb.append(b * nk + kj)
                qo.append(qi * tq)
                ko.append(kj * tk)
                diag = (kj + 1) * tk - 1 > qi * tq
                fl.append((FLAG_FIRST if kj == 0 else 0) | (FLAG_LAST if kj == last else 0)
                          | (FLAG_DIAG if diag else 0))
    return [np.asarray(a, np.int32) for a in (qb, kb, qo, ko, fl)]


def _mla_prompt(qp, kp, w_uv_bd, batch, seq):
    tq, tk = min(FLASH_TQ, seq), min(FLASH_TK, seq)
    assert tq & (tq - 1) == 0 and seq % tq == 0 and seq % tk == 0
    steps = _flash_steps(batch, seq, tq, tk)
    n = batch * seq
    rows = H_A * tq
    grid_spec = pltpu.PrefetchScalarGridSpec(
        num_scalar_prefetch=5,
        grid=(steps[0].shape[0],),
        in_specs=[pl.BlockSpec((H_A, tq, QP_W), lambda i, qb, kb, qo, ko, fl: (0, qb[i], 0)),
                  pl.BlockSpec((tk, QP_W), lambda i, qb, kb, qo, ko, fl: (kb[i], 0)),
                  pl.BlockSpec((H_A * KV_LORA, W_A), lambda i, qb, kb, qo, ko, fl: (0, 0))],
        out_specs=pl.BlockSpec((tq, W_A), lambda i, qb, kb, qo, ko, fl: (qb[i], 0)),
        scratch_shapes=[pltpu.VMEM((rows, LANES), F32), pltpu.VMEM((rows, LANES), F32),
                        pltpu.VMEM((rows, KV_LORA), F32)])
    return pl.pallas_call(
        functools.partial(_flash_body, tq=tq, tk=tk),
        grid_spec=grid_spec,
        out_shape=jax.ShapeDtypeStruct((n, W_A), F32),
        compiler_params=_params(("arbitrary",)),
        name="mla_prompt_flash",
    )(*[jnp.asarray(a) for a in steps], qp, kp, w_uv_bd)


PAGES_PER_STEP = 64


def _page_copies(pt_ref, lat_hbm, kr_hbm, lat_buf, kr_buf, lat_sem, kr_sem, e, step, slot, i, npg):
    page = pt_ref[step * npg + i]
    return (pltpu.make_async_copy(lat_hbm.at[e, page], lat_buf.at[slot, i], lat_sem.at[slot]),
            pltpu.make_async_copy(kr_hbm.at[e, page], kr_buf.at[slot, i], kr_sem.at[slot]))


def _paged_body(pt_ref, ql_ref, qr_ref, kself_ref, krself_ref, lat_hbm, kr_hbm, o_ref,
                lat_buf, kr_buf, lat_sem, kr_sem, lat_sc, kr_sc, m_sc, l_sc, acc_sc, *, npg, t_new, e):
    j = pl.program_id(1)
    chunks = pl.num_programs(1)
    step = pl.program_id(0) * chunks + j
    last_step = pl.num_programs(0) * chunks - 1
    slot = step & 1
    copies = functools.partial(_page_copies, pt_ref, lat_hbm, kr_hbm, lat_buf, kr_buf, lat_sem, kr_sem, e)

    @pl.when(step == 0)
    def _():
        for i in range(npg):
            for cp in copies(0, 0, i, npg):
                cp.start()

    @pl.when(j == 0)
    def _():
        _softmax_init(m_sc, l_sc, acc_sc)
        kr_sc[QK_ROPE:, :] = jnp.zeros((LANES - QK_ROPE, kr_sc.shape[1]), BF16)

    nxt = jnp.minimum(step + 1, last_step)
    for i in range(npg):
        for cp in copies(nxt, 1 - slot, i, npg):
            cp.start()
    for i in range(npg):
        for cp in copies(step, slot, i, npg):
            cp.wait()

    for i in range(npg):
        lat_sc[i * PAGE_SIZE:(i + 1) * PAGE_SIZE, :] = lat_buf[slot, i].astype(BF16)
        kr_sc[:QK_ROPE, i * PAGE_SIZE:(i + 1) * PAGE_SIZE] = kr_buf[slot, i].astype(BF16)

    ql, qr = ql_ref[0], qr_ref[0]

    def attend(lat, kr_t, mask):
        s = (lax.dot_general(ql, lat, NT_DIMS, preferred_element_type=F32)
             + jnp.dot(qr, kr_t, preferred_element_type=F32))
        if mask is not None:
            s = jnp.where(mask, s, -jnp.inf)
        _softmax_update(s, lat, m_sc, l_sc, acc_sc)

    attend(lat_sc[...], kr_sc[...], None)

    @pl.when(j == chunks - 1)
    def _():
        shape = (ql.shape[0], PAGE_SIZE)
        t_row = lax.broadcasted_iota(jnp.int32, shape, 0) & (t_new - 1)
        attend(kself_ref[0], krself_ref[0], lax.broadcasted_iota(jnp.int32, shape, 1) <= t_row)
        o_ref[0] = _softmax_result(l_sc, acc_sc)

    @pl.when(step == last_step)
    def _():
        for i in range(npg):
            for cp in copies(last_step, 1 - slot, i, npg):
                cp.wait()


def _mla_sample(ql, qr, kself, krself, lat_pool, rope_pool_t, page_table, e, t_new):
    batch, rows, _ = ql.shape
    n_pages = page_table.shape[1]
    npg = min(PAGES_PER_STEP, n_pages)
    assert n_pages % npg == 0 and t_new & (t_new - 1) == 0 and t_new <= PAGE_SIZE
    chunks = n_pages // npg
    per_b = lambda b, j, pt: (b, 0, 0)
    grid_spec = pltpu.PrefetchScalarGridSpec(
        num_scalar_prefetch=1,
        grid=(batch, chunks),
        in_specs=[pl.BlockSpec((1, rows, KV_LORA), per_b), pl.BlockSpec((1, rows, LANES), per_b),
                  pl.BlockSpec((1, PAGE_SIZE, KV_LORA), per_b), pl.BlockSpec((1, LANES, PAGE_SIZE), per_b),
                  pl.BlockSpec(memory_space=pl.ANY), pl.BlockSpec(memory_space=pl.ANY)],
        out_specs=pl.BlockSpec((1, rows, KV_LORA), per_b),
        scratch_shapes=[pltpu.VMEM((2, npg, PAGE_SIZE, KV_LORA), F32),
                        pltpu.VMEM((2, npg, QK_ROPE, PAGE_SIZE), F32),
                        pltpu.SemaphoreType.DMA((2,)), pltpu.SemaphoreType.DMA((2,)),
                        pltpu.VMEM((npg * PAGE_SIZE, KV_LORA), BF16),
                        pltpu.VMEM((LANES, npg * PAGE_SIZE), BF16),
                        pltpu.VMEM((rows, LANES), F32), pltpu.VMEM((rows, LANES), F32),
                        pltpu.VMEM((rows, KV_LORA), F32)])
    return pl.pallas_call(
        functools.partial(_paged_body, npg=npg, t_new=t_new, e=e),
        grid_spec=grid_spec,
        out_shape=jax.ShapeDtypeStruct((batch, rows, KV_LORA), F32),
        compiler_params=_params(("arbitrary", "arbitrary")),
        name="mla_sample_paged",
    )(page_table.reshape(-1), ql, qr, kself, krself, lat_pool, rope_pool_t)


LRU_TC = 256


def _block_diag(w):
    nb, d, e = w.shape
    return jnp.einsum('nde,nm->ndme', w, jnp.eye(nb, dtype=w.dtype)).reshape(nb * d, nb * e)


def _lru_coeffs(xc, wa, wx, ba, bx, lam):
    xb = xc.astype(BF16)
    r = jax.nn.sigmoid(jnp.dot(xb, wa, preferred_element_type=F32) + ba)
    ig = jax.nn.sigmoid(jnp.dot(xb, wx, preferred_element_type=F32) + bx)
    neg = -lam
    softplus = jnp.maximum(neg, 0.0) + jnp.log1p(jnp.exp(-jnp.abs(neg)))
    log_a = -LRU_C * r * softplus
    a = jnp.exp(log_a)
    t = jnp.tanh(log_a)
    b = jnp.sqrt(-2.0 * t / (1.0 - t)) * (ig * xc)
    return a, b


def _lru_body(x_ref, h0_ref, buf_ref, cw_ref, cb_ref, wa_ref, wx_ref, ba_ref, bx_ref, lam_ref,
              h_ref, hl_ref, tail_ref, xbuf, hc, *, tc):
    c = pl.program_id(1)

    @pl.when(c == 0)
    def _():
        xbuf[0:SUBLANES] = buf_ref[0]
        hc[...] = h0_ref[0]

    x = x_ref[...]
    xbuf[SUBLANES:SUBLANES + tc] = x
    cw = cw_ref[...]
    xc = cb_ref[...] + x * cw[CONV_W - 1:CONV_W]
    for j in range(CONV_W - 1):
        xc = xc + xbuf[pl.ds(SUBLANES - (CONV_W - 1) + j, tc), :] * cw[j:j + 1]
    xbuf[0:SUBLANES] = x[tc - SUBLANES:tc]
    a, b = _lru_coeffs(xc, wa_ref[...], wx_ref[...], ba_ref[...], bx_ref[...], lam_ref[...])
    row = lax.broadcasted_iota(jnp.int32, a.shape, 0)
    d = 1
    while d < tc:
        keep = row >= d
        a_sh = jnp.where(keep, pltpu.roll(a, d, 0), 1.0)
        b_sh = jnp.where(keep, pltpu.roll(b, d, 0), 0.0)
        b = a * b_sh + b
        a = a * a_sh
        d *= 2
    h = a * hc[...] + b
    h_ref[...] = h
    hc[...] = h[tc - 1:tc]

    @pl.when(c == pl.num_programs(1) - 1)
    def _():
        hl_ref[0] = h[tc - 1:tc]
        tail_ref[0] = x[tc - SUBLANES:tc]


def _rglru_seq(zg, batch, seq, h0, buf, conv_w, conv_b, wa_bd, wx_bd, ba, bx, lam):
    tc = min(LRU_TC, seq)
    assert seq % tc == 0 and tc >= SUBLANES
    nc = seq // tc
    buf8 = jnp.pad(buf, ((0, 0), (SUBLANES - (CONV_W - 1), 0), (0, 0)))
    vec = lambda a: a.reshape(1, W_B)
    fixed = lambda b, c: (0, 0)
    per_b = lambda b, c: (b, 0, 0)
    h, hl, tail = pl.pallas_call(
        functools.partial(_lru_body, tc=tc),
        grid=(batch, nc),
        in_specs=[pl.BlockSpec((tc, W_B), lambda b, c: (b * nc + c, ZE_XB // W_B)),
                  pl.BlockSpec((1, 1, W_B), per_b),
                  pl.BlockSpec((1, SUBLANES, W_B), per_b),
                  pl.BlockSpec((CONV_W, W_B), fixed),
                  pl.BlockSpec((1, W_B), fixed),
                  pl.BlockSpec((W_B, W_B), fixed), pl.BlockSpec((W_B, W_B), fixed),
                  pl.BlockSpec((1, W_B), fixed), pl.BlockSpec((1, W_B), fixed), pl.BlockSpec((1, W_B), fixed)],
        out_specs=[pl.BlockSpec((tc, W_B), lambda b, c: (b * nc + c, 0)),
                   pl.BlockSpec((1, 1, W_B), per_b),
                   pl.BlockSpec((1, SUBLANES, W_B), per_b)],
        out_shape=[jax.ShapeDtypeStruct((batch * seq, W_B), F32),
                   jax.ShapeDtypeStruct((batch, 1, W_B), F32),
                   jax.ShapeDtypeStruct((batch, SUBLANES, W_B), F32)],
        scratch_shapes=[pltpu.VMEM((SUBLANES + tc, W_B), F32), pltpu.VMEM((1, W_B), F32)],
        compiler_params=_params(("arbitrary", "arbitrary")),
        name="rglru_seq",
    )(zg, h0.reshape(batch, 1, W_B), buf8, conv_w, vec(conv_b), wa_bd, wx_bd, vec(ba), vec(bx), vec(lam))
    return h, hl[:, 0], tail[:, SUBLANES - (CONV_W - 1):]


def _out_body(v1_ref, v2_ref, v3_ref, g1_ref, g2_ref, g3_ref, x_ref, w_ref, lg_ref, lb_ref, o_ref):
    def gated(v_ref, g_ref):
        g = g_ref[...]
        return (v_ref[...] * (g * jax.nn.sigmoid(g))).astype(BF16)

    mixed = jnp.concatenate([gated(v1_ref, g1_ref), gated(v2_ref, g2_ref), gated(v3_ref, g3_ref)], axis=1)
    u = DEEPNORM_ALPHA * x_ref[...] + jnp.dot(mixed, w_ref[...], preferred_element_type=F32)
    mu = jnp.mean(u, axis=-1, keepdims=True)
    var = jnp.mean(jnp.square(u - mu), axis=-1, keepdims=True)
    o_ref[...] = (u - mu) * lax.rsqrt(var + NORM_EPS) * lg_ref[...] + lb_ref[...]


def _out_proj_norm(vals, z, gate_cols, x2d, w_out, ln_g, ln_b):
    n = x2d.shape[0]
    tm = min(256, n)
    widths = [v.shape[1] for v in vals]
    row = lambda i: (i, 0)
    fixed = lambda i: (0, 0)
    in_specs = [pl.BlockSpec((tm, w), row) for w in widths]
    for w, off in zip(widths, gate_cols):
        assert off % w == 0
        in_specs.append(pl.BlockSpec((tm, w), functools.partial(lambda i, cb: (i, cb), cb=off // w)))
    in_specs += [pl.BlockSpec((tm, D_MODEL), row),
                 pl.BlockSpec((sum(widths), D_MODEL), fixed),
                 pl.BlockSpec((1, D_MODEL), fixed), pl.BlockSpec((1, D_MODEL), fixed)]
    return pl.pallas_call(
        _out_body,
        grid=(n // tm,),
        in_specs=in_specs,
        out_specs=pl.BlockSpec((tm, D_MODEL), row),
        out_shape=jax.ShapeDtypeStruct((n, D_MODEL), F32),
        compiler_params=_params(("arbitrary",)),
        name="out_proj_norm",
    )(*vals, z, z, z, x2d, w_out.astype(BF16), ln_g.reshape(1, -1), ln_b.reshape(1, -1))


def _mem_attend_rows(q, k_t, v_t):
    kb, vb = k_t.astype(BF16), v_t.astype(BF16)
    q_head = lax.shift_right_logical(lax.broadcasted_iota(jnp.int32, q.shape, 1), DH_M.bit_length() - 1)
    out = jnp.zeros(q.shape, F32)
    for h in range(H_M):
        own = q_head == h
        s = jnp.dot(jnp.where(own, q, 0.0).astype(BF16), kb, preferred_element_type=F32) * (DH_M ** -0.5)
        p = jnp.exp(s - jnp.max(s, axis=-1, keepdims=True))
        p = p / jnp.sum(p, axis=-1, keepdims=True)
        y = lax.dot_general(p.astype(BF16), vb, NT_DIMS, preferred_element_type=F32)
        out = out + jnp.where(own, y, 0.0)
    return out


def _mem_body(q_ref, k_ref, v_ref, o_ref):
    for i in range(k_ref.shape[1]):
        q = q_ref[i] if len(q_ref.shape) == 3 else q_ref[...]
        y = _mem_attend_rows(q, k_ref[0, i], v_ref[0, i])
        if len(o_ref.shape) == 3:
            o_ref[i] = y
        else:
            o_ref[...] = y


def _mem_transposed(mem):
    t = jnp.moveaxis(mem, -3, -1)
    t = t.reshape(t.shape[:-3] + (W_M, N_MEM))
    return t if t.ndim == 4 else t[None]


def _mem_attend_long(z, col, batch, seq, mem_kt, mem_vt, layer):
    tm = min(512, seq)
    nt = seq // tm
    kv = lambda b, i: (layer, b, 0, 0)
    return pl.pallas_call(
        _mem_body,
        grid=(batch, nt),
        in_specs=[pl.BlockSpec((tm, W_M), lambda b, i: (b * nt + i, col // W_M)),
                  pl.BlockSpec((1, 1, W_M, N_MEM), kv), pl.BlockSpec((1, 1, W_M, N_MEM), kv)],
        out_specs=pl.BlockSpec((tm, W_M), lambda b, i: (b * nt + i, 0)),
        out_shape=jax.ShapeDtypeStruct((batch * seq, W_M), F32),
        compiler_params=_params(("arbitrary", "arbitrary")),
        name="mem_attend_long",
    )(z, mem_kt, mem_vt)


MEM_SHORT_NB = 8


def _mem_attend_short(z, col, batch, seq, mem_kt, mem_vt, layer):
    assert seq <= SUBLANES
    nb = min(MEM_SHORT_NB, batch)
    q = jnp.pad(z[:, col:col + W_M].reshape(batch, seq, W_M), ((0, 0), (0, SUBLANES - seq), (0, 0)))
    blk = lambda i: (i, 0, 0)
    kv = lambda i: (layer, i, 0, 0)
    y = pl.pallas_call(
        _mem_body,
        grid=(batch // nb,),
        in_specs=[pl.BlockSpec((nb, SUBLANES, W_M), blk),
                  pl.BlockSpec((1, nb, W_M, N_MEM), kv), pl.BlockSpec((1, nb, W_M, N_MEM), kv)],
        out_specs=pl.BlockSpec((nb, SUBLANES, W_M), blk),
        out_shape=jax.ShapeDtypeStruct((batch, SUBLANES, W_M), F32),
        compiler_params=_params(("arbitrary",)),
        name="mem_attend_short",
    )(q, mem_kt, mem_vt)
    return y[:, :seq].reshape(batch * seq, W_M)


def _chunk_mlp_body(u_ref, v_ref, g_ref, b_ref, w_ref, bias_ref, y_ref, vn_ref, *, rows, chunks):
    gw = W_D // G_D
    for c in range(chunks):
        sl = pl.ds(c * rows, rows)
        v = v_ref[sl, :]
        mu = jnp.mean(v, axis=-1, keepdims=True)
        var = jnp.mean(jnp.square(v - mu), axis=-1, keepdims=True)
        vn = (v - mu) * lax.rsqrt(var + NORM_EPS) * g_ref[...] + b_ref[...]
        vn_ref[sl, :] = vn
        vb = vn.astype(BF16)
        s = jnp.concatenate([jnp.dot(w_ref[g], vb[:, g * gw:(g + 1) * gw], preferred_element_type=F32)
                             for g in range(G_D)], axis=1)
        y_ref[sl, :] = u_ref[sl, :] * (s + bias_ref[...])


def _chunk_mlp_call(z, rows, ln_g_d, ln_b_d, w_mix, bias):
    n = z.shape[0]
    chunks = max(1, min(4, n // rows))
    tm = rows * chunks
    row = lambda i: (i, 0)
    fixed = lambda i: (0, 0)
    return pl.pallas_call(
        functools.partial(_chunk_mlp_body, rows=rows, chunks=chunks),
        grid=(n // tm,),
        in_specs=[pl.BlockSpec((tm, W_D), lambda i: (i, ZO_U // W_D)),
                  pl.BlockSpec((tm, W_D), lambda i: (i, ZO_VD // W_D)),
                  pl.BlockSpec((1, W_D), fixed), pl.BlockSpec((1, W_D), fixed),
                  pl.BlockSpec((G_D, rows, rows), lambda i: (0, 0, 0)),
                  pl.BlockSpec((rows, W_D), fixed)],
        out_specs=[pl.BlockSpec((tm, W_D), row), pl.BlockSpec((tm, W_D), row)],
        out_shape=[jax.ShapeDtypeStruct((n, W_D), F32), jax.ShapeDtypeStruct((n, W_D), F32)],
        compiler_params=_params(("arbitrary",)),
        name="chunk_mlp",
    )(z, z, ln_g_d.reshape(1, -1), ln_b_d.reshape(1, -1), w_mix, bias)


def _chunk_mlp_weights(sg_w, sg_b):
    L = CHUNK_D
    w = jnp.where(jnp.tril(jnp.ones((L, L), dtype=bool)), sg_w[:, :L, :L], 0.0)
    return w.astype(BF16), jnp.repeat(sg_b[:, :L].T, W_D // G_D, axis=1)


def _chunk_mlp_short_body(u_ref, v_ref, g_ref, b_ref, w_ref, bias_ref, y_ref, vn_ref, *, batch, seq):
    v = v_ref[...]
    mu = jnp.mean(v, axis=-1, keepdims=True)
    var = jnp.mean(jnp.square(v - mu), axis=-1, keepdims=True)
    vn = (v - mu) * lax.rsqrt(var + NORM_EPS) * g_ref[...] + b_ref[...]
    vn_ref[...] = vn
    for t in range(seq):
        acc = vn[0:batch] * w_ref[t * seq:t * seq + 1, :]
        for s in range(1, t + 1):
            acc = acc + vn[s * batch:(s + 1) * batch] * w_ref[t * seq + s:t * seq + s + 1, :]
        rows = slice(t * batch, (t + 1) * batch)
        y_ref[rows, :] = u_ref[rows, :] * (acc + bias_ref[t:t + 1, :])


def _chunk_mlp_short(z, batch, seq, ln_g_d, ln_b_d, sg_w, sg_b):
    gw = W_D // G_D
    w_rows = jnp.repeat(jnp.transpose(sg_w[:, :seq, :seq], (1, 2, 0)).reshape(seq * seq, G_D), gw, axis=1)
    bias = jnp.repeat(sg_b[:, :seq].T, gw, axis=1)

    def time_major(off):
        return jnp.swapaxes(z[:, off:off + W_D].reshape(batch, seq, W_D), 0, 1).reshape(seq * batch, W_D)

    y, vn = pl.pallas_call(
        functools.partial(_chunk_mlp_short_body, batch=batch, seq=seq),
        out_shape=[jax.ShapeDtypeStruct((seq * batch, W_D), F32)] * 2,
        compiler_params=pltpu.CompilerParams(vmem_limit_bytes=VMEM_LIMIT),
        name="chunk_mlp_short",
    )(time_major(ZO_U), time_major(ZO_VD), ln_g_d.reshape(1, -1), ln_b_d.reshape(1, -1), w_rows, bias)
    back = lambda a: jnp.swapaxes(a.reshape(seq, batch, W_D), 0, 1).reshape(batch * seq, W_D)
    return back(y), back(vn)


def _cumsum_rows(x):
    row = lax.broadcasted_iota(jnp.int32, x.shape, 0)
    d = 1
    while d < x.shape[0]:
        x = x + jnp.where(row >= d, pltpu.roll(x, d, 0), 0.0)
        d *= 2
    return x


def _mlstm_body(q_ref, k_ref, v_ref, o_ref, gt_ref, bif_ref, c0_ref, n0_ref, m0_ref,
                y_ref, c_ref, n_ref, m_ref, c_sc, n_sc, m_sc, *stage, nb, n_valid):
    L = MLSTM_CHUNK
    step = pl.program_id(1)

    @pl.when(step == 0)
    def _():
        c_sc[...] = c0_ref[...]
        n_sc[...] = n0_ref[...]
        m_sc[...] = m0_ref[...]
        for st in stage:
            st[...] = jnp.zeros(st.shape, F32)

    row = lax.broadcasted_iota(jnp.int32, (L, LANES), 0)
    lane = lax.broadcasted_iota(jnp.int32, (L, LANES), 1)
    tril = lax.broadcasted_iota(jnp.int32, (L, L), 0) >= lax.broadcasted_iota(jnp.int32, (L, L), 1)
    for b in range(nb):
        if stage:
            rows_in = q_ref.shape[1]
            bufs = []
            for st, ref in zip(stage, (q_ref, k_ref, v_ref, o_ref, gt_ref)):
                st[b, 0:rows_in, :] = ref[b]
                bufs.append(st[b])
            q, k, v, o_pre, gts = bufs
        else:
            q, k, v, o_pre, gts = q_ref[b], k_ref[b], v_ref[b], o_ref[b], gt_ref[b]
        gz = gts + bif_ref[...]
        lf = jnp.minimum(gz, 0.0) - jnp.log1p(jnp.exp(-jnp.abs(gz)))
        ig = gz
        if n_valid < L:
            ig = jnp.where(row < n_valid, ig, -jnp.inf)
            lf = jnp.where(row < n_valid, lf, 0.0)
        a = jnp.where(lane < H_C, ig, _cumsum_rows(lf))
        a_t = a.T
        ys = []
        for h in range(H_C):
            hs = slice(h * DH_C, (h + 1) * DH_C)
            ig_row, bc_row = a_t[h:h + 1, :], a_t[H_C + h:H_C + h + 1, :]
            ig_col, bc_col = a[:, h:h + 1], a[:, H_C + h:H_C + h + 1]
            m_old = m_sc[b, h][:, 0:1]
            dmat = jnp.where(tril, bc_col - bc_row + ig_row, -jnp.inf)
            inter = bc_col + m_old
            mt = jnp.maximum(inter, jnp.max(dmat, axis=-1, keepdims=True))
            w = jnp.exp(dmat - mt)
            qh = q[:, hs].astype(BF16)
            kh = k[:, hs] * (DH_C ** -0.5)
            vh = v[:, hs].astype(BF16)
            sc = lax.dot_general(qh, kh.astype(BF16), NT_DIMS, preferred_element_type=F32) * w
            g_inter = jnp.exp(inter - mt)
            c_old = c_sc[b, h]
            n_old = n_sc[b, h]
            num = (jnp.dot(sc.astype(BF16), vh, preferred_element_type=F32)
                   + g_inter * jnp.dot(qh, c_old.astype(BF16), preferred_element_type=F32))
            den = (jnp.sum(sc, axis=-1, keepdims=True)
                   + g_inter * jnp.sum(q[:, hs] * n_old, axis=-1, keepdims=True))
            hh = num / jnp.maximum(jnp.abs(den), jnp.exp(-mt))
            ys.append(jax.nn.sigmoid(o_pre[:, hs]) * hh)
            b_end = bc_col[L - 1:L, :]
            g_col = b_end - bc_col + ig_col
            m_new = jnp.maximum(b_end + m_old, jnp.max(g_col, axis=0, keepdims=True))
            decay = jnp.exp(b_end + m_old - m_new)
            kw = kh * jnp.exp(g_col - m_new)
            c_sc[b, h] = decay * c_old + jnp.dot(kw.T.astype(BF16), vh, preferred_element_type=F32)
            n_sc[b, h] = decay * n_old + jnp.sum(kw, axis=0, keepdims=True)
            m_sc[b, h] = jnp.broadcast_to(m_new, (1, LANES))
        y = jnp.concatenate(ys, axis=1)
        y_ref[b] = y[0:y_ref.shape[1]]

    @pl.when(step == pl.num_programs(1) - 1)
    def _():
        c_ref[...] = c_sc[...]
        n_ref[...] = n_sc[...]
        m_ref[...] = m_sc[...]


MLSTM_NB = 2


def _mlstm_call(z, batch, seq, b_if, c0, n0, m0):
    L = MLSTM_CHUNK
    nb = min(MLSTM_NB, batch)
    long = seq % L == 0
    if long:
        rows, nc, n_valid = L, seq // L, L
        z3 = z.reshape(batch, seq, z.shape[1])
    else:
        assert seq <= SUBLANES
        rows, nc, n_valid = SUBLANES, 1, seq
        z3 = jnp.pad(z.reshape(batch, seq, z.shape[1]), ((0, 0), (0, SUBLANES - seq), (0, 0)))
    bif = jnp.pad(b_if, (0, LANES - 2 * H_C)).reshape(1, LANES)
    m0r = jnp.broadcast_to(m0[:, :, None, None], (batch, H_C, 1, LANES))

    def col(off, w):
        return pl.BlockSpec((nb, rows, w), functools.partial(lambda g, c, cb: (g, c, cb), cb=off // w))

    st4 = lambda g, c: (g, 0, 0, 0)
    in_specs = [col(ZO_Q, W_C), col(ZO_K, W_C), col(ZO_V, W_C), col(ZO_O, W_C), col(ZO_IF, LANES),
                pl.BlockSpec((1, LANES), lambda g, c: (0, 0)),
                pl.BlockSpec((nb, H_C, DH_C, DH_C), st4),
                pl.BlockSpec((nb, H_C, 1, DH_C), st4),
                pl.BlockSpec((nb, H_C, 1, LANES), st4)]
    out_specs = [pl.BlockSpec((nb, rows, W_C), lambda g, c: (g, c, 0)),
                 pl.BlockSpec((nb, H_C, DH_C, DH_C), st4),
                 pl.BlockSpec((nb, H_C, 1, DH_C), st4),
                 pl.BlockSpec((nb, H_C, 1, LANES), st4)]
    scratch = [pltpu.VMEM((nb, H_C, DH_C, DH_C), F32), pltpu.VMEM((nb, H_C, 1, DH_C), F32),
               pltpu.VMEM((nb, H_C, 1, LANES), F32)]
    if not long:
        scratch += [pltpu.VMEM((nb, L, W_C), F32)] * 4 + [pltpu.VMEM((nb, L, LANES), F32)]
    y, c, n, m = pl.pallas_call(
        functools.partial(_mlstm_body, nb=nb, n_valid=n_valid),
        grid=(batch // nb, nc),
        in_specs=in_specs,
        out_specs=out_specs,
        out_shape=[jax.ShapeDtypeStruct((batch, rows * nc, W_C), F32),
                   jax.ShapeDtypeStruct((batch, H_C, DH_C, DH_C), F32),
                   jax.ShapeDtypeStruct((batch, H_C, 1, DH_C), F32),
                   jax.ShapeDtypeStruct((batch, H_C, 1, LANES), F32)],
        scratch_shapes=scratch,
        compiler_params=_params(("arbitrary", "arbitrary")),
        name="mlstm_chunks",
    )(z3, z3, z3, z3, z3, bif, c0, n0[:, :, None, :], m0r)
    return y[:, :seq].reshape(batch * seq, W_C), c, n[:, :, 0], m[:, :, 0, 0]


def _lru_short_body(x_ref, h0_ref, buf_ref, cw_ref, cb_ref, wa_ref, wx_ref, ba_ref, bx_ref, lam_ref,
                    h_ref, *, batch, seq):
    cw = cw_ref[...]
    xp = jnp.concatenate([buf_ref[...], x_ref[...]], axis=0)
    xc = cb_ref[...] + xp[(CONV_W - 1) * batch:] * cw[CONV_W - 1:CONV_W]
    for j in range(CONV_W - 1):
        xc = xc + xp[j * batch:(j + seq) * batch] * cw[j:j + 1]
    a, b = _lru_coeffs(xc, wa_ref[...], wx_ref[...], ba_ref[...], bx_ref[...], lam_ref[...])
    h = h0_ref[...]
    for t in range(seq):
        sl = slice(t * batch, (t + 1) * batch)
        h = a[sl] * h + b[sl]
        h_ref[sl, :] = h


def _rglru_short(zg, batch, seq, h0, buf, conv_w, conv_b, wa_bd, wx_bd, ba, bx, lam):
    assert batch % SUBLANES == 0
    x = zg[:, ZE_XB:ZE_XB + W_B].reshape(batch, seq, W_B)
    xp = jnp.concatenate([buf, x], axis=1)
    x_tm = jnp.swapaxes(x, 0, 1).reshape(seq * batch, W_B)
    buf_tm = jnp.swapaxes(buf, 0, 1).reshape((CONV_W - 1) * batch, W_B)
    vec = lambda a: a.reshape(1, W_B)
    h_tm = pl.pallas_call(
        functools.partial(_lru_short_body, batch=batch, seq=seq),
        out_shape=jax.ShapeDtypeStruct((seq * batch, W_B), F32),
        compiler_params=pltpu.CompilerParams(vmem_limit_bytes=VMEM_LIMIT),
        name="rglru_short",
    )(x_tm, h0, buf_tm, conv_w, vec(conv_b), wa_bd, wx_bd, vec(ba), vec(bx), vec(lam))
    h = jnp.swapaxes(h_tm.reshape(seq, batch, W_B), 0, 1)
    return h.reshape(batch * seq, W_B), h[:, -1], xp[:, -(CONV_W - 1):]


def _mem_kv(mem, w_mk, w_mv):
    B = mem.shape[0]
    kv = _proj(mem, jnp.concatenate([w_mk, w_mv], axis=1))
    return (kv[..., :W_M].reshape(B, N_MEM, H_M, DH_M), kv[..., W_M:].reshape(B, N_MEM, H_M, DH_M))


def _sample_queries(qp, batch, t_new):
    n = qp.shape[1]
    r1 = jnp.stack([qp[h, :, KV_LORA + h * ROPE_HALF:KV_LORA + (h + 1) * ROPE_HALF] for h in range(H_A)])
    r2 = jnp.stack([qp[h, :, KV_LORA + LANES + h * ROPE_HALF:KV_LORA + LANES + (h + 1) * ROPE_HALF]
                    for h in range(H_A)])
    qr = jnp.concatenate([r1, r2, jnp.zeros((H_A, n, LANES - QK_ROPE), qp.dtype)], axis=-1)

    def rows(a):
        w = a.shape[-1]
        return a.reshape(H_A, batch, t_new, w).transpose(1, 0, 2, 3).reshape(batch, H_A * t_new, w)

    return rows(qp[:, :, :KV_LORA]), rows(qr)


def _sample_self_keys(latent, k_rope, batch, t_new):
    kl = jnp.pad(latent.astype(BF16).reshape(batch, t_new, KV_LORA), ((0, 0), (0, PAGE_SIZE - t_new), (0, 0)))
    kr_t = jnp.swapaxes(k_rope.astype(BF16).reshape(batch, t_new, QK_ROPE), 1, 2)
    return kl, jnp.pad(kr_t, ((0, 0), (0, LANES - QK_ROPE), (0, PAGE_SIZE - t_new)))


def _even_layer(x2d, batch, seq, tables, mem, lru_h0, lru_buf, paged, e, weights,
                q_norm, kv_norm, conv_w, conv_b, wa, ba, wx, bx, lam, w_out, ln_g, ln_b):
    w_in_r, w_uq_r, w_uk_r, w_uv_bd = weights
    zg, latent, k_rope, kp, qp = _even_in(x2d, tables, w_in_r, q_norm, kv_norm, w_uq_r, w_uk_r)
    lru_args = (lru_h0, lru_buf, conv_w, conv_b, _block_diag(wa).astype(BF16), _block_diag(wx).astype(BF16),
                ba, bx, lam)
    if paged is None:
        y_a = _mla_prompt(qp, kp, w_uv_bd, batch, seq)
        h_b, h_last, new_buf = _rglru_seq(zg, batch, seq, *lru_args)
        y_m = _mem_attend_long(zg, ZE_QM, batch, seq, *mem)
    else:
        lat_pool, rope_pool, page_table = paged
        o = _mla_sample(*_sample_queries(qp, batch, seq), *_sample_self_keys(latent, k_rope, batch, seq),
                        lat_pool, jnp.swapaxes(rope_pool, 2, 3), page_table, e, seq)
        o = o.reshape(batch, H_A, seq, KV_LORA).transpose(0, 2, 1, 3).reshape(batch * seq, H_A * KV_LORA)
        y_a = _matmul(o, w_uv_bd)
        h_b, h_last, new_buf = _rglru_short(zg, batch, seq, *lru_args)
        y_m = _mem_attend_short(zg, ZE_QM, batch, seq, *mem)
    x_new = _out_proj_norm([y_a, h_b, y_m], zg, (ZE_GA, ZE_GB, ZE_GM), x2d, w_out, ln_g, ln_b)
    return (x_new, latent.reshape(batch, seq, KV_LORA), k_rope.reshape(batch, seq, QK_ROPE), h_last, new_buf)


def _odd_weights(w_in):
    q, k, v, i_pre, f_pre, o_pre, g_c, u_d, v_d, g_d, q_m, g_m = _split_cols(w_in, ODD_SPLITS)
    gates = jnp.pad(jnp.concatenate([i_pre, f_pre], axis=1), ((0, 0), (0, LANES - 2 * H_C)))
    return jnp.concatenate([q, k, v, o_pre, g_c, u_d, v_d, g_d, q_m, g_m, gates], axis=1)


def _odd_layer(x2d, batch, seq, mem, c0, n0, m0, w_in_r, b_if, ln_g_d, ln_b_d, sg_w, sg_b,
               w_out, ln_g, ln_b):
    z = _matmul(x2d, w_in_r)
    y_c, c, n, m = _mlstm_call(z, batch, seq, b_if, c0, n0, m0)
    if seq % CHUNK_D == 0:
        y_d, vn = _chunk_mlp_call(z, CHUNK_D, ln_g_d, ln_b_d, *_chunk_mlp_weights(sg_w, sg_b))
        y_m = _mem_attend_long(z, ZO_QM, batch, seq, *mem)
    else:
        y_d, vn = _chunk_mlp_short(z, batch, seq, ln_g_d, ln_b_d, sg_w, sg_b)
        y_m = _mem_attend_short(z, ZO_QM, batch, seq, *mem)
    x_new = _out_proj_norm([y_c, y_d, y_m], z, (ZO_GC, ZO_GD, ZO_GM), x2d, w_out, ln_g, ln_b)
    return x_new, vn.reshape(batch, seq, W_D), c, n, m


def kernel(x_prompt, x_sample, cache_mla_latent, cache_mla_krope, state_lru_h, state_lru_conv,
           state_mlstm_c, state_mlstm_n, state_mlstm_m, cache_mem_k, cache_mem_v, page_table,
           mem_prompt, w_in_even, mla_q_norm, mla_kv_norm, w_uq, w_uk, w_uv,
           lru_conv_w, lru_conv_b, lru_wa, lru_ba, lru_wx, lru_bx, lru_lambda, w_out_even,
           w_in_odd, mlstm_b_if, sg_ln_g, sg_ln_b, sg_w, sg_b, w_out_odd,
           w_mem_k, w_mem_v, ln_g, ln_b):
    Bp, Tp, _ = x_prompt.shape
    Bs, Ts, _ = x_sample.shape
    past_len = page_table.shape[1] * PAGE_SIZE
    tables_p = _rope_tables(jnp.arange(Tp, dtype=F32))
    tables_s = tuple(jnp.tile(t, (Bs, 1)) for t in _rope_tables(past_len + jnp.arange(Ts, dtype=F32)))

    h0_p = jnp.zeros((Bp, W_B), F32)
    buf0_p = jnp.zeros((Bp, CONV_W - 1, W_B), F32)
    c0_p = jnp.zeros((Bp, H_C, DH_C, DH_C), F32)
    n0_p = jnp.zeros((Bp, H_C, DH_C), F32)
    m0_p = jnp.zeros((Bp, H_C), F32)

    lat_p, kr_p, h_p, conv_p, c_p, n_p, m_p, mk_p, mv_p = [], [], [], [], [], [], [], [], []
    lat_s, kr_s, h_s, conv_s, c_s, n_s, m_s, v_s = [], [], [], [], [], [], [], []

    xp = x_prompt.reshape(Bp * Tp, D_MODEL)
    xs = x_sample.reshape(Bs * Ts, D_MODEL)
    mem_s = (_mem_transposed(cache_mem_k), _mem_transposed(cache_mem_v))
    for l in range(DEPTH):
        mk_l, mv_l = _mem_kv(mem_prompt, w_mem_k[l], w_mem_v[l])
        mk_p.append(mk_l)
        mv_p.append(mv_l)
        mem_p = (_mem_transposed(mk_l), _mem_transposed(mv_l), 0)
        if l % 2 == 0:
            e = l // 2
            weights = _even_weights(w_in_even[e], w_uq[e], w_uk[e], w_uv[e])
            rest = (mla_q_norm[e], mla_kv_norm[e], lru_conv_w[e], lru_conv_b[e], lru_wa[e], lru_ba[e],
                    lru_wx[e], lru_bx[e], lru_lambda[e], w_out_even[e], ln_g[l], ln_b[l])
            xp, la, kr, hl, cb = _even_layer(xp, Bp, Tp, tables_p, mem_p, h0_p, buf0_p, None, e,
                                             weights, *rest)
            lat_p.append(la); kr_p.append(kr); h_p.append(hl); conv_p.append(cb)
            xs, la, kr, hl, cb = _even_layer(xs, Bs, Ts, tables_s, mem_s + (l,),
                                             state_lru_h[e], state_lru_conv[e],
                                             (cache_mla_latent, cache_mla_krope, page_table), e, weights, *rest)
            lat_s.append(la); kr_s.append(kr); h_s.append(hl); conv_s.append(cb)
        else:
            o = l // 2
            ow = (_odd_weights(w_in_odd[o]), mlstm_b_if[o], sg_ln_g[o], sg_ln_b[o], sg_w[o], sg_b[o],
                  w_out_odd[o], ln_g[l], ln_b[l])
            xp, _, cc, nn, mm = _odd_layer(xp, Bp, Tp, mem_p, c0_p, n0_p, m0_p, *ow)
            c_p.append(cc); n_p.append(nn); m_p.append(mm)
            xs, vn, cc, nn, mm = _odd_layer(xs, Bs, Ts, mem_s + (l,),
                                            state_mlstm_c[o], state_mlstm_n[o], state_mlstm_m[o], *ow)
            c_s.append(cc); n_s.append(nn); m_s.append(mm); v_s.append(vn)

    return (xp.reshape(Bp, Tp, D_MODEL), xs.reshape(Bs, Ts, D_MODEL),
            jnp.stack(lat_p), jnp.stack(kr_p), jnp.stack(h_p), jnp.stack(conv_p),
            jnp.stack(c_p), jnp.stack(n_p), jnp.stack(m_p), jnp.stack(mk_p), jnp.stack(mv_p),
            jnp.stack(lat_s), jnp.stack(kr_s), jnp.stack(h_s), jnp.stack(conv_s),
            jnp.stack(c_s), jnp.stack(n_s), jnp.stack(m_s), jnp.stack(v_s))
```

```python
import functools

import jax
import jax.numpy as jnp
import numpy as np
from jax import lax
from jax.experimental import pallas as pl
from jax.experimental.pallas import tpu as pltpu

D_MODEL = 1024
DEPTH = 2
PAGE_SIZE = 128
H_A = 8
Q_LORA = 384
KV_LORA = 256
QK_NOPE = 64
QK_ROPE = 32
ROPE_HALF = QK_ROPE // 2
V_HEAD = 64
W_A = H_A * V_HEAD
ROPE_THETA = 10000.0
MLA_SCALE = (QK_NOPE + QK_ROPE) ** -0.5
W_B = 512
NB_B = 8
BD_B = W_B // NB_B
CONV_W = 4
LRU_C = 8.0
H_C = 4
DH_C = 128
W_C = H_C * DH_C
MLSTM_CHUNK = 128
G_D = 4
W_D = 512
CHUNK_D = 128
N_MEM = 256
H_M = 4
DH_M = 64
W_M = H_M * DH_M
NORM_EPS = 1e-6
DEEPNORM_ALPHA = (2 * DEPTH) ** 0.25

EVEN_SPLITS = (Q_LORA, KV_LORA, QK_ROPE, W_A, W_B, W_B, W_M, W_M)
ODD_SPLITS = (W_C, W_C, W_C, H_C, H_C, W_C, W_C, W_D, W_D, W_D, W_M, W_M)

F32 = jnp.float32
BF16 = jnp.bfloat16
LANES = 128
SUBLANES = 8
VMEM_LIMIT = 48 * 1024 * 1024
LOG2E = 1.4426950408889634
NT_DIMS = (((1,), (1,)), ((), ()))

ZE_GA, ZE_XB, ZE_GB, ZE_QM, ZE_GM = 0, 512, 1024, 1536, 1792
ZE_GATES = 2048
ZE_CKV = 2048
ZE_CQ = ZE_CKV + KV_LORA
ZE_KR1 = ZE_CQ + Q_LORA
ZE_KR2 = ZE_KR1 + LANES
ZE_KRN = ZE_KR2 + LANES
ZE_KRS = ZE_KRN + LANES
ZE_W = ZE_KRS + LANES
QP_W = KV_LORA + 2 * LANES
UQ_NOPE_W = H_A * LANES
ZO_Q, ZO_K, ZO_V, ZO_O, ZO_GC, ZO_U, ZO_VD, ZO_GD = (i * 512 for i in range(8))
ZO_QM, ZO_GM, ZO_IF = 4096, 4352, 4608
ZO_W = ZO_IF + LANES


def _split_cols(z, sizes):
    cuts = [int(c) for c in np.cumsum(sizes)[:-1]]
    return jnp.split(z, cuts, axis=-1)


def _params(sem):
    return pltpu.CompilerParams(dimension_semantics=sem, vmem_limit_bytes=VMEM_LIMIT)


def _mm_body(x_ref, w_ref, o_ref):
    o_ref[...] = jnp.dot(x_ref[...].astype(BF16), w_ref[...], preferred_element_type=F32)


def _row_tile(m, n):
    tm = 512
    while tm > SUBLANES and (tm * n * 4 * 2 > 10 * 1024 * 1024 or m % tm):
        tm //= 2
    return tm


def _matmul(x, w):
    m, k = x.shape
    n = w.shape[1]
    n_pad = -n % LANES
    wb = w.astype(BF16)
    if n_pad:
        wb = jnp.pad(wb, ((0, 0), (0, n_pad)))
    np_ = n + n_pad
    tm = _row_tile(m, np_)
    out = pl.pallas_call(
        _mm_body,
        grid=(m // tm,),
        in_specs=[pl.BlockSpec((tm, k), lambda i: (i, 0)),
                  pl.BlockSpec((k, np_), lambda i: (0, 0))],
        out_specs=pl.BlockSpec((tm, np_), lambda i: (i, 0)),
        out_shape=jax.ShapeDtypeStruct((m, np_), F32),
        compiler_params=_params(("arbitrary",)),
        name="row_matmul",
    )(x, wb)
    return out[:, :n] if n_pad else out


def _proj(x, w):
    lead = x.shape[:-1]
    return _matmul(x.reshape(-1, x.shape[-1]), w).reshape(lead + (w.shape[1],))


def _even_weights(w_in, w_uq, w_uk, w_uv):
    c_q, c_kv, kr, g_a, x_b, g_b, q_m, g_m = _split_cols(w_in, EVEN_SPLITS)
    x1, x2 = kr[:, :ROPE_HALF], kr[:, ROPE_HALF:]

    def lane_pad(a):
        return jnp.pad(a, ((0, 0), (0, LANES - a.shape[1])))

    w_in_r = jnp.concatenate(
        [g_a, x_b, g_b, q_m, g_m, c_kv, c_q, jnp.tile(x1, (1, H_A)), jnp.tile(x2, (1, H_A)),
         lane_pad(kr), lane_pad(jnp.concatenate([x2, x1], axis=1))], axis=1).astype(BF16)
    r = w_uq.reshape(Q_LORA, H_A, QK_NOPE + QK_ROPE)
    nope = jnp.pad(r[:, :, :QK_NOPE], ((0, 0), (0, 0), (0, LANES - QK_NOPE))).reshape(Q_LORA, UQ_NOPE_W)
    r1 = r[:, :, QK_NOPE:QK_NOPE + ROPE_HALF].reshape(Q_LORA, LANES)
    r2 = r[:, :, QK_NOPE + ROPE_HALF:].reshape(Q_LORA, LANES)
    w_uq_r = jnp.concatenate([nope, r1, r2], axis=1).astype(BF16)
    w_uk_r = jnp.pad(jnp.transpose(w_uk, (1, 2, 0)), ((0, 0), (0, LANES - QK_NOPE), (0, 0))).astype(BF16)
    eye = jnp.eye(H_A, dtype=w_uv.dtype)
    w_uv_bd = jnp.einsum('chv,hg->hcgv', w_uv, eye).reshape(H_A * KV_LORA, W_A).astype(BF16)
    return w_in_r, w_uq_r, w_uk_r, w_uv_bd


def _rope_tables(pos):
    inv = ROPE_THETA ** (-jnp.arange(ROPE_HALF, dtype=F32) / ROPE_HALF)
    ang = pos.astype(F32)[:, None] * inv[None, :]
    cos, sin = jnp.cos(ang), jnp.sin(ang)
    zpad = jnp.zeros((pos.shape[0], LANES - QK_ROPE), F32)
    return (jnp.tile(cos, (1, H_A)), jnp.tile(sin, (1, H_A)),
            jnp.concatenate([cos, cos, zpad], axis=1), jnp.concatenate([-sin, sin, zpad], axis=1))


def _rms(x, g):
    return x * lax.rsqrt(jnp.mean(x * x, axis=-1, keepdims=True) + NORM_EPS) * g


def _even_in_body(x_ref, w_ref, qn_ref, kvn_ref, wuq_ref, wuk_ref, cos_ref, sin_ref, cosn_ref, sinn_ref,
                  zg_ref, lat_ref, kr_ref, kp_ref, qp_ref):
    z = jnp.dot(x_ref[...].astype(BF16), w_ref[...], preferred_element_type=F32)
    zg_ref[...] = z[:, :ZE_GATES]
    lat = _rms(z[:, ZE_CKV:ZE_CQ], kvn_ref[...])
    lat_ref[...] = lat
    cos, sin = cos_ref[...], sin_ref[...]
    kr1, kr2 = z[:, ZE_KR1:ZE_KR2], z[:, ZE_KR2:ZE_KRN]
    kp_ref[...] = jnp.concatenate([lat, kr1 * cos - kr2 * sin, kr1 * sin + kr2 * cos], axis=1).astype(BF16)
    kr_nat = z[:, ZE_KRN:ZE_KRS] * cosn_ref[...] + z[:, ZE_KRS:ZE_W] * sinn_ref[...]
    kr_ref[...] = kr_nat[:, :QK_ROPE]
    q = jnp.dot(_rms(z[:, ZE_CQ:ZE_KR1], qn_ref[...]).astype(BF16), wuq_ref[...],
                preferred_element_type=F32)
    q1, q2 = q[:, UQ_NOPE_W:UQ_NOPE_W + LANES], q[:, UQ_NOPE_W + LANES:]
    o1, o2 = q1 * cos - q2 * sin, q1 * sin + q2 * cos
    lane_head = lax.shift_right_logical(lax.broadcasted_iota(jnp.int32, o1.shape, 1), ROPE_HALF.bit_length() - 1)
    for h in range(H_A):
        ql = jnp.dot(q[:, h * LANES:(h + 1) * LANES].astype(BF16), wuk_ref[h], preferred_element_type=F32)
        own = lane_head == h
        qh = jnp.concatenate([ql, jnp.where(own, o1, 0.0), jnp.where(own, o2, 0.0)], axis=1)
        qp_ref[h] = (qh * (MLA_SCALE * LOG2E)).astype(BF16)


def _even_in(x2d, tables, w_in_r, q_norm, kv_norm, w_uq_r, w_uk_r):
    n = x2d.shape[0]
    tm = min(256, n)
    period = tables[0].shape[0] // tm
    row = lambda i: (i, 0)
    fixed2 = lambda i: (0, 0)
    tab = lambda i: (i % period, 0)
    return pl.pallas_call(
        _even_in_body,
        grid=(n // tm,),
        in_specs=[pl.BlockSpec((tm, D_MODEL), row),
                  pl.BlockSpec((D_MODEL, ZE_W), fixed2),
                  pl.BlockSpec((1, Q_LORA), fixed2),
                  pl.BlockSpec((1, KV_LORA), fixed2),
                  pl.BlockSpec((Q_LORA, UQ_NOPE_W + 2 * LANES), fixed2),
                  pl.BlockSpec((H_A, LANES, KV_LORA), lambda i: (0, 0, 0)),
                  pl.BlockSpec((tm, LANES), tab), pl.BlockSpec((tm, LANES), tab),
                  pl.BlockSpec((tm, LANES), tab), pl.BlockSpec((tm, LANES), tab)],
        out_specs=[pl.BlockSpec((tm, ZE_GATES), row),
                   pl.BlockSpec((tm, KV_LORA), row),
                   pl.BlockSpec((tm, QK_ROPE), row),
                   pl.BlockSpec((tm, QP_W), row),
                   pl.BlockSpec((H_A, tm, QP_W), lambda i: (0, i, 0))],
        out_shape=[jax.ShapeDtypeStruct((n, ZE_GATES), F32),
                   jax.ShapeDtypeStruct((n, KV_LORA), F32),
                   jax.ShapeDtypeStruct((n, QK_ROPE), F32),
                   jax.ShapeDtypeStruct((n, QP_W), BF16),
                   jax.ShapeDtypeStruct((H_A, n, QP_W), BF16)],
        compiler_params=_params(("arbitrary",)),
        name="even_in_proj",
    )(x2d, w_in_r, q_norm.reshape(1, -1), kv_norm.reshape(1, -1), w_uq_r, w_uk_r, *tables)


def _softmax_update(s, vals, m_sc, l_sc, acc_sc):
    tiles = [s[:, c * LANES:(c + 1) * LANES] for c in range(s.shape[1] // LANES)]
    m_prev = m_sc[...]
    m_new = jnp.maximum(m_prev, jnp.max(functools.reduce(jnp.maximum, tiles), axis=-1, keepdims=True))
    alpha = jnp.exp2(m_prev - m_new)
    ps = [jnp.exp2(t - m_new) for t in tiles]
    l_sc[...] = alpha * l_sc[...] + functools.reduce(jnp.add, ps)
    pv = jnp.dot(jnp.concatenate(ps, axis=1).astype(BF16), vals, preferred_element_type=F32)
    acc = acc_sc[...]
    acc_sc[...] = jnp.concatenate([acc[:, c * LANES:(c + 1) * LANES] * alpha
                                   for c in range(acc.shape[1] // LANES)], axis=1) + pv
    m_sc[...] = m_new


def _softmax_result(l_sc, acc_sc):
    return acc_sc[...] / jnp.sum(l_sc[...], axis=-1, keepdims=True)


def _softmax_init(m_sc, l_sc, acc_sc):
    m_sc[...] = jnp.full(m_sc.shape, -jnp.inf, F32)
    l_sc[...] = jnp.zeros(l_sc.shape, F32)
    acc_sc[...] = jnp.zeros(acc_sc.shape, F32)


FLASH_TQ = 256
FLASH_TK = 1024
FLAG_FIRST, FLAG_LAST, FLAG_DIAG = 1, 2, 4


def _flash_body(qb, kb, qo, ko, fl, q_ref, k_ref, wuv_ref, o_ref, m_sc, l_sc, acc_sc, *, tq, tk):
    i = pl.program_id(0)
    flags = fl[i]

    @pl.when((flags & FLAG_FIRST) != 0)
    def _():
        _softmax_init(m_sc, l_sc, acc_sc)

    k = k_ref[...]
    s = lax.dot_general(q_ref[...].reshape(H_A * tq, QP_W), k, NT_DIMS, preferred_element_type=F32)
    vals = k[:, :KV_LORA]

    @pl.when((flags & FLAG_DIAG) != 0)
    def _():
        qpos = (lax.broadcasted_iota(jnp.int32, s.shape, 0) & (tq - 1)) + qo[i]
        kpos = lax.broadcasted_iota(jnp.int32, s.shape, 1) + ko[i]
        _softmax_update(jnp.where(kpos <= qpos, s, -jnp.inf), vals, m_sc, l_sc, acc_sc)

    @pl.when((flags & FLAG_DIAG) == 0)
    def _():
        _softmax_update(s, vals, m_sc, l_sc, acc_sc)

    @pl.when((flags & FLAG_LAST) != 0)
    def _():
        o = _softmax_result(l_sc, acc_sc)
        o_all = jnp.concatenate([o[h * tq:(h + 1) * tq] for h in range(H_A)], axis=1).astype(BF16)
        o_ref[...] = jnp.dot(o_all, wuv_ref[...], preferred_element_type=F32)


def _flash_steps(batch, seq, tq, tk):
    nq, nk = seq // tq, seq // tk
    qb, kb, qo, ko, fl = [], [], [], [], []
    for b in range(batch):
        for qi in range(nq):
            last = ((qi + 1) * tq - 1) // tk
            for kj in range(last + 1):
                qb.append(b * nq + qi)
                kb.append(b * nk + kj)
                qo.append(qi * tq)
                ko.append(kj * tk)
                diag = (kj + 1) * tk - 1 > qi * tq
                fl.append((FLAG_FIRST if kj == 0 else 0) | (FLAG_LAST if kj == last else 0)
                          | (FLAG_DIAG if diag else 0))
    return [np.asarray(a, np.int32) for a in (qb, kb, qo, ko, fl)]


def _mla_prompt(qp, kp, w_uv_bd, batch, seq):
    tq, tk = min(FLASH_TQ, seq), min(FLASH_TK, seq)
    assert tq & (tq - 1) == 0 and seq % tq == 0 and seq % tk == 0
    steps = _flash_steps(batch, seq, tq, tk)
    n = batch * seq
    rows = H_A * tq
    grid_spec = pltpu.PrefetchScalarGridSpec(
        num_scalar_prefetch=5,
        grid=(steps[0].shape[0],),
        in_specs=[pl.BlockSpec((H_A, tq, QP_W), lambda i, qb, kb, qo, ko, fl: (0, qb[i], 0)),
                  pl.BlockSpec((tk, QP_W), lambda i, qb, kb, qo, ko, fl: (kb[i], 0)),
                  pl.BlockSpec((H_A * KV_LORA, W_A), lambda i, qb, kb, qo, ko, fl: (0, 0))],
        out_specs=pl.BlockSpec((tq, W_A), lambda i, qb, kb, qo, ko, fl: (qb[i], 0)),
        scratch_shapes=[pltpu.VMEM((rows, LANES), F32), pltpu.VMEM((rows, LANES), F32),
                        pltpu.VMEM((rows, KV_LORA), F32)])
    return pl.pallas_call(
        functools.partial(_flash_body, tq=tq, tk=tk),
        grid_spec=grid_spec,
        out_shape=jax.ShapeDtypeStruct((n, W_A), F32),
        compiler_params=_params(("arbitrary",)),
        name="mla_prompt_flash",
    )(*[jnp.asarray(a) for a in steps], qp, kp, w_uv_bd)


PAGES_PER_STEP = 64


def _page_copies(pt_ref, lat_hbm, kr_hbm, lat_buf, kr_buf, lat_sem, kr_sem, e, step, slot, i, npg):
    page = pt_ref[step * npg + i]
    return (pltpu.make_async_copy(lat_hbm.at[e, page], lat_buf.at[slot, i], lat_sem.at[slot]),
            pltpu.make_async_copy(kr_hbm.at[e, page], kr_buf.at[slot, i], kr_sem.at[slot]))


def _paged_body(pt_ref, ql_ref, qr_ref, kself_ref, krself_ref, lat_hbm, kr_hbm, o_ref,
                lat_buf, kr_buf, lat_sem, kr_sem, lat_sc, kr_sc, m_sc, l_sc, acc_sc, *, npg, t_new, e):
    j = pl.program_id(1)
    chunks = pl.num_programs(1)
    step = pl.program_id(0) * chunks + j
    last_step = pl.num_programs(0) * chunks - 1
    slot = step & 1
    copies = functools.partial(_page_copies, pt_ref, lat_hbm, kr_hbm, lat_buf, kr_buf, lat_sem, kr_sem, e)

    @pl.when(step == 0)
    def _():
        for i in range(npg):
            for cp in copies(0, 0, i, npg):
                cp.start()

    @pl.when(j == 0)
    def _():
        _softmax_init(m_sc, l_sc, acc_sc)

    nxt = jnp.minimum(step + 1, last_step)
    for i in range(npg):
        for cp in copies(nxt, 1 - slot, i, npg):
            cp.start()
    for i in range(npg):
        for cp in copies(step, slot, i, npg):
            cp.wait()

    for i in range(npg):
        lat_sc[i * PAGE_SIZE:(i + 1) * PAGE_SIZE, :] = lat_buf[slot, i].astype(BF16)
        kr_sc[:QK_ROPE, i * PAGE_SIZE:(i + 1) * PAGE_SIZE] = kr_buf[slot, i].astype(BF16)

    ql, qr = ql_ref[0], qr_ref[0]

    def attend(lat, kr_t, mask):
        s = (lax.dot_general(ql, lat, NT_DIMS, preferred_element_type=F32)
             + jnp.dot(qr, kr_t, preferred_element_type=F32))
        if mask is not None:
            s = jnp.where(mask, s, -jnp.inf)
        _softmax_update(s, lat, m_sc, l_sc, acc_sc)

    attend(lat_sc[...], kr_sc[...], None)

    @pl.when(j == chunks - 1)
    def _():
        shape = (ql.shape[0], PAGE_SIZE)
        t_row = lax.broadcasted_iota(jnp.int32, shape, 0) & (t_new - 1)
        attend(kself_ref[0], krself_ref[0], lax.broadcasted_iota(jnp.int32, shape, 1) <= t_row)
        o_ref[0] = _softmax_result(l_sc, acc_sc)

    @pl.when(step == last_step)
    def _():
        for i in range(npg):
            for cp in copies(last_step, 1 - slot, i, npg):
                cp.wait()


def _mla_sample(ql, qr, kself, krself, lat_pool, rope_pool_t, page_table, e, t_new):
    batch, rows, _ = ql.shape
    n_pages = page_table.shape[1]
    npg = min(PAGES_PER_STEP, n_pages)
    assert n_pages % npg == 0 and t_new & (t_new - 1) == 0 and t_new <= PAGE_SIZE
    chunks = n_pages // npg
    per_b = lambda b, j, pt: (b, 0, 0)
    grid_spec = pltpu.PrefetchScalarGridSpec(
        num_scalar_prefetch=1,
        grid=(batch, chunks),
        in_specs=[pl.BlockSpec((1, rows, KV_LORA), per_b), pl.BlockSpec((1, rows, QK_ROPE), per_b),
                  pl.BlockSpec((1, PAGE_SIZE, KV_LORA), per_b), pl.BlockSpec((1, QK_ROPE, PAGE_SIZE), per_b),
                  pl.BlockSpec(memory_space=pl.ANY), pl.BlockSpec(memory_space=pl.ANY)],
        out_specs=pl.BlockSpec((1, rows, KV_LORA), per_b),
        scratch_shapes=[pltpu.VMEM((2, npg, PAGE_SIZE, KV_LORA), F32),
                        pltpu.VMEM((2, npg, QK_ROPE, PAGE_SIZE), F32),
                        pltpu.SemaphoreType.DMA((2,)), pltpu.SemaphoreType.DMA((2,)),
                        pltpu.VMEM((npg * PAGE_SIZE, KV_LORA), BF16),
                        pltpu.VMEM((QK_ROPE, npg * PAGE_SIZE), BF16),
                        pltpu.VMEM((rows, LANES), F32), pltpu.VMEM((rows, LANES), F32),
                        pltpu.VMEM((rows, KV_LORA), F32)])
    return pl.pallas_call(
        functools.partial(_paged_body, npg=npg, t_new=t_new, e=e),
        grid_spec=grid_spec,
        out_shape=jax.ShapeDtypeStruct((batch, rows, KV_LORA), F32),
        compiler_params=_params(("arbitrary", "arbitrary")),
        name="mla_sample_paged",
    )(page_table.reshape(-1), ql, qr, kself, krself, lat_pool, rope_pool_t)


LRU_TC = 256


def _block_diag(w):
    nb, d, e = w.shape
    return jnp.einsum('nde,nm->ndme', w, jnp.eye(nb, dtype=w.dtype)).reshape(nb * d, nb * e)


def _lru_coeffs(xc, wa, wx, ba, bx, lam):
    xb = xc.astype(BF16)
    r = jax.nn.sigmoid(jnp.dot(xb, wa, preferred_element_type=F32) + ba)
    ig = jax.nn.sigmoid(jnp.dot(xb, wx, preferred_element_type=F32) + bx)
    neg = -lam
    softplus = jnp.maximum(neg, 0.0) + jnp.log1p(jnp.exp(-jnp.abs(neg)))
    log_a = -LRU_C * r * softplus
    a = jnp.exp(log_a)
    t = jnp.tanh(log_a)
    b = jnp.sqrt(-2.0 * t / (1.0 - t)) * (ig * xc)
    return a, b


def _lru_body(x_ref, h0_ref, buf_ref, cw_ref, cb_ref, wa_ref, wx_ref, ba_ref, bx_ref, lam_ref,
              h_ref, hl_ref, tail_ref, xbuf, hc, *, tc):
    c = pl.program_id(1)

    @pl.when(c == 0)
    def _():
        xbuf[0:SUBLANES] = buf_ref[0]
        hc[...] = h0_ref[0]

    x = x_ref[...]
    xbuf[SUBLANES:SUBLANES + tc] = x
    cw = cw_ref[...]
    xc = cb_ref[...] + x * cw[CONV_W - 1:CONV_W]
    for j in range(CONV_W - 1):
        xc = xc + xbuf[pl.ds(SUBLANES - (CONV_W - 1) + j, tc), :] * cw[j:j + 1]
    xbuf[0:SUBLANES] = x[tc - SUBLANES:tc]
    a, b = _lru_coeffs(xc, wa_ref[...], wx_ref[...], ba_ref[...], bx_ref[...], lam_ref[...])
    row = lax.broadcasted_iota(jnp.int32, a.shape, 0)
    d = 1
    while d < tc:
        keep = row >= d
        a_sh = jnp.where(keep, pltpu.roll(a, d, 0), 1.0)
        b_sh = jnp.where(keep, pltpu.roll(b, d, 0), 0.0)
        b = a * b_sh + b
        a = a * a_sh
        d *= 2
    h = a * hc[...] + b
    h_ref[...] = h
    hc[...] = h[tc - 1:tc]

    @pl.when(c == pl.num_programs(1) - 1)
    def _():
        hl_ref[0] = h[tc - 1:tc]
        tail_ref[0] = x[tc - SUBLANES:tc]


def _rglru_seq(zg, batch, seq, h0, buf, conv_w, conv_b, wa_bd, wx_bd, ba, bx, lam):
    tc = min(LRU_TC, seq)
    assert seq % tc == 0 and tc >= SUBLANES
    nc = seq // tc
    buf8 = jnp.pad(buf, ((0, 0), (SUBLANES - (CONV_W - 1), 0), (0, 0)))
    vec = lambda a: a.reshape(1, W_B)
    fixed = lambda b, c: (0, 0)
    per_b = lambda b, c: (b, 0, 0)
    h, hl, tail = pl.pallas_call(
        functools.partial(_lru_body, tc=tc),
        grid=(batch, nc),
        in_specs=[pl.BlockSpec((tc, W_B), lambda b, c: (b * nc + c, ZE_XB // W_B)),
                  pl.BlockSpec((1, 1, W_B), per_b),
                  pl.BlockSpec((1, SUBLANES, W_B), per_b),
                  pl.BlockSpec((CONV_W, W_B), fixed),
                  pl.BlockSpec((1, W_B), fixed),
                  pl.BlockSpec((W_B, W_B), fixed), pl.BlockSpec((W_B, W_B), fixed),
                  pl.BlockSpec((1, W_B), fixed), pl.BlockSpec((1, W_B), fixed), pl.BlockSpec((1, W_B), fixed)],
        out_specs=[pl.BlockSpec((tc, W_B), lambda b, c: (b * nc + c, 0)),
                   pl.BlockSpec((1, 1, W_B), per_b),
                   pl.BlockSpec((1, SUBLANES, W_B), per_b)],
        out_shape=[jax.ShapeDtypeStruct((batch * seq, W_B), F32),
                   jax.ShapeDtypeStruct((batch, 1, W_B), F32),
                   jax.ShapeDtypeStruct((batch, SUBLANES, W_B), F32)],
        scratch_shapes=[pltpu.VMEM((SUBLANES + tc, W_B), F32), pltpu.VMEM((1, W_B), F32)],
        compiler_params=_params(("arbitrary", "arbitrary")),
        name="rglru_seq",
    )(zg, h0.reshape(batch, 1, W_B), buf8, conv_w, vec(conv_b), wa_bd, wx_bd, vec(ba), vec(bx), vec(lam))
    return h, hl[:, 0], tail[:, SUBLANES - (CONV_W - 1):]


def _out_body(v1_ref, v2_ref, v3_ref, g1_ref, g2_ref, g3_ref, x_ref, w_ref, lg_ref, lb_ref, o_ref):
    def gated(v_ref, g_ref):
        g = g_ref[...]
        return (v_ref[...] * (g * jax.nn.sigmoid(g))).astype(BF16)

    mixed = jnp.concatenate([gated(v1_ref, g1_ref), gated(v2_ref, g2_ref), gated(v3_ref, g3_ref)], axis=1)
    u = DEEPNORM_ALPHA * x_ref[...] + jnp.dot(mixed, w_ref[...], preferred_element_type=F32)
    mu = jnp.mean(u, axis=-1, keepdims=True)
    var = jnp.mean(jnp.square(u - mu), axis=-1, keepdims=True)
    o_ref[...] = (u - mu) * lax.rsqrt(var + NORM_EPS) * lg_ref[...] + lb_ref[...]


def _out_proj_norm(vals, z, gate_cols, x2d, w_out, ln_g, ln_b):
    n = x2d.shape[0]
    tm = min(256, n)
    widths = [v.shape[1] for v in vals]
    row = lambda i: (i, 0)
    fixed = lambda i: (0, 0)
    in_specs = [pl.BlockSpec((tm, w), row) for w in widths]
    for w, off in zip(widths, gate_cols):
        assert off % w == 0
        in_specs.append(pl.BlockSpec((tm, w), functools.partial(lambda i, cb: (i, cb), cb=off // w)))
    in_specs += [pl.BlockSpec((tm, D_MODEL), row),
                 pl.BlockSpec((sum(widths), D_MODEL), fixed),
                 pl.BlockSpec((1, D_MODEL), fixed), pl.BlockSpec((1, D_MODEL), fixed)]
    return pl.pallas_call(
        _out_body,
        grid=(n // tm,),
        in_specs=in_specs,
        out_specs=pl.BlockSpec((tm, D_MODEL), row),
        out_shape=jax.ShapeDtypeStruct((n, D_MODEL), F32),
        compiler_params=_params(("arbitrary",)),
        name="out_proj_norm",
    )(*vals, z, z, z, x2d, w_out.astype(BF16), ln_g.reshape(1, -1), ln_b.reshape(1, -1))


def _mem_attend_rows(q, k_t, v_t):
    kb, vb = k_t.astype(BF16), v_t.astype(BF16)
    q_head = lax.shift_right_logical(lax.broadcasted_iota(jnp.int32, q.shape, 1), DH_M.bit_length() - 1)
    out = jnp.zeros(q.shape, F32)
    for h in range(H_M):
        own = q_head == h
        s = jnp.dot(jnp.where(own, q, 0.0).astype(BF16), kb, preferred_element_type=F32) * (DH_M ** -0.5)
        p = jnp.exp(s - jnp.max(s, axis=-1, keepdims=True))
        p = p / jnp.sum(p, axis=-1, keepdims=True)
        y = lax.dot_general(p.astype(BF16), vb, NT_DIMS, preferred_element_type=F32)
        out = out + jnp.where(own, y, 0.0)
    return out


def _mem_body(q_ref, k_ref, v_ref, o_ref):
    for i in range(k_ref.shape[1]):
        q = q_ref[i] if len(q_ref.shape) == 3 else q_ref[...]
        y = _mem_attend_rows(q, k_ref[0, i], v_ref[0, i])
        if len(o_ref.shape) == 3:
            o_ref[i] = y
        else:
            o_ref[...] = y


def _mem_transposed(mem):
    t = jnp.moveaxis(mem, -3, -1)
    t = t.reshape(t.shape[:-3] + (W_M, N_MEM))
    return t if t.ndim == 4 else t[None]


def _mem_attend_long(z, col, batch, seq, mem_kt, mem_vt, layer):
    tm = min(512, seq)
    nt = seq // tm
    kv = lambda b, i: (layer, b, 0, 0)
    return pl.pallas_call(
        _mem_body,
        grid=(batch, nt),
        in_specs=[pl.BlockSpec((tm, W_M), lambda b, i: (b * nt + i, col // W_M)),
                  pl.BlockSpec((1, 1, W_M, N_MEM), kv), pl.BlockSpec((1, 1, W_M, N_MEM), kv)],
        out_specs=pl.BlockSpec((tm, W_M), lambda b, i: (b * nt + i, 0)),
        out_shape=jax.ShapeDtypeStruct((batch * seq, W_M), F32),
        compiler_params=_params(("arbitrary", "arbitrary")),
        name="mem_attend_long",
    )(z, mem_kt, mem_vt)


MEM_SHORT_NB = 8


def _mem_attend_short(z, col, batch, seq, mem_kt, mem_vt, layer):
    assert seq <= SUBLANES
    nb = min(MEM_SHORT_NB, batch)
    q = jnp.pad(z[:, col:col + W_M].reshape(batch, seq, W_M), ((0, 0), (0, SUBLANES - seq), (0, 0)))
    blk = lambda i: (i, 0, 0)
    kv = lambda i: (layer, i, 0, 0)
    y = pl.pallas_call(
        _mem_body,
        grid=(batch // nb,),
        in_specs=[pl.BlockSpec((nb, SUBLANES, W_M), blk),
                  pl.BlockSpec((1, nb, W_M, N_MEM), kv), pl.BlockSpec((1, nb, W_M, N_MEM), kv)],
        out_specs=pl.BlockSpec((nb, SUBLANES, W_M), blk),
        out_shape=jax.ShapeDtypeStruct((batch, SUBLANES, W_M), F32),
        compiler_params=_params(("arbitrary",)),
        name="mem_attend_short",
    )(q, mem_kt, mem_vt)
    return y[:, :seq].reshape(batch * seq, W_M)


def _chunk_mlp_body(u_ref, v_ref, g_ref, b_ref, w_ref, bias_ref, y_ref, vn_ref, *, rows, chunks):
    gw = W_D // G_D
    for c in range(chunks):
        sl = pl.ds(c * rows, rows)
        v = v_ref[sl, :]
        mu = jnp.mean(v, axis=-1, keepdims=True)
        var = jnp.mean(jnp.square(v - mu), axis=-1, keepdims=True)
        vn = (v - mu) * lax.rsqrt(var + NORM_EPS) * g_ref[...] + b_ref[...]
        vn_ref[sl, :] = vn
        vb = vn.astype(BF16)
        s = jnp.concatenate([jnp.dot(w_ref[g], vb[:, g * gw:(g + 1) * gw], preferred_element_type=F32)
                             for g in range(G_D)], axis=1)
        y_ref[sl, :] = u_ref[sl, :] * (s + bias_ref[...])


def _chunk_mlp_call(z, rows, ln_g_d, ln_b_d, w_mix, bias):
    n = z.shape[0]
    chunks = max(1, min(4, n // rows))
    tm = rows * chunks
    row = lambda i: (i, 0)
    fixed = lambda i: (0, 0)
    return pl.pallas_call(
        functools.partial(_chunk_mlp_body, rows=rows, chunks=chunks),
        grid=(n // tm,),
        in_specs=[pl.BlockSpec((tm, W_D), lambda i: (i, ZO_U // W_D)),
                  pl.BlockSpec((tm, W_D), lambda i: (i, ZO_VD // W_D)),
                  pl.BlockSpec((1, W_D), fixed), pl.BlockSpec((1, W_D), fixed),
                  pl.BlockSpec((G_D, rows, rows), lambda i: (0, 0, 0)),
                  pl.BlockSpec((rows, W_D), fixed)],
        out_specs=[pl.BlockSpec((tm, W_D), row), pl.BlockSpec((tm, W_D), row)],
        out_shape=[jax.ShapeDtypeStruct((n, W_D), F32), jax.ShapeDtypeStruct((n, W_D), F32)],
        compiler_params=_params(("arbitrary",)),
        name="chunk_mlp",
    )(z, z, ln_g_d.reshape(1, -1), ln_b_d.reshape(1, -1), w_mix, bias)


def _chunk_mlp_weights(sg_w, sg_b):
    L = CHUNK_D
    w = jnp.where(jnp.tril(jnp.ones((L, L), dtype=bool)), sg_w[:, :L, :L], 0.0)
    return w.astype(BF16), jnp.repeat(sg_b[:, :L].T, W_D // G_D, axis=1)


def _chunk_mlp_short_body(u_ref, v_ref, g_ref, b_ref, w_ref, bias_ref, y_ref, vn_ref, *, batch, seq):
    v = v_ref[...]
    mu = jnp.mean(v, axis=-1, keepdims=True)
    var = jnp.mean(jnp.square(v - mu), axis=-1, keepdims=True)
    vn = (v - mu) * lax.rsqrt(var + NORM_EPS) * g_ref[...] + b_ref[...]
    vn_ref[...] = vn
    for t in range(seq):
        acc = vn[0:batch] * w_ref[t * seq:t * seq + 1, :]
        for s in range(1, t + 1):
            acc = acc + vn[s * batch:(s + 1) * batch] * w_ref[t * seq + s:t * seq + s + 1, :]
        rows = slice(t * batch, (t + 1) * batch)
        y_ref[rows, :] = u_ref[rows, :] * (acc + bias_ref[t:t + 1, :])


def _chunk_mlp_short(z, batch, seq, ln_g_d, ln_b_d, sg_w, sg_b):
    gw = W_D // G_D
    w_rows = jnp.repeat(jnp.transpose(sg_w[:, :seq, :seq], (1, 2, 0)).reshape(seq * seq, G_D), gw, axis=1)
    bias = jnp.repeat(sg_b[:, :seq].T, gw, axis=1)

    def time_major(off):
        return jnp.swapaxes(z[:, off:off + W_D].reshape(batch, seq, W_D), 0, 1).reshape(seq * batch, W_D)

    y, vn = pl.pallas_call(
        functools.partial(_chunk_mlp_short_body, batch=batch, seq=seq),
        out_shape=[jax.ShapeDtypeStruct((seq * batch, W_D), F32)] * 2,
        compiler_params=pltpu.CompilerParams(vmem_limit_bytes=VMEM_LIMIT),
        name="chunk_mlp_short",
    )(time_major(ZO_U), time_major(ZO_VD), ln_g_d.reshape(1, -1), ln_b_d.reshape(1, -1), w_rows, bias)
    back = lambda a: jnp.swapaxes(a.reshape(seq, batch, W_D), 0, 1).reshape(batch * seq, W_D)
    return back(y), back(vn)


def _cumsum_rows(x):
    row = lax.broadcasted_iota(jnp.int32, x.shape, 0)
    d = 1
    while d < x.shape[0]:
        x = x + jnp.where(row >= d, pltpu.roll(x, d, 0), 0.0)
        d *= 2
    return x


def _mlstm_body(q_ref, k_ref, v_ref, o_ref, gt_ref, bif_ref, c0_ref, n0_ref, m0_ref,
                y_ref, c_ref, n_ref, m_ref, c_sc, n_sc, m_sc, *stage, nb, n_valid):
    L = MLSTM_CHUNK
    step = pl.program_id(1)

    @pl.when(step == 0)
    def _():
        c_sc[...] = c0_ref[...]
        n_sc[...] = n0_ref[...]
        m_sc[...] = m0_ref[...]
        for st in stage:
            st[...] = jnp.zeros(st.shape, F32)

    row = lax.broadcasted_iota(jnp.int32, (L, LANES), 0)
    lane = lax.broadcasted_iota(jnp.int32, (L, LANES), 1)
    tril = lax.broadcasted_iota(jnp.int32, (L, L), 0) >= lax.broadcasted_iota(jnp.int32, (L, L), 1)
    for b in range(nb):
        if stage:
            rows_in = q_ref.shape[1]
            bufs = []
            for st, ref in zip(stage, (q_ref, k_ref, v_ref, o_ref, gt_ref)):
                st[b, 0:rows_in, :] = ref[b]
                bufs.append(st[b])
            q, k, v, o_pre, gts = bufs
        else:
            q, k, v, o_pre, gts = q_ref[b], k_ref[b], v_ref[b], o_ref[b], gt_ref[b]
        gz = gts + bif_ref[...]
        lf = jnp.minimum(gz, 0.0) - jnp.log1p(jnp.exp(-jnp.abs(gz)))
        ig = gz
        if n_valid < L:
            ig = jnp.where(row < n_valid, ig, -jnp.inf)
            lf = jnp.where(row < n_valid, lf, 0.0)
        a = jnp.where(lane < H_C, ig, _cumsum_rows(lf))
        a_t = a.T
        ys = []
        for h in range(H_C):
            hs = slice(h * DH_C, (h + 1) * DH_C)
            ig_row, bc_row = a_t[h:h + 1, :], a_t[H_C + h:H_C + h + 1, :]
            ig_col, bc_col = a[:, h:h + 1], a[:, H_C + h:H_C + h + 1]
            m_old = m_sc[b, h][:, 0:1]
            dmat = jnp.where(tril, bc_col - bc_row + ig_row, -jnp.inf)
            inter = bc_col + m_old
            mt = jnp.maximum(inter, jnp.max(dmat, axis=-1, keepdims=True))
            w = jnp.exp(dmat - mt)
            qh = q[:, hs].astype(BF16)
            kh = k[:, hs] * (DH_C ** -0.5)
            vh = v[:, hs].astype(BF16)
            sc = lax.dot_general(qh, kh.astype(BF16), NT_DIMS, preferred_element_type=F32) * w
            g_inter = jnp.exp(inter - mt)
            c_old = c_sc[b, h]
            n_old = n_sc[b, h]
            num = (jnp.dot(sc.astype(BF16), vh, preferred_element_type=F32)
                   + g_inter * jnp.dot(qh, c_old.astype(BF16), preferred_element_type=F32))
            den = (jnp.sum(sc, axis=-1, keepdims=True)
                   + g_inter * jnp.sum(q[:, hs] * n_old, axis=-1, keepdims=True))
            hh = num / jnp.maximum(jnp.abs(den), jnp.exp(-mt))
            ys.append(jax.nn.sigmoid(o_pre[:, hs]) * hh)
            b_end = bc_col[L - 1:L, :]
            g_col = b_end - bc_col + ig_col
            m_new = jnp.maximum(b_end + m_old, jnp.max(g_col, axis=0, keepdims=True))
            decay = jnp.exp(b_end + m_old - m_new)
            kw = kh * jnp.exp(g_col - m_new)
            c_sc[b, h] = decay * c_old + jnp.dot(kw.T.astype(BF16), vh, preferred_element_type=F32)
            n_sc[b, h] = decay * n_old + jnp.sum(kw, axis=0, keepdims=True)
            m_sc[b, h] = jnp.broadcast_to(m_new, (1, LANES))
        y = jnp.concatenate(ys, axis=1)
        y_ref[b] = y[0:y_ref.shape[1]]

    @pl.when(step == pl.num_programs(1) - 1)
    def _():
        c_ref[...] = c_sc[...]
        n_ref[...] = n_sc[...]
        m_ref[...] = m_sc[...]


MLSTM_NB = 2


def _mlstm_call(z, batch, seq, b_if, c0, n0, m0):
    L = MLSTM_CHUNK
    nb = min(MLSTM_NB, batch)
    long = seq % L == 0
    if long:
        rows, nc, n_valid = L, seq // L, L
        z3 = z.reshape(batch, seq, z.shape[1])
    else:
        assert seq <= SUBLANES
        rows, nc, n_valid = SUBLANES, 1, seq
        z3 = jnp.pad(z.reshape(batch, seq, z.shape[1]), ((0, 0), (0, SUBLANES - seq), (0, 0)))
    bif = jnp.pad(b_if, (0, LANES - 2 * H_C)).reshape(1, LANES)
    m0r = jnp.broadcast_to(m0[:, :, None, None], (batch, H_C, 1, LANES))

    def col(off, w):
        return pl.BlockSpec((nb, rows, w), functools.partial(lambda g, c, cb: (g, c, cb), cb=off // w))

    st4 = lambda g, c: (g, 0, 0, 0)
    in_specs = [col(ZO_Q, W_C), col(ZO_K, W_C), col(ZO_V, W_C), col(ZO_O, W_C), col(ZO_IF, LANES),
                pl.BlockSpec((1, LANES), lambda g, c: (0, 0)),
                pl.BlockSpec((nb, H_C, DH_C, DH_C), st4),
                pl.BlockSpec((nb, H_C, 1, DH_C), st4),
                pl.BlockSpec((nb, H_C, 1, LANES), st4)]
    out_specs = [pl.BlockSpec((nb, rows, W_C), lambda g, c: (g, c, 0)),
                 pl.BlockSpec((nb, H_C, DH_C, DH_C), st4),
                 pl.BlockSpec((nb, H_C, 1, DH_C), st4),
                 pl.BlockSpec((nb, H_C, 1, LANES), st4)]
    scratch = [pltpu.VMEM((nb, H_C, DH_C, DH_C), F32), pltpu.VMEM((nb, H_C, 1, DH_C), F32),
               pltpu.VMEM((nb, H_C, 1, LANES), F32)]
    if not long:
        scratch += [pltpu.VMEM((nb, L, W_C), F32)] * 4 + [pltpu.VMEM((nb, L, LANES), F32)]
    y, c, n, m = pl.pallas_call(
        functools.partial(_mlstm_body, nb=nb, n_valid=n_valid),
        grid=(batch // nb, nc),
        in_specs=in_specs,
        out_specs=out_specs,
        out_shape=[jax.ShapeDtypeStruct((batch, rows * nc, W_C), F32),
                   jax.ShapeDtypeStruct((batch, H_C, DH_C, DH_C), F32),
                   jax.ShapeDtypeStruct((batch, H_C, 1, DH_C), F32),
                   jax.ShapeDtypeStruct((batch, H_C, 1, LANES), F32)],
        scratch_shapes=scratch,
        compiler_params=_params(("arbitrary", "arbitrary")),
        name="mlstm_chunks",
    )(z3, z3, z3, z3, z3, bif, c0, n0[:, :, None, :], m0r)
    return y[:, :seq].reshape(batch * seq, W_C), c, n[:, :, 0], m[:, :, 0, 0]


def _lru_short_body(x_ref, h0_ref, buf_ref, cw_ref, cb_ref, wa_ref, wx_ref, ba_ref, bx_ref, lam_ref,
                    h_ref, *, batch, seq):
    cw = cw_ref[...]
    xp = jnp.concatenate([buf_ref[...], x_ref[...]], axis=0)
    xc = cb_ref[...] + xp[(CONV_W - 1) * batch:] * cw[CONV_W - 1:CONV_W]
    for j in range(CONV_W - 1):
        xc = xc + xp[j * batch:(j + seq) * batch] * cw[j:j + 1]
    a, b = _lru_coeffs(xc, wa_ref[...], wx_ref[...], ba_ref[...], bx_ref[...], lam_ref[...])
    h = h0_ref[...]
    for t in range(seq):
        sl = slice(t * batch, (t + 1) * batch)
        h = a[sl] * h + b[sl]
        h_ref[sl, :] = h


def _rglru_short(zg, batch, seq, h0, buf, conv_w, conv_b, wa_bd, wx_bd, ba, bx, lam):
    assert batch % SUBLANES == 0
    x = zg[:, ZE_XB:ZE_XB + W_B].reshape(batch, seq, W_B)
    xp = jnp.concatenate([buf, x], axis=1)
    x_tm = jnp.swapaxes(x, 0, 1).reshape(seq * batch, W_B)
    buf_tm = jnp.swapaxes(buf, 0, 1).reshape((CONV_W - 1) * batch, W_B)
    vec = lambda a: a.reshape(1, W_B)
    h_tm = pl.pallas_call(
        functools.partial(_lru_short_body, batch=batch, seq=seq),
        out_shape=jax.ShapeDtypeStruct((seq * batch, W_B), F32),
        compiler_params=pltpu.CompilerParams(vmem_limit_bytes=VMEM_LIMIT),
        name="rglru_short",
    )(x_tm, h0, buf_tm, conv_w, vec(conv_b), wa_bd, wx_bd, vec(ba), vec(bx), vec(lam))
    h = jnp.swapaxes(h_tm.reshape(seq, batch, W_B), 0, 1)
    return h.reshape(batch * seq, W_B), h[:, -1], xp[:, -(CONV_W - 1):]


def _mem_kv(mem, w_mk, w_mv):
    B = mem.shape[0]
    kv = _proj(mem, jnp.concatenate([w_mk, w_mv], axis=1))
    return (kv[..., :W_M].reshape(B, N_MEM, H_M, DH_M), kv[..., W_M:].reshape(B, N_MEM, H_M, DH_M))


def _sample_queries(qp, batch, t_new):
    n = qp.shape[1]
    r1 = jnp.stack([qp[h, :, KV_LORA + h * ROPE_HALF:KV_LORA + (h + 1) * ROPE_HALF] for h in range(H_A)])
    r2 = jnp.stack([qp[h, :, KV_LORA + LANES + h * ROPE_HALF:KV_LORA + LANES + (h + 1) * ROPE_HALF]
                    for h in range(H_A)])
    qr = jnp.concatenate([r1, r2], axis=-1)

    def rows(a):
        w = a.shape[-1]
        return a.reshape(H_A, batch, t_new, w).transpose(1, 0, 2, 3).reshape(batch, H_A * t_new, w)

    return rows(qp[:, :, :KV_LORA]), rows(qr)


def _sample_self_keys(latent, k_rope, batch, t_new):
    kl = jnp.pad(latent.astype(BF16).reshape(batch, t_new, KV_LORA), ((0, 0), (0, PAGE_SIZE - t_new), (0, 0)))
    kr_t = jnp.swapaxes(k_rope.astype(BF16).reshape(batch, t_new, QK_ROPE), 1, 2)
    return kl, jnp.pad(kr_t, ((0, 0), (0, 0), (0, PAGE_SIZE - t_new)))


def _even_layer(x2d, batch, seq, tables, mem, lru_h0, lru_buf, paged, e, weights,
                q_norm, kv_norm, conv_w, conv_b, wa, ba, wx, bx, lam, w_out, ln_g, ln_b):
    w_in_r, w_uq_r, w_uk_r, w_uv_bd = weights
    zg, latent, k_rope, kp, qp = _even_in(x2d, tables, w_in_r, q_norm, kv_norm, w_uq_r, w_uk_r)
    lru_args = (lru_h0, lru_buf, conv_w, conv_b, _block_diag(wa).astype(BF16), _block_diag(wx).astype(BF16),
                ba, bx, lam)
    if paged is None:
        y_a = _mla_prompt(qp, kp, w_uv_bd, batch, seq)
        h_b, h_last, new_buf = _rglru_seq(zg, batch, seq, *lru_args)
        y_m = _mem_attend_long(zg, ZE_QM, batch, seq, *mem)
    else:
        lat_pool, rope_pool, page_table = paged
        o = _mla_sample(*_sample_queries(qp, batch, seq), *_sample_self_keys(latent, k_rope, batch, seq),
                        lat_pool, jnp.swapaxes(rope_pool, 2, 3), page_table, e, seq)
        o = o.reshape(batch, H_A, seq, KV_LORA).transpose(0, 2, 1, 3).reshape(batch * seq, H_A * KV_LORA)
        y_a = _matmul(o, w_uv_bd)
        h_b, h_last, new_buf = _rglru_short(zg, batch, seq, *lru_args)
        y_m = _mem_attend_short(zg, ZE_QM, batch, seq, *mem)
    x_new = _out_proj_norm([y_a, h_b, y_m], zg, (ZE_GA, ZE_GB, ZE_GM), x2d, w_out, ln_g, ln_b)
    return (x_new, latent.reshape(batch, seq, KV_LORA), k_rope.reshape(batch, seq, QK_ROPE), h_last, new_buf)


def _odd_weights(w_in):
    q, k, v, i_pre, f_pre, o_pre, g_c, u_d, v_d, g_d, q_m, g_m = _split_cols(w_in, ODD_SPLITS)
    gates = jnp.pad(jnp.concatenate([i_pre, f_pre], axis=1), ((0, 0), (0, LANES - 2 * H_C)))
    return jnp.concatenate([q, k, v, o_pre, g_c, u_d, v_d, g_d, q_m, g_m, gates], axis=1)


def _odd_layer(x2d, batch, seq, mem, c0, n0, m0, w_in_r, b_if, ln_g_d, ln_b_d, sg_w, sg_b,
               w_out, ln_g, ln_b):
    z = _matmul(x2d, w_in_r)
    y_c, c, n, m = _mlstm_call(z, batch, seq, b_if, c0, n0, m0)
    if seq % CHUNK_D == 0:
        y_d, vn = _chunk_mlp_call(z, CHUNK_D, ln_g_d, ln_b_d, *_chunk_mlp_weights(sg_w, sg_b))
        y_m = _mem_attend_long(z, ZO_QM, batch, seq, *mem)
    else:
        y_d, vn = _chunk_mlp_short(z, batch, seq, ln_g_d, ln_b_d, sg_w, sg_b)
        y_m = _mem_attend_short(z, ZO_QM, batch, seq, *mem)
    x_new = _out_proj_norm([y_c, y_d, y_m], z, (ZO_GC, ZO_GD, ZO_GM), x2d, w_out, ln_g, ln_b)
    return x_new, vn.reshape(batch, seq, W_D), c, n, m


def kernel(x_prompt, x_sample, cache_mla_latent, cache_mla_krope, state_lru_h, state_lru_conv,
           state_mlstm_c, state_mlstm_n, state_mlstm_m, cache_mem_k, cache_mem_v, page_table,
           mem_prompt, w_in_even, mla_q_norm, mla_kv_norm, w_uq, w_uk, w_uv,
           lru_conv_w, lru_conv_b, lru_wa, lru_ba, lru_wx, lru_bx, lru_lambda, w_out_even,
           w_in_odd, mlstm_b_if, sg_ln_g, sg_ln_b, sg_w, sg_b, w_out_odd,
           w_mem_k, w_mem_v, ln_g, ln_b):
    Bp, Tp, _ = x_prompt.shape
    Bs, Ts, _ = x_sample.shape
    past_len = page_table.shape[1] * PAGE_SIZE
    tables_p = _rope_tables(jnp.arange(Tp, dtype=F32))
    tables_s = tuple(jnp.tile(t, (Bs, 1)) for t in _rope_tables(past_len + jnp.arange(Ts, dtype=F32)))

    h0_p = jnp.zeros((Bp, W_B), F32)
    buf0_p = jnp.zeros((Bp, CONV_W - 1, W_B), F32)
    c0_p = jnp.zeros((Bp, H_C, DH_C, DH_C), F32)
    n0_p = jnp.zeros((Bp, H_C, DH_C), F32)
    m0_p = jnp.zeros((Bp, H_C), F32)

    lat_p, kr_p, h_p, conv_p, c_p, n_p, m_p, mk_p, mv_p = [], [], [], [], [], [], [], [], []
    lat_s, kr_s, h_s, conv_s, c_s, n_s, m_s, v_s = [], [], [], [], [], [], [], []

    xp = x_prompt.reshape(Bp * Tp, D_MODEL)
    xs = x_sample.reshape(Bs * Ts, D_MODEL)
    mem_s = (_mem_transposed(cache_mem_k), _mem_transposed(cache_mem_v))
    for l in range(DEPTH):
        mk_l, mv_l = _mem_kv(mem_prompt, w_mem_k[l], w_mem_v[l])
        mk_p.append(mk_l)
        mv_p.append(mv_l)
        mem_p = (_mem_transposed(mk_l), _mem_transposed(mv_l), 0)
        if l % 2 == 0:
            e = l // 2
            weights = _even_weights(w_in_even[e], w_uq[e], w_uk[e], w_uv[e])
            rest = (mla_q_norm[e], mla_kv_norm[e], lru_conv_w[e], lru_conv_b[e], lru_wa[e], lru_ba[e],
                    lru_wx[e], lru_bx[e], lru_lambda[e], w_out_even[e], ln_g[l], ln_b[l])
            xp, la, kr, hl, cb = _even_layer(xp, Bp, Tp, tables_p, mem_p, h0_p, buf0_p, None, e,
                                             weights, *rest)
            lat_p.append(la); kr_p.append(kr); h_p.append(hl); conv_p.append(cb)
            xs, la, kr, hl, cb = _even_layer(xs, Bs, Ts, tables_s, mem_s + (l,),
                                             state_lru_h[e], state_lru_conv[e],
                                             (cache_mla_latent, cache_mla_krope, page_table), e, weights, *rest)
            lat_s.append(la); kr_s.append(kr); h_s.append(hl); conv_s.append(cb)
        else:
            o = l // 2
            ow = (_odd_weights(w_in_odd[o]), mlstm_b_if[o], sg_ln_g[o], sg_ln_b[o], sg_w[o], sg_b[o],
                  w_out_odd[o], ln_g[l], ln_b[l])
            xp, _, cc, nn, mm = _odd_layer(xp, Bp, Tp, mem_p, c0_p, n0_p, m0_p, *ow)
            c_p.append(cc); n_p.append(nn); m_p.append(mm)
            xs, vn, cc, nn, mm = _odd_layer(xs, Bs, Ts, mem_s + (l,),
                                            state_mlstm_c[o], state_mlstm_n[o], state_mlstm_m[o], *ow)
            c_s.append(cc); n_s.append(nn); m_s.append(mm); v_s.append(vn)

    return (xp.reshape(Bp, Tp, D_MODEL), xs.reshape(Bs, Ts, D_MODEL),
            jnp.stack(lat_p), jnp.stack(kr_p), jnp.stack(h_p), jnp.stack(conv_p),
            jnp.stack(c_p), jnp.stack(n_p), jnp.stack(m_p), jnp.stack(mk_p), jnp.stack(mv_p),
            jnp.stack(lat_s), jnp.stack(kr_s), jnp.stack(h_s), jnp.stack(conv_s),
            jnp.stack(c_s), jnp.stack(n_s), jnp.stack(m_s), jnp.stack(v_s))
```

```python
import functools

import jax
import jax.numpy as jnp
import numpy as np
from jax import lax
from jax.experimental import pallas as pl
from jax.experimental.pallas import tpu as pltpu

D_MODEL = 1024
DEPTH = 2
PAGE_SIZE = 128
H_A = 8
Q_LORA = 384
KV_LORA = 256
QK_NOPE = 64
QK_ROPE = 32
ROPE_HALF = QK_ROPE // 2
V_HEAD = 64
W_A = H_A * V_HEAD
ROPE_THETA = 10000.0
MLA_SCALE = (QK_NOPE + QK_ROPE) ** -0.5
W_B = 512
NB_B = 8
BD_B = W_B // NB_B
CONV_W = 4
LRU_C = 8.0
H_C = 4
DH_C = 128
W_C = H_C * DH_C
MLSTM_CHUNK = 128
G_D = 4
W_D = 512
CHUNK_D = 128
N_MEM = 256
H_M = 4
DH_M = 64
W_M = H_M * DH_M
NORM_EPS = 1e-6
DEEPNORM_ALPHA = (2 * DEPTH) ** 0.25

EVEN_SPLITS = (Q_LORA, KV_LORA, QK_ROPE, W_A, W_B, W_B, W_M, W_M)
ODD_SPLITS = (W_C, W_C, W_C, H_C, H_C, W_C, W_C, W_D, W_D, W_D, W_M, W_M)

F32 = jnp.float32
BF16 = jnp.bfloat16
LANES = 128
SUBLANES = 8
VMEM_LIMIT = 48 * 1024 * 1024
LOG2E = 1.4426950408889634
NT_DIMS = (((1,), (1,)), ((), ()))

ZE_GA, ZE_XB, ZE_GB, ZE_QM, ZE_GM = 0, 512, 1024, 1536, 1792
ZE_GATES = 2048
ZE_CKV = 2048
ZE_CQ = ZE_CKV + KV_LORA
ZE_KR1 = ZE_CQ + Q_LORA
ZE_KR2 = ZE_KR1 + LANES
ZE_KRN = ZE_KR2 + LANES
ZE_KRS = ZE_KRN + LANES
ZE_W = ZE_KRS + LANES
QP_W = KV_LORA + 2 * LANES
UQ_NOPE_W = H_A * LANES
ZO_Q, ZO_K, ZO_V, ZO_O, ZO_GC, ZO_U, ZO_VD, ZO_GD = (i * 512 for i in range(8))
ZO_QM, ZO_GM, ZO_IF = 4096, 4352, 4608
ZO_W = ZO_IF + LANES


def _split_cols(z, sizes):
    cuts = [int(c) for c in np.cumsum(sizes)[:-1]]
    return jnp.split(z, cuts, axis=-1)


def _params(sem):
    return pltpu.CompilerParams(dimension_semantics=sem, vmem_limit_bytes=VMEM_LIMIT)


def _mm_body(x_ref, w_ref, o_ref):
    o_ref[...] = jnp.dot(x_ref[...].astype(BF16), w_ref[...], preferred_element_type=F32)


def _row_tile(m, n):
    tm = 512
    while tm > SUBLANES and (tm * n * 4 * 2 > 10 * 1024 * 1024 or m % tm):
        tm //= 2
    return tm


def _matmul(x, w):
    m, k = x.shape
    n = w.shape[1]
    n_pad = -n % LANES
    wb = w.astype(BF16)
    if n_pad:
        wb = jnp.pad(wb, ((0, 0), (0, n_pad)))
    np_ = n + n_pad
    tm = _row_tile(m, np_)
    out = pl.pallas_call(
        _mm_body,
        grid=(m // tm,),
        in_specs=[pl.BlockSpec((tm, k), lambda i: (i, 0)),
                  pl.BlockSpec((k, np_), lambda i: (0, 0))],
        out_specs=pl.BlockSpec((tm, np_), lambda i: (i, 0)),
        out_shape=jax.ShapeDtypeStruct((m, np_), F32),
        compiler_params=_params(("arbitrary",)),
        name="row_matmul",
    )(x, wb)
    return out[:, :n] if n_pad else out


def _proj(x, w):
    lead = x.shape[:-1]
    return _matmul(x.reshape(-1, x.shape[-1]), w).reshape(lead + (w.shape[1],))


def _even_weights(w_in, w_uq, w_uk, w_uv):
    c_q, c_kv, kr, g_a, x_b, g_b, q_m, g_m = _split_cols(w_in, EVEN_SPLITS)
    x1, x2 = kr[:, :ROPE_HALF], kr[:, ROPE_HALF:]

    def lane_pad(a):
        return jnp.pad(a, ((0, 0), (0, LANES - a.shape[1])))

    w_in_r = jnp.concatenate(
        [g_a, x_b, g_b, q_m, g_m, c_kv, c_q, jnp.tile(x1, (1, H_A)), jnp.tile(x2, (1, H_A)),
         lane_pad(kr), lane_pad(jnp.concatenate([x2, x1], axis=1))], axis=1).astype(BF16)
    r = w_uq.reshape(Q_LORA, H_A, QK_NOPE + QK_ROPE)
    nope = jnp.pad(r[:, :, :QK_NOPE], ((0, 0), (0, 0), (0, LANES - QK_NOPE))).reshape(Q_LORA, UQ_NOPE_W)
    r1 = r[:, :, QK_NOPE:QK_NOPE + ROPE_HALF].reshape(Q_LORA, LANES)
    r2 = r[:, :, QK_NOPE + ROPE_HALF:].reshape(Q_LORA, LANES)
    w_uq_r = jnp.concatenate([nope, r1, r2], axis=1).astype(BF16)
    w_uk_r = jnp.pad(jnp.transpose(w_uk, (1, 2, 0)), ((0, 0), (0, LANES - QK_NOPE), (0, 0))).astype(BF16)
    eye = jnp.eye(H_A, dtype=w_uv.dtype)
    w_uv_bd = jnp.einsum('chv,hg->hcgv', w_uv, eye).reshape(H_A * KV_LORA, W_A).astype(BF16)
    return w_in_r, w_uq_r, w_uk_r, w_uv_bd


def _rope_tables(pos):
    inv = ROPE_THETA ** (-jnp.arange(ROPE_HALF, dtype=F32) / ROPE_HALF)
    ang = pos.astype(F32)[:, None] * inv[None, :]
    cos, sin = jnp.cos(ang), jnp.sin(ang)
    zpad = jnp.zeros((pos.shape[0], LANES - QK_ROPE), F32)
    return (jnp.tile(cos, (1, H_A)), jnp.tile(sin, (1, H_A)),
            jnp.concatenate([cos, cos, zpad], axis=1), jnp.concatenate([-sin, sin, zpad], axis=1))


def _rms(x, g):
    return x * lax.rsqrt(jnp.mean(x * x, axis=-1, keepdims=True) + NORM_EPS) * g


def _even_in_body(x_ref, w_ref, qn_ref, kvn_ref, wuq_ref, wuk_ref, cos_ref, sin_ref, cosn_ref, sinn_ref,
                  zg_ref, lat_ref, kr_ref, kp_ref, qp_ref):
    z = jnp.dot(x_ref[...].astype(BF16), w_ref[...], preferred_element_type=F32)
    zg_ref[...] = z[:, :ZE_GATES]
    lat = _rms(z[:, ZE_CKV:ZE_CQ], kvn_ref[...])
    lat_ref[...] = lat
    cos, sin = cos_ref[...], sin_ref[...]
    kr1, kr2 = z[:, ZE_KR1:ZE_KR2], z[:, ZE_KR2:ZE_KRN]
    kp_ref[...] = jnp.concatenate([lat, kr1 * cos - kr2 * sin, kr1 * sin + kr2 * cos], axis=1).astype(BF16)
    kr_nat = z[:, ZE_KRN:ZE_KRS] * cosn_ref[...] + z[:, ZE_KRS:ZE_W] * sinn_ref[...]
    kr_ref[...] = kr_nat[:, :QK_ROPE]
    q = jnp.dot(_rms(z[:, ZE_CQ:ZE_KR1], qn_ref[...]).astype(BF16), wuq_ref[...],
                preferred_element_type=F32)
    q1, q2 = q[:, UQ_NOPE_W:UQ_NOPE_W + LANES], q[:, UQ_NOPE_W + LANES:]
    o1, o2 = q1 * cos - q2 * sin, q1 * sin + q2 * cos
    lane_head = lax.shift_right_logical(lax.broadcasted_iota(jnp.int32, o1.shape, 1), ROPE_HALF.bit_length() - 1)
    for h in range(H_A):
        ql = jnp.dot(q[:, h * LANES:(h + 1) * LANES].astype(BF16), wuk_ref[h], preferred_element_type=F32)
        own = lane_head == h
        qh = jnp.concatenate([ql, jnp.where(own, o1, 0.0), jnp.where(own, o2, 0.0)], axis=1)
        qp_ref[h] = (qh * (MLA_SCALE * LOG2E)).astype(BF16)


def _even_in(x2d, tables, w_in_r, q_norm, kv_norm, w_uq_r, w_uk_r):
    n = x2d.shape[0]
    tm = min(256, n)
    period = tables[0].shape[0] // tm
    row = lambda i: (i, 0)
    fixed2 = lambda i: (0, 0)
    tab = lambda i: (i % period, 0)
    return pl.pallas_call(
        _even_in_body,
        grid=(n // tm,),
        in_specs=[pl.BlockSpec((tm, D_MODEL), row),
                  pl.BlockSpec((D_MODEL, ZE_W), fixed2),
                  pl.BlockSpec((1, Q_LORA), fixed2),
                  pl.BlockSpec((1, KV_LORA), fixed2),
                  pl.BlockSpec((Q_LORA, UQ_NOPE_W + 2 * LANES), fixed2),
                  pl.BlockSpec((H_A, LANES, KV_LORA), lambda i: (0, 0, 0)),
                  pl.BlockSpec((tm, LANES), tab), pl.BlockSpec((tm, LANES), tab),
                  pl.BlockSpec((tm, LANES), tab), pl.BlockSpec((tm, LANES), tab)],
        out_specs=[pl.BlockSpec((tm, ZE_GATES), row),
                   pl.BlockSpec((tm, KV_LORA), row),
                   pl.BlockSpec((tm, QK_ROPE), row),
                   pl.BlockSpec((tm, QP_W), row),
                   pl.BlockSpec((H_A, tm, QP_W), lambda i: (0, i, 0))],
        out_shape=[jax.ShapeDtypeStruct((n, ZE_GATES), F32),
                   jax.ShapeDtypeStruct((n, KV_LORA), F32),
                   jax.ShapeDtypeStruct((n, QK_ROPE), F32),
                   jax.ShapeDtypeStruct((n, QP_W), BF16),
                   jax.ShapeDtypeStruct((H_A, n, QP_W), BF16)],
        compiler_params=_params(("arbitrary",)),
        name="even_in_proj",
    )(x2d, w_in_r, q_norm.reshape(1, -1), kv_norm.reshape(1, -1), w_uq_r, w_uk_r, *tables)


def _softmax_update(s, vals, m_sc, l_sc, acc_sc):
    tiles = [s[:, c * LANES:(c + 1) * LANES] for c in range(s.shape[1] // LANES)]
    m_prev = m_sc[...]
    m_new = jnp.maximum(m_prev, jnp.max(functools.reduce(jnp.maximum, tiles), axis=-1, keepdims=True))
    alpha = jnp.exp2(m_prev - m_new)
    ps = [jnp.exp2(t - m_new) for t in tiles]
    l_sc[...] = alpha * l_sc[...] + functools.reduce(jnp.add, ps)
    pv = jnp.dot(jnp.concatenate(ps, axis=1).astype(BF16), vals, preferred_element_type=F32)
    acc = acc_sc[...]
    acc_sc[...] = jnp.concatenate([acc[:, c * LANES:(c + 1) * LANES] * alpha
                                   for c in range(acc.shape[1] // LANES)], axis=1) + pv
    m_sc[...] = m_new


def _softmax_result(l_sc, acc_sc):
    return acc_sc[...] / jnp.sum(l_sc[...], axis=-1, keepdims=True)


def _softmax_init(m_sc, l_sc, acc_sc):
    m_sc[...] = jnp.full(m_sc.shape, -jnp.inf, F32)
    l_sc[...] = jnp.zeros(l_sc.shape, F32)
    acc_sc[...] = jnp.zeros(acc_sc.shape, F32)


FLASH_TQ = 256
FLASH_TK = 1024
FLAG_FIRST, FLAG_LAST, FLAG_DIAG = 1, 2, 4


def _flash_body(qb, kb, qo, ko, fl, q_ref, k_ref, wuv_ref, o_ref, m_sc, l_sc, acc_sc, *, tq, tk):
    i = pl.program_id(0)
    flags = fl[i]

    @pl.when((flags & FLAG_FIRST) != 0)
    def _():
        _softmax_init(m_sc, l_sc, acc_sc)

    k = k_ref[...]
    s = lax.dot_general(q_ref[...].reshape(H_A * tq, QP_W), k, NT_DIMS, preferred_element_type=F32)
    vals = k[:, :KV_LORA]

    @pl.when((flags & FLAG_DIAG) != 0)
    def _():
        qpos = (lax.broadcasted_iota(jnp.int32, s.shape, 0) & (tq - 1)) + qo[i]
        kpos = lax.broadcasted_iota(jnp.int32, s.shape, 1) + ko[i]
        _softmax_update(jnp.where(kpos <= qpos, s, -jnp.inf), vals, m_sc, l_sc, acc_sc)

    @pl.when((flags & FLAG_DIAG) == 0)
    def _():
        _softmax_update(s, vals, m_sc, l_sc, acc_sc)

    @pl.when((flags & FLAG_LAST) != 0)
    def _():
        o = _softmax_result(l_sc, acc_sc)
        o_all = jnp.concatenate([o[h * tq:(h + 1) * tq] for h in range(H_A)], axis=1).astype(BF16)
        o_ref[...] = jnp.dot(o_all, wuv_ref[...], preferred_element_type=F32)


def _flash_steps(batch, seq, tq, tk):
    nq, nk = seq // tq, seq // tk
    qb, kb, qo, ko, fl = [], [], [], [], []
    for b in range(batch):
        for qi in range(nq):
            last = ((qi + 1) * tq - 1) // tk
            for kj in range(last + 1):
                qb.append(b * nq + qi)
                kb.append(b * nk + kj)
                qo.append(qi * tq)
                ko.append(kj * tk)
                diag = (kj + 1) * tk - 1 > qi * tq
                fl.append((FLAG_FIRST if kj == 0 else 0) | (FLAG_LAST if kj == last else 0)
                          | (FLAG_DIAG if diag else 0))
    return [np.asarray(a, np.int32) for a in (qb, kb, qo, ko, fl)]


def _mla_prompt(qp, kp, w_uv_bd, batch, seq):
    tq, tk = min(FLASH_TQ, seq), min(FLASH_TK, seq)
    assert tq & (tq - 1) == 0 and seq % tq == 0 and seq % tk == 0
    steps = _flash_steps(batch, seq, tq, tk)
    n = batch * seq
    rows = H_A * tq
    grid_spec = pltpu.PrefetchScalarGridSpec(
        num_scalar_prefetch=5,
        grid=(steps[0].shape[0],),
        in_specs=[pl.BlockSpec((H_A, tq, QP_W), lambda i, qb, kb, qo, ko, fl: (0, qb[i], 0)),
                  pl.BlockSpec((tk, QP_W), lambda i, qb, kb, qo, ko, fl: (kb[i], 0)),
                  pl.BlockSpec((H_A * KV_LORA, W_A), lambda i, qb, kb, qo, ko, fl: (0, 0))],
        out_specs=pl.BlockSpec((tq, W_A), lambda i, qb, kb, qo, ko, fl: (qb[i], 0)),
        scratch_shapes=[pltpu.VMEM((rows, LANES), F32), pltpu.VMEM((rows, LANES), F32),
                        pltpu.VMEM((rows, KV_LORA), F32)])
    return pl.pallas_call(
        functools.partial(_flash_body, tq=tq, tk=tk),
        grid_spec=grid_spec,
        out_shape=jax.ShapeDtypeStruct((n, W_A), F32),
        compiler_params=_params(("arbitrary",)),
        name="mla_prompt_flash",
    )(*[jnp.asarray(a) for a in steps], qp, kp, w_uv_bd)


PAGES_PER_STEP = 64


def _page_copies(pt_ref, lat_hbm, kr_hbm, lat_buf, kr_buf, lat_sem, kr_sem, e, step, slot, i, npg):
    page = pt_ref[step * npg + i]
    return (pltpu.make_async_copy(lat_hbm.at[e, page], lat_buf.at[slot, i], lat_sem.at[slot]),
            pltpu.make_async_copy(kr_hbm.at[e, page], kr_buf.at[slot, i], kr_sem.at[slot]))


def _paged_body(pt_ref, ql_ref, qr_ref, kself_ref, krself_ref, lat_hbm, kr_hbm, o_ref,
                lat_buf, kr_buf, lat_sem, kr_sem, lat_sc, kr_sc, m_sc, l_sc, acc_sc, *, npg, t_new, e):
    j = pl.program_id(1)
    chunks = pl.num_programs(1)
    step = pl.program_id(0) * chunks + j
    last_step = pl.num_programs(0) * chunks - 1
    slot = step & 1
    copies = functools.partial(_page_copies, pt_ref, lat_hbm, kr_hbm, lat_buf, kr_buf, lat_sem, kr_sem, e)

    @pl.when(step == 0)
    def _():
        for i in range(npg):
            for cp in copies(0, 0, i, npg):
                cp.start()

    @pl.when(j == 0)
    def _():
        _softmax_init(m_sc, l_sc, acc_sc)

    nxt = jnp.minimum(step + 1, last_step)
    for i in range(npg):
        for cp in copies(nxt, 1 - slot, i, npg):
            cp.start()
    for i in range(npg):
        for cp in copies(step, slot, i, npg):
            cp.wait()

    for i in range(npg):
        lat_sc[i * PAGE_SIZE:(i + 1) * PAGE_SIZE, :] = lat_buf[slot, i].astype(BF16)
        kr_sc[:QK_ROPE, i * PAGE_SIZE:(i + 1) * PAGE_SIZE] = kr_buf[slot, i].astype(BF16)

    ql, qr = ql_ref[0], qr_ref[0]

    def attend(lat, kr_t, mask):
        s = (lax.dot_general(ql, lat, NT_DIMS, preferred_element_type=F32)
             + jnp.dot(qr, kr_t, preferred_element_type=F32))
        if mask is not None:
            s = jnp.where(mask, s, -jnp.inf)
        _softmax_update(s, lat, m_sc, l_sc, acc_sc)

    attend(lat_sc[...], kr_sc[...], None)

    @pl.when(j == chunks - 1)
    def _():
        shape = (ql.shape[0], PAGE_SIZE)
        t_row = lax.broadcasted_iota(jnp.int32, shape, 0) & (t_new - 1)
        attend(kself_ref[0], krself_ref[0], lax.broadcasted_iota(jnp.int32, shape, 1) <= t_row)
        o_ref[0] = _softmax_result(l_sc, acc_sc)

    @pl.when(step == last_step)
    def _():
        for i in range(npg):
            for cp in copies(last_step, 1 - slot, i, npg):
                cp.wait()


def _mla_sample(ql, qr, kself, krself, lat_pool, rope_pool_t, page_table, e, t_new):
    batch, rows, _ = ql.shape
    n_pages = page_table.shape[1]
    npg = min(PAGES_PER_STEP, n_pages)
    assert n_pages % npg == 0 and t_new & (t_new - 1) == 0 and t_new <= PAGE_SIZE
    chunks = n_pages // npg
    per_b = lambda b, j, pt: (b, 0, 0)
    grid_spec = pltpu.PrefetchScalarGridSpec(
        num_scalar_prefetch=1,
        grid=(batch, chunks),
        in_specs=[pl.BlockSpec((1, rows, KV_LORA), per_b), pl.BlockSpec((1, rows, QK_ROPE), per_b),
                  pl.BlockSpec((1, PAGE_SIZE, KV_LORA), per_b), pl.BlockSpec((1, QK_ROPE, PAGE_SIZE), per_b),
                  pl.BlockSpec(memory_space=pl.ANY), pl.BlockSpec(memory_space=pl.ANY)],
        out_specs=pl.BlockSpec((1, rows, KV_LORA), per_b),
        scratch_shapes=[pltpu.VMEM((2, npg, PAGE_SIZE, KV_LORA), F32),
                        pltpu.VMEM((2, npg, QK_ROPE, PAGE_SIZE), F32),
                        pltpu.SemaphoreType.DMA((2,)), pltpu.SemaphoreType.DMA((2,)),
                        pltpu.VMEM((npg * PAGE_SIZE, KV_LORA), BF16),
                        pltpu.VMEM((QK_ROPE, npg * PAGE_SIZE), BF16),
                        pltpu.VMEM((rows, LANES), F32), pltpu.VMEM((rows, LANES), F32),
                        pltpu.VMEM((rows, KV_LORA), F32)])
    return pl.pallas_call(
        functools.partial(_paged_body, npg=npg, t_new=t_new, e=e),
        grid_spec=grid_spec,
        out_shape=jax.ShapeDtypeStruct((batch, rows, KV_LORA), F32),
        compiler_params=_params(("arbitrary", "arbitrary")),
        name="mla_sample_paged",
    )(page_table.reshape(-1), ql, qr, kself, krself, lat_pool, rope_pool_t)


LRU_TC = 256


def _block_diag(w):
    nb, d, e = w.shape
    return jnp.einsum('nde,nm->ndme', w, jnp.eye(nb, dtype=w.dtype)).reshape(nb * d, nb * e)


def _lru_coeffs(xc, wa, wx, ba, bx, lam):
    xb = xc.astype(BF16)
    r = jax.nn.sigmoid(jnp.dot(xb, wa, preferred_element_type=F32) + ba)
    ig = jax.nn.sigmoid(jnp.dot(xb, wx, preferred_element_type=F32) + bx)
    neg = -lam
    softplus = jnp.maximum(neg, 0.0) + jnp.log1p(jnp.exp(-jnp.abs(neg)))
    log_a = -LRU_C * r * softplus
    a = jnp.exp(log_a)
    t = jnp.tanh(log_a)
    b = jnp.sqrt(-2.0 * t / (1.0 - t)) * (ig * xc)
    return a, b


def _lru_body(x_ref, h0_ref, buf_ref, cw_ref, cb_ref, wa_ref, wx_ref, ba_ref, bx_ref, lam_ref,
              h_ref, hl_ref, tail_ref, xbuf, hc, *, tc):
    c = pl.program_id(1)

    @pl.when(c == 0)
    def _():
        xbuf[0:SUBLANES] = buf_ref[0]
        hc[...] = h0_ref[0]

    x = x_ref[...]
    xbuf[SUBLANES:SUBLANES + tc] = x
    cw = cw_ref[...]
    xc = cb_ref[...] + x * cw[CONV_W - 1:CONV_W]
    for j in range(CONV_W - 1):
        xc = xc + xbuf[pl.ds(SUBLANES - (CONV_W - 1) + j, tc), :] * cw[j:j + 1]
    xbuf[0:SUBLANES] = x[tc - SUBLANES:tc]
    a, b = _lru_coeffs(xc, wa_ref[...], wx_ref[...], ba_ref[...], bx_ref[...], lam_ref[...])
    row = lax.broadcasted_iota(jnp.int32, a.shape, 0)
    d = 1
    while d < tc:
        keep = row >= d
        a_sh = jnp.where(keep, pltpu.roll(a, d, 0), 1.0)
        b_sh = jnp.where(keep, pltpu.roll(b, d, 0), 0.0)
        b = a * b_sh + b
        a = a * a_sh
        d *= 2
    h = a * hc[...] + b
    h_ref[...] = h
    hc[...] = h[tc - 1:tc]

    @pl.when(c == pl.num_programs(1) - 1)
    def _():
        hl_ref[0] = h[tc - 1:tc]
        tail_ref[0] = x[tc - SUBLANES:tc]


def _rglru_seq(zg, batch, seq, h0, buf, conv_w, conv_b, wa_bd, wx_bd, ba, bx, lam):
    tc = min(LRU_TC, seq)
    assert seq % tc == 0 and tc >= SUBLANES
    nc = seq // tc
    buf8 = jnp.pad(buf, ((0, 0), (SUBLANES - (CONV_W - 1), 0), (0, 0)))
    vec = lambda a: a.reshape(1, W_B)
    fixed = lambda b, c: (0, 0)
    per_b = lambda b, c: (b, 0, 0)
    h, hl, tail = pl.pallas_call(
        functools.partial(_lru_body, tc=tc),
        grid=(batch, nc),
        in_specs=[pl.BlockSpec((tc, W_B), lambda b, c: (b * nc + c, ZE_XB // W_B)),
                  pl.BlockSpec((1, 1, W_B), per_b),
                  pl.BlockSpec((1, SUBLANES, W_B), per_b),
                  pl.BlockSpec((CONV_W, W_B), fixed),
                  pl.BlockSpec((1, W_B), fixed),
                  pl.BlockSpec((W_B, W_B), fixed), pl.BlockSpec((W_B, W_B), fixed),
                  pl.BlockSpec((1, W_B), fixed), pl.BlockSpec((1, W_B), fixed), pl.BlockSpec((1, W_B), fixed)],
        out_specs=[pl.BlockSpec((tc, W_B), lambda b, c: (b * nc + c, 0)),
                   pl.BlockSpec((1, 1, W_B), per_b),
                   pl.BlockSpec((1, SUBLANES, W_B), per_b)],
        out_shape=[jax.ShapeDtypeStruct((batch * seq, W_B), F32),
                   jax.ShapeDtypeStruct((batch, 1, W_B), F32),
                   jax.ShapeDtypeStruct((batch, SUBLANES, W_B), F32)],
        scratch_shapes=[pltpu.VMEM((SUBLANES + tc, W_B), F32), pltpu.VMEM((1, W_B), F32)],
        compiler_params=_params(("arbitrary", "arbitrary")),
        name="rglru_seq",
    )(zg, h0.reshape(batch, 1, W_B), buf8, conv_w, vec(conv_b), wa_bd, wx_bd, vec(ba), vec(bx), vec(lam))
    return h, hl[:, 0], tail[:, SUBLANES - (CONV_W - 1):]


def _out_body(v1_ref, v2_ref, v3_ref, g1_ref, g2_ref, g3_ref, x_ref, w_ref, lg_ref, lb_ref, o_ref):
    def gated(v_ref, g_ref):
        g = g_ref[...]
        return (v_ref[...] * (g * jax.nn.sigmoid(g))).astype(BF16)

    mixed = jnp.concatenate([gated(v1_ref, g1_ref), gated(v2_ref, g2_ref), gated(v3_ref, g3_ref)], axis=1)
    u = DEEPNORM_ALPHA * x_ref[...] + jnp.dot(mixed, w_ref[...], preferred_element_type=F32)
    mu = jnp.mean(u, axis=-1, keepdims=True)
    var = jnp.mean(jnp.square(u - mu), axis=-1, keepdims=True)
    o_ref[...] = (u - mu) * lax.rsqrt(var + NORM_EPS) * lg_ref[...] + lb_ref[...]


def _out_proj_norm(vals, z, gate_cols, x2d, w_out, ln_g, ln_b):
    n = x2d.shape[0]
    tm = min(256, n)
    widths = [v.shape[1] for v in vals]
    row = lambda i: (i, 0)
    fixed = lambda i: (0, 0)
    in_specs = [pl.BlockSpec((tm, w), row) for w in widths]
    for w, off in zip(widths, gate_cols):
        assert off % w == 0
        in_specs.append(pl.BlockSpec((tm, w), functools.partial(lambda i, cb: (i, cb), cb=off // w)))
    in_specs += [pl.BlockSpec((tm, D_MODEL), row),
                 pl.BlockSpec((sum(widths), D_MODEL), fixed),
                 pl.BlockSpec((1, D_MODEL), fixed), pl.BlockSpec((1, D_MODEL), fixed)]
    return pl.pallas_call(
        _out_body,
        grid=(n // tm,),
        in_specs=in_specs,
        out_specs=pl.BlockSpec((tm, D_MODEL), row),
        out_shape=jax.ShapeDtypeStruct((n, D_MODEL), F32),
        compiler_params=_params(("arbitrary",)),
        name="out_proj_norm",
    )(*vals, z, z, z, x2d, w_out.astype(BF16), ln_g.reshape(1, -1), ln_b.reshape(1, -1))


def _mem_attend_rows(q, k_t, v_t):
    rows = q.shape[0]
    q_head = lax.shift_right_logical(lax.broadcasted_iota(jnp.int32, q.shape, 1), DH_M.bit_length() - 1)
    qs = jnp.concatenate([jnp.where(q_head == h, q, 0.0) for h in range(H_M)], axis=0).astype(BF16)
    s = jnp.dot(qs, k_t.astype(BF16), preferred_element_type=F32) * (DH_M ** -0.5)
    p = jnp.exp(s - jnp.max(s, axis=-1, keepdims=True))
    p = p / jnp.sum(p, axis=-1, keepdims=True)
    y = lax.dot_general(p.astype(BF16), v_t.astype(BF16), NT_DIMS, preferred_element_type=F32)
    out = jnp.where(q_head == 0, y[0:rows], 0.0)
    for h in range(1, H_M):
        out = out + jnp.where(q_head == h, y[h * rows:(h + 1) * rows], 0.0)
    return out


def _mem_body(q_ref, k_ref, v_ref, o_ref):
    for i in range(k_ref.shape[1]):
        q = q_ref[i] if len(q_ref.shape) == 3 else q_ref[...]
        y = _mem_attend_rows(q, k_ref[0, i], v_ref[0, i])
        if len(o_ref.shape) == 3:
            o_ref[i] = y
        else:
            o_ref[...] = y


def _mem_transposed(mem):
    t = jnp.moveaxis(mem, -3, -1)
    t = t.reshape(t.shape[:-3] + (W_M, N_MEM))
    return t if t.ndim == 4 else t[None]


def _mem_attend_long(z, col, batch, seq, mem_kt, mem_vt, layer):
    tm = min(512, seq)
    nt = seq // tm
    kv = lambda b, i: (layer, b, 0, 0)
    return pl.pallas_call(
        _mem_body,
        grid=(batch, nt),
        in_specs=[pl.BlockSpec((tm, W_M), lambda b, i: (b * nt + i, col // W_M)),
                  pl.BlockSpec((1, 1, W_M, N_MEM), kv), pl.BlockSpec((1, 1, W_M, N_MEM), kv)],
        out_specs=pl.BlockSpec((tm, W_M), lambda b, i: (b * nt + i, 0)),
        out_shape=jax.ShapeDtypeStruct((batch * seq, W_M), F32),
        compiler_params=_params(("arbitrary", "arbitrary")),
        name="mem_attend_long",
    )(z, mem_kt, mem_vt)


MEM_SHORT_NB = 8


def _mem_attend_short(z, col, batch, seq, mem_kt, mem_vt, layer):
    assert seq <= SUBLANES
    nb = min(MEM_SHORT_NB, batch)
    q = jnp.pad(z[:, col:col + W_M].reshape(batch, seq, W_M), ((0, 0), (0, SUBLANES - seq), (0, 0)))
    blk = lambda i: (i, 0, 0)
    kv = lambda i: (layer, i, 0, 0)
    y = pl.pallas_call(
        _mem_body,
        grid=(batch // nb,),
        in_specs=[pl.BlockSpec((nb, SUBLANES, W_M), blk),
                  pl.BlockSpec((1, nb, W_M, N_MEM), kv), pl.BlockSpec((1, nb, W_M, N_MEM), kv)],
        out_specs=pl.BlockSpec((nb, SUBLANES, W_M), blk),
        out_shape=jax.ShapeDtypeStruct((batch, SUBLANES, W_M), F32),
        compiler_params=_params(("arbitrary",)),
        name="mem_attend_short",
    )(q, mem_kt, mem_vt)
    return y[:, :seq].reshape(batch * seq, W_M)


def _chunk_mlp_body(u_ref, v_ref, g_ref, b_ref, w_ref, bias_ref, y_ref, vn_ref, *, rows, chunks):
    gw = W_D // G_D
    for c in range(chunks):
        sl = pl.ds(c * rows, rows)
        v = v_ref[sl, :]
        mu = jnp.mean(v, axis=-1, keepdims=True)
        var = jnp.mean(jnp.square(v - mu), axis=-1, keepdims=True)
        vn = (v - mu) * lax.rsqrt(var + NORM_EPS) * g_ref[...] + b_ref[...]
        vn_ref[sl, :] = vn
        vb = vn.astype(BF16)
        s = jnp.concatenate([jnp.dot(w_ref[g], vb[:, g * gw:(g + 1) * gw], preferred_element_type=F32)
                             for g in range(G_D)], axis=1)
        y_ref[sl, :] = u_ref[sl, :] * (s + bias_ref[...])


def _chunk_mlp_call(z, rows, ln_g_d, ln_b_d, w_mix, bias):
    n = z.shape[0]
    chunks = max(1, min(4, n // rows))
    tm = rows * chunks
    row = lambda i: (i, 0)
    fixed = lambda i: (0, 0)
    return pl.pallas_call(
        functools.partial(_chunk_mlp_body, rows=rows, chunks=chunks),
        grid=(n // tm,),
        in_specs=[pl.BlockSpec((tm, W_D), lambda i: (i, ZO_U // W_D)),
                  pl.BlockSpec((tm, W_D), lambda i: (i, ZO_VD // W_D)),
                  pl.BlockSpec((1, W_D), fixed), pl.BlockSpec((1, W_D), fixed),
                  pl.BlockSpec((G_D, rows, rows), lambda i: (0, 0, 0)),
                  pl.BlockSpec((rows, W_D), fixed)],
        out_specs=[pl.BlockSpec((tm, W_D), row), pl.BlockSpec((tm, W_D), row)],
        out_shape=[jax.ShapeDtypeStruct((n, W_D), F32), jax.ShapeDtypeStruct((n, W_D), F32)],
        compiler_params=_params(("arbitrary",)),
        name="chunk_mlp",
    )(z, z, ln_g_d.reshape(1, -1), ln_b_d.reshape(1, -1), w_mix, bias)


def _chunk_mlp_weights(sg_w, sg_b):
    L = CHUNK_D
    w = jnp.where(jnp.tril(jnp.ones((L, L), dtype=bool)), sg_w[:, :L, :L], 0.0)
    return w.astype(BF16), jnp.repeat(sg_b[:, :L].T, W_D // G_D, axis=1)


def _chunk_mlp_short_body(u_ref, v_ref, g_ref, b_ref, w_ref, bias_ref, y_ref, vn_ref, *, batch, seq):
    v = v_ref[...]
    mu = jnp.mean(v, axis=-1, keepdims=True)
    var = jnp.mean(jnp.square(v - mu), axis=-1, keepdims=True)
    vn = (v - mu) * lax.rsqrt(var + NORM_EPS) * g_ref[...] + b_ref[...]
    vn_ref[...] = vn
    for t in range(seq):
        acc = vn[0:batch] * w_ref[t * seq:t * seq + 1, :]
        for s in range(1, t + 1):
            acc = acc + vn[s * batch:(s + 1) * batch] * w_ref[t * seq + s:t * seq + s + 1, :]
        rows = slice(t * batch, (t + 1) * batch)
        y_ref[rows, :] = u_ref[rows, :] * (acc + bias_ref[t:t + 1, :])


def _chunk_mlp_short(z, batch, seq, ln_g_d, ln_b_d, sg_w, sg_b):
    gw = W_D // G_D
    w_rows = jnp.repeat(jnp.transpose(sg_w[:, :seq, :seq], (1, 2, 0)).reshape(seq * seq, G_D), gw, axis=1)
    bias = jnp.repeat(sg_b[:, :seq].T, gw, axis=1)

    def time_major(off):
        return jnp.swapaxes(z[:, off:off + W_D].reshape(batch, seq, W_D), 0, 1).reshape(seq * batch, W_D)

    y, vn = pl.pallas_call(
        functools.partial(_chunk_mlp_short_body, batch=batch, seq=seq),
        out_shape=[jax.ShapeDtypeStruct((seq * batch, W_D), F32)] * 2,
        compiler_params=pltpu.CompilerParams(vmem_limit_bytes=VMEM_LIMIT),
        name="chunk_mlp_short",
    )(time_major(ZO_U), time_major(ZO_VD), ln_g_d.reshape(1, -1), ln_b_d.reshape(1, -1), w_rows, bias)
    back = lambda a: jnp.swapaxes(a.reshape(seq, batch, W_D), 0, 1).reshape(batch * seq, W_D)
    return back(y), back(vn)


def _cumsum_rows(x):
    row = lax.broadcasted_iota(jnp.int32, x.shape, 0)
    d = 1
    while d < x.shape[0]:
        x = x + jnp.where(row >= d, pltpu.roll(x, d, 0), 0.0)
        d *= 2
    return x


def _mlstm_body(q_ref, k_ref, v_ref, o_ref, gt_ref, bif_ref, c0_ref, n0_ref, m0_ref,
                y_ref, c_ref, n_ref, m_ref, c_sc, n_sc, m_sc, *stage, nb, n_valid):
    L = MLSTM_CHUNK
    step = pl.program_id(1)
    heads = [slice(h * DH_C, (h + 1) * DH_C) for h in range(H_C)]

    @pl.when(step == 0)
    def _():
        c_sc[...] = jnp.zeros(c_sc.shape, F32)
        for b in range(nb):
            for h, hs in enumerate(heads):
                c_sc[b, hs, hs] = c0_ref[b, h]
                n_sc[b, :, hs] = n0_ref[b, h]
        m_sc[...] = m0_ref[...]
        for st in stage:
            st[...] = jnp.zeros(st.shape, F32)

    shift = DH_C.bit_length() - 1
    row = lax.broadcasted_iota(jnp.int32, (L, LANES), 0)
    lane = lax.broadcasted_iota(jnp.int32, (L, LANES), 1)
    tril = lax.broadcasted_iota(jnp.int32, (L, L), 0) >= lax.broadcasted_iota(jnp.int32, (L, L), 1)
    lane_head = lax.shift_right_logical(lax.broadcasted_iota(jnp.int32, (L, W_C), 1), shift)
    diag_blocks = (lax.shift_right_logical(lax.broadcasted_iota(jnp.int32, (W_C, W_C), 0), shift)
                   == lax.shift_right_logical(lax.broadcasted_iota(jnp.int32, (W_C, W_C), 1), shift))

    def per_head(cols, width):
        return jnp.concatenate([jnp.broadcast_to(c, (c.shape[0], width)) for c in cols], axis=1)

    for b in range(nb):
        if stage:
            rows_in = q_ref.shape[1]
            bufs = []
            for st, ref in zip(stage, (q_ref, k_ref, v_ref, o_ref, gt_ref)):
                st[b, 0:rows_in, :] = ref[b]
                bufs.append(st[b])
            q, k, v, o_pre, gts = bufs
        else:
            q, k, v, o_pre, gts = q_ref[b], k_ref[b], v_ref[b], o_ref[b], gt_ref[b]
        gz = gts + bif_ref[...]
        lf = jnp.minimum(gz, 0.0) - jnp.log1p(jnp.exp(-jnp.abs(gz)))
        ig = gz
        if n_valid < L:
            ig = jnp.where(row < n_valid, ig, -jnp.inf)
            lf = jnp.where(row < n_valid, lf, 0.0)
        a = jnp.where(lane < H_C, ig, _cumsum_rows(lf))
        a_t = a.T
        ks = k * (DH_C ** -0.5)
        vb = v.astype(BF16)
        q_stack = jnp.concatenate([jnp.where(lane_head == h, q, 0.0) for h in range(H_C)], axis=0)
        qk = lax.dot_general(q_stack.astype(BF16), ks.astype(BF16), NT_DIMS, preferred_element_type=F32)
        ws, mts, g_inters, wgs, decays = [], [], [], [], []
        for h in range(H_C):
            ig_row, bc_row = a_t[h:h + 1, :], a_t[H_C + h:H_C + h + 1, :]
            ig_col, bc_col = a[:, h:h + 1], a[:, H_C + h:H_C + h + 1]
            m_old = m_sc[b, h][:, 0:1]
            dmat = jnp.where(tril, bc_col - bc_row + ig_row, -jnp.inf)
            inter = bc_col + m_old
            mt = jnp.maximum(inter, jnp.max(dmat, axis=-1, keepdims=True))
            ws.append(jnp.exp(dmat - mt))
            mts.append(mt)
            g_inters.append(jnp.exp(inter - mt))
            b_end = bc_col[L - 1:L, :]
            g_col = b_end - bc_col + ig_col
            m_new = jnp.maximum(b_end + m_old, jnp.max(g_col, axis=0, keepdims=True))
            decays.append(jnp.exp(b_end + m_old - m_new))
            wgs.append(jnp.exp(g_col - m_new))
            m_sc[b, h] = jnp.broadcast_to(m_new, (1, LANES))
        sc = qk * jnp.concatenate(ws, axis=0)
        sv = jnp.dot(sc.astype(BF16), vb, preferred_element_type=F32)
        num = jnp.where(lane_head == 0, sv[0:L], 0.0)
        for h in range(1, H_C):
            num = num + jnp.where(lane_head == h, sv[h * L:(h + 1) * L], 0.0)
        c_old = c_sc[b]
        n_old = n_sc[b]
        num = num + per_head(g_inters, DH_C) * jnp.dot(q.astype(BF16), c_old.astype(BF16),
                                                        preferred_element_type=F32)
        row_sum = jnp.sum(sc, axis=-1, keepdims=True)
        qn = q * n_old
        dens = []
        for h, hs in enumerate(heads):
            den = row_sum[h * L:(h + 1) * L] + g_inters[h] * jnp.sum(qn[:, hs], axis=-1, keepdims=True)
            dens.append(jnp.maximum(jnp.abs(den), jnp.exp(-mts[h])))
        y = jax.nn.sigmoid(o_pre) * (num / per_head(dens, DH_C))
        y_ref[b] = y[0:y_ref.shape[1]]
        kw = ks * per_head(wgs, DH_C)
        decay = per_head(decays, DH_C)
        upd = jnp.dot(kw.T.astype(BF16), vb, preferred_element_type=F32)
        c_sc[b] = decay * c_old + jnp.where(diag_blocks, upd, 0.0)
        n_sc[b] = decay * n_old + jnp.sum(kw, axis=0, keepdims=True)

    @pl.when(step == pl.num_programs(1) - 1)
    def _():
        for b in range(nb):
            for h, hs in enumerate(heads):
                c_ref[b, h] = c_sc[b, hs, hs]
                n_ref[b, h] = n_sc[b, :, hs]
        m_ref[...] = m_sc[...]


MLSTM_NB = 2


def _mlstm_call(z, batch, seq, b_if, c0, n0, m0):
    L = MLSTM_CHUNK
    nb = min(MLSTM_NB, batch)
    long = seq % L == 0
    if long:
        rows, nc, n_valid = L, seq // L, L
        z3 = z.reshape(batch, seq, z.shape[1])
    else:
        assert seq <= SUBLANES
        rows, nc, n_valid = SUBLANES, 1, seq
        z3 = jnp.pad(z.reshape(batch, seq, z.shape[1]), ((0, 0), (0, SUBLANES - seq), (0, 0)))
    bif = jnp.pad(b_if, (0, LANES - 2 * H_C)).reshape(1, LANES)
    m0r = jnp.broadcast_to(m0[:, :, None, None], (batch, H_C, 1, LANES))

    def col(off, w):
        return pl.BlockSpec((nb, rows, w), functools.partial(lambda g, c, cb: (g, c, cb), cb=off // w))

    st4 = lambda g, c: (g, 0, 0, 0)
    in_specs = [col(ZO_Q, W_C), col(ZO_K, W_C), col(ZO_V, W_C), col(ZO_O, W_C), col(ZO_IF, LANES),
                pl.BlockSpec((1, LANES), lambda g, c: (0, 0)),
                pl.BlockSpec((nb, H_C, DH_C, DH_C), st4),
                pl.BlockSpec((nb, H_C, 1, DH_C), st4),
                pl.BlockSpec((nb, H_C, 1, LANES), st4)]
    out_specs = [pl.BlockSpec((nb, rows, W_C), lambda g, c: (g, c, 0)),
                 pl.BlockSpec((nb, H_C, DH_C, DH_C), st4),
                 pl.BlockSpec((nb, H_C, 1, DH_C), st4),
                 pl.BlockSpec((nb, H_C, 1, LANES), st4)]
    scratch = [pltpu.VMEM((nb, W_C, W_C), F32), pltpu.VMEM((nb, 1, W_C), F32),
               pltpu.VMEM((nb, H_C, 1, LANES), F32)]
    if not long:
        scratch += [pltpu.VMEM((nb, L, W_C), F32)] * 4 + [pltpu.VMEM((nb, L, LANES), F32)]
    y, c, n, m = pl.pallas_call(
        functools.partial(_mlstm_body, nb=nb, n_valid=n_valid),
        grid=(batch // nb, nc),
        in_specs=in_specs,
        out_specs=out_specs,
        out_shape=[jax.ShapeDtypeStruct((batch, rows * nc, W_C), F32),
                   jax.ShapeDtypeStruct((batch, H_C, DH_C, DH_C), F32),
                   jax.ShapeDtypeStruct((batch, H_C, 1, DH_C), F32),
                   jax.ShapeDtypeStruct((batch, H_C, 1, LANES), F32)],
        scratch_shapes=scratch,
        compiler_params=_params(("arbitrary", "arbitrary")),
        name="mlstm_chunks",
    )(z3, z3, z3, z3, z3, bif, c0, n0[:, :, None, :], m0r)
    return y[:, :seq].reshape(batch * seq, W_C), c, n[:, :, 0], m[:, :, 0, 0]


def _lru_short_body(x_ref, h0_ref, buf_ref, cw_ref, cb_ref, wa_ref, wx_ref, ba_ref, bx_ref, lam_ref,
                    h_ref, *, batch, seq):
    cw = cw_ref[...]
    xp = jnp.concatenate([buf_ref[...], x_ref[...]], axis=0)
    xc = cb_ref[...] + xp[(CONV_W - 1) * batch:] * cw[CONV_W - 1:CONV_W]
    for j in range(CONV_W - 1):
        xc = xc + xp[j * batch:(j + seq) * batch] * cw[j:j + 1]
    a, b = _lru_coeffs(xc, wa_ref[...], wx_ref[...], ba_ref[...], bx_ref[...], lam_ref[...])
    h = h0_ref[...]
    for t in range(seq):
        sl = slice(t * batch, (t + 1) * batch)
        h = a[sl] * h + b[sl]
        h_ref[sl, :] = h


def _rglru_short(zg, batch, seq, h0, buf, conv_w, conv_b, wa_bd, wx_bd, ba, bx, lam):
    assert batch % SUBLANES == 0
    x = zg[:, ZE_XB:ZE_XB + W_B].reshape(batch, seq, W_B)
    xp = jnp.concatenate([buf, x], axis=1)
    x_tm = jnp.swapaxes(x, 0, 1).reshape(seq * batch, W_B)
    buf_tm = jnp.swapaxes(buf, 0, 1).reshape((CONV_W - 1) * batch, W_B)
    vec = lambda a: a.reshape(1, W_B)
    h_tm = pl.pallas_call(
        functools.partial(_lru_short_body, batch=batch, seq=seq),
        out_shape=jax.ShapeDtypeStruct((seq * batch, W_B), F32),
        compiler_params=pltpu.CompilerParams(vmem_limit_bytes=VMEM_LIMIT),
        name="rglru_short",
    )(x_tm, h0, buf_tm, conv_w, vec(conv_b), wa_bd, wx_bd, vec(ba), vec(bx), vec(lam))
    h = jnp.swapaxes(h_tm.reshape(seq, batch, W_B), 0, 1)
    return h.reshape(batch * seq, W_B), h[:, -1], xp[:, -(CONV_W - 1):]


def _mem_kv(mem, w_mk, w_mv):
    B = mem.shape[0]
    kv = _proj(mem, jnp.concatenate([w_mk, w_mv], axis=1))
    return (kv[..., :W_M].reshape(B, N_MEM, H_M, DH_M), kv[..., W_M:].reshape(B, N_MEM, H_M, DH_M))


def _sample_queries(qp, batch, t_new):
    n = qp.shape[1]
    r1 = jnp.stack([qp[h, :, KV_LORA + h * ROPE_HALF:KV_LORA + (h + 1) * ROPE_HALF] for h in range(H_A)])
    r2 = jnp.stack([qp[h, :, KV_LORA + LANES + h * ROPE_HALF:KV_LORA + LANES + (h + 1) * ROPE_HALF]
                    for h in range(H_A)])
    qr = jnp.concatenate([r1, r2], axis=-1)

    def rows(a):
        w = a.shape[-1]
        return a.reshape(H_A, batch, t_new, w).transpose(1, 0, 2, 3).reshape(batch, H_A * t_new, w)

    return rows(qp[:, :, :KV_LORA]), rows(qr)


def _sample_self_keys(latent, k_rope, batch, t_new):
    kl = jnp.pad(latent.astype(BF16).reshape(batch, t_new, KV_LORA), ((0, 0), (0, PAGE_SIZE - t_new), (0, 0)))
    kr_t = jnp.swapaxes(k_rope.astype(BF16).reshape(batch, t_new, QK_ROPE), 1, 2)
    return kl, jnp.pad(kr_t, ((0, 0), (0, 0), (0, PAGE_SIZE - t_new)))


def _even_layer(x2d, batch, seq, tables, mem, lru_h0, lru_buf, paged, e, weights,
                q_norm, kv_norm, conv_w, conv_b, wa, ba, wx, bx, lam, w_out, ln_g, ln_b):
    w_in_r, w_uq_r, w_uk_r, w_uv_bd = weights
    zg, latent, k_rope, kp, qp = _even_in(x2d, tables, w_in_r, q_norm, kv_norm, w_uq_r, w_uk_r)
    lru_args = (lru_h0, lru_buf, conv_w, conv_b, _block_diag(wa).astype(BF16), _block_diag(wx).astype(BF16),
                ba, bx, lam)
    if paged is None:
        y_a = _mla_prompt(qp, kp, w_uv_bd, batch, seq)
        h_b, h_last, new_buf = _rglru_seq(zg, batch, seq, *lru_args)
        y_m = _mem_attend_long(zg, ZE_QM, batch, seq, *mem)
    else:
        lat_pool, rope_pool, page_table = paged
        o = _mla_sample(*_sample_queries(qp, batch, seq), *_sample_self_keys(latent, k_rope, batch, seq),
                        lat_pool, jnp.swapaxes(rope_pool, 2, 3), page_table, e, seq)
        o = o.reshape(batch, H_A, seq, KV_LORA).transpose(0, 2, 1, 3).reshape(batch * seq, H_A * KV_LORA)
        y_a = _matmul(o, w_uv_bd)
        h_b, h_last, new_buf = _rglru_short(zg, batch, seq, *lru_args)
        y_m = _mem_attend_short(zg, ZE_QM, batch, seq, *mem)
    x_new = _out_proj_norm([y_a, h_b, y_m], zg, (ZE_GA, ZE_GB, ZE_GM), x2d, w_out, ln_g, ln_b)
    return (x_new, latent.reshape(batch, seq, KV_LORA), k_rope.reshape(batch, seq, QK_ROPE), h_last, new_buf)


def _odd_weights(w_in):
    q, k, v, i_pre, f_pre, o_pre, g_c, u_d, v_d, g_d, q_m, g_m = _split_cols(w_in, ODD_SPLITS)
    gates = jnp.pad(jnp.concatenate([i_pre, f_pre], axis=1), ((0, 0), (0, LANES - 2 * H_C)))
    return jnp.concatenate([q, k, v, o_pre, g_c, u_d, v_d, g_d, q_m, g_m, gates], axis=1)


def _odd_layer(x2d, batch, seq, mem, c0, n0, m0, w_in_r, b_if, ln_g_d, ln_b_d, sg_w, sg_b,
               w_out, ln_g, ln_b):
    z = _matmul(x2d, w_in_r)
    y_c, c, n, m = _mlstm_call(z, batch, seq, b_if, c0, n0, m0)
    if seq % CHUNK_D == 0:
        y_d, vn = _chunk_mlp_call(z, CHUNK_D, ln_g_d, ln_b_d, *_chunk_mlp_weights(sg_w, sg_b))
        y_m = _mem_attend_long(z, ZO_QM, batch, seq, *mem)
    else:
        y_d, vn = _chunk_mlp_short(z, batch, seq, ln_g_d, ln_b_d, sg_w, sg_b)
        y_m = _mem_attend_short(z, ZO_QM, batch, seq, *mem)
    x_new = _out_proj_norm([y_c, y_d, y_m], z, (ZO_GC, ZO_GD, ZO_GM), x2d, w_out, ln_g, ln_b)
    return x_new, vn.reshape(batch, seq, W_D), c, n, m


def kernel(x_prompt, x_sample, cache_mla_latent, cache_mla_krope, state_lru_h, state_lru_conv,
           state_mlstm_c, state_mlstm_n, state_mlstm_m, cache_mem_k, cache_mem_v, page_table,
           mem_prompt, w_in_even, mla_q_norm, mla_kv_norm, w_uq, w_uk, w_uv,
           lru_conv_w, lru_conv_b, lru_wa, lru_ba, lru_wx, lru_bx, lru_lambda, w_out_even,
           w_in_odd, mlstm_b_if, sg_ln_g, sg_ln_b, sg_w, sg_b, w_out_odd,
           w_mem_k, w_mem_v, ln_g, ln_b):
    Bp, Tp, _ = x_prompt.shape
    Bs, Ts, _ = x_sample.shape
    past_len = page_table.shape[1] * PAGE_SIZE
    tables_p = _rope_tables(jnp.arange(Tp, dtype=F32))
    tables_s = tuple(jnp.tile(t, (Bs, 1)) for t in _rope_tables(past_len + jnp.arange(Ts, dtype=F32)))

    h0_p = jnp.zeros((Bp, W_B), F32)
    buf0_p = jnp.zeros((Bp, CONV_W - 1, W_B), F32)
    c0_p = jnp.zeros((Bp, H_C, DH_C, DH_C), F32)
    n0_p = jnp.zeros((Bp, H_C, DH_C), F32)
    m0_p = jnp.zeros((Bp, H_C), F32)

    lat_p, kr_p, h_p, conv_p, c_p, n_p, m_p, mk_p, mv_p = [], [], [], [], [], [], [], [], []
    lat_s, kr_s, h_s, conv_s, c_s, n_s, m_s, v_s = [], [], [], [], [], [], [], []

    xp = x_prompt.reshape(Bp * Tp, D_MODEL)
    xs = x_sample.reshape(Bs * Ts, D_MODEL)
    mem_s = (_mem_transposed(cache_mem_k), _mem_transposed(cache_mem_v))
    for l in range(DEPTH):
        mk_l, mv_l = _mem_kv(mem_prompt, w_mem_k[l], w_mem_v[l])
        mk_p.append(mk_l)
        mv_p.append(mv_l)
        mem_p = (_mem_transposed(mk_l), _mem_transposed(mv_l), 0)
        if l % 2 == 0:
            e = l // 2
            weights = _even_weights(w_in_even[e], w_uq[e], w_uk[e], w_uv[e])
            rest = (mla_q_norm[e], mla_kv_norm[e], lru_conv_w[e], lru_conv_b[e], lru_wa[e], lru_ba[e],
                    lru_wx[e], lru_bx[e], lru_lambda[e], w_out_even[e], ln_g[l], ln_b[l])
            xp, la, kr, hl, cb = _even_layer(xp, Bp, Tp, tables_p, mem_p, h0_p, buf0_p, None, e,
                                             weights, *rest)
            lat_p.append(la); kr_p.append(kr); h_p.append(hl); conv_p.append(cb)
            xs, la, kr, hl, cb = _even_layer(xs, Bs, Ts, tables_s, mem_s + (l,),
                                             state_lru_h[e], state_lru_conv[e],
                                             (cache_mla_latent, cache_mla_krope, page_table), e, weights, *rest)
            lat_s.append(la); kr_s.append(kr); h_s.append(hl); conv_s.append(cb)
        else:
            o = l // 2
            ow = (_odd_weights(w_in_odd[o]), mlstm_b_if[o], sg_ln_g[o], sg_ln_b[o], sg_w[o], sg_b[o],
                  w_out_odd[o], ln_g[l], ln_b[l])
            xp, _, cc, nn, mm = _odd_layer(xp, Bp, Tp, mem_p, c0_p, n0_p, m0_p, *ow)
            c_p.append(cc); n_p.append(nn); m_p.append(mm)
            xs, vn, cc, nn, mm = _odd_layer(xs, Bs, Ts, mem_s + (l,),
                                            state_mlstm_c[o], state_mlstm_n[o], state_mlstm_m[o], *ow)
            c_s.append(cc); n_s.append(nn); m_s.append(mm); v_s.append(vn)

    return (xp.reshape(Bp, Tp, D_MODEL), xs.reshape(Bs, Ts, D_MODEL),
            jnp.stack(lat_p), jnp.stack(kr_p), jnp.stack(h_p), jnp.stack(conv_p),
            jnp.stack(c_p), jnp.stack(n_p), jnp.stack(m_p), jnp.stack(mk_p), jnp.stack(mv_p),
            jnp.stack(lat_s), jnp.stack(kr_s), jnp.stack(h_s), jnp.stack(conv_s),
            jnp.stack(c_s), jnp.stack(n_s), jnp.stack(m_s), jnp.stack(v_s))
```

```python
import functools

import jax
import jax.numpy as jnp
import numpy as np
from jax import lax
from jax.experimental import pallas as pl
from jax.experimental.pallas import tpu as pltpu

D_MODEL = 1024
DEPTH = 2
PAGE_SIZE = 128
H_A = 8
Q_LORA = 384
KV_LORA = 256
QK_NOPE = 64
QK_ROPE = 32
ROPE_HALF = QK_ROPE // 2
V_HEAD = 64
W_A = H_A * V_HEAD
ROPE_THETA = 10000.0
MLA_SCALE = (QK_NOPE + QK_ROPE) ** -0.5
W_B = 512
NB_B = 8
BD_B = W_B // NB_B
CONV_W = 4
LRU_C = 8.0
H_C = 4
DH_C = 128
W_C = H_C * DH_C
MLSTM_CHUNK = 128
G_D = 4
W_D = 512
CHUNK_D = 128
N_MEM = 256
H_M = 4
DH_M = 64
W_M = H_M * DH_M
NORM_EPS = 1e-6
DEEPNORM_ALPHA = (2 * DEPTH) ** 0.25

EVEN_SPLITS = (Q_LORA, KV_LORA, QK_ROPE, W_A, W_B, W_B, W_M, W_M)
ODD_SPLITS = (W_C, W_C, W_C, H_C, H_C, W_C, W_C, W_D, W_D, W_D, W_M, W_M)

F32 = jnp.float32
BF16 = jnp.bfloat16
LANES = 128
SUBLANES = 8
VMEM_LIMIT = 48 * 1024 * 1024
LOG2E = 1.4426950408889634
NT_DIMS = (((1,), (1,)), ((), ()))

ZE_GA, ZE_XB, ZE_GB, ZE_QM, ZE_GM = 0, 512, 1024, 1536, 1792
ZE_GATES = 2048
ZE_CKV = 2048
ZE_CQ = ZE_CKV + KV_LORA
ZE_KR1 = ZE_CQ + Q_LORA
ZE_KR2 = ZE_KR1 + LANES
ZE_KRN = ZE_KR2 + LANES
ZE_KRS = ZE_KRN + LANES
ZE_W = ZE_KRS + LANES
QP_W = KV_LORA + 2 * LANES
UQ_NOPE_W = H_A * LANES
ZO_Q, ZO_K, ZO_V, ZO_O, ZO_GC, ZO_U, ZO_VD, ZO_GD = (i * 512 for i in range(8))
ZO_QM, ZO_GM, ZO_IF = 4096, 4352, 4608
ZO_W = ZO_IF + LANES


def _split_cols(z, sizes):
    cuts = [int(c) for c in np.cumsum(sizes)[:-1]]
    return jnp.split(z, cuts, axis=-1)


def _params(sem):
    return pltpu.CompilerParams(dimension_semantics=sem, vmem_limit_bytes=VMEM_LIMIT)


def _mm_body(x_ref, w_ref, o_ref):
    o_ref[...] = jnp.dot(x_ref[...].astype(BF16), w_ref[...], preferred_element_type=F32)


def _row_tile(m, n):
    tm = 512
    while tm > SUBLANES and (tm * n * 4 * 2 > 10 * 1024 * 1024 or m % tm):
        tm //= 2
    return tm


def _matmul(x, w):
    m, k = x.shape
    n = w.shape[1]
    n_pad = -n % LANES
    wb = w.astype(BF16)
    if n_pad:
        wb = jnp.pad(wb, ((0, 0), (0, n_pad)))
    np_ = n + n_pad
    tm = _row_tile(m, np_)
    out = pl.pallas_call(
        _mm_body,
        grid=(m // tm,),
        in_specs=[pl.BlockSpec((tm, k), lambda i: (i, 0)),
                  pl.BlockSpec((k, np_), lambda i: (0, 0))],
        out_specs=pl.BlockSpec((tm, np_), lambda i: (i, 0)),
        out_shape=jax.ShapeDtypeStruct((m, np_), F32),
        compiler_params=_params(("arbitrary",)),
        name="row_matmul",
    )(x, wb)
    return out[:, :n] if n_pad else out


def _proj(x, w):
    lead = x.shape[:-1]
    return _matmul(x.reshape(-1, x.shape[-1]), w).reshape(lead + (w.shape[1],))


def _even_weights(w_in, w_uq, w_uk, w_uv):
    c_q, c_kv, kr, g_a, x_b, g_b, q_m, g_m = _split_cols(w_in, EVEN_SPLITS)
    x1, x2 = kr[:, :ROPE_HALF], kr[:, ROPE_HALF:]

    def lane_pad(a):
        return jnp.pad(a, ((0, 0), (0, LANES - a.shape[1])))

    w_in_r = jnp.concatenate(
        [g_a, x_b, g_b, q_m, g_m, c_kv, c_q, jnp.tile(x1, (1, H_A)), jnp.tile(x2, (1, H_A)),
         lane_pad(kr), lane_pad(jnp.concatenate([x2, x1], axis=1))], axis=1).astype(BF16)
    r = w_uq.reshape(Q_LORA, H_A, QK_NOPE + QK_ROPE)
    nope = jnp.pad(r[:, :, :QK_NOPE], ((0, 0), (0, 0), (0, LANES - QK_NOPE))).reshape(Q_LORA, UQ_NOPE_W)
    r1 = r[:, :, QK_NOPE:QK_NOPE + ROPE_HALF].reshape(Q_LORA, LANES)
    r2 = r[:, :, QK_NOPE + ROPE_HALF:].reshape(Q_LORA, LANES)
    w_uq_r = jnp.concatenate([nope, r1, r2], axis=1).astype(BF16)
    w_uk_r = jnp.pad(jnp.transpose(w_uk, (1, 2, 0)), ((0, 0), (0, LANES - QK_NOPE), (0, 0))).astype(BF16)
    eye = jnp.eye(H_A, dtype=w_uv.dtype)
    w_uv_bd = jnp.einsum('chv,hg->hcgv', w_uv, eye).reshape(H_A * KV_LORA, W_A).astype(BF16)
    return w_in_r, w_uq_r, w_uk_r, w_uv_bd


def _rope_tables(pos):
    inv = ROPE_THETA ** (-jnp.arange(ROPE_HALF, dtype=F32) / ROPE_HALF)
    ang = pos.astype(F32)[:, None] * inv[None, :]
    cos, sin = jnp.cos(ang), jnp.sin(ang)
    zpad = jnp.zeros((pos.shape[0], LANES - QK_ROPE), F32)
    return (jnp.tile(cos, (1, H_A)), jnp.tile(sin, (1, H_A)),
            jnp.concatenate([cos, cos, zpad], axis=1), jnp.concatenate([-sin, sin, zpad], axis=1))


def _rms(x, g):
    return x * lax.rsqrt(jnp.mean(x * x, axis=-1, keepdims=True) + NORM_EPS) * g


def _even_in_body(x_ref, w_ref, qn_ref, kvn_ref, wuq_ref, wuk_ref, cos_ref, sin_ref, cosn_ref, sinn_ref,
                  zg_ref, lat_ref, kr_ref, kp_ref, qp_ref):
    z = jnp.dot(x_ref[...].astype(BF16), w_ref[...], preferred_element_type=F32)
    zg_ref[...] = z[:, :ZE_GATES]
    lat = _rms(z[:, ZE_CKV:ZE_CQ], kvn_ref[...])
    lat_ref[...] = lat
    cos, sin = cos_ref[...], sin_ref[...]
    kr1, kr2 = z[:, ZE_KR1:ZE_KR2], z[:, ZE_KR2:ZE_KRN]
    kp_ref[...] = jnp.concatenate([lat, kr1 * cos - kr2 * sin, kr1 * sin + kr2 * cos], axis=1).astype(BF16)
    kr_nat = z[:, ZE_KRN:ZE_KRS] * cosn_ref[...] + z[:, ZE_KRS:ZE_W] * sinn_ref[...]
    kr_ref[...] = kr_nat[:, :QK_ROPE]
    q = jnp.dot(_rms(z[:, ZE_CQ:ZE_KR1], qn_ref[...]).astype(BF16), wuq_ref[...],
                preferred_element_type=F32)
    q1, q2 = q[:, UQ_NOPE_W:UQ_NOPE_W + LANES], q[:, UQ_NOPE_W + LANES:]
    o1, o2 = q1 * cos - q2 * sin, q1 * sin + q2 * cos
    lane_head = lax.shift_right_logical(lax.broadcasted_iota(jnp.int32, o1.shape, 1), ROPE_HALF.bit_length() - 1)
    for h in range(H_A):
        ql = jnp.dot(q[:, h * LANES:(h + 1) * LANES].astype(BF16), wuk_ref[h], preferred_element_type=F32)
        own = lane_head == h
        qh = jnp.concatenate([ql, jnp.where(own, o1, 0.0), jnp.where(own, o2, 0.0)], axis=1)
        qp_ref[h] = (qh * (MLA_SCALE * LOG2E)).astype(BF16)


def _even_in(x2d, tables, w_in_r, q_norm, kv_norm, w_uq_r, w_uk_r):
    n = x2d.shape[0]
    tm = min(256, n)
    period = tables[0].shape[0] // tm
    row = lambda i: (i, 0)
    fixed2 = lambda i: (0, 0)
    tab = lambda i: (i % period, 0)
    return pl.pallas_call(
        _even_in_body,
        grid=(n // tm,),
        in_specs=[pl.BlockSpec((tm, D_MODEL), row),
                  pl.BlockSpec((D_MODEL, ZE_W), fixed2),
                  pl.BlockSpec((1, Q_LORA), fixed2),
                  pl.BlockSpec((1, KV_LORA), fixed2),
                  pl.BlockSpec((Q_LORA, UQ_NOPE_W + 2 * LANES), fixed2),
                  pl.BlockSpec((H_A, LANES, KV_LORA), lambda i: (0, 0, 0)),
                  pl.BlockSpec((tm, LANES), tab), pl.BlockSpec((tm, LANES), tab),
                  pl.BlockSpec((tm, LANES), tab), pl.BlockSpec((tm, LANES), tab)],
        out_specs=[pl.BlockSpec((tm, ZE_GATES), row),
                   pl.BlockSpec((tm, KV_LORA), row),
                   pl.BlockSpec((tm, QK_ROPE), row),
                   pl.BlockSpec((tm, QP_W), row),
                   pl.BlockSpec((H_A, tm, QP_W), lambda i: (0, i, 0))],
        out_shape=[jax.ShapeDtypeStruct((n, ZE_GATES), F32),
                   jax.ShapeDtypeStruct((n, KV_LORA), F32),
                   jax.ShapeDtypeStruct((n, QK_ROPE), F32),
                   jax.ShapeDtypeStruct((n, QP_W), BF16),
                   jax.ShapeDtypeStruct((H_A, n, QP_W), BF16)],
        compiler_params=_params(("arbitrary",)),
        name="even_in_proj",
    )(x2d, w_in_r, q_norm.reshape(1, -1), kv_norm.reshape(1, -1), w_uq_r, w_uk_r, *tables)


def _softmax_update(s, vals, m_sc, l_sc, acc_sc):
    tiles = [s[:, c * LANES:(c + 1) * LANES] for c in range(s.shape[1] // LANES)]
    m_prev = m_sc[...]
    m_new = jnp.maximum(m_prev, jnp.max(functools.reduce(jnp.maximum, tiles), axis=-1, keepdims=True))
    alpha = jnp.exp2(m_prev - m_new)
    ps = [jnp.exp2(t - m_new) for t in tiles]
    l_sc[...] = alpha * l_sc[...] + functools.reduce(jnp.add, ps)
    pv = jnp.dot(jnp.concatenate(ps, axis=1).astype(BF16), vals, preferred_element_type=F32)
    acc = acc_sc[...]
    acc_sc[...] = jnp.concatenate([acc[:, c * LANES:(c + 1) * LANES] * alpha
                                   for c in range(acc.shape[1] // LANES)], axis=1) + pv
    m_sc[...] = m_new


def _softmax_result(l_sc, acc_sc):
    return acc_sc[...] / jnp.sum(l_sc[...], axis=-1, keepdims=True)


def _softmax_init(m_sc, l_sc, acc_sc):
    m_sc[...] = jnp.full(m_sc.shape, -jnp.inf, F32)
    l_sc[...] = jnp.zeros(l_sc.shape, F32)
    acc_sc[...] = jnp.zeros(acc_sc.shape, F32)


FLASH_TQ = 256
FLASH_TK = 1024
FLAG_FIRST, FLAG_LAST = 1, 2
KIND_SHIFT = 2


def _flash_body(qb, kb, qo, ko, fl, q_ref, k_ref, wuv_ref, o_ref, m_sc, l_sc, acc_sc, s_sc, *, tq, tk):
    i = pl.program_id(0)
    flags = fl[i]
    kind = lax.shift_right_logical(flags, KIND_SHIFT)
    widths = [tk] + [v * tq for v in range(1, tk // tq + 1)]

    @pl.when((flags & FLAG_FIRST) != 0)
    def _():
        _softmax_init(m_sc, l_sc, acc_sc)

    for v, n in enumerate(widths):
        @pl.when(kind == v)
        def _():
            s_sc[:, 0:n] = lax.dot_general(q_ref[...].reshape(H_A * tq, QP_W), k_ref[0:n, :], NT_DIMS,
                                           preferred_element_type=F32)

    for v, n in enumerate(widths):
        @pl.when(kind == v)
        def _():
            s = s_sc[:, 0:n]
            if v > 0:
                qpos = (lax.broadcasted_iota(jnp.int32, s.shape, 0) & (tq - 1)) + qo[i]
                kpos = lax.broadcasted_iota(jnp.int32, s.shape, 1) + ko[i]
                s = jnp.where(kpos <= qpos, s, -jnp.inf)
            _softmax_update(s, k_ref[0:n, 0:KV_LORA], m_sc, l_sc, acc_sc)

    @pl.when((flags & FLAG_LAST) != 0)
    def _():
        o = _softmax_result(l_sc, acc_sc)
        o_all = jnp.concatenate([o[h * tq:(h + 1) * tq] for h in range(H_A)], axis=1).astype(BF16)
        o_ref[...] = jnp.dot(o_all, wuv_ref[...], preferred_element_type=F32)


def _flash_steps(batch, seq, tq, tk):
    nq, nk = seq // tq, seq // tk
    qb, kb, qo, ko, fl = [], [], [], [], []
    for b in range(batch):
        for qi in range(nq):
            last = ((qi + 1) * tq - 1) // tk
            for kj in range(last + 1):
                qb.append(b * nq + qi)
                kb.append(b * nk + kj)
                qo.append(qi * tq)
                ko.append(kj * tk)
                causal = (kj + 1) * tk - 1 > qi * tq
                kind = ((qi + 1) * tq - kj * tk) // tq if causal else 0
                fl.append((FLAG_FIRST if kj == 0 else 0) | (FLAG_LAST if kj == last else 0) | (kind << KIND_SHIFT))
    return [np.asarray(a, np.int32) for a in (qb, kb, qo, ko, fl)]


def _mla_prompt(qp, kp, w_uv_bd, batch, seq):
    tq, tk = min(FLASH_TQ, seq), min(FLASH_TK, seq)
    assert tq & (tq - 1) == 0 and seq % tq == 0 and seq % tk == 0 and tk % tq == 0
    steps = _flash_steps(batch, seq, tq, tk)
    n = batch * seq
    rows = H_A * tq
    grid_spec = pltpu.PrefetchScalarGridSpec(
        num_scalar_prefetch=5,
        grid=(steps[0].shape[0],),
        in_specs=[pl.BlockSpec((H_A, tq, QP_W), lambda i, qb, kb, qo, ko, fl: (0, qb[i], 0)),
                  pl.BlockSpec((tk, QP_W), lambda i, qb, kb, qo, ko, fl: (kb[i], 0)),
                  pl.BlockSpec((H_A * KV_LORA, W_A), lambda i, qb, kb, qo, ko, fl: (0, 0))],
        out_specs=pl.BlockSpec((tq, W_A), lambda i, qb, kb, qo, ko, fl: (qb[i], 0)),
        scratch_shapes=[pltpu.VMEM((rows, LANES), F32), pltpu.VMEM((rows, LANES), F32),
                        pltpu.VMEM((rows, KV_LORA), F32), pltpu.VMEM((rows, tk), F32)])
    return pl.pallas_call(
        functools.partial(_flash_body, tq=tq, tk=tk),
        grid_spec=grid_spec,
        out_shape=jax.ShapeDtypeStruct((n, W_A), F32),
        compiler_params=_params(("arbitrary",)),
        name="mla_prompt_flash",
    )(*[jnp.asarray(a) for a in steps], qp, kp, w_uv_bd)


PAGES_PER_STEP = 64


def _page_copies(pt_ref, lat_hbm, kr_hbm, lat_buf, kr_buf, lat_sem, kr_sem, e, step, slot, i, npg):
    page = pt_ref[step * npg + i]
    return (pltpu.make_async_copy(lat_hbm.at[e, page], lat_buf.at[slot, i], lat_sem.at[slot]),
            pltpu.make_async_copy(kr_hbm.at[e, page], kr_buf.at[slot, i], kr_sem.at[slot]))


def _paged_body(pt_ref, ql_ref, qr_ref, kself_ref, krself_ref, lat_hbm, kr_hbm, o_ref,
                lat_buf, kr_buf, lat_sem, kr_sem, lat_sc, kr_sc, m_sc, l_sc, acc_sc, *, npg, t_new, e):
    j = pl.program_id(1)
    chunks = pl.num_programs(1)
    step = pl.program_id(0) * chunks + j
    last_step = pl.num_programs(0) * chunks - 1
    slot = step & 1
    copies = functools.partial(_page_copies, pt_ref, lat_hbm, kr_hbm, lat_buf, kr_buf, lat_sem, kr_sem, e)

    @pl.when(step == 0)
    def _():
        for i in range(npg):
            for cp in copies(0, 0, i, npg):
                cp.start()

    @pl.when(j == 0)
    def _():
        _softmax_init(m_sc, l_sc, acc_sc)

    nxt = jnp.minimum(step + 1, last_step)
    for i in range(npg):
        for cp in copies(nxt, 1 - slot, i, npg):
            cp.start()
    for i in range(npg):
        for cp in copies(step, slot, i, npg):
            cp.wait()

    for i in range(npg):
        lat_sc[i * PAGE_SIZE:(i + 1) * PAGE_SIZE, :] = lat_buf[slot, i].astype(BF16)
        kr_sc[:QK_ROPE, i * PAGE_SIZE:(i + 1) * PAGE_SIZE] = kr_buf[slot, i].astype(BF16)

    ql, qr = ql_ref[0], qr_ref[0]

    def attend(lat, kr_t, mask):
        s = (lax.dot_general(ql, lat, NT_DIMS, preferred_element_type=F32)
             + jnp.dot(qr, kr_t, preferred_element_type=F32))
        if mask is not None:
            s = jnp.where(mask, s, -jnp.inf)
        _softmax_update(s, lat, m_sc, l_sc, acc_sc)

    attend(lat_sc[...], kr_sc[...], None)

    @pl.when(j == chunks - 1)
    def _():
        shape = (ql.shape[0], PAGE_SIZE)
        t_row = lax.broadcasted_iota(jnp.int32, shape, 0) & (t_new - 1)
        attend(kself_ref[0], krself_ref[0], lax.broadcasted_iota(jnp.int32, shape, 1) <= t_row)
        o_ref[0] = _softmax_result(l_sc, acc_sc)

    @pl.when(step == last_step)
    def _():
        for i in range(npg):
            for cp in copies(last_step, 1 - slot, i, npg):
                cp.wait()


def _mla_sample(ql, qr, kself, krself, lat_pool, rope_pool_t, page_table, e, t_new):
    batch, rows, _ = ql.shape
    n_pages = page_table.shape[1]
    npg = min(PAGES_PER_STEP, n_pages)
    assert n_pages % npg == 0 and t_new & (t_new - 1) == 0 and t_new <= PAGE_SIZE
    chunks = n_pages // npg
    per_b = lambda b, j, pt: (b, 0, 0)
    grid_spec = pltpu.PrefetchScalarGridSpec(
        num_scalar_prefetch=1,
        grid=(batch, chunks),
        in_specs=[pl.BlockSpec((1, rows, KV_LORA), per_b), pl.BlockSpec((1, rows, QK_ROPE), per_b),
                  pl.BlockSpec((1, PAGE_SIZE, KV_LORA), per_b), pl.BlockSpec((1, QK_ROPE, PAGE_SIZE), per_b),
                  pl.BlockSpec(memory_space=pl.ANY), pl.BlockSpec(memory_space=pl.ANY)],
        out_specs=pl.BlockSpec((1, rows, KV_LORA), per_b),
        scratch_shapes=[pltpu.VMEM((2, npg, PAGE_SIZE, KV_LORA), F32),
                        pltpu.VMEM((2, npg, QK_ROPE, PAGE_SIZE), F32),
                        pltpu.SemaphoreType.DMA((2,)), pltpu.SemaphoreType.DMA((2,)),
                        pltpu.VMEM((npg * PAGE_SIZE, KV_LORA), BF16),
                        pltpu.VMEM((QK_ROPE, npg * PAGE_SIZE), BF16),
                        pltpu.VMEM((rows, LANES), F32), pltpu.VMEM((rows, LANES), F32),
                        pltpu.VMEM((rows, KV_LORA), F32)])
    return pl.pallas_call(
        functools.partial(_paged_body, npg=npg, t_new=t_new, e=e),
        grid_spec=grid_spec,
        out_shape=jax.ShapeDtypeStruct((batch, rows, KV_LORA), F32),
        compiler_params=_params(("arbitrary", "arbitrary")),
        name="mla_sample_paged",
    )(page_table.reshape(-1), ql, qr, kself, krself, lat_pool, rope_pool_t)


LRU_TC = 256


def _block_diag(w):
    nb, d, e = w.shape
    return jnp.einsum('nde,nm->ndme', w, jnp.eye(nb, dtype=w.dtype)).reshape(nb * d, nb * e)


def _lru_coeffs(xc, wa, wx, ba, bx, lam):
    xb = xc.astype(BF16)
    r = jax.nn.sigmoid(jnp.dot(xb, wa, preferred_element_type=F32) + ba)
    ig = jax.nn.sigmoid(jnp.dot(xb, wx, preferred_element_type=F32) + bx)
    neg = -lam
    softplus = jnp.maximum(neg, 0.0) + jnp.log1p(jnp.exp(-jnp.abs(neg)))
    log_a = -LRU_C * r * softplus
    a = jnp.exp(log_a)
    t = jnp.tanh(log_a)
    b = jnp.sqrt(-2.0 * t / (1.0 - t)) * (ig * xc)
    return a, b


def _lru_body(x_ref, h0_ref, buf_ref, cw_ref, cb_ref, wa_ref, wx_ref, ba_ref, bx_ref, lam_ref,
              h_ref, hl_ref, tail_ref, xbuf, hc, *, tc):
    c = pl.program_id(1)

    @pl.when(c == 0)
    def _():
        xbuf[0:SUBLANES] = buf_ref[0]
        hc[...] = h0_ref[0]

    x = x_ref[...]
    xbuf[SUBLANES:SUBLANES + tc] = x
    cw = cw_ref[...]
    xc = cb_ref[...] + x * cw[CONV_W - 1:CONV_W]
    for j in range(CONV_W - 1):
        xc = xc + xbuf[pl.ds(SUBLANES - (CONV_W - 1) + j, tc), :] * cw[j:j + 1]
    xbuf[0:SUBLANES] = x[tc - SUBLANES:tc]
    a, b = _lru_coeffs(xc, wa_ref[...], wx_ref[...], ba_ref[...], bx_ref[...], lam_ref[...])
    groups = tc // SUBLANES
    a = a.reshape(groups, SUBLANES, W_B)
    b = b.reshape(groups, SUBLANES, W_B)
    sub = lax.broadcasted_iota(jnp.int32, a.shape, 1)
    d = 1
    while d < SUBLANES:
        keep = sub >= d
        a_sh = jnp.where(keep, pltpu.roll(a, d, 1), 1.0)
        b_sh = jnp.where(keep, pltpu.roll(b, d, 1), 0.0)
        b = a * b_sh + b
        a = a * a_sh
        d *= 2
    carry = hc[...]
    for g in range(groups):
        hg = a[g] * carry + b[g]
        h_ref[g * SUBLANES:(g + 1) * SUBLANES, :] = hg
        carry = hg[SUBLANES - 1:SUBLANES]
    hc[...] = carry

    @pl.when(c == pl.num_programs(1) - 1)
    def _():
        hl_ref[0] = carry
        tail_ref[0] = x[tc - SUBLANES:tc]


def _rglru_seq(zg, batch, seq, h0, buf, conv_w, conv_b, wa_bd, wx_bd, ba, bx, lam):
    tc = min(LRU_TC, seq)
    assert seq % tc == 0 and tc >= SUBLANES
    nc = seq // tc
    buf8 = jnp.pad(buf, ((0, 0), (SUBLANES - (CONV_W - 1), 0), (0, 0)))
    vec = lambda a: a.reshape(1, W_B)
    fixed = lambda b, c: (0, 0)
    per_b = lambda b, c: (b, 0, 0)
    h, hl, tail = pl.pallas_call(
        functools.partial(_lru_body, tc=tc),
        grid=(batch, nc),
        in_specs=[pl.BlockSpec((tc, W_B), lambda b, c: (b * nc + c, ZE_XB // W_B)),
                  pl.BlockSpec((1, 1, W_B), per_b),
                  pl.BlockSpec((1, SUBLANES, W_B), per_b),
                  pl.BlockSpec((CONV_W, W_B), fixed),
                  pl.BlockSpec((1, W_B), fixed),
                  pl.BlockSpec((W_B, W_B), fixed), pl.BlockSpec((W_B, W_B), fixed),
                  pl.BlockSpec((1, W_B), fixed), pl.BlockSpec((1, W_B), fixed), pl.BlockSpec((1, W_B), fixed)],
        out_specs=[pl.BlockSpec((tc, W_B), lambda b, c: (b * nc + c, 0)),
                   pl.BlockSpec((1, 1, W_B), per_b),
                   pl.BlockSpec((1, SUBLANES, W_B), per_b)],
        out_shape=[jax.ShapeDtypeStruct((batch * seq, W_B), F32),
                   jax.ShapeDtypeStruct((batch, 1, W_B), F32),
                   jax.ShapeDtypeStruct((batch, SUBLANES, W_B), F32)],
        scratch_shapes=[pltpu.VMEM((SUBLANES + tc, W_B), F32), pltpu.VMEM((1, W_B), F32)],
        compiler_params=_params(("arbitrary", "arbitrary")),
        name="rglru_seq",
    )(zg, h0.reshape(batch, 1, W_B), buf8, conv_w, vec(conv_b), wa_bd, wx_bd, vec(ba), vec(bx), vec(lam))
    return h, hl[:, 0], tail[:, SUBLANES - (CONV_W - 1):]


def _out_body(v1_ref, v2_ref, v3_ref, g1_ref, g2_ref, g3_ref, x_ref, w_ref, lg_ref, lb_ref, o_ref):
    def gated(v_ref, g_ref):
        g = g_ref[...]
        return (v_ref[...] * (g * jax.nn.sigmoid(g))).astype(BF16)

    mixed = jnp.concatenate([gated(v1_ref, g1_ref), gated(v2_ref, g2_ref), gated(v3_ref, g3_ref)], axis=1)
    u = DEEPNORM_ALPHA * x_ref[...] + jnp.dot(mixed, w_ref[...], preferred_element_type=F32)
    mu = jnp.mean(u, axis=-1, keepdims=True)
    var = jnp.mean(jnp.square(u - mu), axis=-1, keepdims=True)
    o_ref[...] = (u - mu) * lax.rsqrt(var + NORM_EPS) * lg_ref[...] + lb_ref[...]


def _out_proj_norm(vals, z, gate_cols, x2d, w_out, ln_g, ln_b):
    n = x2d.shape[0]
    tm = min(256, n)
    widths = [v.shape[1] for v in vals]
    row = lambda i: (i, 0)
    fixed = lambda i: (0, 0)
    in_specs = [pl.BlockSpec((tm, w), row) for w in widths]
    for w, off in zip(widths, gate_cols):
        assert off % w == 0
        in_specs.append(pl.BlockSpec((tm, w), functools.partial(lambda i, cb: (i, cb), cb=off // w)))
    in_specs += [pl.BlockSpec((tm, D_MODEL), row),
                 pl.BlockSpec((sum(widths), D_MODEL), fixed),
                 pl.BlockSpec((1, D_MODEL), fixed), pl.BlockSpec((1, D_MODEL), fixed)]
    return pl.pallas_call(
        _out_body,
        grid=(n // tm,),
        in_specs=in_specs,
        out_specs=pl.BlockSpec((tm, D_MODEL), row),
        out_shape=jax.ShapeDtypeStruct((n, D_MODEL), F32),
        compiler_params=_params(("arbitrary",)),
        name="out_proj_norm",
    )(*vals, z, z, z, x2d, w_out.astype(BF16), ln_g.reshape(1, -1), ln_b.reshape(1, -1))


def _mem_attend_rows(q, k_t, v_t):
    rows = q.shape[0]
    q_head = lax.shift_right_logical(lax.broadcasted_iota(jnp.int32, q.shape, 1), DH_M.bit_length() - 1)
    qs = jnp.concatenate([jnp.where(q_head == h, q, 0.0) for h in range(H_M)], axis=0).astype(BF16)
    s = jnp.dot(qs, k_t.astype(BF16), preferred_element_type=F32) * (DH_M ** -0.5)
    p = jnp.exp(s - jnp.max(s, axis=-1, keepdims=True))
    p = p / jnp.sum(p, axis=-1, keepdims=True)
    y = lax.dot_general(p.astype(BF16), v_t.astype(BF16), NT_DIMS, preferred_element_type=F32)
    out = jnp.where(q_head == 0, y[0:rows], 0.0)
    for h in range(1, H_M):
        out = out + jnp.where(q_head == h, y[h * rows:(h + 1) * rows], 0.0)
    return out


def _mem_body(q_ref, k_ref, v_ref, o_ref):
    for i in range(k_ref.shape[1]):
        q = q_ref[i] if len(q_ref.shape) == 3 else q_ref[...]
        y = _mem_attend_rows(q, k_ref[0, i], v_ref[0, i])
        if len(o_ref.shape) == 3:
            o_ref[i] = y
        else:
            o_ref[...] = y


def _mem_transposed(mem):
    t = jnp.moveaxis(mem, -3, -1)
    t = t.reshape(t.shape[:-3] + (W_M, N_MEM))
    return t if t.ndim == 4 else t[None]


def _mem_attend_long(z, col, batch, seq, mem_kt, mem_vt, layer):
    tm = min(512, seq)
    nt = seq // tm
    kv = lambda b, i: (layer, b, 0, 0)
    return pl.pallas_call(
        _mem_body,
        grid=(batch, nt),
        in_specs=[pl.BlockSpec((tm, W_M), lambda b, i: (b * nt + i, col // W_M)),
                  pl.BlockSpec((1, 1, W_M, N_MEM), kv), pl.BlockSpec((1, 1, W_M, N_MEM), kv)],
        out_specs=pl.BlockSpec((tm, W_M), lambda b, i: (b * nt + i, 0)),
        out_shape=jax.ShapeDtypeStruct((batch * seq, W_M), F32),
        compiler_params=_params(("arbitrary", "arbitrary")),
        name="mem_attend_long",
    )(z, mem_kt, mem_vt)


MEM_SHORT_NB = 8


def _mem_attend_short(z, col, batch, seq, mem_kt, mem_vt, layer):
    assert seq <= SUBLANES
    nb = min(MEM_SHORT_NB, batch)
    q = jnp.pad(z[:, col:col + W_M].reshape(batch, seq, W_M), ((0, 0), (0, SUBLANES - seq), (0, 0)))
    blk = lambda i: (i, 0, 0)
    kv = lambda i: (layer, i, 0, 0)
    y = pl.pallas_call(
        _mem_body,
        grid=(batch // nb,),
        in_specs=[pl.BlockSpec((nb, SUBLANES, W_M), blk),
                  pl.BlockSpec((1, nb, W_M, N_MEM), kv), pl.BlockSpec((1, nb, W_M, N_MEM), kv)],
        out_specs=pl.BlockSpec((nb, SUBLANES, W_M), blk),
        out_shape=jax.ShapeDtypeStruct((batch, SUBLANES, W_M), F32),
        compiler_params=_params(("arbitrary",)),
        name="mem_attend_short",
    )(q, mem_kt, mem_vt)
    return y[:, :seq].reshape(batch * seq, W_M)


def _chunk_mlp_body(u_ref, v_ref, g_ref, b_ref, w_ref, bias_ref, y_ref, vn_ref, *, rows, chunks):
    gw = W_D // G_D
    for c in range(chunks):
        sl = pl.ds(c * rows, rows)
        v = v_ref[sl, :]
        mu = jnp.mean(v, axis=-1, keepdims=True)
        var = jnp.mean(jnp.square(v - mu), axis=-1, keepdims=True)
        vn = (v - mu) * lax.rsqrt(var + NORM_EPS) * g_ref[...] + b_ref[...]
        vn_ref[sl, :] = vn
        vb = vn.astype(BF16)
        s = jnp.concatenate([jnp.dot(w_ref[g], vb[:, g * gw:(g + 1) * gw], preferred_element_type=F32)
                             for g in range(G_D)], axis=1)
        y_ref[sl, :] = u_ref[sl, :] * (s + bias_ref[...])


def _chunk_mlp_call(z, rows, ln_g_d, ln_b_d, w_mix, bias):
    n = z.shape[0]
    chunks = max(1, min(4, n // rows))
    tm = rows * chunks
    row = lambda i: (i, 0)
    fixed = lambda i: (0, 0)
    return pl.pallas_call(
        functools.partial(_chunk_mlp_body, rows=rows, chunks=chunks),
        grid=(n // tm,),
        in_specs=[pl.BlockSpec((tm, W_D), lambda i: (i, ZO_U // W_D)),
                  pl.BlockSpec((tm, W_D), lambda i: (i, ZO_VD // W_D)),
                  pl.BlockSpec((1, W_D), fixed), pl.BlockSpec((1, W_D), fixed),
                  pl.BlockSpec((G_D, rows, rows), lambda i: (0, 0, 0)),
                  pl.BlockSpec((rows, W_D), fixed)],
        out_specs=[pl.BlockSpec((tm, W_D), row), pl.BlockSpec((tm, W_D), row)],
        out_shape=[jax.ShapeDtypeStruct((n, W_D), F32), jax.ShapeDtypeStruct((n, W_D), F32)],
        compiler_params=_params(("arbitrary",)),
        name="chunk_mlp",
    )(z, z, ln_g_d.reshape(1, -1), ln_b_d.reshape(1, -1), w_mix, bias)


def _chunk_mlp_weights(sg_w, sg_b):
    L = CHUNK_D
    w = jnp.where(jnp.tril(jnp.ones((L, L), dtype=bool)), sg_w[:, :L, :L], 0.0)
    return w.astype(BF16), jnp.repeat(sg_b[:, :L].T, W_D // G_D, axis=1)


def _chunk_mlp_short_body(u_ref, v_ref, g_ref, b_ref, w_ref, bias_ref, y_ref, vn_ref, *, batch, seq):
    v = v_ref[...]
    mu = jnp.mean(v, axis=-1, keepdims=True)
    var = jnp.mean(jnp.square(v - mu), axis=-1, keepdims=True)
    vn = (v - mu) * lax.rsqrt(var + NORM_EPS) * g_ref[...] + b_ref[...]
    vn_ref[...] = vn
    for t in range(seq):
        acc = vn[0:batch] * w_ref[t * seq:t * seq + 1, :]
        for s in range(1, t + 1):
            acc = acc + vn[s * batch:(s + 1) * batch] * w_ref[t * seq + s:t * seq + s + 1, :]
        rows = slice(t * batch, (t + 1) * batch)
        y_ref[rows, :] = u_ref[rows, :] * (acc + bias_ref[t:t + 1, :])


def _chunk_mlp_short(z, batch, seq, ln_g_d, ln_b_d, sg_w, sg_b):
    gw = W_D // G_D
    w_rows = jnp.repeat(jnp.transpose(sg_w[:, :seq, :seq], (1, 2, 0)).reshape(seq * seq, G_D), gw, axis=1)
    bias = jnp.repeat(sg_b[:, :seq].T, gw, axis=1)

    def time_major(off):
        return jnp.swapaxes(z[:, off:off + W_D].reshape(batch, seq, W_D), 0, 1).reshape(seq * batch, W_D)

    y, vn = pl.pallas_call(
        functools.partial(_chunk_mlp_short_body, batch=batch, seq=seq),
        out_shape=[jax.ShapeDtypeStruct((seq * batch, W_D), F32)] * 2,
        compiler_params=pltpu.CompilerParams(vmem_limit_bytes=VMEM_LIMIT),
        name="chunk_mlp_short",
    )(time_major(ZO_U), time_major(ZO_VD), ln_g_d.reshape(1, -1), ln_b_d.reshape(1, -1), w_rows, bias)
    back = lambda a: jnp.swapaxes(a.reshape(seq, batch, W_D), 0, 1).reshape(batch * seq, W_D)
    return back(y), back(vn)


def _cumsum_rows(x):
    row = lax.broadcasted_iota(jnp.int32, x.shape, 0)
    d = 1
    while d < x.shape[0]:
        x = x + jnp.where(row >= d, pltpu.roll(x, d, 0), 0.0)
        d *= 2
    return x


def _mlstm_body(q_ref, k_ref, v_ref, o_ref, gt_ref, bif_ref, c0_ref, n0_ref, m0_ref,
                y_ref, c_ref, n_ref, m_ref, c_sc, n_sc, m_sc, *stage, nb, n_valid):
    L = MLSTM_CHUNK
    step = pl.program_id(1)
    heads = [slice(h * DH_C, (h + 1) * DH_C) for h in range(H_C)]

    @pl.when(step == 0)
    def _():
        c_sc[...] = jnp.zeros(c_sc.shape, F32)
        for b in range(nb):
            for h, hs in enumerate(heads):
                c_sc[b, hs, hs] = c0_ref[b, h]
                n_sc[b, :, hs] = n0_ref[b, h]
        m_sc[...] = m0_ref[...]
        for st in stage:
            st[...] = jnp.zeros(st.shape, F32)

    shift = DH_C.bit_length() - 1
    row = lax.broadcasted_iota(jnp.int32, (L, LANES), 0)
    lane = lax.broadcasted_iota(jnp.int32, (L, LANES), 1)
    tril = lax.broadcasted_iota(jnp.int32, (L, L), 0) >= lax.broadcasted_iota(jnp.int32, (L, L), 1)
    lane_head = lax.shift_right_logical(lax.broadcasted_iota(jnp.int32, (L, W_C), 1), shift)
    diag_blocks = (lax.shift_right_logical(lax.broadcasted_iota(jnp.int32, (W_C, W_C), 0), shift)
                   == lax.shift_right_logical(lax.broadcasted_iota(jnp.int32, (W_C, W_C), 1), shift))

    def per_head(cols, width):
        return jnp.concatenate([jnp.broadcast_to(c, (c.shape[0], width)) for c in cols], axis=1)

    for b in range(nb):
        if stage:
            rows_in = q_ref.shape[1]
            bufs = []
            for st, ref in zip(stage, (q_ref, k_ref, v_ref, o_ref, gt_ref)):
                st[b, 0:rows_in, :] = ref[b]
                bufs.append(st[b])
            q, k, v, o_pre, gts = bufs
        else:
            q, k, v, o_pre, gts = q_ref[b], k_ref[b], v_ref[b], o_ref[b], gt_ref[b]
        gz = gts + bif_ref[...]
        lf = jnp.minimum(gz, 0.0) - jnp.log1p(jnp.exp(-jnp.abs(gz)))
        ig = gz
        if n_valid < L:
            ig = jnp.where(row < n_valid, ig, -jnp.inf)
            lf = jnp.where(row < n_valid, lf, 0.0)
        a = jnp.where(lane < H_C, ig, _cumsum_rows(lf))
        a_t = a.T
        ks = k * (DH_C ** -0.5)
        vb = v.astype(BF16)
        q_stack = jnp.concatenate([jnp.where(lane_head == h, q, 0.0) for h in range(H_C)], axis=0)
        qk = lax.dot_general(q_stack.astype(BF16), ks.astype(BF16), NT_DIMS, preferred_element_type=F32)
        ws, mts, g_inters, wgs, decays = [], [], [], [], []
        for h in range(H_C):
            ig_row, bc_row = a_t[h:h + 1, :], a_t[H_C + h:H_C + h + 1, :]
            ig_col, bc_col = a[:, h:h + 1], a[:, H_C + h:H_C + h + 1]
            m_old = m_sc[b, h][:, 0:1]
            dmat = jnp.where(tril, bc_col - bc_row + ig_row, -jnp.inf)
            inter = bc_col + m_old
            mt = jnp.maximum(inter, jnp.max(dmat, axis=-1, keepdims=True))
            ws.append(jnp.exp(dmat - mt))
            mts.append(mt)
            g_inters.append(jnp.exp(inter - mt))
            b_end = bc_col[L - 1:L, :]
            g_col = b_end - bc_col + ig_col
            m_new = jnp.maximum(b_end + m_old, jnp.max(g_col, axis=0, keepdims=True))
            decays.append(jnp.exp(b_end + m_old - m_new))
            wgs.append(jnp.exp(g_col - m_new))
            m_sc[b, h] = jnp.broadcast_to(m_new, (1, LANES))
        sc = qk * jnp.concatenate(ws, axis=0)
        sv = jnp.dot(sc.astype(BF16), vb, preferred_element_type=F32)
        num = jnp.where(lane_head == 0, sv[0:L], 0.0)
        for h in range(1, H_C):
            num = num + jnp.where(lane_head == h, sv[h * L:(h + 1) * L], 0.0)
        c_old = c_sc[b]
        n_old = n_sc[b]
        num = num + per_head(g_inters, DH_C) * jnp.dot(q.astype(BF16), c_old.astype(BF16),
                                                        preferred_element_type=F32)
        row_sum = jnp.sum(sc, axis=-1, keepdims=True)
        qn = q * n_old
        dens = []
        for h, hs in enumerate(heads):
            den = row_sum[h * L:(h + 1) * L] + g_inters[h] * jnp.sum(qn[:, hs], axis=-1, keepdims=True)
            dens.append(jnp.maximum(jnp.abs(den), jnp.exp(-mts[h])))
        y = jax.nn.sigmoid(o_pre) * (num / per_head(dens, DH_C))
        y_ref[b] = y[0:y_ref.shape[1]]
        kw = ks * per_head(wgs, DH_C)
        decay = per_head(decays, DH_C)
        upd = jnp.dot(kw.T.astype(BF16), vb, preferred_element_type=F32)
        c_sc[b] = decay * c_old + jnp.where(diag_blocks, upd, 0.0)
        n_sc[b] = decay * n_old + jnp.sum(kw, axis=0, keepdims=True)

    @pl.when(step == pl.num_programs(1) - 1)
    def _():
        for b in range(nb):
            for h, hs in enumerate(heads):
                c_ref[b, h] = c_sc[b, hs, hs]
                n_ref[b, h] = n_sc[b, :, hs]
        m_ref[...] = m_sc[...]


MLSTM_NB = 2


def _mlstm_call(z, batch, seq, b_if, c0, n0, m0):
    L = MLSTM_CHUNK
    nb = min(MLSTM_NB, batch)
    long = seq % L == 0
    if long:
        rows, nc, n_valid = L, seq // L, L
        z3 = z.reshape(batch, seq, z.shape[1])
    else:
        assert seq <= SUBLANES
        rows, nc, n_valid = SUBLANES, 1, seq
        z3 = jnp.pad(z.reshape(batch, seq, z.shape[1]), ((0, 0), (0, SUBLANES - seq), (0, 0)))
    bif = jnp.pad(b_if, (0, LANES - 2 * H_C)).reshape(1, LANES)
    m0r = jnp.broadcast_to(m0[:, :, None, None], (batch, H_C, 1, LANES))

    def col(off, w):
        return pl.BlockSpec((nb, rows, w), functools.partial(lambda g, c, cb: (g, c, cb), cb=off // w))

    st4 = lambda g, c: (g, 0, 0, 0)
    in_specs = [col(ZO_Q, W_C), col(ZO_K, W_C), col(ZO_V, W_C), col(ZO_O, W_C), col(ZO_IF, LANES),
                pl.BlockSpec((1, LANES), lambda g, c: (0, 0)),
                pl.BlockSpec((nb, H_C, DH_C, DH_C), st4),
                pl.BlockSpec((nb, H_C, 1, DH_C), st4),
                pl.BlockSpec((nb, H_C, 1, LANES), st4)]
    out_specs = [pl.BlockSpec((nb, rows, W_C), lambda g, c: (g, c, 0)),
                 pl.BlockSpec((nb, H_C, DH_C, DH_C), st4),
                 pl.BlockSpec((nb, H_C, 1, DH_C), st4),
                 pl.BlockSpec((nb, H_C, 1, LANES), st4)]
    scratch = [pltpu.VMEM((nb, W_C, W_C), F32), pltpu.VMEM((nb, 1, W_C), F32),
               pltpu.VMEM((nb, H_C, 1, LANES), F32)]
    if not long:
        scratch += [pltpu.VMEM((nb, L, W_C), F32)] * 4 + [pltpu.VMEM((nb, L, LANES), F32)]
    y, c, n, m = pl.pallas_call(
        functools.partial(_mlstm_body, nb=nb, n_valid=n_valid),
        grid=(batch // nb, nc),
        in_specs=in_specs,
        out_specs=out_specs,
        out_shape=[jax.ShapeDtypeStruct((batch, rows * nc, W_C), F32),
                   jax.ShapeDtypeStruct((batch, H_C, DH_C, DH_C), F32),
                   jax.ShapeDtypeStruct((batch, H_C, 1, DH_C), F32),
                   jax.ShapeDtypeStruct((batch, H_C, 1, LANES), F32)],
        scratch_shapes=scratch,
        compiler_params=_params(("arbitrary", "arbitrary")),
        name="mlstm_chunks",
    )(z3, z3, z3, z3, z3, bif, c0, n0[:, :, None, :], m0r)
    return y[:, :seq].reshape(batch * seq, W_C), c, n[:, :, 0], m[:, :, 0, 0]


def _lru_short_body(x_ref, h0_ref, buf_ref, cw_ref, cb_ref, wa_ref, wx_ref, ba_ref, bx_ref, lam_ref,
                    h_ref, *, batch, seq):
    cw = cw_ref[...]
    xp = jnp.concatenate([buf_ref[...], x_ref[...]], axis=0)
    xc = cb_ref[...] + xp[(CONV_W - 1) * batch:] * cw[CONV_W - 1:CONV_W]
    for j in range(CONV_W - 1):
        xc = xc + xp[j * batch:(j + seq) * batch] * cw[j:j + 1]
    a, b = _lru_coeffs(xc, wa_ref[...], wx_ref[...], ba_ref[...], bx_ref[...], lam_ref[...])
    h = h0_ref[...]
    for t in range(seq):
        sl = slice(t * batch, (t + 1) * batch)
        h = a[sl] * h + b[sl]
        h_ref[sl, :] = h


def _rglru_short(zg, batch, seq, h0, buf, conv_w, conv_b, wa_bd, wx_bd, ba, bx, lam):
    assert batch % SUBLANES == 0
    x = zg[:, ZE_XB:ZE_XB + W_B].reshape(batch, seq, W_B)
    xp = jnp.concatenate([buf, x], axis=1)
    x_tm = jnp.swapaxes(x, 0, 1).reshape(seq * batch, W_B)
    buf_tm = jnp.swapaxes(buf, 0, 1).reshape((CONV_W - 1) * batch, W_B)
    vec = lambda a: a.reshape(1, W_B)
    h_tm = pl.pallas_call(
        functools.partial(_lru_short_body, batch=batch, seq=seq),
        out_shape=jax.ShapeDtypeStruct((seq * batch, W_B), F32),
        compiler_params=pltpu.CompilerParams(vmem_limit_bytes=VMEM_LIMIT),
        name="rglru_short",
    )(x_tm, h0, buf_tm, conv_w, vec(conv_b), wa_bd, wx_bd, vec(ba), vec(bx), vec(lam))
    h = jnp.swapaxes(h_tm.reshape(seq, batch, W_B), 0, 1)
    return h.reshape(batch * seq, W_B), h[:, -1], xp[:, -(CONV_W - 1):]


def _mem_kv(mem, w_mk, w_mv):
    B = mem.shape[0]
    kv = _proj(mem, jnp.concatenate([w_mk, w_mv], axis=1))
    return (kv[..., :W_M].reshape(B, N_MEM, H_M, DH_M), kv[..., W_M:].reshape(B, N_MEM, H_M, DH_M))


def _sample_queries(qp, batch, t_new):
    n = qp.shape[1]
    r1 = jnp.stack([qp[h, :, KV_LORA + h * ROPE_HALF:KV_LORA + (h + 1) * ROPE_HALF] for h in range(H_A)])
    r2 = jnp.stack([qp[h, :, KV_LORA + LANES + h * ROPE_HALF:KV_LORA + LANES + (h + 1) * ROPE_HALF]
                    for h in range(H_A)])
    qr = jnp.concatenate([r1, r2], axis=-1)

    def rows(a):
        w = a.shape[-1]
        return a.reshape(H_A, batch, t_new, w).transpose(1, 0, 2, 3).reshape(batch, H_A * t_new, w)

    return rows(qp[:, :, :KV_LORA]), rows(qr)


def _sample_self_keys(latent, k_rope, batch, t_new):
    kl = jnp.pad(latent.astype(BF16).reshape(batch, t_new, KV_LORA), ((0, 0), (0, PAGE_SIZE - t_new), (0, 0)))
    kr_t = jnp.swapaxes(k_rope.astype(BF16).reshape(batch, t_new, QK_ROPE), 1, 2)
    return kl, jnp.pad(kr_t, ((0, 0), (0, 0), (0, PAGE_SIZE - t_new)))


def _even_layer(x2d, batch, seq, tables, mem, lru_h0, lru_buf, paged, e, weights,
                q_norm, kv_norm, conv_w, conv_b, wa, ba, wx, bx, lam, w_out, ln_g, ln_b):
    w_in_r, w_uq_r, w_uk_r, w_uv_bd = weights
    zg, latent, k_rope, kp, qp = _even_in(x2d, tables, w_in_r, q_norm, kv_norm, w_uq_r, w_uk_r)
    lru_args = (lru_h0, lru_buf, conv_w, conv_b, _block_diag(wa).astype(BF16), _block_diag(wx).astype(BF16),
                ba, bx, lam)
    if paged is None:
        y_a = _mla_prompt(qp, kp, w_uv_bd, batch, seq)
        h_b, h_last, new_buf = _rglru_seq(zg, batch, seq, *lru_args)
        y_m = _mem_attend_long(zg, ZE_QM, batch, seq, *mem)
    else:
        lat_pool, rope_pool, page_table = paged
        o = _mla_sample(*_sample_queries(qp, batch, seq), *_sample_self_keys(latent, k_rope, batch, seq),
                        lat_pool, jnp.swapaxes(rope_pool, 2, 3), page_table, e, seq)
        o = o.reshape(batch, H_A, seq, KV_LORA).transpose(0, 2, 1, 3).reshape(batch * seq, H_A * KV_LORA)
        y_a = _matmul(o, w_uv_bd)
        h_b, h_last, new_buf = _rglru_short(zg, batch, seq, *lru_args)
        y_m = _mem_attend_short(zg, ZE_QM, batch, seq, *mem)
    x_new = _out_proj_norm([y_a, h_b, y_m], zg, (ZE_GA, ZE_GB, ZE_GM), x2d, w_out, ln_g, ln_b)
    return (x_new, latent.reshape(batch, seq, KV_LORA), k_rope.reshape(batch, seq, QK_ROPE), h_last, new_buf)


def _odd_weights(w_in):
    q, k, v, i_pre, f_pre, o_pre, g_c, u_d, v_d, g_d, q_m, g_m = _split_cols(w_in, ODD_SPLITS)
    gates = jnp.pad(jnp.concatenate([i_pre, f_pre], axis=1), ((0, 0), (0, LANES - 2 * H_C)))
    return jnp.concatenate([q, k, v, o_pre, g_c, u_d, v_d, g_d, q_m, g_m, gates], axis=1)


def _odd_layer(x2d, batch, seq, mem, c0, n0, m0, w_in_r, b_if, ln_g_d, ln_b_d, sg_w, sg_b,
               w_out, ln_g, ln_b):
    z = _matmul(x2d, w_in_r)
    y_c, c, n, m = _mlstm_call(z, batch, seq, b_if, c0, n0, m0)
    if seq % CHUNK_D == 0:
        y_d, vn = _chunk_mlp_call(z, CHUNK_D, ln_g_d, ln_b_d, *_chunk_mlp_weights(sg_w, sg_b))
        y_m = _mem_attend_long(z, ZO_QM, batch, seq, *mem)
    else:
        y_d, vn = _chunk_mlp_short(z, batch, seq, ln_g_d, ln_b_d, sg_w, sg_b)
        y_m = _mem_attend_short(z, ZO_QM, batch, seq, *mem)
    x_new = _out_proj_norm([y_c, y_d, y_m], z, (ZO_GC, ZO_GD, ZO_GM), x2d, w_out, ln_g, ln_b)
    return x_new, vn.reshape(batch, seq, W_D), c, n, m


def kernel(x_prompt, x_sample, cache_mla_latent, cache_mla_krope, state_lru_h, state_lru_conv,
           state_mlstm_c, state_mlstm_n, state_mlstm_m, cache_mem_k, cache_mem_v, page_table,
           mem_prompt, w_in_even, mla_q_norm, mla_kv_norm, w_uq, w_uk, w_uv,
           lru_conv_w, lru_conv_b, lru_wa, lru_ba, lru_wx, lru_bx, lru_lambda, w_out_even,
           w_in_odd, mlstm_b_if, sg_ln_g, sg_ln_b, sg_w, sg_b, w_out_odd,
           w_mem_k, w_mem_v, ln_g, ln_b):
    Bp, Tp, _ = x_prompt.shape
    Bs, Ts, _ = x_sample.shape
    past_len = page_table.shape[1] * PAGE_SIZE
    tables_p = _rope_tables(jnp.arange(Tp, dtype=F32))
    tables_s = tuple(jnp.tile(t, (Bs, 1)) for t in _rope_tables(past_len + jnp.arange(Ts, dtype=F32)))

    h0_p = jnp.zeros((Bp, W_B), F32)
    buf0_p = jnp.zeros((Bp, CONV_W - 1, W_B), F32)
    c0_p = jnp.zeros((Bp, H_C, DH_C, DH_C), F32)
    n0_p = jnp.zeros((Bp, H_C, DH_C), F32)
    m0_p = jnp.zeros((Bp, H_C), F32)

    lat_p, kr_p, h_p, conv_p, c_p, n_p, m_p, mk_p, mv_p = [], [], [], [], [], [], [], [], []
    lat_s, kr_s, h_s, conv_s, c_s, n_s, m_s, v_s = [], [], [], [], [], [], [], []

    xp = x_prompt.reshape(Bp * Tp, D_MODEL)
    xs = x_sample.reshape(Bs * Ts, D_MODEL)
    mem_s = (_mem_transposed(cache_mem_k), _mem_transposed(cache_mem_v))
    for l in range(DEPTH):
        mk_l, mv_l = _mem_kv(mem_prompt, w_mem_k[l], w_mem_v[l])
        mk_p.append(mk_l)
        mv_p.append(mv_l)
        mem_p = (_mem_transposed(mk_l), _mem_transposed(mv_l), 0)
        if l % 2 == 0:
            e = l // 2
            weights = _even_weights(w_in_even[e], w_uq[e], w_uk[e], w_uv[e])
            rest = (mla_q_norm[e], mla_kv_norm[e], lru_conv_w[e], lru_conv_b[e], lru_wa[e], lru_ba[e],
                    lru_wx[e], lru_bx[e], lru_lambda[e], w_out_even[e], ln_g[l], ln_b[l])
            xp, la, kr, hl, cb = _even_layer(xp, Bp, Tp, tables_p, mem_p, h0_p, buf0_p, None, e,
                                             weights, *rest)
            lat_p.append(la); kr_p.append(kr); h_p.append(hl); conv_p.append(cb)
            xs, la, kr, hl, cb = _even_layer(xs, Bs, Ts, tables_s, mem_s + (l,),
                                             state_lru_h[e], state_lru_conv[e],
                                             (cache_mla_latent, cache_mla_krope, page_table), e, weights, *rest)
            lat_s.append(la); kr_s.append(kr); h_s.append(hl); conv_s.append(cb)
        else:
            o = l // 2
            ow = (_odd_weights(w_in_odd[o]), mlstm_b_if[o], sg_ln_g[o], sg_ln_b[o], sg_w[o], sg_b[o],
                  w_out_odd[o], ln_g[l], ln_b[l])
            xp, _, cc, nn, mm = _odd_layer(xp, Bp, Tp, mem_p, c0_p, n0_p, m0_p, *ow)
            c_p.append(cc); n_p.append(nn); m_p.append(mm)
            xs, vn, cc, nn, mm = _odd_layer(xs, Bs, Ts, mem_s + (l,),
                                            state_mlstm_c[o], state_mlstm_n[o], state_mlstm_m[o], *ow)
            c_s.append(cc); n_s.append(nn); m_s.append(mm); v_s.append(vn)

    return (xp.reshape(Bp, Tp, D_MODEL), xs.reshape(Bs, Ts, D_MODEL),
            jnp.stack(lat_p), jnp.stack(kr_p), jnp.stack(h_p), jnp.stack(conv_p),
            jnp.stack(c_p), jnp.stack(n_p), jnp.stack(m_p), jnp.stack(mk_p), jnp.stack(mv_p),
            jnp.stack(lat_s), jnp.stack(kr_s), jnp.stack(h_s), jnp.stack(conv_s),
            jnp.stack(c_s), jnp.stack(n_s), jnp.stack(m_s), jnp.stack(v_s))
```

```python
import functools

import jax
import jax.numpy as jnp
import numpy as np
from jax import lax
from jax.experimental import pallas as pl
from jax.experimental.pallas import tpu as pltpu

D_MODEL = 1024
DEPTH = 2
PAGE_SIZE = 128
H_A = 8
Q_LORA = 384
KV_LORA = 256
QK_NOPE = 64
QK_ROPE = 32
ROPE_HALF = QK_ROPE // 2
V_HEAD = 64
W_A = H_A * V_HEAD
ROPE_THETA = 10000.0
MLA_SCALE = (QK_NOPE + QK_ROPE) ** -0.5
W_B = 512
NB_B = 8
BD_B = W_B // NB_B
CONV_W = 4
LRU_C = 8.0
H_C = 4
DH_C = 128
W_C = H_C * DH_C
MLSTM_CHUNK = 128
G_D = 4
W_D = 512
CHUNK_D = 128
N_MEM = 256
H_M = 4
DH_M = 64
W_M = H_M * DH_M
NORM_EPS = 1e-6
DEEPNORM_ALPHA = (2 * DEPTH) ** 0.25

EVEN_SPLITS = (Q_LORA, KV_LORA, QK_ROPE, W_A, W_B, W_B, W_M, W_M)
ODD_SPLITS = (W_C, W_C, W_C, H_C, H_C, W_C, W_C, W_D, W_D, W_D, W_M, W_M)

F32 = jnp.float32
BF16 = jnp.bfloat16
LANES = 128
SUBLANES = 8
VMEM_LIMIT = 48 * 1024 * 1024
LOG2E = 1.4426950408889634
NT_DIMS = (((1,), (1,)), ((), ()))

ZE_GA, ZE_XB, ZE_GB, ZE_QM, ZE_GM = 0, 512, 1024, 1536, 1792
ZE_GATES = 2048
ZE_CKV = 2048
ZE_CQ = ZE_CKV + KV_LORA
ZE_KR1 = ZE_CQ + Q_LORA
ZE_KR2 = ZE_KR1 + LANES
ZE_KRN = ZE_KR2 + LANES
ZE_KRS = ZE_KRN + LANES
ZE_W = ZE_KRS + LANES
QP_W = KV_LORA + 2 * LANES
UQ_NOPE_W = H_A * LANES
ZO_Q, ZO_K, ZO_V, ZO_O, ZO_GC, ZO_U, ZO_VD, ZO_GD = (i * 512 for i in range(8))
ZO_QM, ZO_GM, ZO_IF = 4096, 4352, 4608
ZO_W = ZO_IF + LANES


def _split_cols(z, sizes):
    cuts = [int(c) for c in np.cumsum(sizes)[:-1]]
    return jnp.split(z, cuts, axis=-1)


def _params(sem):
    return pltpu.CompilerParams(dimension_semantics=sem, vmem_limit_bytes=VMEM_LIMIT)


def _mm_body(x_ref, w_ref, o_ref):
    o_ref[...] = jnp.dot(x_ref[...].astype(BF16), w_ref[...], preferred_element_type=F32)


def _row_tile(m, n):
    tm = 512
    while tm > SUBLANES and (tm * n * 4 * 2 > 10 * 1024 * 1024 or m % tm):
        tm //= 2
    return tm


def _matmul(x, w):
    m, k = x.shape
    n = w.shape[1]
    n_pad = -n % LANES
    wb = w.astype(BF16)
    if n_pad:
        wb = jnp.pad(wb, ((0, 0), (0, n_pad)))
    np_ = n + n_pad
    tm = _row_tile(m, np_)
    out = pl.pallas_call(
        _mm_body,
        grid=(m // tm,),
        in_specs=[pl.BlockSpec((tm, k), lambda i: (i, 0)),
                  pl.BlockSpec((k, np_), lambda i: (0, 0))],
        out_specs=pl.BlockSpec((tm, np_), lambda i: (i, 0)),
        out_shape=jax.ShapeDtypeStruct((m, np_), F32),
        compiler_params=_params(("arbitrary",)),
        name="row_matmul",
    )(x, wb)
    return out[:, :n] if n_pad else out


def _proj(x, w):
    lead = x.shape[:-1]
    return _matmul(x.reshape(-1, x.shape[-1]), w).reshape(lead + (w.shape[1],))


def _even_weights(w_in, w_uq, w_uk, w_uv):
    c_q, c_kv, kr, g_a, x_b, g_b, q_m, g_m = _split_cols(w_in, EVEN_SPLITS)
    x1, x2 = kr[:, :ROPE_HALF], kr[:, ROPE_HALF:]

    def lane_pad(a):
        return jnp.pad(a, ((0, 0), (0, LANES - a.shape[1])))

    w_in_r = jnp.concatenate(
        [g_a, x_b, g_b, q_m, g_m, c_kv, c_q, jnp.tile(x1, (1, H_A)), jnp.tile(x2, (1, H_A)),
         lane_pad(kr), lane_pad(jnp.concatenate([x2, x1], axis=1))], axis=1).astype(BF16)
    r = w_uq.reshape(Q_LORA, H_A, QK_NOPE + QK_ROPE)
    nope = jnp.pad(r[:, :, :QK_NOPE], ((0, 0), (0, 0), (0, LANES - QK_NOPE))).reshape(Q_LORA, UQ_NOPE_W)
    r1 = r[:, :, QK_NOPE:QK_NOPE + ROPE_HALF].reshape(Q_LORA, LANES)
    r2 = r[:, :, QK_NOPE + ROPE_HALF:].reshape(Q_LORA, LANES)
    w_uq_r = jnp.concatenate([nope, r1, r2], axis=1).astype(BF16)
    w_uk_r = jnp.pad(jnp.transpose(w_uk, (1, 2, 0)), ((0, 0), (0, LANES - QK_NOPE), (0, 0))).astype(BF16)
    eye = jnp.eye(H_A, dtype=w_uv.dtype)
    w_uv_bd = jnp.einsum('chv,hg->hcgv', w_uv, eye).reshape(H_A * KV_LORA, W_A).astype(BF16)
    return w_in_r, w_uq_r, w_uk_r, w_uv_bd


def _rope_tables(pos):
    inv = ROPE_THETA ** (-jnp.arange(ROPE_HALF, dtype=F32) / ROPE_HALF)
    ang = pos.astype(F32)[:, None] * inv[None, :]
    cos, sin = jnp.cos(ang), jnp.sin(ang)
    zpad = jnp.zeros((pos.shape[0], LANES - QK_ROPE), F32)
    return (jnp.tile(cos, (1, H_A)), jnp.tile(sin, (1, H_A)),
            jnp.concatenate([cos, cos, zpad], axis=1), jnp.concatenate([-sin, sin, zpad], axis=1))


def _rms(x, g):
    return x * lax.rsqrt(jnp.mean(x * x, axis=-1, keepdims=True) + NORM_EPS) * g


def _even_in_body(x_ref, w_ref, qn_ref, kvn_ref, wuq_ref, wuk_ref, cos_ref, sin_ref, cosn_ref, sinn_ref,
                  zg_ref, lat_ref, kr_ref, kp_ref, qp_ref):
    z = jnp.dot(x_ref[...].astype(BF16), w_ref[...], preferred_element_type=F32)
    zg_ref[...] = z[:, :ZE_GATES]
    lat = _rms(z[:, ZE_CKV:ZE_CQ], kvn_ref[...])
    lat_ref[...] = lat
    cos, sin = cos_ref[...], sin_ref[...]
    kr1, kr2 = z[:, ZE_KR1:ZE_KR2], z[:, ZE_KR2:ZE_KRN]
    kp_ref[...] = jnp.concatenate([lat, kr1 * cos - kr2 * sin, kr1 * sin + kr2 * cos], axis=1).astype(BF16)
    kr_nat = z[:, ZE_KRN:ZE_KRS] * cosn_ref[...] + z[:, ZE_KRS:ZE_W] * sinn_ref[...]
    kr_ref[...] = kr_nat[:, :QK_ROPE]
    q = jnp.dot(_rms(z[:, ZE_CQ:ZE_KR1], qn_ref[...]).astype(BF16), wuq_ref[...],
                preferred_element_type=F32)
    q1, q2 = q[:, UQ_NOPE_W:UQ_NOPE_W + LANES], q[:, UQ_NOPE_W + LANES:]
    o1, o2 = q1 * cos - q2 * sin, q1 * sin + q2 * cos
    lane_head = lax.shift_right_logical(lax.broadcasted_iota(jnp.int32, o1.shape, 1), ROPE_HALF.bit_length() - 1)
    for h in range(H_A):
        ql = jnp.dot(q[:, h * LANES:(h + 1) * LANES].astype(BF16), wuk_ref[h], preferred_element_type=F32)
        own = lane_head == h
        qh = jnp.concatenate([ql, jnp.where(own, o1, 0.0), jnp.where(own, o2, 0.0)], axis=1)
        qp_ref[h] = (qh * (MLA_SCALE * LOG2E)).astype(BF16)


def _even_in(x2d, tables, w_in_r, q_norm, kv_norm, w_uq_r, w_uk_r):
    n = x2d.shape[0]
    tm = min(256, n)
    period = tables[0].shape[0] // tm
    row = lambda i: (i, 0)
    fixed2 = lambda i: (0, 0)
    tab = lambda i: (i % period, 0)
    return pl.pallas_call(
        _even_in_body,
        grid=(n // tm,),
        in_specs=[pl.BlockSpec((tm, D_MODEL), row),
                  pl.BlockSpec((D_MODEL, ZE_W), fixed2),
                  pl.BlockSpec((1, Q_LORA), fixed2),
                  pl.BlockSpec((1, KV_LORA), fixed2),
                  pl.BlockSpec((Q_LORA, UQ_NOPE_W + 2 * LANES), fixed2),
                  pl.BlockSpec((H_A, LANES, KV_LORA), lambda i: (0, 0, 0)),
                  pl.BlockSpec((tm, LANES), tab), pl.BlockSpec((tm, LANES), tab),
                  pl.BlockSpec((tm, LANES), tab), pl.BlockSpec((tm, LANES), tab)],
        out_specs=[pl.BlockSpec((tm, ZE_GATES), row),
                   pl.BlockSpec((tm, KV_LORA), row),
                   pl.BlockSpec((tm, QK_ROPE), row),
                   pl.BlockSpec((tm, QP_W), row),
                   pl.BlockSpec((H_A, tm, QP_W), lambda i: (0, i, 0))],
        out_shape=[jax.ShapeDtypeStruct((n, ZE_GATES), F32),
                   jax.ShapeDtypeStruct((n, KV_LORA), F32),
                   jax.ShapeDtypeStruct((n, QK_ROPE), F32),
                   jax.ShapeDtypeStruct((n, QP_W), BF16),
                   jax.ShapeDtypeStruct((H_A, n, QP_W), BF16)],
        compiler_params=_params(("arbitrary",)),
        name="even_in_proj",
    )(x2d, w_in_r, q_norm.reshape(1, -1), kv_norm.reshape(1, -1), w_uq_r, w_uk_r, *tables)


def _softmax_update(s, vals, m_sc, l_sc, acc_sc):
    tiles = [s[:, c * LANES:(c + 1) * LANES] for c in range(s.shape[1] // LANES)]
    m_prev = m_sc[...]
    m_new = jnp.maximum(m_prev, jnp.max(functools.reduce(jnp.maximum, tiles), axis=-1, keepdims=True))
    alpha = jnp.exp2(m_prev - m_new)
    ps = [jnp.exp2(t - m_new) for t in tiles]
    l_sc[...] = alpha * l_sc[...] + functools.reduce(jnp.add, ps)
    pv = jnp.dot(jnp.concatenate(ps, axis=1).astype(BF16), vals, preferred_element_type=F32)
    acc = acc_sc[...]
    acc_sc[...] = jnp.concatenate([acc[:, c * LANES:(c + 1) * LANES] * alpha
                                   for c in range(acc.shape[1] // LANES)], axis=1) + pv
    m_sc[...] = m_new


def _softmax_result(l_sc, acc_sc):
    return acc_sc[...] / jnp.sum(l_sc[...], axis=-1, keepdims=True)


def _softmax_init(m_sc, l_sc, acc_sc):
    m_sc[...] = jnp.full(m_sc.shape, -jnp.inf, F32)
    l_sc[...] = jnp.zeros(l_sc.shape, F32)
    acc_sc[...] = jnp.zeros(acc_sc.shape, F32)


FLASH_TQ = 256
FLASH_TK = 1024
FLAG_FIRST, FLAG_LAST = 1, 2
KIND_SHIFT = 2


def _flash_body(qb, kb, qo, ko, fl, q_ref, k_ref, wuv_ref, o_ref, m_sc, l_sc, acc_sc, s_sc, *, tq, tk):
    i = pl.program_id(0)
    flags = fl[i]
    kind = lax.shift_right_logical(flags, KIND_SHIFT)
    widths = [tk] + [v * tq for v in range(1, tk // tq + 1)]

    @pl.when((flags & FLAG_FIRST) != 0)
    def _():
        _softmax_init(m_sc, l_sc, acc_sc)

    for v, n in enumerate(widths):
        @pl.when(kind == v)
        def _():
            s_sc[:, 0:n] = lax.dot_general(q_ref[...].reshape(H_A * tq, QP_W), k_ref[0:n, :], NT_DIMS,
                                           preferred_element_type=F32)

    for v, n in enumerate(widths):
        @pl.when(kind == v)
        def _():
            s = s_sc[:, 0:n]
            if v > 0:
                qpos = (lax.broadcasted_iota(jnp.int32, s.shape, 0) & (tq - 1)) + qo[i]
                kpos = lax.broadcasted_iota(jnp.int32, s.shape, 1) + ko[i]
                s = jnp.where(kpos <= qpos, s, -jnp.inf)
            _softmax_update(s, k_ref[0:n, 0:KV_LORA], m_sc, l_sc, acc_sc)

    @pl.when((flags & FLAG_LAST) != 0)
    def _():
        o = _softmax_result(l_sc, acc_sc)
        o_all = jnp.concatenate([o[h * tq:(h + 1) * tq] for h in range(H_A)], axis=1).astype(BF16)
        o_ref[...] = jnp.dot(o_all, wuv_ref[...], preferred_element_type=F32)


def _flash_steps(batch, seq, tq, tk):
    nq, nk = seq // tq, seq // tk
    qb, kb, qo, ko, fl = [], [], [], [], []
    for b in range(batch):
        for qi in range(nq):
            last = ((qi + 1) * tq - 1) // tk
            for kj in range(last + 1):
                qb.append(b * nq + qi)
                kb.append(b * nk + kj)
                qo.append(qi * tq)
                ko.append(kj * tk)
                causal = (kj + 1) * tk - 1 > qi * tq
                kind = ((qi + 1) * tq - kj * tk) // tq if causal else 0
                fl.append((FLAG_FIRST if kj == 0 else 0) | (FLAG_LAST if kj == last else 0) | (kind << KIND_SHIFT))
    return [np.asarray(a, np.int32) for a in (qb, kb, qo, ko, fl)]


def _mla_prompt(qp, kp, w_uv_bd, batch, seq):
    tq, tk = min(FLASH_TQ, seq), min(FLASH_TK, seq)
    assert tq & (tq - 1) == 0 and seq % tq == 0 and seq % tk == 0 and tk % tq == 0
    steps = _flash_steps(batch, seq, tq, tk)
    n = batch * seq
    rows = H_A * tq
    grid_spec = pltpu.PrefetchScalarGridSpec(
        num_scalar_prefetch=5,
        grid=(steps[0].shape[0],),
        in_specs=[pl.BlockSpec((H_A, tq, QP_W), lambda i, qb, kb, qo, ko, fl: (0, qb[i], 0)),
                  pl.BlockSpec((tk, QP_W), lambda i, qb, kb, qo, ko, fl: (kb[i], 0)),
                  pl.BlockSpec((H_A * KV_LORA, W_A), lambda i, qb, kb, qo, ko, fl: (0, 0))],
        out_specs=pl.BlockSpec((tq, W_A), lambda i, qb, kb, qo, ko, fl: (qb[i], 0)),
        scratch_shapes=[pltpu.VMEM((rows, LANES), F32), pltpu.VMEM((rows, LANES), F32),
                        pltpu.VMEM((rows, KV_LORA), F32), pltpu.VMEM((rows, tk), F32)])
    return pl.pallas_call(
        functools.partial(_flash_body, tq=tq, tk=tk),
        grid_spec=grid_spec,
        out_shape=jax.ShapeDtypeStruct((n, W_A), F32),
        compiler_params=_params(("arbitrary",)),
        name="mla_prompt_flash",
    )(*[jnp.asarray(a) for a in steps], qp, kp, w_uv_bd)


PAGES_PER_STEP = 64


def _page_copies(pt_ref, lat_hbm, kr_hbm, lat_buf, kr_buf, lat_sem, kr_sem, e, step, slot, i, npg):
    page = pt_ref[step * npg + i]
    return (pltpu.make_async_copy(lat_hbm.at[e, page], lat_buf.at[slot, i], lat_sem.at[slot]),
            pltpu.make_async_copy(kr_hbm.at[e, page], kr_buf.at[slot, i], kr_sem.at[slot]))


def _paged_body(pt_ref, ql_ref, qr_ref, kself_ref, krself_ref, lat_hbm, kr_hbm, o_ref,
                lat_buf, kr_buf, lat_sem, kr_sem, lat_sc, kr_sc, m_sc, l_sc, acc_sc, *, npg, t_new, e):
    j = pl.program_id(1)
    chunks = pl.num_programs(1)
    step = pl.program_id(0) * chunks + j
    last_step = pl.num_programs(0) * chunks - 1
    slot = step & 1
    copies = functools.partial(_page_copies, pt_ref, lat_hbm, kr_hbm, lat_buf, kr_buf, lat_sem, kr_sem, e)

    @pl.when(step == 0)
    def _():
        for i in range(npg):
            for cp in copies(0, 0, i, npg):
                cp.start()

    @pl.when(j == 0)
    def _():
        _softmax_init(m_sc, l_sc, acc_sc)

    nxt = jnp.minimum(step + 1, last_step)
    for i in range(npg):
        for cp in copies(nxt, 1 - slot, i, npg):
            cp.start()
    for i in range(npg):
        for cp in copies(step, slot, i, npg):
            cp.wait()

    for i in range(npg):
        lat_sc[i * PAGE_SIZE:(i + 1) * PAGE_SIZE, :] = lat_buf[slot, i].astype(BF16)
        kr_sc[:QK_ROPE, i * PAGE_SIZE:(i + 1) * PAGE_SIZE] = kr_buf[slot, i].astype(BF16)

    ql, qr = ql_ref[0], qr_ref[0]

    def attend(lat, kr_t, mask):
        s = (lax.dot_general(ql, lat, NT_DIMS, preferred_element_type=F32)
             + jnp.dot(qr, kr_t, preferred_element_type=F32))
        if mask is not None:
            s = jnp.where(mask, s, -jnp.inf)
        _softmax_update(s, lat, m_sc, l_sc, acc_sc)

    attend(lat_sc[...], kr_sc[...], None)

    @pl.when(j == chunks - 1)
    def _():
        shape = (ql.shape[0], PAGE_SIZE)
        t_row = lax.broadcasted_iota(jnp.int32, shape, 0) & (t_new - 1)
        attend(kself_ref[0], krself_ref[0], lax.broadcasted_iota(jnp.int32, shape, 1) <= t_row)
        o_ref[0] = _softmax_result(l_sc, acc_sc)

    @pl.when(step == last_step)
    def _():
        for i in range(npg):
            for cp in copies(last_step, 1 - slot, i, npg):
                cp.wait()


def _mla_sample(ql, qr, kself, krself, lat_pool, rope_pool_t, page_table, e, t_new):
    batch, rows, _ = ql.shape
    n_pages = page_table.shape[1]
    npg = min(PAGES_PER_STEP, n_pages)
    assert n_pages % npg == 0 and t_new & (t_new - 1) == 0 and t_new <= PAGE_SIZE
    chunks = n_pages // npg
    per_b = lambda b, j, pt: (b, 0, 0)
    grid_spec = pltpu.PrefetchScalarGridSpec(
        num_scalar_prefetch=1,
        grid=(batch, chunks),
        in_specs=[pl.BlockSpec((1, rows, KV_LORA), per_b), pl.BlockSpec((1, rows, QK_ROPE), per_b),
                  pl.BlockSpec((1, PAGE_SIZE, KV_LORA), per_b), pl.BlockSpec((1, QK_ROPE, PAGE_SIZE), per_b),
                  pl.BlockSpec(memory_space=pl.ANY), pl.BlockSpec(memory_space=pl.ANY)],
        out_specs=pl.BlockSpec((1, rows, KV_LORA), per_b),
        scratch_shapes=[pltpu.VMEM((2, npg, PAGE_SIZE, KV_LORA), F32),
                        pltpu.VMEM((2, npg, QK_ROPE, PAGE_SIZE), F32),
                        pltpu.SemaphoreType.DMA((2,)), pltpu.SemaphoreType.DMA((2,)),
                        pltpu.VMEM((npg * PAGE_SIZE, KV_LORA), BF16),
                        pltpu.VMEM((QK_ROPE, npg * PAGE_SIZE), BF16),
                        pltpu.VMEM((rows, LANES), F32), pltpu.VMEM((rows, LANES), F32),
                        pltpu.VMEM((rows, KV_LORA), F32)])
    return pl.pallas_call(
        functools.partial(_paged_body, npg=npg, t_new=t_new, e=e),
        grid_spec=grid_spec,
        out_shape=jax.ShapeDtypeStruct((batch, rows, KV_LORA), F32),
        compiler_params=_params(("arbitrary", "arbitrary")),
        name="mla_sample_paged",
    )(page_table.reshape(-1), ql, qr, kself, krself, lat_pool, rope_pool_t)


LRU_TC = 256


def _block_diag(w):
    nb, d, e = w.shape
    return jnp.einsum('nde,nm->ndme', w, jnp.eye(nb, dtype=w.dtype)).reshape(nb * d, nb * e)


def _lru_coeffs(xc, wa, wx, ba, bx, lam):
    xb = xc.astype(BF16)
    r = jax.nn.sigmoid(jnp.dot(xb, wa, preferred_element_type=F32) + ba)
    ig = jax.nn.sigmoid(jnp.dot(xb, wx, preferred_element_type=F32) + bx)
    neg = -lam
    softplus = jnp.maximum(neg, 0.0) + jnp.log1p(jnp.exp(-jnp.abs(neg)))
    log_a = -LRU_C * r * softplus
    a = jnp.exp(log_a)
    t = jnp.tanh(log_a)
    b = jnp.sqrt(-2.0 * t / (1.0 - t)) * (ig * xc)
    return a, b


def _lru_body(x_ref, h0_ref, buf_ref, cw_ref, cb_ref, wa_ref, wx_ref, ba_ref, bx_ref, lam_ref,
              h_ref, hl_ref, tail_ref, xbuf, hc, *, tc):
    c = pl.program_id(1)

    @pl.when(c == 0)
    def _():
        xbuf[0:SUBLANES] = buf_ref[0]
        hc[...] = h0_ref[0]

    x = x_ref[...]
    xbuf[SUBLANES:SUBLANES + tc] = x
    cw = cw_ref[...]
    xc = cb_ref[...] + x * cw[CONV_W - 1:CONV_W]
    for j in range(CONV_W - 1):
        xc = xc + xbuf[pl.ds(SUBLANES - (CONV_W - 1) + j, tc), :] * cw[j:j + 1]
    xbuf[0:SUBLANES] = x[tc - SUBLANES:tc]
    a, b = _lru_coeffs(xc, wa_ref[...], wx_ref[...], ba_ref[...], bx_ref[...], lam_ref[...])
    groups = tc // SUBLANES
    a = a.reshape(groups, SUBLANES, W_B)
    b = b.reshape(groups, SUBLANES, W_B)
    sub = lax.broadcasted_iota(jnp.int32, a.shape, 1)
    d = 1
    while d < SUBLANES:
        keep = sub >= d
        a_sh = jnp.where(keep, pltpu.roll(a, d, 1), 1.0)
        b_sh = jnp.where(keep, pltpu.roll(b, d, 1), 0.0)
        b = a * b_sh + b
        a = a * a_sh
        d *= 2
    carry = hc[...]
    for g in range(groups):
        hg = a[g] * carry + b[g]
        h_ref[g * SUBLANES:(g + 1) * SUBLANES, :] = hg
        carry = hg[SUBLANES - 1:SUBLANES]
    hc[...] = carry

    @pl.when(c == pl.num_programs(1) - 1)
    def _():
        hl_ref[0] = carry
        tail_ref[0] = x[tc - SUBLANES:tc]


def _rglru_seq(zg, batch, seq, h0, buf, conv_w, conv_b, wa_bd, wx_bd, ba, bx, lam):
    tc = min(LRU_TC, seq)
    assert seq % tc == 0 and tc >= SUBLANES
    nc = seq // tc
    buf8 = jnp.pad(buf, ((0, 0), (SUBLANES - (CONV_W - 1), 0), (0, 0)))
    vec = lambda a: a.reshape(1, W_B)
    fixed = lambda b, c: (0, 0)
    per_b = lambda b, c: (b, 0, 0)
    h, hl, tail = pl.pallas_call(
        functools.partial(_lru_body, tc=tc),
        grid=(batch, nc),
        in_specs=[pl.BlockSpec((tc, W_B), lambda b, c: (b * nc + c, ZE_XB // W_B)),
                  pl.BlockSpec((1, 1, W_B), per_b),
                  pl.BlockSpec((1, SUBLANES, W_B), per_b),
                  pl.BlockSpec((CONV_W, W_B), fixed),
                  pl.BlockSpec((1, W_B), fixed),
                  pl.BlockSpec((W_B, W_B), fixed), pl.BlockSpec((W_B, W_B), fixed),
                  pl.BlockSpec((1, W_B), fixed), pl.BlockSpec((1, W_B), fixed), pl.BlockSpec((1, W_B), fixed)],
        out_specs=[pl.BlockSpec((tc, W_B), lambda b, c: (b * nc + c, 0)),
                   pl.BlockSpec((1, 1, W_B), per_b),
                   pl.BlockSpec((1, SUBLANES, W_B), per_b)],
        out_shape=[jax.ShapeDtypeStruct((batch * seq, W_B), F32),
                   jax.ShapeDtypeStruct((batch, 1, W_B), F32),
                   jax.ShapeDtypeStruct((batch, SUBLANES, W_B), F32)],
        scratch_shapes=[pltpu.VMEM((SUBLANES + tc, W_B), F32), pltpu.VMEM((1, W_B), F32)],
        compiler_params=_params(("arbitrary", "arbitrary")),
        name="rglru_seq",
    )(zg, h0.reshape(batch, 1, W_B), buf8, conv_w, vec(conv_b), wa_bd, wx_bd, vec(ba), vec(bx), vec(lam))
    return h, hl[:, 0], tail[:, SUBLANES - (CONV_W - 1):]


def _out_body(v1_ref, v2_ref, v3_ref, g1_ref, g2_ref, g3_ref, x_ref, w_ref, lg_ref, lb_ref, o_ref):
    def gated(v_ref, g_ref):
        g = g_ref[...]
        return (v_ref[...] * (g * jax.nn.sigmoid(g))).astype(BF16)

    mixed = jnp.concatenate([gated(v1_ref, g1_ref), gated(v2_ref, g2_ref), gated(v3_ref, g3_ref)], axis=1)
    u = DEEPNORM_ALPHA * x_ref[...] + jnp.dot(mixed, w_ref[...], preferred_element_type=F32)
    mu = jnp.mean(u, axis=-1, keepdims=True)
    var = jnp.mean(jnp.square(u - mu), axis=-1, keepdims=True)
    o_ref[...] = (u - mu) * lax.rsqrt(var + NORM_EPS) * lg_ref[...] + lb_ref[...]


def _out_proj_norm(vals, z, gate_cols, x2d, w_out, ln_g, ln_b):
    n = x2d.shape[0]
    tm = min(512, n)
    widths = [v.shape[1] for v in vals]
    row = lambda i: (i, 0)
    fixed = lambda i: (0, 0)
    in_specs = [pl.BlockSpec((tm, w), row) for w in widths]
    for w, off in zip(widths, gate_cols):
        assert off % w == 0
        in_specs.append(pl.BlockSpec((tm, w), functools.partial(lambda i, cb: (i, cb), cb=off // w)))
    in_specs += [pl.BlockSpec((tm, D_MODEL), row),
                 pl.BlockSpec((sum(widths), D_MODEL), fixed),
                 pl.BlockSpec((1, D_MODEL), fixed), pl.BlockSpec((1, D_MODEL), fixed)]
    return pl.pallas_call(
        _out_body,
        grid=(n // tm,),
        in_specs=in_specs,
        out_specs=pl.BlockSpec((tm, D_MODEL), row),
        out_shape=jax.ShapeDtypeStruct((n, D_MODEL), F32),
        compiler_params=_params(("arbitrary",)),
        name="out_proj_norm",
    )(*vals, z, z, z, x2d, w_out.astype(BF16), ln_g.reshape(1, -1), ln_b.reshape(1, -1))


def _mem_attend_rows(q, k_t, v_t):
    rows = q.shape[0]
    q_head = lax.shift_right_logical(lax.broadcasted_iota(jnp.int32, q.shape, 1), DH_M.bit_length() - 1)
    qs = jnp.concatenate([jnp.where(q_head == h, q, 0.0) for h in range(H_M)], axis=0).astype(BF16)
    s = jnp.dot(qs, k_t.astype(BF16), preferred_element_type=F32) * (DH_M ** -0.5)
    p = jnp.exp(s - jnp.max(s, axis=-1, keepdims=True))
    p = p / jnp.sum(p, axis=-1, keepdims=True)
    y = lax.dot_general(p.astype(BF16), v_t.astype(BF16), NT_DIMS, preferred_element_type=F32)
    out = jnp.where(q_head == 0, y[0:rows], 0.0)
    for h in range(1, H_M):
        out = out + jnp.where(q_head == h, y[h * rows:(h + 1) * rows], 0.0)
    return out


def _mem_body(q_ref, k_ref, v_ref, o_ref):
    for i in range(k_ref.shape[1]):
        q = q_ref[i] if len(q_ref.shape) == 3 else q_ref[...]
        y = _mem_attend_rows(q, k_ref[0, i], v_ref[0, i])
        if len(o_ref.shape) == 3:
            o_ref[i] = y
        else:
            o_ref[...] = y


def _mem_transposed(mem):
    t = jnp.moveaxis(mem, -3, -1)
    t = t.reshape(t.shape[:-3] + (W_M, N_MEM))
    return t if t.ndim == 4 else t[None]


def _mem_attend_long(z, col, batch, seq, mem_kt, mem_vt, layer):
    tm = min(1024, seq)
    nt = seq // tm
    kv = lambda b, i: (layer, b, 0, 0)
    return pl.pallas_call(
        _mem_body,
        grid=(batch, nt),
        in_specs=[pl.BlockSpec((tm, W_M), lambda b, i: (b * nt + i, col // W_M)),
                  pl.BlockSpec((1, 1, W_M, N_MEM), kv), pl.BlockSpec((1, 1, W_M, N_MEM), kv)],
        out_specs=pl.BlockSpec((tm, W_M), lambda b, i: (b * nt + i, 0)),
        out_shape=jax.ShapeDtypeStruct((batch * seq, W_M), F32),
        compiler_params=_params(("arbitrary", "arbitrary")),
        name="mem_attend_long",
    )(z, mem_kt, mem_vt)


MEM_SHORT_NB = 8


def _mem_attend_short(z, col, batch, seq, mem_kt, mem_vt, layer):
    assert seq <= SUBLANES
    nb = min(MEM_SHORT_NB, batch)
    q = jnp.pad(z[:, col:col + W_M].reshape(batch, seq, W_M), ((0, 0), (0, SUBLANES - seq), (0, 0)))
    blk = lambda i: (i, 0, 0)
    kv = lambda i: (layer, i, 0, 0)
    y = pl.pallas_call(
        _mem_body,
        grid=(batch // nb,),
        in_specs=[pl.BlockSpec((nb, SUBLANES, W_M), blk),
                  pl.BlockSpec((1, nb, W_M, N_MEM), kv), pl.BlockSpec((1, nb, W_M, N_MEM), kv)],
        out_specs=pl.BlockSpec((nb, SUBLANES, W_M), blk),
        out_shape=jax.ShapeDtypeStruct((batch, SUBLANES, W_M), F32),
        compiler_params=_params(("arbitrary",)),
        name="mem_attend_short",
    )(q, mem_kt, mem_vt)
    return y[:, :seq].reshape(batch * seq, W_M)


def _chunk_mlp_body(u_ref, v_ref, g_ref, b_ref, w_ref, bias_ref, y_ref, vn_ref, *, rows, chunks):
    gw = W_D // G_D
    for c in range(chunks):
        sl = pl.ds(c * rows, rows)
        v = v_ref[sl, :]
        mu = jnp.mean(v, axis=-1, keepdims=True)
        var = jnp.mean(jnp.square(v - mu), axis=-1, keepdims=True)
        vn = (v - mu) * lax.rsqrt(var + NORM_EPS) * g_ref[...] + b_ref[...]
        vn_ref[sl, :] = vn
        vb = vn.astype(BF16)
        s = jnp.concatenate([jnp.dot(w_ref[g], vb[:, g * gw:(g + 1) * gw], preferred_element_type=F32)
                             for g in range(G_D)], axis=1)
        y_ref[sl, :] = u_ref[sl, :] * (s + bias_ref[...])


def _chunk_mlp_call(z, rows, ln_g_d, ln_b_d, w_mix, bias):
    n = z.shape[0]
    chunks = max(1, min(4, n // rows))
    tm = rows * chunks
    row = lambda i: (i, 0)
    fixed = lambda i: (0, 0)
    return pl.pallas_call(
        functools.partial(_chunk_mlp_body, rows=rows, chunks=chunks),
        grid=(n // tm,),
        in_specs=[pl.BlockSpec((tm, W_D), lambda i: (i, ZO_U // W_D)),
                  pl.BlockSpec((tm, W_D), lambda i: (i, ZO_VD // W_D)),
                  pl.BlockSpec((1, W_D), fixed), pl.BlockSpec((1, W_D), fixed),
                  pl.BlockSpec((G_D, rows, rows), lambda i: (0, 0, 0)),
                  pl.BlockSpec((rows, W_D), fixed)],
        out_specs=[pl.BlockSpec((tm, W_D), row), pl.BlockSpec((tm, W_D), row)],
        out_shape=[jax.ShapeDtypeStruct((n, W_D), F32), jax.ShapeDtypeStruct((n, W_D), F32)],
        compiler_params=_params(("arbitrary",)),
        name="chunk_mlp",
    )(z, z, ln_g_d.reshape(1, -1), ln_b_d.reshape(1, -1), w_mix, bias)


def _chunk_mlp_weights(sg_w, sg_b):
    L = CHUNK_D
    w = jnp.where(jnp.tril(jnp.ones((L, L), dtype=bool)), sg_w[:, :L, :L], 0.0)
    return w.astype(BF16), jnp.repeat(sg_b[:, :L].T, W_D // G_D, axis=1)


def _chunk_mlp_short_body(u_ref, v_ref, g_ref, b_ref, w_ref, bias_ref, y_ref, vn_ref, *, batch, seq):
    v = v_ref[...]
    mu = jnp.mean(v, axis=-1, keepdims=True)
    var = jnp.mean(jnp.square(v - mu), axis=-1, keepdims=True)
    vn = (v - mu) * lax.rsqrt(var + NORM_EPS) * g_ref[...] + b_ref[...]
    vn_ref[...] = vn
    for t in range(seq):
        acc = vn[0:batch] * w_ref[t * seq:t * seq + 1, :]
        for s in range(1, t + 1):
            acc = acc + vn[s * batch:(s + 1) * batch] * w_ref[t * seq + s:t * seq + s + 1, :]
        rows = slice(t * batch, (t + 1) * batch)
        y_ref[rows, :] = u_ref[rows, :] * (acc + bias_ref[t:t + 1, :])


def _chunk_mlp_short(z, batch, seq, ln_g_d, ln_b_d, sg_w, sg_b):
    gw = W_D // G_D
    w_rows = jnp.repeat(jnp.transpose(sg_w[:, :seq, :seq], (1, 2, 0)).reshape(seq * seq, G_D), gw, axis=1)
    bias = jnp.repeat(sg_b[:, :seq].T, gw, axis=1)

    def time_major(off):
        return jnp.swapaxes(z[:, off:off + W_D].reshape(batch, seq, W_D), 0, 1).reshape(seq * batch, W_D)

    y, vn = pl.pallas_call(
        functools.partial(_chunk_mlp_short_body, batch=batch, seq=seq),
        out_shape=[jax.ShapeDtypeStruct((seq * batch, W_D), F32)] * 2,
        compiler_params=pltpu.CompilerParams(vmem_limit_bytes=VMEM_LIMIT),
        name="chunk_mlp_short",
    )(time_major(ZO_U), time_major(ZO_VD), ln_g_d.reshape(1, -1), ln_b_d.reshape(1, -1), w_rows, bias)
    back = lambda a: jnp.swapaxes(a.reshape(seq, batch, W_D), 0, 1).reshape(batch * seq, W_D)
    return back(y), back(vn)


def _cumsum_rows(x):
    row = lax.broadcasted_iota(jnp.int32, x.shape, 0)
    d = 1
    while d < x.shape[0]:
        x = x + jnp.where(row >= d, pltpu.roll(x, d, 0), 0.0)
        d *= 2
    return x


def _mlstm_body(q_ref, k_ref, v_ref, o_ref, gt_ref, bif_ref, c0_ref, n0_ref, m0_ref,
                y_ref, c_ref, n_ref, m_ref, c_sc, n_sc, m_sc, *stage, nb, n_valid):
    L = MLSTM_CHUNK
    step = pl.program_id(1)
    heads = [slice(h * DH_C, (h + 1) * DH_C) for h in range(H_C)]

    @pl.when(step == 0)
    def _():
        c_sc[...] = jnp.zeros(c_sc.shape, F32)
        for b in range(nb):
            for h, hs in enumerate(heads):
                c_sc[b, hs, hs] = c0_ref[b, h]
                n_sc[b, :, hs] = n0_ref[b, h]
        m_sc[...] = m0_ref[...]
        for st in stage:
            st[...] = jnp.zeros(st.shape, F32)

    shift = DH_C.bit_length() - 1
    row = lax.broadcasted_iota(jnp.int32, (L, LANES), 0)
    lane = lax.broadcasted_iota(jnp.int32, (L, LANES), 1)
    tril = lax.broadcasted_iota(jnp.int32, (L, L), 0) >= lax.broadcasted_iota(jnp.int32, (L, L), 1)
    lane_head = lax.shift_right_logical(lax.broadcasted_iota(jnp.int32, (L, W_C), 1), shift)
    diag_blocks = (lax.shift_right_logical(lax.broadcasted_iota(jnp.int32, (W_C, W_C), 0), shift)
                   == lax.shift_right_logical(lax.broadcasted_iota(jnp.int32, (W_C, W_C), 1), shift))

    def per_head(cols, width):
        return jnp.concatenate([jnp.broadcast_to(c, (c.shape[0], width)) for c in cols], axis=1)

    for b in range(nb):
        if stage:
            rows_in = q_ref.shape[1]
            bufs = []
            for st, ref in zip(stage, (q_ref, k_ref, v_ref, o_ref, gt_ref)):
                st[b, 0:rows_in, :] = ref[b]
                bufs.append(st[b])
            q, k, v, o_pre, gts = bufs
        else:
            q, k, v, o_pre, gts = q_ref[b], k_ref[b], v_ref[b], o_ref[b], gt_ref[b]
        gz = gts + bif_ref[...]
        lf = jnp.minimum(gz, 0.0) - jnp.log1p(jnp.exp(-jnp.abs(gz)))
        ig = gz
        if n_valid < L:
            ig = jnp.where(row < n_valid, ig, -jnp.inf)
            lf = jnp.where(row < n_valid, lf, 0.0)
        a = jnp.where(lane < H_C, ig, _cumsum_rows(lf))
        a_t = a.T
        ks = k * (DH_C ** -0.5)
        vb = v.astype(BF16)
        q_stack = jnp.concatenate([jnp.where(lane_head == h, q, 0.0) for h in range(H_C)], axis=0)
        qk = lax.dot_general(q_stack.astype(BF16), ks.astype(BF16), NT_DIMS, preferred_element_type=F32)
        ws, mts, g_inters, wgs, decays = [], [], [], [], []
        for h in range(H_C):
            ig_row, bc_row = a_t[h:h + 1, :], a_t[H_C + h:H_C + h + 1, :]
            ig_col, bc_col = a[:, h:h + 1], a[:, H_C + h:H_C + h + 1]
            m_old = m_sc[b, h][:, 0:1]
            dmat = jnp.where(tril, bc_col - bc_row + ig_row, -jnp.inf)
            inter = bc_col + m_old
            mt = jnp.maximum(inter, jnp.max(dmat, axis=-1, keepdims=True))
            ws.append(jnp.exp(dmat - mt))
            mts.append(mt)
            g_inters.append(jnp.exp(inter - mt))
            b_end = bc_col[L - 1:L, :]
            g_col = b_end - bc_col + ig_col
            m_new = jnp.maximum(b_end + m_old, jnp.max(g_col, axis=0, keepdims=True))
            decays.append(jnp.exp(b_end + m_old - m_new))
            wgs.append(jnp.exp(g_col - m_new))
            m_sc[b, h] = jnp.broadcast_to(m_new, (1, LANES))
        sc = qk * jnp.concatenate(ws, axis=0)
        sv = jnp.dot(sc.astype(BF16), vb, preferred_element_type=F32)
        num = jnp.where(lane_head == 0, sv[0:L], 0.0)
        for h in range(1, H_C):
            num = num + jnp.where(lane_head == h, sv[h * L:(h + 1) * L], 0.0)
        c_old = c_sc[b]
        n_old = n_sc[b]
        num = num + per_head(g_inters, DH_C) * jnp.dot(q.astype(BF16), c_old.astype(BF16),
                                                        preferred_element_type=F32)
        row_sum = jnp.sum(sc, axis=-1, keepdims=True)
        qn = q * n_old
        dens = []
        for h, hs in enumerate(heads):
            den = row_sum[h * L:(h + 1) * L] + g_inters[h] * jnp.sum(qn[:, hs], axis=-1, keepdims=True)
            dens.append(jnp.maximum(jnp.abs(den), jnp.exp(-mts[h])))
        y = jax.nn.sigmoid(o_pre) * (num / per_head(dens, DH_C))
        y_ref[b] = y[0:y_ref.shape[1]]
        kw = ks * per_head(wgs, DH_C)
        decay = per_head(decays, DH_C)
        upd = jnp.dot(kw.T.astype(BF16), vb, preferred_element_type=F32)
        c_sc[b] = decay * c_old + jnp.where(diag_blocks, upd, 0.0)
        n_sc[b] = decay * n_old + jnp.sum(kw, axis=0, keepdims=True)

    @pl.when(step == pl.num_programs(1) - 1)
    def _():
        for b in range(nb):
            for h, hs in enumerate(heads):
                c_ref[b, h] = c_sc[b, hs, hs]
                n_ref[b, h] = n_sc[b, :, hs]
        m_ref[...] = m_sc[...]


MLSTM_NB = 4


def _mlstm_call(z, batch, seq, b_if, c0, n0, m0):
    L = MLSTM_CHUNK
    nb = min(MLSTM_NB, batch)
    long = seq % L == 0
    if long:
        rows, nc, n_valid = L, seq // L, L
        z3 = z.reshape(batch, seq, z.shape[1])
    else:
        assert seq <= SUBLANES
        rows, nc, n_valid = SUBLANES, 1, seq
        z3 = jnp.pad(z.reshape(batch, seq, z.shape[1]), ((0, 0), (0, SUBLANES - seq), (0, 0)))
    bif = jnp.pad(b_if, (0, LANES - 2 * H_C)).reshape(1, LANES)
    m0r = jnp.broadcast_to(m0[:, :, None, None], (batch, H_C, 1, LANES))

    def col(off, w):
        return pl.BlockSpec((nb, rows, w), functools.partial(lambda g, c, cb: (g, c, cb), cb=off // w))

    st4 = lambda g, c: (g, 0, 0, 0)
    in_specs = [col(ZO_Q, W_C), col(ZO_K, W_C), col(ZO_V, W_C), col(ZO_O, W_C), col(ZO_IF, LANES),
                pl.BlockSpec((1, LANES), lambda g, c: (0, 0)),
                pl.BlockSpec((nb, H_C, DH_C, DH_C), st4),
                pl.BlockSpec((nb, H_C, 1, DH_C), st4),
                pl.BlockSpec((nb, H_C, 1, LANES), st4)]
    out_specs = [pl.BlockSpec((nb, rows, W_C), lambda g, c: (g, c, 0)),
                 pl.BlockSpec((nb, H_C, DH_C, DH_C), st4),
                 pl.BlockSpec((nb, H_C, 1, DH_C), st4),
                 pl.BlockSpec((nb, H_C, 1, LANES), st4)]
    scratch = [pltpu.VMEM((nb, W_C, W_C), F32), pltpu.VMEM((nb, 1, W_C), F32),
               pltpu.VMEM((nb, H_C, 1, LANES), F32)]
    if not long:
        scratch += [pltpu.VMEM((nb, L, W_C), F32)] * 4 + [pltpu.VMEM((nb, L, LANES), F32)]
    y, c, n, m = pl.pallas_call(
        functools.partial(_mlstm_body, nb=nb, n_valid=n_valid),
        grid=(batch // nb, nc),
        in_specs=in_specs,
        out_specs=out_specs,
        out_shape=[jax.ShapeDtypeStruct((batch, rows * nc, W_C), F32),
                   jax.ShapeDtypeStruct((batch, H_C, DH_C, DH_C), F32),
                   jax.ShapeDtypeStruct((batch, H_C, 1, DH_C), F32),
                   jax.ShapeDtypeStruct((batch, H_C, 1, LANES), F32)],
        scratch_shapes=scratch,
        compiler_params=_params(("arbitrary", "arbitrary")),
        name="mlstm_chunks",
    )(z3, z3, z3, z3, z3, bif, c0, n0[:, :, None, :], m0r)
    return y[:, :seq].reshape(batch * seq, W_C), c, n[:, :, 0], m[:, :, 0, 0]


def _lru_short_body(x_ref, h0_ref, buf_ref, cw_ref, cb_ref, wa_ref, wx_ref, ba_ref, bx_ref, lam_ref,
                    h_ref, *, batch, seq):
    cw = cw_ref[...]
    xp = jnp.concatenate([buf_ref[...], x_ref[...]], axis=0)
    xc = cb_ref[...] + xp[(CONV_W - 1) * batch:] * cw[CONV_W - 1:CONV_W]
    for j in range(CONV_W - 1):
        xc = xc + xp[j * batch:(j + seq) * batch] * cw[j:j + 1]
    a, b = _lru_coeffs(xc, wa_ref[...], wx_ref[...], ba_ref[...], bx_ref[...], lam_ref[...])
    h = h0_ref[...]
    for t in range(seq):
        sl = slice(t * batch, (t + 1) * batch)
        h = a[sl] * h + b[sl]
        h_ref[sl, :] = h


def _rglru_short(zg, batch, seq, h0, buf, conv_w, conv_b, wa_bd, wx_bd, ba, bx, lam):
    assert batch % SUBLANES == 0
    x = zg[:, ZE_XB:ZE_XB + W_B].reshape(batch, seq, W_B)
    xp = jnp.concatenate([buf, x], axis=1)
    x_tm = jnp.swapaxes(x, 0, 1).reshape(seq * batch, W_B)
    buf_tm = jnp.swapaxes(buf, 0, 1).reshape((CONV_W - 1) * batch, W_B)
    vec = lambda a: a.reshape(1, W_B)
    h_tm = pl.pallas_call(
        functools.partial(_lru_short_body, batch=batch, seq=seq),
        out_shape=jax.ShapeDtypeStruct((seq * batch, W_B), F32),
        compiler_params=pltpu.CompilerParams(vmem_limit_bytes=VMEM_LIMIT),
        name="rglru_short",
    )(x_tm, h0, buf_tm, conv_w, vec(conv_b), wa_bd, wx_bd, vec(ba), vec(bx), vec(lam))
    h = jnp.swapaxes(h_tm.reshape(seq, batch, W_B), 0, 1)
    return h.reshape(batch * seq, W_B), h[:, -1], xp[:, -(CONV_W - 1):]


def _mem_kv(mem, w_mk, w_mv):
    B = mem.shape[0]
    kv = _proj(mem, jnp.concatenate([w_mk, w_mv], axis=1))
    return (kv[..., :W_M].reshape(B, N_MEM, H_M, DH_M), kv[..., W_M:].reshape(B, N_MEM, H_M, DH_M))


def _sample_queries(qp, batch, t_new):
    n = qp.shape[1]
    r1 = jnp.stack([qp[h, :, KV_LORA + h * ROPE_HALF:KV_LORA + (h + 1) * ROPE_HALF] for h in range(H_A)])
    r2 = jnp.stack([qp[h, :, KV_LORA + LANES + h * ROPE_HALF:KV_LORA + LANES + (h + 1) * ROPE_HALF]
                    for h in range(H_A)])
    qr = jnp.concatenate([r1, r2], axis=-1)

    def rows(a):
        w = a.shape[-1]
        return a.reshape(H_A, batch, t_new, w).transpose(1, 0, 2, 3).reshape(batch, H_A * t_new, w)

    return rows(qp[:, :, :KV_LORA]), rows(qr)


def _sample_self_keys(latent, k_rope, batch, t_new):
    kl = jnp.pad(latent.astype(BF16).reshape(batch, t_new, KV_LORA), ((0, 0), (0, PAGE_SIZE - t_new), (0, 0)))
    kr_t = jnp.swapaxes(k_rope.astype(BF16).reshape(batch, t_new, QK_ROPE), 1, 2)
    return kl, jnp.pad(kr_t, ((0, 0), (0, 0), (0, PAGE_SIZE - t_new)))


def _even_layer(x2d, batch, seq, tables, mem, lru_h0, lru_buf, paged, e, weights,
                q_norm, kv_norm, conv_w, conv_b, wa, ba, wx, bx, lam, w_out, ln_g, ln_b):
    w_in_r, w_uq_r, w_uk_r, w_uv_bd = weights
    zg, latent, k_rope, kp, qp = _even_in(x2d, tables, w_in_r, q_norm, kv_norm, w_uq_r, w_uk_r)
    lru_args = (lru_h0, lru_buf, conv_w, conv_b, _block_diag(wa).astype(BF16), _block_diag(wx).astype(BF16),
                ba, bx, lam)
    if paged is None:
        y_a = _mla_prompt(qp, kp, w_uv_bd, batch, seq)
        h_b, h_last, new_buf = _rglru_seq(zg, batch, seq, *lru_args)
        y_m = _mem_attend_long(zg, ZE_QM, batch, seq, *mem)
    else:
        lat_pool, rope_pool, page_table = paged
        o = _mla_sample(*_sample_queries(qp, batch, seq), *_sample_self_keys(latent, k_rope, batch, seq),
                        lat_pool, jnp.swapaxes(rope_pool, 2, 3), page_table, e, seq)
        o = o.reshape(batch, H_A, seq, KV_LORA).transpose(0, 2, 1, 3).reshape(batch * seq, H_A * KV_LORA)
        y_a = _matmul(o, w_uv_bd)
        h_b, h_last, new_buf = _rglru_short(zg, batch, seq, *lru_args)
        y_m = _mem_attend_short(zg, ZE_QM, batch, seq, *mem)
    x_new = _out_proj_norm([y_a, h_b, y_m], zg, (ZE_GA, ZE_GB, ZE_GM), x2d, w_out, ln_g, ln_b)
    return (x_new, latent.reshape(batch, seq, KV_LORA), k_rope.reshape(batch, seq, QK_ROPE), h_last, new_buf)


def _odd_weights(w_in):
    q, k, v, i_pre, f_pre, o_pre, g_c, u_d, v_d, g_d, q_m, g_m = _split_cols(w_in, ODD_SPLITS)
    gates = jnp.pad(jnp.concatenate([i_pre, f_pre], axis=1), ((0, 0), (0, LANES - 2 * H_C)))
    return jnp.concatenate([q, k, v, o_pre, g_c, u_d, v_d, g_d, q_m, g_m, gates], axis=1)


def _odd_layer(x2d, batch, seq, mem, c0, n0, m0, w_in_r, b_if, ln_g_d, ln_b_d, sg_w, sg_b,
               w_out, ln_g, ln_b):
    z = _matmul(x2d, w_in_r)
    y_c, c, n, m = _mlstm_call(z, batch, seq, b_if, c0, n0, m0)
    if seq % CHUNK_D == 0:
        y_d, vn = _chunk_mlp_call(z, CHUNK_D, ln_g_d, ln_b_d, *_chunk_mlp_weights(sg_w, sg_b))
        y_m = _mem_attend_long(z, ZO_QM, batch, seq, *mem)
    else:
        y_d, vn = _chunk_mlp_short(z, batch, seq, ln_g_d, ln_b_d, sg_w, sg_b)
        y_m = _mem_attend_short(z, ZO_QM, batch, seq, *mem)
    x_new = _out_proj_norm([y_c, y_d, y_m], z, (ZO_GC, ZO_GD, ZO_GM), x2d, w_out, ln_g, ln_b)
    return x_new, vn.reshape(batch, seq, W_D), c, n, m


def kernel(x_prompt, x_sample, cache_mla_latent, cache_mla_krope, state_lru_h, state_lru_conv,
           state_mlstm_c, state_mlstm_n, state_mlstm_m, cache_mem_k, cache_mem_v, page_table,
           mem_prompt, w_in_even, mla_q_norm, mla_kv_norm, w_uq, w_uk, w_uv,
           lru_conv_w, lru_conv_b, lru_wa, lru_ba, lru_wx, lru_bx, lru_lambda, w_out_even,
           w_in_odd, mlstm_b_if, sg_ln_g, sg_ln_b, sg_w, sg_b, w_out_odd,
           w_mem_k, w_mem_v, ln_g, ln_b):
    Bp, Tp, _ = x_prompt.shape
    Bs, Ts, _ = x_sample.shape
    past_len = page_table.shape[1] * PAGE_SIZE
    tables_p = _rope_tables(jnp.arange(Tp, dtype=F32))
    tables_s = tuple(jnp.tile(t, (Bs, 1)) for t in _rope_tables(past_len + jnp.arange(Ts, dtype=F32)))

    h0_p = jnp.zeros((Bp, W_B), F32)
    buf0_p = jnp.zeros((Bp, CONV_W - 1, W_B), F32)
    c0_p = jnp.zeros((Bp, H_C, DH_C, DH_C), F32)
    n0_p = jnp.zeros((Bp, H_C, DH_C), F32)
    m0_p = jnp.zeros((Bp, H_C), F32)

    lat_p, kr_p, h_p, conv_p, c_p, n_p, m_p, mk_p, mv_p = [], [], [], [], [], [], [], [], []
    lat_s, kr_s, h_s, conv_s, c_s, n_s, m_s, v_s = [], [], [], [], [], [], [], []

    xp = x_prompt.reshape(Bp * Tp, D_MODEL)
    xs = x_sample.reshape(Bs * Ts, D_MODEL)
    mem_s = (_mem_transposed(cache_mem_k), _mem_transposed(cache_mem_v))
    for l in range(DEPTH):
        mk_l, mv_l = _mem_kv(mem_prompt, w_mem_k[l], w_mem_v[l])
        mk_p.append(mk_l)
        mv_p.append(mv_l)
        mem_p = (_mem_transposed(mk_l), _mem_transposed(mv_l), 0)
        if l % 2 == 0:
            e = l // 2
            weights = _even_weights(w_in_even[e], w_uq[e], w_uk[e], w_uv[e])
            rest = (mla_q_norm[e], mla_kv_norm[e], lru_conv_w[e], lru_conv_b[e], lru_wa[e], lru_ba[e],
                    lru_wx[e], lru_bx[e], lru_lambda[e], w_out_even[e], ln_g[l], ln_b[l])
            xp, la, kr, hl, cb = _even_layer(xp, Bp, Tp, tables_p, mem_p, h0_p, buf0_p, None, e,
                                             weights, *rest)
            lat_p.append(la); kr_p.append(kr); h_p.append(hl); conv_p.append(cb)
            xs, la, kr, hl, cb = _even_layer(xs, Bs, Ts, tables_s, mem_s + (l,),
                                             state_lru_h[e], state_lru_conv[e],
                                             (cache_mla_latent, cache_mla_krope, page_table), e, weights, *rest)
            lat_s.append(la); kr_s.append(kr); h_s.append(hl); conv_s.append(cb)
        else:
            o = l // 2
            ow = (_odd_weights(w_in_odd[o]), mlstm_b_if[o], sg_ln_g[o], sg_ln_b[o], sg_w[o], sg_b[o],
                  w_out_odd[o], ln_g[l], ln_b[l])
            xp, _, cc, nn, mm = _odd_layer(xp, Bp, Tp, mem_p, c0_p, n0_p, m0_p, *ow)
            c_p.append(cc); n_p.append(nn); m_p.append(mm)
            xs, vn, cc, nn, mm = _odd_layer(xs, Bs, Ts, mem_s + (l,),
                                            state_mlstm_c[o], state_mlstm_n[o], state_mlstm_m[o], *ow)
            c_s.append(cc); n_s.append(nn); m_s.append(mm); v_s.append(vn)

    return (xp.reshape(Bp, Tp, D_MODEL), xs.reshape(Bs, Ts, D_MODEL),
            jnp.stack(lat_p), jnp.stack(kr_p), jnp.stack(h_p), jnp.stack(conv_p),
            jnp.stack(c_p), jnp.stack(n_p), jnp.stack(m_p), jnp.stack(mk_p), jnp.stack(mv_p),
            jnp.stack(lat_s), jnp.stack(kr_s), jnp.stack(h_s), jnp.stack(conv_s),
            jnp.stack(c_s), jnp.stack(n_s), jnp.stack(m_s), jnp.stack(v_s))
```

```python
import functools

import jax
import jax.numpy as jnp
import numpy as np
from jax import lax
from jax.experimental import pallas as pl
from jax.experimental.pallas import tpu as pltpu

D_MODEL = 1024
DEPTH = 2
PAGE_SIZE = 128
H_A = 8
Q_LORA = 384
KV_LORA = 256
QK_NOPE = 64
QK_ROPE = 32
ROPE_HALF = QK_ROPE // 2
V_HEAD = 64
W_A = H_A * V_HEAD
ROPE_THETA = 10000.0
MLA_SCALE = (QK_NOPE + QK_ROPE) ** -0.5
W_B = 512
NB_B = 8
BD_B = W_B // NB_B
CONV_W = 4
LRU_C = 8.0
H_C = 4
DH_C = 128
W_C = H_C * DH_C
MLSTM_CHUNK = 128
G_D = 4
W_D = 512
CHUNK_D = 128
N_MEM = 256
H_M = 4
DH_M = 64
W_M = H_M * DH_M
NORM_EPS = 1e-6
DEEPNORM_ALPHA = (2 * DEPTH) ** 0.25

EVEN_SPLITS = (Q_LORA, KV_LORA, QK_ROPE, W_A, W_B, W_B, W_M, W_M)
ODD_SPLITS = (W_C, W_C, W_C, H_C, H_C, W_C, W_C, W_D, W_D, W_D, W_M, W_M)

F32 = jnp.float32
BF16 = jnp.bfloat16
LANES = 128
SUBLANES = 8
VMEM_LIMIT = 48 * 1024 * 1024
LOG2E = 1.4426950408889634
NT_DIMS = (((1,), (1,)), ((), ()))

ZE_GA, ZE_XB, ZE_GB, ZE_QM, ZE_GM = 0, 512, 1024, 1536, 1792
ZE_GATES = 2048
ZE_CKV = 2048
ZE_CQ = ZE_CKV + KV_LORA
ZE_KR1 = ZE_CQ + Q_LORA
ZE_KR2 = ZE_KR1 + LANES
ZE_KRN = ZE_KR2 + LANES
ZE_KRS = ZE_KRN + LANES
ZE_W = ZE_KRS + LANES
QP_W = KV_LORA + 2 * LANES
UQ_NOPE_W = H_A * LANES
ZO_Q, ZO_K, ZO_V, ZO_O, ZO_GC, ZO_U, ZO_VD, ZO_GD = (i * 512 for i in range(8))
ZO_QM, ZO_GM, ZO_IF = 4096, 4352, 4608
ZO_W = ZO_IF + LANES


def _split_cols(z, sizes):
    cuts = [int(c) for c in np.cumsum(sizes)[:-1]]
    return jnp.split(z, cuts, axis=-1)


def _params(sem):
    return pltpu.CompilerParams(dimension_semantics=sem, vmem_limit_bytes=VMEM_LIMIT)


def _mm_body(x_ref, w_ref, o_ref):
    o_ref[...] = jnp.dot(x_ref[...].astype(BF16), w_ref[...], preferred_element_type=F32)


def _row_tile(m, n):
    tm = 512
    while tm > SUBLANES and (tm * n * 4 * 2 > 10 * 1024 * 1024 or m % tm):
        tm //= 2
    return tm


def _matmul(x, w):
    m, k = x.shape
    n = w.shape[1]
    n_pad = -n % LANES
    wb = w.astype(BF16)
    if n_pad:
        wb = jnp.pad(wb, ((0, 0), (0, n_pad)))
    np_ = n + n_pad
    tm = _row_tile(m, np_)
    out = pl.pallas_call(
        _mm_body,
        grid=(m // tm,),
        in_specs=[pl.BlockSpec((tm, k), lambda i: (i, 0)),
                  pl.BlockSpec((k, np_), lambda i: (0, 0))],
        out_specs=pl.BlockSpec((tm, np_), lambda i: (i, 0)),
        out_shape=jax.ShapeDtypeStruct((m, np_), F32),
        compiler_params=_params(("arbitrary",)),
        name="row_matmul",
    )(x, wb)
    return out[:, :n] if n_pad else out


def _proj(x, w):
    lead = x.shape[:-1]
    return _matmul(x.reshape(-1, x.shape[-1]), w).reshape(lead + (w.shape[1],))


def _even_weights(w_in, w_uq, w_uk, w_uv):
    c_q, c_kv, kr, g_a, x_b, g_b, q_m, g_m = _split_cols(w_in, EVEN_SPLITS)
    x1, x2 = kr[:, :ROPE_HALF], kr[:, ROPE_HALF:]

    def lane_pad(a):
        return jnp.pad(a, ((0, 0), (0, LANES - a.shape[1])))

    w_in_r = jnp.concatenate(
        [g_a, x_b, g_b, q_m, g_m, c_kv, c_q, jnp.tile(x1, (1, H_A)), jnp.tile(x2, (1, H_A)),
         lane_pad(kr), lane_pad(jnp.concatenate([x2, x1], axis=1))], axis=1).astype(BF16)
    r = w_uq.reshape(Q_LORA, H_A, QK_NOPE + QK_ROPE)
    nope = jnp.pad(r[:, :, :QK_NOPE], ((0, 0), (0, 0), (0, LANES - QK_NOPE))).reshape(Q_LORA, UQ_NOPE_W)
    r1 = r[:, :, QK_NOPE:QK_NOPE + ROPE_HALF].reshape(Q_LORA, LANES)
    r2 = r[:, :, QK_NOPE + ROPE_HALF:].reshape(Q_LORA, LANES)
    w_uq_r = jnp.concatenate([nope, r1, r2], axis=1).astype(BF16)
    w_uk_r = jnp.pad(jnp.transpose(w_uk, (1, 2, 0)), ((0, 0), (0, LANES - QK_NOPE), (0, 0))).astype(BF16)
    eye = jnp.eye(H_A, dtype=w_uv.dtype)
    w_uv_bd = jnp.einsum('chv,hg->hcgv', w_uv, eye).reshape(H_A * KV_LORA, W_A).astype(BF16)
    return w_in_r, w_uq_r, w_uk_r, w_uv_bd


def _rope_tables(pos):
    inv = ROPE_THETA ** (-jnp.arange(ROPE_HALF, dtype=F32) / ROPE_HALF)
    ang = pos.astype(F32)[:, None] * inv[None, :]
    cos, sin = jnp.cos(ang), jnp.sin(ang)
    zpad = jnp.zeros((pos.shape[0], LANES - QK_ROPE), F32)
    return (jnp.tile(cos, (1, H_A)), jnp.tile(sin, (1, H_A)),
            jnp.concatenate([cos, cos, zpad], axis=1), jnp.concatenate([-sin, sin, zpad], axis=1))


def _rms(x, g):
    return x * lax.rsqrt(jnp.mean(x * x, axis=-1, keepdims=True) + NORM_EPS) * g


def _even_in_body(x_ref, w_ref, qn_ref, kvn_ref, wuq_ref, wuk_ref, cos_ref, sin_ref, cosn_ref, sinn_ref,
                  zg_ref, lat_ref, kr_ref, kp_ref, qp_ref):
    z = jnp.dot(x_ref[...].astype(BF16), w_ref[...], preferred_element_type=F32)
    zg_ref[...] = z[:, :ZE_GATES]
    lat = _rms(z[:, ZE_CKV:ZE_CQ], kvn_ref[...])
    lat_ref[...] = lat
    cos, sin = cos_ref[...], sin_ref[...]
    kr1, kr2 = z[:, ZE_KR1:ZE_KR2], z[:, ZE_KR2:ZE_KRN]
    kp_ref[...] = jnp.concatenate([lat, kr1 * cos - kr2 * sin, kr1 * sin + kr2 * cos], axis=1).astype(BF16)
    kr_nat = z[:, ZE_KRN:ZE_KRS] * cosn_ref[...] + z[:, ZE_KRS:ZE_W] * sinn_ref[...]
    kr_ref[...] = kr_nat[:, :QK_ROPE]
    q = jnp.dot(_rms(z[:, ZE_CQ:ZE_KR1], qn_ref[...]).astype(BF16), wuq_ref[...],
                preferred_element_type=F32)
    q1, q2 = q[:, UQ_NOPE_W:UQ_NOPE_W + LANES], q[:, UQ_NOPE_W + LANES:]
    o1, o2 = q1 * cos - q2 * sin, q1 * sin + q2 * cos
    lane_head = lax.shift_right_logical(lax.broadcasted_iota(jnp.int32, o1.shape, 1), ROPE_HALF.bit_length() - 1)
    for h in range(H_A):
        ql = jnp.dot(q[:, h * LANES:(h + 1) * LANES].astype(BF16), wuk_ref[h], preferred_element_type=F32)
        own = lane_head == h
        qh = jnp.concatenate([ql, jnp.where(own, o1, 0.0), jnp.where(own, o2, 0.0)], axis=1)
        qp_ref[h] = (qh * (MLA_SCALE * LOG2E)).astype(BF16)


def _even_in(x2d, tables, w_in_r, q_norm, kv_norm, w_uq_r, w_uk_r):
    n = x2d.shape[0]
    tm = min(256, n)
    period = tables[0].shape[0] // tm
    row = lambda i: (i, 0)
    fixed2 = lambda i: (0, 0)
    tab = lambda i: (i % period, 0)
    return pl.pallas_call(
        _even_in_body,
        grid=(n // tm,),
        in_specs=[pl.BlockSpec((tm, D_MODEL), row),
                  pl.BlockSpec((D_MODEL, ZE_W), fixed2),
                  pl.BlockSpec((1, Q_LORA), fixed2),
                  pl.BlockSpec((1, KV_LORA), fixed2),
                  pl.BlockSpec((Q_LORA, UQ_NOPE_W + 2 * LANES), fixed2),
                  pl.BlockSpec((H_A, LANES, KV_LORA), lambda i: (0, 0, 0)),
                  pl.BlockSpec((tm, LANES), tab), pl.BlockSpec((tm, LANES), tab),
                  pl.BlockSpec((tm, LANES), tab), pl.BlockSpec((tm, LANES), tab)],
        out_specs=[pl.BlockSpec((tm, ZE_GATES), row),
                   pl.BlockSpec((tm, KV_LORA), row),
                   pl.BlockSpec((tm, QK_ROPE), row),
                   pl.BlockSpec((tm, QP_W), row),
                   pl.BlockSpec((H_A, tm, QP_W), lambda i: (0, i, 0))],
        out_shape=[jax.ShapeDtypeStruct((n, ZE_GATES), F32),
                   jax.ShapeDtypeStruct((n, KV_LORA), F32),
                   jax.ShapeDtypeStruct((n, QK_ROPE), F32),
                   jax.ShapeDtypeStruct((n, QP_W), BF16),
                   jax.ShapeDtypeStruct((H_A, n, QP_W), BF16)],
        compiler_params=_params(("arbitrary",)),
        name="even_in_proj",
    )(x2d, w_in_r, q_norm.reshape(1, -1), kv_norm.reshape(1, -1), w_uq_r, w_uk_r, *tables)


def _softmax_update(s, vals, m_sc, l_sc, acc_sc):
    tiles = [s[:, c * LANES:(c + 1) * LANES] for c in range(s.shape[1] // LANES)]
    m_prev = m_sc[...]
    m_new = jnp.maximum(m_prev, jnp.max(functools.reduce(jnp.maximum, tiles), axis=-1, keepdims=True))
    alpha = jnp.exp2(m_prev - m_new)
    ps = [jnp.exp2(t - m_new) for t in tiles]
    l_sc[...] = alpha * l_sc[...] + functools.reduce(jnp.add, ps)
    pv = jnp.dot(jnp.concatenate(ps, axis=1).astype(BF16), vals, preferred_element_type=F32)
    acc = acc_sc[...]
    acc_sc[...] = jnp.concatenate([acc[:, c * LANES:(c + 1) * LANES] * alpha
                                   for c in range(acc.shape[1] // LANES)], axis=1) + pv
    m_sc[...] = m_new


def _softmax_result(l_sc, acc_sc):
    return acc_sc[...] / jnp.sum(l_sc[...], axis=-1, keepdims=True)


def _softmax_init(m_sc, l_sc, acc_sc):
    m_sc[...] = jnp.full(m_sc.shape, -jnp.inf, F32)
    l_sc[...] = jnp.zeros(l_sc.shape, F32)
    acc_sc[...] = jnp.zeros(acc_sc.shape, F32)


FLASH_TQ = 256
FLASH_TK = 1024
FLAG_FIRST, FLAG_LAST = 1, 2
KIND_SHIFT = 2


def _flash_body(qb, kb, qo, ko, fl, q_ref, k_ref, wuv_ref, o_ref, m_sc, l_sc, acc_sc, s_sc, *, tq, tk):
    i = pl.program_id(0)
    flags = fl[i]
    kind = lax.shift_right_logical(flags, KIND_SHIFT)
    widths = [tk] + [v * tq for v in range(1, tk // tq + 1)]

    @pl.when((flags & FLAG_FIRST) != 0)
    def _():
        _softmax_init(m_sc, l_sc, acc_sc)

    for v, n in enumerate(widths):
        @pl.when(kind == v)
        def _():
            s_sc[:, 0:n] = lax.dot_general(q_ref[...].reshape(H_A * tq, QP_W), k_ref[0:n, :], NT_DIMS,
                                           preferred_element_type=F32)

    for v, n in enumerate(widths):
        @pl.when(kind == v)
        def _():
            s = s_sc[:, 0:n]
            if v > 0:
                qpos = (lax.broadcasted_iota(jnp.int32, s.shape, 0) & (tq - 1)) + qo[i]
                kpos = lax.broadcasted_iota(jnp.int32, s.shape, 1) + ko[i]
                s = jnp.where(kpos <= qpos, s, -jnp.inf)
            _softmax_update(s, k_ref[0:n, 0:KV_LORA], m_sc, l_sc, acc_sc)

    @pl.when((flags & FLAG_LAST) != 0)
    def _():
        o = _softmax_result(l_sc, acc_sc)
        o_all = jnp.concatenate([o[h * tq:(h + 1) * tq] for h in range(H_A)], axis=1).astype(BF16)
        o_ref[...] = jnp.dot(o_all, wuv_ref[...], preferred_element_type=F32)


def _flash_steps(batch, seq, tq, tk):
    nq, nk = seq // tq, seq // tk
    qb, kb, qo, ko, fl = [], [], [], [], []
    for b in range(batch):
        for qi in range(nq):
            last = ((qi + 1) * tq - 1) // tk
            for kj in range(last + 1):
                qb.append(b * nq + qi)
                kb.append(b * nk + kj)
                qo.append(qi * tq)
                ko.append(kj * tk)
                causal = (kj + 1) * tk - 1 > qi * tq
                kind = ((qi + 1) * tq - kj * tk) // tq if causal else 0
                fl.append((FLAG_FIRST if kj == 0 else 0) | (FLAG_LAST if kj == last else 0) | (kind << KIND_SHIFT))
    return [np.asarray(a, np.int32) for a in (qb, kb, qo, ko, fl)]


def _mla_prompt(qp, kp, w_uv_bd, batch, seq):
    tq, tk = min(FLASH_TQ, seq), min(FLASH_TK, seq)
    assert tq & (tq - 1) == 0 and seq % tq == 0 and seq % tk == 0 and tk % tq == 0
    steps = _flash_steps(batch, seq, tq, tk)
    n = batch * seq
    rows = H_A * tq
    grid_spec = pltpu.PrefetchScalarGridSpec(
        num_scalar_prefetch=5,
        grid=(steps[0].shape[0],),
        in_specs=[pl.BlockSpec((H_A, tq, QP_W), lambda i, qb, kb, qo, ko, fl: (0, qb[i], 0)),
                  pl.BlockSpec((tk, QP_W), lambda i, qb, kb, qo, ko, fl: (kb[i], 0)),
                  pl.BlockSpec((H_A * KV_LORA, W_A), lambda i, qb, kb, qo, ko, fl: (0, 0))],
        out_specs=pl.BlockSpec((tq, W_A), lambda i, qb, kb, qo, ko, fl: (qb[i], 0)),
        scratch_shapes=[pltpu.VMEM((rows, LANES), F32), pltpu.VMEM((rows, LANES), F32),
                        pltpu.VMEM((rows, KV_LORA), F32), pltpu.VMEM((rows, tk), F32)])
    return pl.pallas_call(
        functools.partial(_flash_body, tq=tq, tk=tk),
        grid_spec=grid_spec,
        out_shape=jax.ShapeDtypeStruct((n, W_A), F32),
        compiler_params=_params(("arbitrary",)),
        name="mla_prompt_flash",
    )(*[jnp.asarray(a) for a in steps], qp, kp, w_uv_bd)


PAGES_PER_STEP = 64


def _page_copies(pt_ref, lat_hbm, kr_hbm, lat_buf, kr_buf, lat_sem, kr_sem, e, step, slot, i, npg):
    page = pt_ref[step * npg + i]
    return (pltpu.make_async_copy(lat_hbm.at[e, page], lat_buf.at[slot, i], lat_sem.at[slot]),
            pltpu.make_async_copy(kr_hbm.at[e, page], kr_buf.at[slot, i], kr_sem.at[slot]))


def _paged_body(pt_ref, ql_ref, qr_ref, kself_ref, krself_ref, lat_hbm, kr_hbm, o_ref,
                lat_buf, kr_buf, lat_sem, kr_sem, lat_sc, kr_sc, m_sc, l_sc, acc_sc, *, npg, t_new, e):
    j = pl.program_id(1)
    chunks = pl.num_programs(1)
    step = pl.program_id(0) * chunks + j
    last_step = pl.num_programs(0) * chunks - 1
    slot = step & 1
    copies = functools.partial(_page_copies, pt_ref, lat_hbm, kr_hbm, lat_buf, kr_buf, lat_sem, kr_sem, e)

    @pl.when(step == 0)
    def _():
        for i in range(npg):
            for cp in copies(0, 0, i, npg):
                cp.start()

    @pl.when(j == 0)
    def _():
        _softmax_init(m_sc, l_sc, acc_sc)

    nxt = jnp.minimum(step + 1, last_step)
    for i in range(npg):
        for cp in copies(nxt, 1 - slot, i, npg):
            cp.start()
    for i in range(npg):
        for cp in copies(step, slot, i, npg):
            cp.wait()

    for i in range(npg):
        lat_sc[i * PAGE_SIZE:(i + 1) * PAGE_SIZE, :] = lat_buf[slot, i].astype(BF16)
        kr_sc[:QK_ROPE, i * PAGE_SIZE:(i + 1) * PAGE_SIZE] = kr_buf[slot, i].astype(BF16)

    ql, qr = ql_ref[0], qr_ref[0]

    def attend(lat, kr_t, mask):
        s = (lax.dot_general(ql, lat, NT_DIMS, preferred_element_type=F32)
             + jnp.dot(qr, kr_t, preferred_element_type=F32))
        if mask is not None:
            s = jnp.where(mask, s, -jnp.inf)
        _softmax_update(s, lat, m_sc, l_sc, acc_sc)

    attend(lat_sc[...], kr_sc[...], None)

    @pl.when(j == chunks - 1)
    def _():
        shape = (ql.shape[0], PAGE_SIZE)
        t_row = lax.broadcasted_iota(jnp.int32, shape, 0) & (t_new - 1)
        attend(kself_ref[0], krself_ref[0], lax.broadcasted_iota(jnp.int32, shape, 1) <= t_row)
        o_ref[0] = _softmax_result(l_sc, acc_sc)

    @pl.when(step == last_step)
    def _():
        for i in range(npg):
            for cp in copies(last_step, 1 - slot, i, npg):
                cp.wait()


def _mla_sample(ql, qr, kself, krself, lat_pool, rope_pool_t, page_table, e, t_new):
    batch, rows, _ = ql.shape
    n_pages = page_table.shape[1]
    npg = min(PAGES_PER_STEP, n_pages)
    assert n_pages % npg == 0 and t_new & (t_new - 1) == 0 and t_new <= PAGE_SIZE
    chunks = n_pages // npg
    per_b = lambda b, j, pt: (b, 0, 0)
    grid_spec = pltpu.PrefetchScalarGridSpec(
        num_scalar_prefetch=1,
        grid=(batch, chunks),
        in_specs=[pl.BlockSpec((1, rows, KV_LORA), per_b), pl.BlockSpec((1, rows, QK_ROPE), per_b),
                  pl.BlockSpec((1, PAGE_SIZE, KV_LORA), per_b), pl.BlockSpec((1, QK_ROPE, PAGE_SIZE), per_b),
                  pl.BlockSpec(memory_space=pl.ANY), pl.BlockSpec(memory_space=pl.ANY)],
        out_specs=pl.BlockSpec((1, rows, KV_LORA), per_b),
        scratch_shapes=[pltpu.VMEM((2, npg, PAGE_SIZE, KV_LORA), F32),
                        pltpu.VMEM((2, npg, QK_ROPE, PAGE_SIZE), F32),
                        pltpu.SemaphoreType.DMA((2,)), pltpu.SemaphoreType.DMA((2,)),
                        pltpu.VMEM((npg * PAGE_SIZE, KV_LORA), BF16),
                        pltpu.VMEM((QK_ROPE, npg * PAGE_SIZE), BF16),
                        pltpu.VMEM((rows, LANES), F32), pltpu.VMEM((rows, LANES), F32),
                        pltpu.VMEM((rows, KV_LORA), F32)])
    return pl.pallas_call(
        functools.partial(_paged_body, npg=npg, t_new=t_new, e=e),
        grid_spec=grid_spec,
        out_shape=jax.ShapeDtypeStruct((batch, rows, KV_LORA), F32),
        compiler_params=_params(("arbitrary", "arbitrary")),
        name="mla_sample_paged",
    )(page_table.reshape(-1), ql, qr, kself, krself, lat_pool, rope_pool_t)


LRU_TC = 512


def _block_diag(w):
    nb, d, e = w.shape
    return jnp.einsum('nde,nm->ndme', w, jnp.eye(nb, dtype=w.dtype)).reshape(nb * d, nb * e)


def _lru_coeffs(xc, wa, wx, ba, bx, lam):
    xb = xc.astype(BF16)
    r = jax.nn.sigmoid(jnp.dot(xb, wa, preferred_element_type=F32) + ba)
    ig = jax.nn.sigmoid(jnp.dot(xb, wx, preferred_element_type=F32) + bx)
    neg = -lam
    softplus = jnp.maximum(neg, 0.0) + jnp.log1p(jnp.exp(-jnp.abs(neg)))
    log_a = -LRU_C * r * softplus
    a = jnp.exp(log_a)
    t = jnp.tanh(log_a)
    b = jnp.sqrt(-2.0 * t / (1.0 - t)) * (ig * xc)
    return a, b


def _lru_body(x_ref, h0_ref, buf_ref, cw_ref, cb_ref, wa_ref, wx_ref, ba_ref, bx_ref, lam_ref,
              h_ref, hl_ref, tail_ref, xbuf, hc, *, tc):
    c = pl.program_id(1)

    @pl.when(c == 0)
    def _():
        xbuf[0:SUBLANES] = buf_ref[0]
        hc[...] = h0_ref[0]

    x = x_ref[...]
    xbuf[SUBLANES:SUBLANES + tc] = x
    cw = cw_ref[...]
    xc = cb_ref[...] + x * cw[CONV_W - 1:CONV_W]
    for j in range(CONV_W - 1):
        xc = xc + xbuf[pl.ds(SUBLANES - (CONV_W - 1) + j, tc), :] * cw[j:j + 1]
    xbuf[0:SUBLANES] = x[tc - SUBLANES:tc]
    a, b = _lru_coeffs(xc, wa_ref[...], wx_ref[...], ba_ref[...], bx_ref[...], lam_ref[...])
    groups = tc // SUBLANES
    a = a.reshape(groups, SUBLANES, W_B)
    b = b.reshape(groups, SUBLANES, W_B)
    sub = lax.broadcasted_iota(jnp.int32, a.shape, 1)
    d = 1
    while d < SUBLANES:
        keep = sub >= d
        a_sh = jnp.where(keep, pltpu.roll(a, d, 1), 1.0)
        b_sh = jnp.where(keep, pltpu.roll(b, d, 1), 0.0)
        b = a * b_sh + b
        a = a * a_sh
        d *= 2
    carry = hc[...]
    for g in range(groups):
        hg = a[g] * carry + b[g]
        h_ref[g * SUBLANES:(g + 1) * SUBLANES, :] = hg
        carry = hg[SUBLANES - 1:SUBLANES]
    hc[...] = carry

    @pl.when(c == pl.num_programs(1) - 1)
    def _():
        hl_ref[0] = carry
        tail_ref[0] = x[tc - SUBLANES:tc]


def _rglru_seq(zg, batch, seq, h0, buf, conv_w, conv_b, wa_bd, wx_bd, ba, bx, lam):
    tc = min(LRU_TC, seq)
    assert seq % tc == 0 and tc >= SUBLANES
    nc = seq // tc
    buf8 = jnp.pad(buf, ((0, 0), (SUBLANES - (CONV_W - 1), 0), (0, 0)))
    vec = lambda a: a.reshape(1, W_B)
    fixed = lambda b, c: (0, 0)
    per_b = lambda b, c: (b, 0, 0)
    h, hl, tail = pl.pallas_call(
        functools.partial(_lru_body, tc=tc),
        grid=(batch, nc),
        in_specs=[pl.BlockSpec((tc, W_B), lambda b, c: (b * nc + c, ZE_XB // W_B)),
                  pl.BlockSpec((1, 1, W_B), per_b),
                  pl.BlockSpec((1, SUBLANES, W_B), per_b),
                  pl.BlockSpec((CONV_W, W_B), fixed),
                  pl.BlockSpec((1, W_B), fixed),
                  pl.BlockSpec((W_B, W_B), fixed), pl.BlockSpec((W_B, W_B), fixed),
                  pl.BlockSpec((1, W_B), fixed), pl.BlockSpec((1, W_B), fixed), pl.BlockSpec((1, W_B), fixed)],
        out_specs=[pl.BlockSpec((tc, W_B), lambda b, c: (b * nc + c, 0)),
                   pl.BlockSpec((1, 1, W_B), per_b),
                   pl.BlockSpec((1, SUBLANES, W_B), per_b)],
        out_shape=[jax.ShapeDtypeStruct((batch * seq, W_B), F32),
                   jax.ShapeDtypeStruct((batch, 1, W_B), F32),
                   jax.ShapeDtypeStruct((batch, SUBLANES, W_B), F32)],
        scratch_shapes=[pltpu.VMEM((SUBLANES + tc, W_B), F32), pltpu.VMEM((1, W_B), F32)],
        compiler_params=_params(("arbitrary", "arbitrary")),
        name="rglru_seq",
    )(zg, h0.reshape(batch, 1, W_B), buf8, conv_w, vec(conv_b), wa_bd, wx_bd, vec(ba), vec(bx), vec(lam))
    return h, hl[:, 0], tail[:, SUBLANES - (CONV_W - 1):]


def _out_body(v1_ref, v2_ref, v3_ref, g1_ref, g2_ref, g3_ref, x_ref, w_ref, lg_ref, lb_ref, o_ref):
    def gated(v_ref, g_ref):
        g = g_ref[...]
        return (v_ref[...] * (g * jax.nn.sigmoid(g))).astype(BF16)

    mixed = jnp.concatenate([gated(v1_ref, g1_ref), gated(v2_ref, g2_ref), gated(v3_ref, g3_ref)], axis=1)
    u = DEEPNORM_ALPHA * x_ref[...] + jnp.dot(mixed, w_ref[...], preferred_element_type=F32)
    mu = jnp.mean(u, axis=-1, keepdims=True)
    var = jnp.mean(jnp.square(u - mu), axis=-1, keepdims=True)
    o_ref[...] = (u - mu) * lax.rsqrt(var + NORM_EPS) * lg_ref[...] + lb_ref[...]


def _out_proj_norm(vals, z, gate_cols, x2d, w_out, ln_g, ln_b):
    n = x2d.shape[0]
    tm = min(512, n)
    widths = [v.shape[1] for v in vals]
    row = lambda i: (i, 0)
    fixed = lambda i: (0, 0)
    in_specs = [pl.BlockSpec((tm, w), row) for w in widths]
    for w, off in zip(widths, gate_cols):
        assert off % w == 0
        in_specs.append(pl.BlockSpec((tm, w), functools.partial(lambda i, cb: (i, cb), cb=off // w)))
    in_specs += [pl.BlockSpec((tm, D_MODEL), row),
                 pl.BlockSpec((sum(widths), D_MODEL), fixed),
                 pl.BlockSpec((1, D_MODEL), fixed), pl.BlockSpec((1, D_MODEL), fixed)]
    return pl.pallas_call(
        _out_body,
        grid=(n // tm,),
        in_specs=in_specs,
        out_specs=pl.BlockSpec((tm, D_MODEL), row),
        out_shape=jax.ShapeDtypeStruct((n, D_MODEL), F32),
        compiler_params=_params(("arbitrary",)),
        name="out_proj_norm",
    )(*vals, z, z, z, x2d, w_out.astype(BF16), ln_g.reshape(1, -1), ln_b.reshape(1, -1))


def _mem_attend_rows(q, k_t, v_t):
    rows = q.shape[0]
    q_head = lax.shift_right_logical(lax.broadcasted_iota(jnp.int32, q.shape, 1), DH_M.bit_length() - 1)
    qs = jnp.concatenate([jnp.where(q_head == h, q, 0.0) for h in range(H_M)], axis=0).astype(BF16)
    s = jnp.dot(qs, k_t.astype(BF16), preferred_element_type=F32) * (DH_M ** -0.5)
    p = jnp.exp(s - jnp.max(s, axis=-1, keepdims=True))
    p = p / jnp.sum(p, axis=-1, keepdims=True)
    y = lax.dot_general(p.astype(BF16), v_t.astype(BF16), NT_DIMS, preferred_element_type=F32)
    out = jnp.where(q_head == 0, y[0:rows], 0.0)
    for h in range(1, H_M):
        out = out + jnp.where(q_head == h, y[h * rows:(h + 1) * rows], 0.0)
    return out


def _mem_body(q_ref, k_ref, v_ref, o_ref):
    for i in range(k_ref.shape[1]):
        q = q_ref[i] if len(q_ref.shape) == 3 else q_ref[...]
        y = _mem_attend_rows(q, k_ref[0, i], v_ref[0, i])
        if len(o_ref.shape) == 3:
            o_ref[i] = y
        else:
            o_ref[...] = y


def _mem_transposed(mem):
    t = jnp.moveaxis(mem, -3, -1)
    t = t.reshape(t.shape[:-3] + (W_M, N_MEM))
    return t if t.ndim == 4 else t[None]


def _mem_attend_long(z, col, batch, seq, mem_kt, mem_vt, layer):
    tm = min(1024, seq)
    nt = seq // tm
    kv = lambda b, i: (layer, b, 0, 0)
    return pl.pallas_call(
        _mem_body,
        grid=(batch, nt),
        in_specs=[pl.BlockSpec((tm, W_M), lambda b, i: (b * nt + i, col // W_M)),
                  pl.BlockSpec((1, 1, W_M, N_MEM), kv), pl.BlockSpec((1, 1, W_M, N_MEM), kv)],
        out_specs=pl.BlockSpec((tm, W_M), lambda b, i: (b * nt + i, 0)),
        out_shape=jax.ShapeDtypeStruct((batch * seq, W_M), F32),
        compiler_params=_params(("arbitrary", "arbitrary")),
        name="mem_attend_long",
    )(z, mem_kt, mem_vt)


MEM_SHORT_NB = 8


def _mem_attend_short(z, col, batch, seq, mem_kt, mem_vt, layer):
    assert seq <= SUBLANES
    nb = min(MEM_SHORT_NB, batch)
    q = jnp.pad(z[:, col:col + W_M].reshape(batch, seq, W_M), ((0, 0), (0, SUBLANES - seq), (0, 0)))
    blk = lambda i: (i, 0, 0)
    kv = lambda i: (layer, i, 0, 0)
    y = pl.pallas_call(
        _mem_body,
        grid=(batch // nb,),
        in_specs=[pl.BlockSpec((nb, SUBLANES, W_M), blk),
                  pl.BlockSpec((1, nb, W_M, N_MEM), kv), pl.BlockSpec((1, nb, W_M, N_MEM), kv)],
        out_specs=pl.BlockSpec((nb, SUBLANES, W_M), blk),
        out_shape=jax.ShapeDtypeStruct((batch, SUBLANES, W_M), F32),
        compiler_params=_params(("arbitrary",)),
        name="mem_attend_short",
    )(q, mem_kt, mem_vt)
    return y[:, :seq].reshape(batch * seq, W_M)


def _chunk_mlp_body(u_ref, v_ref, g_ref, b_ref, w_ref, bias_ref, y_ref, vn_ref, *, rows, chunks):
    gw = W_D // G_D
    for c in range(chunks):
        sl = pl.ds(c * rows, rows)
        v = v_ref[sl, :]
        mu = jnp.mean(v, axis=-1, keepdims=True)
        var = jnp.mean(jnp.square(v - mu), axis=-1, keepdims=True)
        vn = (v - mu) * lax.rsqrt(var + NORM_EPS) * g_ref[...] + b_ref[...]
        vn_ref[sl, :] = vn
        vb = vn.astype(BF16)
        s = jnp.concatenate([jnp.dot(w_ref[g], vb[:, g * gw:(g + 1) * gw], preferred_element_type=F32)
                             for g in range(G_D)], axis=1)
        y_ref[sl, :] = u_ref[sl, :] * (s + bias_ref[...])


def _chunk_mlp_call(z, rows, ln_g_d, ln_b_d, w_mix, bias):
    n = z.shape[0]
    chunks = max(1, min(8, n // rows))
    tm = rows * chunks
    row = lambda i: (i, 0)
    fixed = lambda i: (0, 0)
    return pl.pallas_call(
        functools.partial(_chunk_mlp_body, rows=rows, chunks=chunks),
        grid=(n // tm,),
        in_specs=[pl.BlockSpec((tm, W_D), lambda i: (i, ZO_U // W_D)),
                  pl.BlockSpec((tm, W_D), lambda i: (i, ZO_VD // W_D)),
                  pl.BlockSpec((1, W_D), fixed), pl.BlockSpec((1, W_D), fixed),
                  pl.BlockSpec((G_D, rows, rows), lambda i: (0, 0, 0)),
                  pl.BlockSpec((rows, W_D), fixed)],
        out_specs=[pl.BlockSpec((tm, W_D), row), pl.BlockSpec((tm, W_D), row)],
        out_shape=[jax.ShapeDtypeStruct((n, W_D), F32), jax.ShapeDtypeStruct((n, W_D), F32)],
        compiler_params=_params(("arbitrary",)),
        name="chunk_mlp",
    )(z, z, ln_g_d.reshape(1, -1), ln_b_d.reshape(1, -1), w_mix, bias)


def _chunk_mlp_weights(sg_w, sg_b):
    L = CHUNK_D
    w = jnp.where(jnp.tril(jnp.ones((L, L), dtype=bool)), sg_w[:, :L, :L], 0.0)
    return w.astype(BF16), jnp.repeat(sg_b[:, :L].T, W_D // G_D, axis=1)


def _chunk_mlp_short_body(u_ref, v_ref, g_ref, b_ref, w_ref, bias_ref, y_ref, vn_ref, *, batch, seq):
    v = v_ref[...]
    mu = jnp.mean(v, axis=-1, keepdims=True)
    var = jnp.mean(jnp.square(v - mu), axis=-1, keepdims=True)
    vn = (v - mu) * lax.rsqrt(var + NORM_EPS) * g_ref[...] + b_ref[...]
    vn_ref[...] = vn
    for t in range(seq):
        acc = vn[0:batch] * w_ref[t * seq:t * seq + 1, :]
        for s in range(1, t + 1):
            acc = acc + vn[s * batch:(s + 1) * batch] * w_ref[t * seq + s:t * seq + s + 1, :]
        rows = slice(t * batch, (t + 1) * batch)
        y_ref[rows, :] = u_ref[rows, :] * (acc + bias_ref[t:t + 1, :])


def _chunk_mlp_short(z, batch, seq, ln_g_d, ln_b_d, sg_w, sg_b):
    gw = W_D // G_D
    w_rows = jnp.repeat(jnp.transpose(sg_w[:, :seq, :seq], (1, 2, 0)).reshape(seq * seq, G_D), gw, axis=1)
    bias = jnp.repeat(sg_b[:, :seq].T, gw, axis=1)

    def time_major(off):
        return jnp.swapaxes(z[:, off:off + W_D].reshape(batch, seq, W_D), 0, 1).reshape(seq * batch, W_D)

    y, vn = pl.pallas_call(
        functools.partial(_chunk_mlp_short_body, batch=batch, seq=seq),
        out_shape=[jax.ShapeDtypeStruct((seq * batch, W_D), F32)] * 2,
        compiler_params=pltpu.CompilerParams(vmem_limit_bytes=VMEM_LIMIT),
        name="chunk_mlp_short",
    )(time_major(ZO_U), time_major(ZO_VD), ln_g_d.reshape(1, -1), ln_b_d.reshape(1, -1), w_rows, bias)
    back = lambda a: jnp.swapaxes(a.reshape(seq, batch, W_D), 0, 1).reshape(batch * seq, W_D)
    return back(y), back(vn)


def _cumsum_rows(x):
    row = lax.broadcasted_iota(jnp.int32, x.shape, 0)
    d = 1
    while d < x.shape[0]:
        x = x + jnp.where(row >= d, pltpu.roll(x, d, 0), 0.0)
        d *= 2
    return x


def _mlstm_body(q_ref, k_ref, v_ref, o_ref, gt_ref, bif_ref, c0_ref, n0_ref, m0_ref,
                y_ref, c_ref, n_ref, m_ref, c_sc, n_sc, m_sc, *stage, nb, n_valid):
    L = MLSTM_CHUNK
    step = pl.program_id(1)
    heads = [slice(h * DH_C, (h + 1) * DH_C) for h in range(H_C)]

    @pl.when(step == 0)
    def _():
        c_sc[...] = jnp.zeros(c_sc.shape, F32)
        for b in range(nb):
            for h, hs in enumerate(heads):
                c_sc[b, hs, hs] = c0_ref[b, h]
                n_sc[b, :, hs] = n0_ref[b, h]
        m_sc[...] = m0_ref[...]
        for st in stage:
            st[...] = jnp.zeros(st.shape, F32)

    shift = DH_C.bit_length() - 1
    row = lax.broadcasted_iota(jnp.int32, (L, LANES), 0)
    lane = lax.broadcasted_iota(jnp.int32, (L, LANES), 1)
    tril = lax.broadcasted_iota(jnp.int32, (L, L), 0) >= lax.broadcasted_iota(jnp.int32, (L, L), 1)
    lane_head = lax.shift_right_logical(lax.broadcasted_iota(jnp.int32, (L, W_C), 1), shift)
    diag_blocks = (lax.shift_right_logical(lax.broadcasted_iota(jnp.int32, (W_C, W_C), 0), shift)
                   == lax.shift_right_logical(lax.broadcasted_iota(jnp.int32, (W_C, W_C), 1), shift))

    def per_head(cols, width):
        return jnp.concatenate([jnp.broadcast_to(c, (c.shape[0], width)) for c in cols], axis=1)

    for b in range(nb):
        if stage:
            rows_in = q_ref.shape[1]
            bufs = []
            for st, ref in zip(stage, (q_ref, k_ref, v_ref, o_ref, gt_ref)):
                st[b, 0:rows_in, :] = ref[b]
                bufs.append(st[b])
            q, k, v, o_pre, gts = bufs
        else:
            q, k, v, o_pre, gts = q_ref[b], k_ref[b], v_ref[b], o_ref[b], gt_ref[b]
        gz = gts + bif_ref[...]
        lf = jnp.minimum(gz, 0.0) - jnp.log1p(jnp.exp(-jnp.abs(gz)))
        ig = gz
        if n_valid < L:
            ig = jnp.where(row < n_valid, ig, -jnp.inf)
            lf = jnp.where(row < n_valid, lf, 0.0)
        a = jnp.where(lane < H_C, ig, _cumsum_rows(lf))
        a_t = a.T
        ks = k * (DH_C ** -0.5)
        vb = v.astype(BF16)
        q_stack = jnp.concatenate([jnp.where(lane_head == h, q, 0.0) for h in range(H_C)], axis=0)
        qk = lax.dot_general(q_stack.astype(BF16), ks.astype(BF16), NT_DIMS, preferred_element_type=F32)
        ws, mts, g_inters, wgs, decays = [], [], [], [], []
        for h in range(H_C):
            ig_row, bc_row = a_t[h:h + 1, :], a_t[H_C + h:H_C + h + 1, :]
            ig_col, bc_col = a[:, h:h + 1], a[:, H_C + h:H_C + h + 1]
            m_old = m_sc[b, h][:, 0:1]
            dmat = jnp.where(tril, bc_col - bc_row + ig_row, -jnp.inf)
            inter = bc_col + m_old
            mt = jnp.maximum(inter, jnp.max(dmat, axis=-1, keepdims=True))
            ws.append(jnp.exp(dmat - mt))
            mts.append(mt)
            g_inters.append(jnp.exp(inter - mt))
            b_end = bc_col[L - 1:L, :]
            g_col = b_end - bc_col + ig_col
            m_new = jnp.maximum(b_end + m_old, jnp.max(g_col, axis=0, keepdims=True))
            decays.append(jnp.exp(b_end + m_old - m_new))
            wgs.append(jnp.exp(g_col - m_new))
            m_sc[b, h] = jnp.broadcast_to(m_new, (1, LANES))
        sc = qk * jnp.concatenate(ws, axis=0)
        sv = jnp.dot(sc.astype(BF16), vb, preferred_element_type=F32)
        num = jnp.where(lane_head == 0, sv[0:L], 0.0)
        for h in range(1, H_C):
            num = num + jnp.where(lane_head == h, sv[h * L:(h + 1) * L], 0.0)
        c_old = c_sc[b]
        n_old = n_sc[b]
        num = num + per_head(g_inters, DH_C) * jnp.dot(q.astype(BF16), c_old.astype(BF16),
                                                        preferred_element_type=F32)
        row_sum = jnp.sum(sc, axis=-1, keepdims=True)
        qn = q * n_old
        dens = []
        for h, hs in enumerate(heads):
            den = row_sum[h * L:(h + 1) * L] + g_inters[h] * jnp.sum(qn[:, hs], axis=-1, keepdims=True)
            dens.append(jnp.maximum(jnp.abs(den), jnp.exp(-mts[h])))
        y = jax.nn.sigmoid(o_pre) * (num / per_head(dens, DH_C))
        y_ref[b] = y[0:y_ref.shape[1]]
        kw = ks * per_head(wgs, DH_C)
        decay = per_head(decays, DH_C)
        upd = jnp.dot(kw.T.astype(BF16), vb, preferred_element_type=F32)
        c_sc[b] = decay * c_old + jnp.where(diag_blocks, upd, 0.0)
        n_sc[b] = decay * n_old + jnp.sum(kw, axis=0, keepdims=True)

    @pl.when(step == pl.num_programs(1) - 1)
    def _():
        for b in range(nb):
            for h, hs in enumerate(heads):
                c_ref[b, h] = c_sc[b, hs, hs]
                n_ref[b, h] = n_sc[b, :, hs]
        m_ref[...] = m_sc[...]


MLSTM_NB = 4


def _mlstm_call(z, batch, seq, b_if, c0, n0, m0):
    L = MLSTM_CHUNK
    nb = min(MLSTM_NB, batch)
    long = seq % L == 0
    if long:
        rows, nc, n_valid = L, seq // L, L
        z3 = z.reshape(batch, seq, z.shape[1])
    else:
        assert seq <= SUBLANES
        rows, nc, n_valid = SUBLANES, 1, seq
        z3 = jnp.pad(z.reshape(batch, seq, z.shape[1]), ((0, 0), (0, SUBLANES - seq), (0, 0)))
    bif = jnp.pad(b_if, (0, LANES - 2 * H_C)).reshape(1, LANES)
    m0r = jnp.broadcast_to(m0[:, :, None, None], (batch, H_C, 1, LANES))

    def col(off, w):
        return pl.BlockSpec((nb, rows, w), functools.partial(lambda g, c, cb: (g, c, cb), cb=off // w))

    st4 = lambda g, c: (g, 0, 0, 0)
    in_specs = [col(ZO_Q, W_C), col(ZO_K, W_C), col(ZO_V, W_C), col(ZO_O, W_C), col(ZO_IF, LANES),
                pl.BlockSpec((1, LANES), lambda g, c: (0, 0)),
                pl.BlockSpec((nb, H_C, DH_C, DH_C), st4),
                pl.BlockSpec((nb, H_C, 1, DH_C), st4),
                pl.BlockSpec((nb, H_C, 1, LANES), st4)]
    out_specs = [pl.BlockSpec((nb, rows, W_C), lambda g, c: (g, c, 0)),
                 pl.BlockSpec((nb, H_C, DH_C, DH_C), st4),
                 pl.BlockSpec((nb, H_C, 1, DH_C), st4),
                 pl.BlockSpec((nb, H_C, 1, LANES), st4)]
    scratch = [pltpu.VMEM((nb, W_C, W_C), F32), pltpu.VMEM((nb, 1, W_C), F32),
               pltpu.VMEM((nb, H_C, 1, LANES), F32)]
    if not long:
        scratch += [pltpu.VMEM((nb, L, W_C), F32)] * 4 + [pltpu.VMEM((nb, L, LANES), F32)]
    y, c, n, m = pl.pallas_call(
        functools.partial(_mlstm_body, nb=nb, n_valid=n_valid),
        grid=(batch // nb, nc),
        in_specs=in_specs,
        out_specs=out_specs,
        out_shape=[jax.ShapeDtypeStruct((batch, rows * nc, W_C), F32),
                   jax.ShapeDtypeStruct((batch, H_C, DH_C, DH_C), F32),
                   jax.ShapeDtypeStruct((batch, H_C, 1, DH_C), F32),
                   jax.ShapeDtypeStruct((batch, H_C, 1, LANES), F32)],
        scratch_shapes=scratch,
        compiler_params=_params(("arbitrary", "arbitrary")),
        name="mlstm_chunks",
    )(z3, z3, z3, z3, z3, bif, c0, n0[:, :, None, :], m0r)
    return y[:, :seq].reshape(batch * seq, W_C), c, n[:, :, 0], m[:, :, 0, 0]


def _lru_short_body(x_ref, h0_ref, buf_ref, cw_ref, cb_ref, wa_ref, wx_ref, ba_ref, bx_ref, lam_ref,
                    h_ref, *, batch, seq):
    cw = cw_ref[...]
    xp = jnp.concatenate([buf_ref[...], x_ref[...]], axis=0)
    xc = cb_ref[...] + xp[(CONV_W - 1) * batch:] * cw[CONV_W - 1:CONV_W]
    for j in range(CONV_W - 1):
        xc = xc + xp[j * batch:(j + seq) * batch] * cw[j:j + 1]
    a, b = _lru_coeffs(xc, wa_ref[...], wx_ref[...], ba_ref[...], bx_ref[...], lam_ref[...])
    h = h0_ref[...]
    for t in range(seq):
        sl = slice(t * batch, (t + 1) * batch)
        h = a[sl] * h + b[sl]
        h_ref[sl, :] = h


def _rglru_short(zg, batch, seq, h0, buf, conv_w, conv_b, wa_bd, wx_bd, ba, bx, lam):
    assert batch % SUBLANES == 0
    x = zg[:, ZE_XB:ZE_XB + W_B].reshape(batch, seq, W_B)
    xp = jnp.concatenate([buf, x], axis=1)
    x_tm = jnp.swapaxes(x, 0, 1).reshape(seq * batch, W_B)
    buf_tm = jnp.swapaxes(buf, 0, 1).reshape((CONV_W - 1) * batch, W_B)
    vec = lambda a: a.reshape(1, W_B)
    h_tm = pl.pallas_call(
        functools.partial(_lru_short_body, batch=batch, seq=seq),
        out_shape=jax.ShapeDtypeStruct((seq * batch, W_B), F32),
        compiler_params=pltpu.CompilerParams(vmem_limit_bytes=VMEM_LIMIT),
        name="rglru_short",
    )(x_tm, h0, buf_tm, conv_w, vec(conv_b), wa_bd, wx_bd, vec(ba), vec(bx), vec(lam))
    h = jnp.swapaxes(h_tm.reshape(seq, batch, W_B), 0, 1)
    return h.reshape(batch * seq, W_B), h[:, -1], xp[:, -(CONV_W - 1):]


def _mem_kv(mem, w_mk, w_mv):
    B = mem.shape[0]
    kv = _proj(mem, jnp.concatenate([w_mk, w_mv], axis=1))
    return (kv[..., :W_M].reshape(B, N_MEM, H_M, DH_M), kv[..., W_M:].reshape(B, N_MEM, H_M, DH_M))


def _sample_queries(qp, batch, t_new):
    n = qp.shape[1]
    r1 = jnp.stack([qp[h, :, KV_LORA + h * ROPE_HALF:KV_LORA + (h + 1) * ROPE_HALF] for h in range(H_A)])
    r2 = jnp.stack([qp[h, :, KV_LORA + LANES + h * ROPE_HALF:KV_LORA + LANES + (h + 1) * ROPE_HALF]
                    for h in range(H_A)])
    qr = jnp.concatenate([r1, r2], axis=-1)

    def rows(a):
        w = a.shape[-1]
        return a.reshape(H_A, batch, t_new, w).transpose(1, 0, 2, 3).reshape(batch, H_A * t_new, w)

    return rows(qp[:, :, :KV_LORA]), rows(qr)


def _sample_self_keys(latent, k_rope, batch, t_new):
    kl = jnp.pad(latent.astype(BF16).reshape(batch, t_new, KV_LORA), ((0, 0), (0, PAGE_SIZE - t_new), (0, 0)))
    kr_t = jnp.swapaxes(k_rope.astype(BF16).reshape(batch, t_new, QK_ROPE), 1, 2)
    return kl, jnp.pad(kr_t, ((0, 0), (0, 0), (0, PAGE_SIZE - t_new)))


def _even_layer(x2d, batch, seq, tables, mem, lru_h0, lru_buf, paged, e, weights,
                q_norm, kv_norm, conv_w, conv_b, wa, ba, wx, bx, lam, w_out, ln_g, ln_b):
    w_in_r, w_uq_r, w_uk_r, w_uv_bd = weights
    zg, latent, k_rope, kp, qp = _even_in(x2d, tables, w_in_r, q_norm, kv_norm, w_uq_r, w_uk_r)
    lru_args = (lru_h0, lru_buf, conv_w, conv_b, _block_diag(wa).astype(BF16), _block_diag(wx).astype(BF16),
                ba, bx, lam)
    if paged is None:
        y_a = _mla_prompt(qp, kp, w_uv_bd, batch, seq)
        h_b, h_last, new_buf = _rglru_seq(zg, batch, seq, *lru_args)
        y_m = _mem_attend_long(zg, ZE_QM, batch, seq, *mem)
    else:
        lat_pool, rope_pool, page_table = paged
        o = _mla_sample(*_sample_queries(qp, batch, seq), *_sample_self_keys(latent, k_rope, batch, seq),
                        lat_pool, jnp.swapaxes(rope_pool, 2, 3), page_table, e, seq)
        o = o.reshape(batch, H_A, seq, KV_LORA).transpose(0, 2, 1, 3).reshape(batch * seq, H_A * KV_LORA)
        y_a = _matmul(o, w_uv_bd)
        h_b, h_last, new_buf = _rglru_short(zg, batch, seq, *lru_args)
        y_m = _mem_attend_short(zg, ZE_QM, batch, seq, *mem)
    x_new = _out_proj_norm([y_a, h_b, y_m], zg, (ZE_GA, ZE_GB, ZE_GM), x2d, w_out, ln_g, ln_b)
    return (x_new, latent.reshape(batch, seq, KV_LORA), k_rope.reshape(batch, seq, QK_ROPE), h_last, new_buf)


def _odd_weights(w_in):
    q, k, v, i_pre, f_pre, o_pre, g_c, u_d, v_d, g_d, q_m, g_m = _split_cols(w_in, ODD_SPLITS)
    gates = jnp.pad(jnp.concatenate([i_pre, f_pre], axis=1), ((0, 0), (0, LANES - 2 * H_C)))
    return jnp.concatenate([q, k, v, o_pre, g_c, u_d, v_d, g_d, q_m, g_m, gates], axis=1)


def _odd_layer(x2d, batch, seq, mem, c0, n0, m0, w_in_r, b_if, ln_g_d, ln_b_d, sg_w, sg_b,
               w_out, ln_g, ln_b):
    z = _matmul(x2d, w_in_r)
    y_c, c, n, m = _mlstm_call(z, batch, seq, b_if, c0, n0, m0)
    if seq % CHUNK_D == 0:
        y_d, vn = _chunk_mlp_call(z, CHUNK_D, ln_g_d, ln_b_d, *_chunk_mlp_weights(sg_w, sg_b))
        y_m = _mem_attend_long(z, ZO_QM, batch, seq, *mem)
    else:
        y_d, vn = _chunk_mlp_short(z, batch, seq, ln_g_d, ln_b_d, sg_w, sg_b)
        y_m = _mem_attend_short(z, ZO_QM, batch, seq, *mem)
    x_new = _out_proj_norm([y_c, y_d, y_m], z, (ZO_GC, ZO_GD, ZO_GM), x2d, w_out, ln_g, ln_b)
    return x_new, vn.reshape(batch, seq, W_D), c, n, m


def kernel(x_prompt, x_sample, cache_mla_latent, cache_mla_krope, state_lru_h, state_lru_conv,
           state_mlstm_c, state_mlstm_n, state_mlstm_m, cache_mem_k, cache_mem_v, page_table,
           mem_prompt, w_in_even, mla_q_norm, mla_kv_norm, w_uq, w_uk, w_uv,
           lru_conv_w, lru_conv_b, lru_wa, lru_ba, lru_wx, lru_bx, lru_lambda, w_out_even,
           w_in_odd, mlstm_b_if, sg_ln_g, sg_ln_b, sg_w, sg_b, w_out_odd,
           w_mem_k, w_mem_v, ln_g, ln_b):
    Bp, Tp, _ = x_prompt.shape
    Bs, Ts, _ = x_sample.shape
    past_len = page_table.shape[1] * PAGE_SIZE
    tables_p = _rope_tables(jnp.arange(Tp, dtype=F32))
    tables_s = tuple(jnp.tile(t, (Bs, 1)) for t in _rope_tables(past_len + jnp.arange(Ts, dtype=F32)))

    h0_p = jnp.zeros((Bp, W_B), F32)
    buf0_p = jnp.zeros((Bp, CONV_W - 1, W_B), F32)
    c0_p = jnp.zeros((Bp, H_C, DH_C, DH_C), F32)
    n0_p = jnp.zeros((Bp, H_C, DH_C), F32)
    m0_p = jnp.zeros((Bp, H_C), F32)

    lat_p, kr_p, h_p, conv_p, c_p, n_p, m_p, mk_p, mv_p = [], [], [], [], [], [], [], [], []
    lat_s, kr_s, h_s, conv_s, c_s, n_s, m_s, v_s = [], [], [], [], [], [], [], []

    xp = x_prompt.reshape(Bp * Tp, D_MODEL)
    xs = x_sample.reshape(Bs * Ts, D_MODEL)
    mem_s = (_mem_transposed(cache_mem_k), _mem_transposed(cache_mem_v))
    for l in range(DEPTH):
        mk_l, mv_l = _mem_kv(mem_prompt, w_mem_k[l], w_mem_v[l])
        mk_p.append(mk_l)
        mv_p.append(mv_l)
        mem_p = (_mem_transposed(mk_l), _mem_transposed(mv_l), 0)
        if l % 2 == 0:
            e = l // 2
            weights = _even_weights(w_in_even[e], w_uq[e], w_uk[e], w_uv[e])
            rest = (mla_q_norm[e], mla_kv_norm[e], lru_conv_w[e], lru_conv_b[e], lru_wa[e], lru_ba[e],
                    lru_wx[e], lru_bx[e], lru_lambda[e], w_out_even[e], ln_g[l], ln_b[l])
            xp, la, kr, hl, cb = _even_layer(xp, Bp, Tp, tables_p, mem_p, h0_p, buf0_p, None, e,
                                             weights, *rest)
            lat_p.append(la); kr_p.append(kr); h_p.append(hl); conv_p.append(cb)
            xs, la, kr, hl, cb = _even_layer(xs, Bs, Ts, tables_s, mem_s + (l,),
                                             state_lru_h[e], state_lru_conv[e],
                                             (cache_mla_latent, cache_mla_krope, page_table), e, weights, *rest)
            lat_s.append(la); kr_s.append(kr); h_s.append(hl); conv_s.append(cb)
        else:
            o = l // 2
            ow = (_odd_weights(w_in_odd[o]), mlstm_b_if[o], sg_ln_g[o], sg_ln_b[o], sg_w[o], sg_b[o],
                  w_out_odd[o], ln_g[l], ln_b[l])
            xp, _, cc, nn, mm = _odd_layer(xp, Bp, Tp, mem_p, c0_p, n0_p, m0_p, *ow)
            c_p.append(cc); n_p.append(nn); m_p.append(mm)
            xs, vn, cc, nn, mm = _odd_layer(xs, Bs, Ts, mem_s + (l,),
                                            state_mlstm_c[o], state_mlstm_n[o], state_mlstm_m[o], *ow)
            c_s.append(cc); n_s.append(nn); m_s.append(mm); v_s.append(vn)

    return (xp.reshape(Bp, Tp, D_MODEL), xs.reshape(Bs, Ts, D_MODEL),
            jnp.stack(lat_p), jnp.stack(kr_p), jnp.stack(h_p), jnp.stack(conv_p),
            jnp.stack(c_p), jnp.stack(n_p), jnp.stack(m_p), jnp.stack(mk_p), jnp.stack(mv_p),
            jnp.stack(lat_s), jnp.stack(kr_s), jnp.stack(h_s), jnp.stack(conv_s),
            jnp.stack(c_s), jnp.stack(n_s), jnp.stack(m_s), jnp.stack(v_s))
```

```python
import functools

import jax
import jax.numpy as jnp
import numpy as np
from jax import lax
from jax.experimental import pallas as pl
from jax.experimental.pallas import tpu as pltpu

D_MODEL = 1024
DEPTH = 2
PAGE_SIZE = 128
H_A = 8
Q_LORA = 384
KV_LORA = 256
QK_NOPE = 64
QK_ROPE = 32
ROPE_HALF = QK_ROPE // 2
V_HEAD = 64
W_A = H_A * V_HEAD
ROPE_THETA = 10000.0
MLA_SCALE = (QK_NOPE + QK_ROPE) ** -0.5
W_B = 512
NB_B = 8
BD_B = W_B // NB_B
CONV_W = 4
LRU_C = 8.0
H_C = 4
DH_C = 128
W_C = H_C * DH_C
MLSTM_CHUNK = 128
G_D = 4
W_D = 512
CHUNK_D = 128
N_MEM = 256
H_M = 4
DH_M = 64
W_M = H_M * DH_M
NORM_EPS = 1e-6
DEEPNORM_ALPHA = (2 * DEPTH) ** 0.25

EVEN_SPLITS = (Q_LORA, KV_LORA, QK_ROPE, W_A, W_B, W_B, W_M, W_M)
ODD_SPLITS = (W_C, W_C, W_C, H_C, H_C, W_C, W_C, W_D, W_D, W_D, W_M, W_M)

F32 = jnp.float32
BF16 = jnp.bfloat16
LANES = 128
SUBLANES = 8
VMEM_LIMIT = 48 * 1024 * 1024
LOG2E = 1.4426950408889634
NT_DIMS = (((1,), (1,)), ((), ()))

ZE_GA = 0
ZE_XB = ZE_GA + W_A
ZE_GB = ZE_XB + W_B
ZE_QM = ZE_GB + W_B
ZE_GM = ZE_QM + W_M
ZE_GATES = ZE_GM + W_M
ZE_CKV = ZE_GATES
ZE_CQ = ZE_CKV + KV_LORA
ZE_KR1 = ZE_CQ + Q_LORA
ZE_KR2 = ZE_KR1 + LANES
ZE_KRN = ZE_KR2 + LANES
ZE_KRS = ZE_KRN + LANES
ZE_W = ZE_KRS + LANES
QP_W = KV_LORA + 2 * LANES
UQ_NOPE_W = H_A * LANES
assert W_C == W_D
ZO_Q, ZO_K, ZO_V, ZO_O, ZO_GC, ZO_U, ZO_VD, ZO_GD = (i * W_C for i in range(8))
ZO_QM = 8 * W_C
ZO_GM = ZO_QM + W_M
ZO_IF = ZO_GM + W_M
ZO_W = ZO_IF + LANES


def _split_cols(z, sizes):
    cuts = [int(c) for c in np.cumsum(sizes)[:-1]]
    return jnp.split(z, cuts, axis=-1)


def _params(sem):
    return pltpu.CompilerParams(dimension_semantics=sem, vmem_limit_bytes=VMEM_LIMIT)


def _mm_body(x_ref, w_ref, o_ref):
    o_ref[...] = jnp.dot(x_ref[...].astype(BF16), w_ref[...], preferred_element_type=F32)


MATMUL_OUT_TILE_BYTES = 10 * 1024 * 1024
EVEN_IN_TM = 256
OUT_PROJ_TM = 512
MEM_LONG_TM = 1024
CHUNK_MLP_CHUNKS = 8


def _row_tile(m, n):
    tm = 512
    while tm > SUBLANES and (tm * n * 4 * 2 > MATMUL_OUT_TILE_BYTES or m % tm):
        tm //= 2
    return tm


def _matmul(x, w):
    m, k = x.shape
    n = w.shape[1]
    n_pad = -n % LANES
    wb = w.astype(BF16)
    if n_pad:
        wb = jnp.pad(wb, ((0, 0), (0, n_pad)))
    np_ = n + n_pad
    tm = _row_tile(m, np_)
    out = pl.pallas_call(
        _mm_body,
        grid=(m // tm,),
        in_specs=[pl.BlockSpec((tm, k), lambda i: (i, 0)),
                  pl.BlockSpec((k, np_), lambda i: (0, 0))],
        out_specs=pl.BlockSpec((tm, np_), lambda i: (i, 0)),
        out_shape=jax.ShapeDtypeStruct((m, np_), F32),
        compiler_params=_params(("arbitrary",)),
        name="row_matmul",
    )(x, wb)
    return out[:, :n] if n_pad else out


def _proj(x, w):
    lead = x.shape[:-1]
    return _matmul(x.reshape(-1, x.shape[-1]), w).reshape(lead + (w.shape[1],))


def _even_weights(w_in, w_uq, w_uk, w_uv):
    c_q, c_kv, kr, g_a, x_b, g_b, q_m, g_m = _split_cols(w_in, EVEN_SPLITS)
    x1, x2 = kr[:, :ROPE_HALF], kr[:, ROPE_HALF:]

    def lane_pad(a):
        return jnp.pad(a, ((0, 0), (0, LANES - a.shape[1])))

    w_in_r = jnp.concatenate(
        [g_a, x_b, g_b, q_m, g_m, c_kv, c_q, jnp.tile(x1, (1, H_A)), jnp.tile(x2, (1, H_A)),
         lane_pad(kr), lane_pad(jnp.concatenate([x2, x1], axis=1))], axis=1).astype(BF16)
    r = w_uq.reshape(Q_LORA, H_A, QK_NOPE + QK_ROPE)
    nope = jnp.pad(r[:, :, :QK_NOPE], ((0, 0), (0, 0), (0, LANES - QK_NOPE))).reshape(Q_LORA, UQ_NOPE_W)
    r1 = r[:, :, QK_NOPE:QK_NOPE + ROPE_HALF].reshape(Q_LORA, LANES)
    r2 = r[:, :, QK_NOPE + ROPE_HALF:].reshape(Q_LORA, LANES)
    w_uq_r = jnp.concatenate([nope, r1, r2], axis=1).astype(BF16)
    w_uk_r = jnp.pad(jnp.transpose(w_uk, (1, 2, 0)), ((0, 0), (0, LANES - QK_NOPE), (0, 0))).astype(BF16)
    eye = jnp.eye(H_A, dtype=w_uv.dtype)
    w_uv_bd = jnp.einsum('chv,hg->hcgv', w_uv, eye).reshape(H_A * KV_LORA, W_A).astype(BF16)
    return w_in_r, w_uq_r, w_uk_r, w_uv_bd


def _rope_tables(pos):
    inv = ROPE_THETA ** (-jnp.arange(ROPE_HALF, dtype=F32) / ROPE_HALF)
    ang = pos.astype(F32)[:, None] * inv[None, :]
    cos, sin = jnp.cos(ang), jnp.sin(ang)
    zpad = jnp.zeros((pos.shape[0], LANES - QK_ROPE), F32)
    return (jnp.tile(cos, (1, H_A)), jnp.tile(sin, (1, H_A)),
            jnp.concatenate([cos, cos, zpad], axis=1), jnp.concatenate([-sin, sin, zpad], axis=1))


def _rms(x, g):
    return x * lax.rsqrt(jnp.mean(x * x, axis=-1, keepdims=True) + NORM_EPS) * g


def _even_in_body(x_ref, w_ref, qn_ref, kvn_ref, wuq_ref, wuk_ref, cos_ref, sin_ref, cosn_ref, sinn_ref,
                  zg_ref, lat_ref, kr_ref, kp_ref, qp_ref):
    z = jnp.dot(x_ref[...].astype(BF16), w_ref[...], preferred_element_type=F32)
    zg_ref[...] = z[:, :ZE_GATES]
    lat = _rms(z[:, ZE_CKV:ZE_CQ], kvn_ref[...])
    lat_ref[...] = lat
    cos, sin = cos_ref[...], sin_ref[...]
    kr1, kr2 = z[:, ZE_KR1:ZE_KR2], z[:, ZE_KR2:ZE_KRN]
    kp_ref[...] = jnp.concatenate([lat, kr1 * cos - kr2 * sin, kr1 * sin + kr2 * cos], axis=1).astype(BF16)
    kr_nat = z[:, ZE_KRN:ZE_KRS] * cosn_ref[...] + z[:, ZE_KRS:ZE_W] * sinn_ref[...]
    kr_ref[...] = kr_nat[:, :QK_ROPE]
    q = jnp.dot(_rms(z[:, ZE_CQ:ZE_KR1], qn_ref[...]).astype(BF16), wuq_ref[...],
                preferred_element_type=F32)
    q1, q2 = q[:, UQ_NOPE_W:UQ_NOPE_W + LANES], q[:, UQ_NOPE_W + LANES:]
    o1, o2 = q1 * cos - q2 * sin, q1 * sin + q2 * cos
    lane_head = lax.shift_right_logical(lax.broadcasted_iota(jnp.int32, o1.shape, 1), ROPE_HALF.bit_length() - 1)
    for h in range(H_A):
        ql = jnp.dot(q[:, h * LANES:(h + 1) * LANES].astype(BF16), wuk_ref[h], preferred_element_type=F32)
        own = lane_head == h
        qh = jnp.concatenate([ql, jnp.where(own, o1, 0.0), jnp.where(own, o2, 0.0)], axis=1)
        qp_ref[h] = (qh * (MLA_SCALE * LOG2E)).astype(BF16)


def _even_in(x2d, tables, w_in_r, q_norm, kv_norm, w_uq_r, w_uk_r):
    n = x2d.shape[0]
    tm = min(EVEN_IN_TM, n)
    period = tables[0].shape[0] // tm
    row = lambda i: (i, 0)
    fixed2 = lambda i: (0, 0)
    tab = lambda i: (i % period, 0)
    return pl.pallas_call(
        _even_in_body,
        grid=(n // tm,),
        in_specs=[pl.BlockSpec((tm, D_MODEL), row),
                  pl.BlockSpec((D_MODEL, ZE_W), fixed2),
                  pl.BlockSpec((1, Q_LORA), fixed2),
                  pl.BlockSpec((1, KV_LORA), fixed2),
                  pl.BlockSpec((Q_LORA, UQ_NOPE_W + 2 * LANES), fixed2),
                  pl.BlockSpec((H_A, LANES, KV_LORA), lambda i: (0, 0, 0)),
                  pl.BlockSpec((tm, LANES), tab), pl.BlockSpec((tm, LANES), tab),
                  pl.BlockSpec((tm, LANES), tab), pl.BlockSpec((tm, LANES), tab)],
        out_specs=[pl.BlockSpec((tm, ZE_GATES), row),
                   pl.BlockSpec((tm, KV_LORA), row),
                   pl.BlockSpec((tm, QK_ROPE), row),
                   pl.BlockSpec((tm, QP_W), row),
                   pl.BlockSpec((H_A, tm, QP_W), lambda i: (0, i, 0))],
        out_shape=[jax.ShapeDtypeStruct((n, ZE_GATES), F32),
                   jax.ShapeDtypeStruct((n, KV_LORA), F32),
                   jax.ShapeDtypeStruct((n, QK_ROPE), F32),
                   jax.ShapeDtypeStruct((n, QP_W), BF16),
                   jax.ShapeDtypeStruct((H_A, n, QP_W), BF16)],
        compiler_params=_params(("arbitrary",)),
        name="even_in_proj",
    )(x2d, w_in_r, q_norm.reshape(1, -1), kv_norm.reshape(1, -1), w_uq_r, w_uk_r, *tables)


def _softmax_update(s, vals, m_sc, l_sc, acc_sc):
    tiles = [s[:, c * LANES:(c + 1) * LANES] for c in range(s.shape[1] // LANES)]
    m_prev = m_sc[...]
    m_new = jnp.maximum(m_prev, jnp.max(functools.reduce(jnp.maximum, tiles), axis=-1, keepdims=True))
    alpha = jnp.exp2(m_prev - m_new)
    ps = [jnp.exp2(t - m_new) for t in tiles]
    l_sc[...] = alpha * l_sc[...] + functools.reduce(jnp.add, ps)
    pv = jnp.dot(jnp.concatenate(ps, axis=1).astype(BF16), vals, preferred_element_type=F32)
    acc = acc_sc[...]
    acc_sc[...] = jnp.concatenate([acc[:, c * LANES:(c + 1) * LANES] * alpha
                                   for c in range(acc.shape[1] // LANES)], axis=1) + pv
    m_sc[...] = m_new


def _softmax_result(l_sc, acc_sc):
    return acc_sc[...] / jnp.sum(l_sc[...], axis=-1, keepdims=True)


def _softmax_init(m_sc, l_sc, acc_sc):
    m_sc[...] = jnp.full(m_sc.shape, -jnp.inf, F32)
    l_sc[...] = jnp.zeros(l_sc.shape, F32)
    acc_sc[...] = jnp.zeros(acc_sc.shape, F32)


FLASH_TQ = 256
FLASH_TK = 1024
FLAG_FIRST, FLAG_LAST = 1, 2
KIND_SHIFT = 2


def _flash_body(qb, kb, qo, ko, fl, q_ref, k_ref, wuv_ref, o_ref, m_sc, l_sc, acc_sc, s_sc, *, tq, tk):
    i = pl.program_id(0)
    flags = fl[i]
    kind = lax.shift_right_logical(flags, KIND_SHIFT)
    widths = [tk] + [v * tq for v in range(1, tk // tq + 1)]

    @pl.when((flags & FLAG_FIRST) != 0)
    def _():
        _softmax_init(m_sc, l_sc, acc_sc)

    for v, n in enumerate(widths):
        @pl.when(kind == v)
        def _():
            s_sc[:, 0:n] = lax.dot_general(q_ref[...].reshape(H_A * tq, QP_W), k_ref[0:n, :], NT_DIMS,
                                           preferred_element_type=F32)

    for v, n in enumerate(widths):
        @pl.when(kind == v)
        def _():
            s = s_sc[:, 0:n]
            if v > 0:
                qpos = (lax.broadcasted_iota(jnp.int32, s.shape, 0) & (tq - 1)) + qo[i]
                kpos = lax.broadcasted_iota(jnp.int32, s.shape, 1) + ko[i]
                s = jnp.where(kpos <= qpos, s, -jnp.inf)
            _softmax_update(s, k_ref[0:n, 0:KV_LORA], m_sc, l_sc, acc_sc)

    @pl.when((flags & FLAG_LAST) != 0)
    def _():
        o = _softmax_result(l_sc, acc_sc)
        o_all = jnp.concatenate([o[h * tq:(h + 1) * tq] for h in range(H_A)], axis=1).astype(BF16)
        o_ref[...] = jnp.dot(o_all, wuv_ref[...], preferred_element_type=F32)


def _flash_steps(batch, seq, tq, tk):
    nq, nk = seq // tq, seq // tk
    qb, kb, qo, ko, fl = [], [], [], [], []
    for b in range(batch):
        for qi in range(nq):
            last = ((qi + 1) * tq - 1) // tk
            for kj in range(last + 1):
                qb.append(b * nq + qi)
                kb.append(b * nk + kj)
                qo.append(qi * tq)
                ko.append(kj * tk)
                causal = (kj + 1) * tk - 1 > qi * tq
                kind = ((qi + 1) * tq - kj * tk) // tq if causal else 0
                fl.append((FLAG_FIRST if kj == 0 else 0) | (FLAG_LAST if kj == last else 0) | (kind << KIND_SHIFT))
    return [np.asarray(a, np.int32) for a in (qb, kb, qo, ko, fl)]


def _mla_prompt(qp, kp, w_uv_bd, batch, seq):
    tq, tk = min(FLASH_TQ, seq), min(FLASH_TK, seq)
    assert tq & (tq - 1) == 0 and seq % tq == 0 and seq % tk == 0 and tk % tq == 0
    steps = _flash_steps(batch, seq, tq, tk)
    n = batch * seq
    rows = H_A * tq
    grid_spec = pltpu.PrefetchScalarGridSpec(
        num_scalar_prefetch=5,
        grid=(steps[0].shape[0],),
        in_specs=[pl.BlockSpec((H_A, tq, QP_W), lambda i, qb, kb, qo, ko, fl: (0, qb[i], 0)),
                  pl.BlockSpec((tk, QP_W), lambda i, qb, kb, qo, ko, fl: (kb[i], 0)),
                  pl.BlockSpec((H_A * KV_LORA, W_A), lambda i, qb, kb, qo, ko, fl: (0, 0))],
        out_specs=pl.BlockSpec((tq, W_A), lambda i, qb, kb, qo, ko, fl: (qb[i], 0)),
        scratch_shapes=[pltpu.VMEM((rows, LANES), F32), pltpu.VMEM((rows, LANES), F32),
                        pltpu.VMEM((rows, KV_LORA), F32), pltpu.VMEM((rows, tk), F32)])
    return pl.pallas_call(
        functools.partial(_flash_body, tq=tq, tk=tk),
        grid_spec=grid_spec,
        out_shape=jax.ShapeDtypeStruct((n, W_A), F32),
        compiler_params=_params(("arbitrary",)),
        name="mla_prompt_flash",
    )(*[jnp.asarray(a) for a in steps], qp, kp, w_uv_bd)


PAGES_PER_STEP = 64


def _page_copies(pt_ref, lat_hbm, kr_hbm, lat_buf, kr_buf, lat_sem, kr_sem, e, step, slot, i, npg):
    page = pt_ref[step * npg + i]
    return (pltpu.make_async_copy(lat_hbm.at[e, page], lat_buf.at[slot, i], lat_sem.at[slot]),
            pltpu.make_async_copy(kr_hbm.at[e, page], kr_buf.at[slot, i], kr_sem.at[slot]))


def _paged_body(pt_ref, ql_ref, qr_ref, kself_ref, krself_ref, lat_hbm, kr_hbm, o_ref,
                lat_buf, kr_buf, lat_sem, kr_sem, lat_sc, kr_sc, m_sc, l_sc, acc_sc, *, npg, t_new, e):
    j = pl.program_id(1)
    chunks = pl.num_programs(1)
    step = pl.program_id(0) * chunks + j
    last_step = pl.num_programs(0) * chunks - 1
    slot = step & 1
    copies = functools.partial(_page_copies, pt_ref, lat_hbm, kr_hbm, lat_buf, kr_buf, lat_sem, kr_sem, e)

    @pl.when(step == 0)
    def _():
        for i in range(npg):
            for cp in copies(0, 0, i, npg):
                cp.start()

    @pl.when(j == 0)
    def _():
        _softmax_init(m_sc, l_sc, acc_sc)

    nxt = jnp.minimum(step + 1, last_step)
    for i in range(npg):
        for cp in copies(nxt, 1 - slot, i, npg):
            cp.start()
    for i in range(npg):
        for cp in copies(step, slot, i, npg):
            cp.wait()

    for i in range(npg):
        lat_sc[i * PAGE_SIZE:(i + 1) * PAGE_SIZE, :] = lat_buf[slot, i].astype(BF16)
        kr_sc[:QK_ROPE, i * PAGE_SIZE:(i + 1) * PAGE_SIZE] = kr_buf[slot, i].astype(BF16)

    ql, qr = ql_ref[0], qr_ref[0]

    def attend(lat, kr_t, mask):
        s = (lax.dot_general(ql, lat, NT_DIMS, preferred_element_type=F32)
             + jnp.dot(qr, kr_t, preferred_element_type=F32))
        if mask is not None:
            s = jnp.where(mask, s, -jnp.inf)
        _softmax_update(s, lat, m_sc, l_sc, acc_sc)

    attend(lat_sc[...], kr_sc[...], None)

    @pl.when(j == chunks - 1)
    def _():
        shape = (ql.shape[0], PAGE_SIZE)
        t_row = lax.broadcasted_iota(jnp.int32, shape, 0) & (t_new - 1)
        attend(kself_ref[0], krself_ref[0], lax.broadcasted_iota(jnp.int32, shape, 1) <= t_row)
        o_ref[0] = _softmax_result(l_sc, acc_sc)

    @pl.when(step == last_step)
    def _():
        for i in range(npg):
            for cp in copies(last_step, 1 - slot, i, npg):
                cp.wait()


def _mla_sample(ql, qr, kself, krself, lat_pool, rope_pool_t, page_table, e, t_new):
    batch, rows, _ = ql.shape
    n_pages = page_table.shape[1]
    npg = min(PAGES_PER_STEP, n_pages)
    assert n_pages % npg == 0 and t_new & (t_new - 1) == 0 and t_new <= PAGE_SIZE
    chunks = n_pages // npg
    per_b = lambda b, j, pt: (b, 0, 0)
    grid_spec = pltpu.PrefetchScalarGridSpec(
        num_scalar_prefetch=1,
        grid=(batch, chunks),
        in_specs=[pl.BlockSpec((1, rows, KV_LORA), per_b), pl.BlockSpec((1, rows, QK_ROPE), per_b),
                  pl.BlockSpec((1, PAGE_SIZE, KV_LORA), per_b), pl.BlockSpec((1, QK_ROPE, PAGE_SIZE), per_b),
                  pl.BlockSpec(memory_space=pl.ANY), pl.BlockSpec(memory_space=pl.ANY)],
        out_specs=pl.BlockSpec((1, rows, KV_LORA), per_b),
        scratch_shapes=[pltpu.VMEM((2, npg, PAGE_SIZE, KV_LORA), F32),
                        pltpu.VMEM((2, npg, QK_ROPE, PAGE_SIZE), F32),
                        pltpu.SemaphoreType.DMA((2,)), pltpu.SemaphoreType.DMA((2,)),
                        pltpu.VMEM((npg * PAGE_SIZE, KV_LORA), BF16),
                        pltpu.VMEM((QK_ROPE, npg * PAGE_SIZE), BF16),
                        pltpu.VMEM((rows, LANES), F32), pltpu.VMEM((rows, LANES), F32),
                        pltpu.VMEM((rows, KV_LORA), F32)])
    return pl.pallas_call(
        functools.partial(_paged_body, npg=npg, t_new=t_new, e=e),
        grid_spec=grid_spec,
        out_shape=jax.ShapeDtypeStruct((batch, rows, KV_LORA), F32),
        compiler_params=_params(("arbitrary", "arbitrary")),
        name="mla_sample_paged",
    )(page_table.reshape(-1), ql, qr, kself, krself, lat_pool, rope_pool_t)


LRU_TC = 512


def _block_diag(w):
    nb, d, e = w.shape
    return jnp.einsum('nde,nm->ndme', w, jnp.eye(nb, dtype=w.dtype)).reshape(nb * d, nb * e)


def _lru_coeffs(xc, wa, wx, ba, bx, lam):
    xb = xc.astype(BF16)
    r = jax.nn.sigmoid(jnp.dot(xb, wa, preferred_element_type=F32) + ba)
    ig = jax.nn.sigmoid(jnp.dot(xb, wx, preferred_element_type=F32) + bx)
    neg = -lam
    softplus = jnp.maximum(neg, 0.0) + jnp.log1p(jnp.exp(-jnp.abs(neg)))
    log_a = -LRU_C * r * softplus
    a = jnp.exp(log_a)
    t = jnp.tanh(log_a)
    b = jnp.sqrt(-2.0 * t / (1.0 - t)) * (ig * xc)
    return a, b


def _lru_body(x_ref, h0_ref, buf_ref, cw_ref, cb_ref, wa_ref, wx_ref, ba_ref, bx_ref, lam_ref,
              h_ref, hl_ref, tail_ref, xbuf, hc, *, tc):
    c = pl.program_id(1)

    @pl.when(c == 0)
    def _():
        xbuf[0:SUBLANES] = buf_ref[0]
        hc[...] = h0_ref[0]

    x = x_ref[...]
    xbuf[SUBLANES:SUBLANES + tc] = x
    cw = cw_ref[...]
    xc = cb_ref[...] + x * cw[CONV_W - 1:CONV_W]
    for j in range(CONV_W - 1):
        xc = xc + xbuf[pl.ds(SUBLANES - (CONV_W - 1) + j, tc), :] * cw[j:j + 1]
    xbuf[0:SUBLANES] = x[tc - SUBLANES:tc]
    a, b = _lru_coeffs(xc, wa_ref[...], wx_ref[...], ba_ref[...], bx_ref[...], lam_ref[...])
    groups = tc // SUBLANES
    a = a.reshape(groups, SUBLANES, W_B)
    b = b.reshape(groups, SUBLANES, W_B)
    sub = lax.broadcasted_iota(jnp.int32, a.shape, 1)
    d = 1
    while d < SUBLANES:
        keep = sub >= d
        a_sh = jnp.where(keep, pltpu.roll(a, d, 1), 1.0)
        b_sh = jnp.where(keep, pltpu.roll(b, d, 1), 0.0)
        b = a * b_sh + b
        a = a * a_sh
        d *= 2
    carry = hc[...]
    for g in range(groups):
        hg = a[g] * carry + b[g]
        h_ref[g * SUBLANES:(g + 1) * SUBLANES, :] = hg
        carry = hg[SUBLANES - 1:SUBLANES]
    hc[...] = carry

    @pl.when(c == pl.num_programs(1) - 1)
    def _():
        hl_ref[0] = carry
        tail_ref[0] = x[tc - SUBLANES:tc]


def _rglru_seq(zg, batch, seq, h0, buf, conv_w, conv_b, wa_bd, wx_bd, ba, bx, lam):
    tc = min(LRU_TC, seq)
    assert seq % tc == 0 and tc >= SUBLANES
    nc = seq // tc
    buf8 = jnp.pad(buf, ((0, 0), (SUBLANES - (CONV_W - 1), 0), (0, 0)))
    vec = lambda a: a.reshape(1, W_B)
    fixed = lambda b, c: (0, 0)
    per_b = lambda b, c: (b, 0, 0)
    h, hl, tail = pl.pallas_call(
        functools.partial(_lru_body, tc=tc),
        grid=(batch, nc),
        in_specs=[pl.BlockSpec((tc, W_B), lambda b, c: (b * nc + c, ZE_XB // W_B)),
                  pl.BlockSpec((1, 1, W_B), per_b),
                  pl.BlockSpec((1, SUBLANES, W_B), per_b),
                  pl.BlockSpec((CONV_W, W_B), fixed),
                  pl.BlockSpec((1, W_B), fixed),
                  pl.BlockSpec((W_B, W_B), fixed), pl.BlockSpec((W_B, W_B), fixed),
                  pl.BlockSpec((1, W_B), fixed), pl.BlockSpec((1, W_B), fixed), pl.BlockSpec((1, W_B), fixed)],
        out_specs=[pl.BlockSpec((tc, W_B), lambda b, c: (b * nc + c, 0)),
                   pl.BlockSpec((1, 1, W_B), per_b),
                   pl.BlockSpec((1, SUBLANES, W_B), per_b)],
        out_shape=[jax.ShapeDtypeStruct((batch * seq, W_B), F32),
                   jax.ShapeDtypeStruct((batch, 1, W_B), F32),
                   jax.ShapeDtypeStruct((batch, SUBLANES, W_B), F32)],
        scratch_shapes=[pltpu.VMEM((SUBLANES + tc, W_B), F32), pltpu.VMEM((1, W_B), F32)],
        compiler_params=_params(("arbitrary", "arbitrary")),
        name="rglru_seq",
    )(zg, h0.reshape(batch, 1, W_B), buf8, conv_w, vec(conv_b), wa_bd, wx_bd, vec(ba), vec(bx), vec(lam))
    return h, hl[:, 0], tail[:, SUBLANES - (CONV_W - 1):]


def _out_body(v1_ref, v2_ref, v3_ref, g1_ref, g2_ref, g3_ref, x_ref, w_ref, lg_ref, lb_ref, o_ref):
    def gated(v_ref, g_ref):
        g = g_ref[...]
        return (v_ref[...] * (g * jax.nn.sigmoid(g))).astype(BF16)

    mixed = jnp.concatenate([gated(v1_ref, g1_ref), gated(v2_ref, g2_ref), gated(v3_ref, g3_ref)], axis=1)
    u = DEEPNORM_ALPHA * x_ref[...] + jnp.dot(mixed, w_ref[...], preferred_element_type=F32)
    mu = jnp.mean(u, axis=-1, keepdims=True)
    var = jnp.mean(jnp.square(u - mu), axis=-1, keepdims=True)
    o_ref[...] = (u - mu) * lax.rsqrt(var + NORM_EPS) * lg_ref[...] + lb_ref[...]


def _out_proj_norm(vals, z, gate_cols, x2d, w_out, ln_g, ln_b):
    n = x2d.shape[0]
    tm = min(OUT_PROJ_TM, n)
    widths = [v.shape[1] for v in vals]
    row = lambda i: (i, 0)
    fixed = lambda i: (0, 0)
    in_specs = [pl.BlockSpec((tm, w), row) for w in widths]
    for w, off in zip(widths, gate_cols):
        assert off % w == 0
        in_specs.append(pl.BlockSpec((tm, w), functools.partial(lambda i, cb: (i, cb), cb=off // w)))
    in_specs += [pl.BlockSpec((tm, D_MODEL), row),
                 pl.BlockSpec((sum(widths), D_MODEL), fixed),
                 pl.BlockSpec((1, D_MODEL), fixed), pl.BlockSpec((1, D_MODEL), fixed)]
    return pl.pallas_call(
        _out_body,
        grid=(n // tm,),
        in_specs=in_specs,
        out_specs=pl.BlockSpec((tm, D_MODEL), row),
        out_shape=jax.ShapeDtypeStruct((n, D_MODEL), F32),
        compiler_params=_params(("arbitrary",)),
        name="out_proj_norm",
    )(*vals, z, z, z, x2d, w_out.astype(BF16), ln_g.reshape(1, -1), ln_b.reshape(1, -1))


def _mem_attend_rows(q, k_t, v_t):
    rows = q.shape[0]
    q_head = lax.shift_right_logical(lax.broadcasted_iota(jnp.int32, q.shape, 1), DH_M.bit_length() - 1)
    qs = jnp.concatenate([jnp.where(q_head == h, q, 0.0) for h in range(H_M)], axis=0).astype(BF16)
    s = jnp.dot(qs, k_t.astype(BF16), preferred_element_type=F32) * (DH_M ** -0.5)
    p = jnp.exp(s - jnp.max(s, axis=-1, keepdims=True))
    p = p / jnp.sum(p, axis=-1, keepdims=True)
    y = lax.dot_general(p.astype(BF16), v_t.astype(BF16), NT_DIMS, preferred_element_type=F32)
    out = jnp.where(q_head == 0, y[0:rows], 0.0)
    for h in range(1, H_M):
        out = out + jnp.where(q_head == h, y[h * rows:(h + 1) * rows], 0.0)
    return out


def _mem_body(q_ref, k_ref, v_ref, o_ref):
    for i in range(k_ref.shape[1]):
        q = q_ref[i] if len(q_ref.shape) == 3 else q_ref[...]
        y = _mem_attend_rows(q, k_ref[0, i], v_ref[0, i])
        if len(o_ref.shape) == 3:
            o_ref[i] = y
        else:
            o_ref[...] = y


def _mem_transposed(mem):
    t = jnp.moveaxis(mem, -3, -1)
    t = t.reshape(t.shape[:-3] + (W_M, N_MEM))
    return t if t.ndim == 4 else t[None]


def _mem_attend_long(z, col, batch, seq, mem_kt, mem_vt, layer):
    tm = min(MEM_LONG_TM, seq)
    nt = seq // tm
    kv = lambda b, i: (layer, b, 0, 0)
    return pl.pallas_call(
        _mem_body,
        grid=(batch, nt),
        in_specs=[pl.BlockSpec((tm, W_M), lambda b, i: (b * nt + i, col // W_M)),
                  pl.BlockSpec((1, 1, W_M, N_MEM), kv), pl.BlockSpec((1, 1, W_M, N_MEM), kv)],
        out_specs=pl.BlockSpec((tm, W_M), lambda b, i: (b * nt + i, 0)),
        out_shape=jax.ShapeDtypeStruct((batch * seq, W_M), F32),
        compiler_params=_params(("arbitrary", "arbitrary")),
        name="mem_attend_long",
    )(z, mem_kt, mem_vt)


MEM_SHORT_NB = 8


def _mem_attend_short(z, col, batch, seq, mem_kt, mem_vt, layer):
    assert seq <= SUBLANES
    nb = min(MEM_SHORT_NB, batch)
    q = jnp.pad(z[:, col:col + W_M].reshape(batch, seq, W_M), ((0, 0), (0, SUBLANES - seq), (0, 0)))
    blk = lambda i: (i, 0, 0)
    kv = lambda i: (layer, i, 0, 0)
    y = pl.pallas_call(
        _mem_body,
        grid=(batch // nb,),
        in_specs=[pl.BlockSpec((nb, SUBLANES, W_M), blk),
                  pl.BlockSpec((1, nb, W_M, N_MEM), kv), pl.BlockSpec((1, nb, W_M, N_MEM), kv)],
        out_specs=pl.BlockSpec((nb, SUBLANES, W_M), blk),
        out_shape=jax.ShapeDtypeStruct((batch, SUBLANES, W_M), F32),
        compiler_params=_params(("arbitrary",)),
        name="mem_attend_short",
    )(q, mem_kt, mem_vt)
    return y[:, :seq].reshape(batch * seq, W_M)


def _chunk_mlp_body(u_ref, v_ref, g_ref, b_ref, w_ref, bias_ref, y_ref, vn_ref, *, rows, chunks):
    gw = W_D // G_D
    for c in range(chunks):
        sl = pl.ds(c * rows, rows)
        v = v_ref[sl, :]
        mu = jnp.mean(v, axis=-1, keepdims=True)
        var = jnp.mean(jnp.square(v - mu), axis=-1, keepdims=True)
        vn = (v - mu) * lax.rsqrt(var + NORM_EPS) * g_ref[...] + b_ref[...]
        vn_ref[sl, :] = vn
        vb = vn.astype(BF16)
        s = jnp.concatenate([jnp.dot(w_ref[g], vb[:, g * gw:(g + 1) * gw], preferred_element_type=F32)
                             for g in range(G_D)], axis=1)
        y_ref[sl, :] = u_ref[sl, :] * (s + bias_ref[...])


def _chunk_mlp_call(z, rows, ln_g_d, ln_b_d, w_mix, bias):
    n = z.shape[0]
    chunks = max(1, min(CHUNK_MLP_CHUNKS, n // rows))
    tm = rows * chunks
    row = lambda i: (i, 0)
    fixed = lambda i: (0, 0)
    return pl.pallas_call(
        functools.partial(_chunk_mlp_body, rows=rows, chunks=chunks),
        grid=(n // tm,),
        in_specs=[pl.BlockSpec((tm, W_D), lambda i: (i, ZO_U // W_D)),
                  pl.BlockSpec((tm, W_D), lambda i: (i, ZO_VD // W_D)),
                  pl.BlockSpec((1, W_D), fixed), pl.BlockSpec((1, W_D), fixed),
                  pl.BlockSpec((G_D, rows, rows), lambda i: (0, 0, 0)),
                  pl.BlockSpec((rows, W_D), fixed)],
        out_specs=[pl.BlockSpec((tm, W_D), row), pl.BlockSpec((tm, W_D), row)],
        out_shape=[jax.ShapeDtypeStruct((n, W_D), F32), jax.ShapeDtypeStruct((n, W_D), F32)],
        compiler_params=_params(("arbitrary",)),
        name="chunk_mlp",
    )(z, z, ln_g_d.reshape(1, -1), ln_b_d.reshape(1, -1), w_mix, bias)


def _chunk_mlp_weights(sg_w, sg_b):
    L = CHUNK_D
    w = jnp.where(jnp.tril(jnp.ones((L, L), dtype=bool)), sg_w[:, :L, :L], 0.0)
    return w.astype(BF16), jnp.repeat(sg_b[:, :L].T, W_D // G_D, axis=1)


def _chunk_mlp_short_body(u_ref, v_ref, g_ref, b_ref, w_ref, bias_ref, y_ref, vn_ref, *, batch, seq):
    v = v_ref[...]
    mu = jnp.mean(v, axis=-1, keepdims=True)
    var = jnp.mean(jnp.square(v - mu), axis=-1, keepdims=True)
    vn = (v - mu) * lax.rsqrt(var + NORM_EPS) * g_ref[...] + b_ref[...]
    vn_ref[...] = vn
    for t in range(seq):
        acc = vn[0:batch] * w_ref[t * seq:t * seq + 1, :]
        for s in range(1, t + 1):
            acc = acc + vn[s * batch:(s + 1) * batch] * w_ref[t * seq + s:t * seq + s + 1, :]
        rows = slice(t * batch, (t + 1) * batch)
        y_ref[rows, :] = u_ref[rows, :] * (acc + bias_ref[t:t + 1, :])


def _chunk_mlp_short(z, batch, seq, ln_g_d, ln_b_d, sg_w, sg_b):
    gw = W_D // G_D
    w_rows = jnp.repeat(jnp.transpose(sg_w[:, :seq, :seq], (1, 2, 0)).reshape(seq * seq, G_D), gw, axis=1)
    bias = jnp.repeat(sg_b[:, :seq].T, gw, axis=1)

    def time_major(off):
        return jnp.swapaxes(z[:, off:off + W_D].reshape(batch, seq, W_D), 0, 1).reshape(seq * batch, W_D)

    y, vn = pl.pallas_call(
        functools.partial(_chunk_mlp_short_body, batch=batch, seq=seq),
        out_shape=[jax.ShapeDtypeStruct((seq * batch, W_D), F32)] * 2,
        compiler_params=pltpu.CompilerParams(vmem_limit_bytes=VMEM_LIMIT),
        name="chunk_mlp_short",
    )(time_major(ZO_U), time_major(ZO_VD), ln_g_d.reshape(1, -1), ln_b_d.reshape(1, -1), w_rows, bias)
    back = lambda a: jnp.swapaxes(a.reshape(seq, batch, W_D), 0, 1).reshape(batch * seq, W_D)
    return back(y), back(vn)


def _cumsum_rows(x):
    row = lax.broadcasted_iota(jnp.int32, x.shape, 0)
    d = 1
    while d < x.shape[0]:
        x = x + jnp.where(row >= d, pltpu.roll(x, d, 0), 0.0)
        d *= 2
    return x


def _mlstm_body(q_ref, k_ref, v_ref, o_ref, gt_ref, bif_ref, c0_ref, n0_ref, m0_ref,
                y_ref, c_ref, n_ref, m_ref, c_sc, n_sc, m_sc, *stage, nb, n_valid):
    L = MLSTM_CHUNK
    step = pl.program_id(1)
    heads = [slice(h * DH_C, (h + 1) * DH_C) for h in range(H_C)]

    @pl.when(step == 0)
    def _():
        c_sc[...] = jnp.zeros(c_sc.shape, F32)
        for b in range(nb):
            for h, hs in enumerate(heads):
                c_sc[b, hs, hs] = c0_ref[b, h]
                n_sc[b, :, hs] = n0_ref[b, h]
        m_sc[...] = m0_ref[...]
        for st in stage:
            st[...] = jnp.zeros(st.shape, F32)

    shift = DH_C.bit_length() - 1
    row = lax.broadcasted_iota(jnp.int32, (L, LANES), 0)
    lane = lax.broadcasted_iota(jnp.int32, (L, LANES), 1)
    tril = lax.broadcasted_iota(jnp.int32, (L, L), 0) >= lax.broadcasted_iota(jnp.int32, (L, L), 1)
    lane_head = lax.shift_right_logical(lax.broadcasted_iota(jnp.int32, (L, W_C), 1), shift)
    diag_blocks = (lax.shift_right_logical(lax.broadcasted_iota(jnp.int32, (W_C, W_C), 0), shift)
                   == lax.shift_right_logical(lax.broadcasted_iota(jnp.int32, (W_C, W_C), 1), shift))

    def per_head(cols, width):
        return jnp.concatenate([jnp.broadcast_to(c, (c.shape[0], width)) for c in cols], axis=1)

    for b in range(nb):
        if stage:
            rows_in = q_ref.shape[1]
            bufs = []
            for st, ref in zip(stage, (q_ref, k_ref, v_ref, o_ref, gt_ref)):
                st[b, 0:rows_in, :] = ref[b]
                bufs.append(st[b])
            q, k, v, o_pre, gts = bufs
        else:
            q, k, v, o_pre, gts = q_ref[b], k_ref[b], v_ref[b], o_ref[b], gt_ref[b]
        gz = gts + bif_ref[...]
        lf = jnp.minimum(gz, 0.0) - jnp.log1p(jnp.exp(-jnp.abs(gz)))
        ig = gz
        if n_valid < L:
            ig = jnp.where(row < n_valid, ig, -jnp.inf)
            lf = jnp.where(row < n_valid, lf, 0.0)
        a = jnp.where(lane < H_C, ig, _cumsum_rows(lf))
        a_t = a.T
        ks = k * (DH_C ** -0.5)
        vb = v.astype(BF16)
        q_stack = jnp.concatenate([jnp.where(lane_head == h, q, 0.0) for h in range(H_C)], axis=0)
        qk = lax.dot_general(q_stack.astype(BF16), ks.astype(BF16), NT_DIMS, preferred_element_type=F32)
        ws, mts, g_inters, wgs, decays = [], [], [], [], []
        for h in range(H_C):
            ig_row, bc_row = a_t[h:h + 1, :], a_t[H_C + h:H_C + h + 1, :]
            ig_col, bc_col = a[:, h:h + 1], a[:, H_C + h:H_C + h + 1]
            m_old = m_sc[b, h][:, 0:1]
            dmat = jnp.where(tril, bc_col - bc_row + ig_row, -jnp.inf)
            inter = bc_col + m_old
            mt = jnp.maximum(inter, jnp.max(dmat, axis=-1, keepdims=True))
            ws.append(jnp.exp(dmat - mt))
            mts.append(mt)
            g_inters.append(jnp.exp(inter - mt))
            b_end = bc_col[L - 1:L, :]
            g_col = b_end - bc_col + ig_col
            m_new = jnp.maximum(b_end + m_old, jnp.max(g_col, axis=0, keepdims=True))
            decays.append(jnp.exp(b_end + m_old - m_new))
            wgs.append(jnp.exp(g_col - m_new))
            m_sc[b, h] = jnp.broadcast_to(m_new, (1, LANES))
        sc = qk * jnp.concatenate(ws, axis=0)
        sv = jnp.dot(sc.astype(BF16), vb, preferred_element_type=F32)
        num = jnp.where(lane_head == 0, sv[0:L], 0.0)
        for h in range(1, H_C):
            num = num + jnp.where(lane_head == h, sv[h * L:(h + 1) * L], 0.0)
        c_old = c_sc[b]
        n_old = n_sc[b]
        num = num + per_head(g_inters, DH_C) * jnp.dot(q.astype(BF16), c_old.astype(BF16),
                                                        preferred_element_type=F32)
        row_sum = jnp.sum(sc, axis=-1, keepdims=True)
        qn = q * n_old
        dens = []
        for h, hs in enumerate(heads):
            den = row_sum[h * L:(h + 1) * L] + g_inters[h] * jnp.sum(qn[:, hs], axis=-1, keepdims=True)
            dens.append(jnp.maximum(jnp.abs(den), jnp.exp(-mts[h])))
        y = jax.nn.sigmoid(o_pre) * (num / per_head(dens, DH_C))
        y_ref[b] = y[0:y_ref.shape[1]]
        kw = ks * per_head(wgs, DH_C)
        decay = per_head(decays, DH_C)
        upd = jnp.dot(kw.T.astype(BF16), vb, preferred_element_type=F32)
        c_sc[b] = decay * c_old + jnp.where(diag_blocks, upd, 0.0)
        n_sc[b] = decay * n_old + jnp.sum(kw, axis=0, keepdims=True)

    @pl.when(step == pl.num_programs(1) - 1)
    def _():
        for b in range(nb):
            for h, hs in enumerate(heads):
                c_ref[b, h] = c_sc[b, hs, hs]
                n_ref[b, h] = n_sc[b, :, hs]
        m_ref[...] = m_sc[...]


MLSTM_NB = 4


def _mlstm_call(z, batch, seq, b_if, c0, n0, m0):
    L = MLSTM_CHUNK
    nb = min(MLSTM_NB, batch)
    long = seq % L == 0
    if long:
        rows, nc, n_valid = L, seq // L, L
        z3 = z.reshape(batch, seq, z.shape[1])
    else:
        assert seq <= SUBLANES
        rows, nc, n_valid = SUBLANES, 1, seq
        z3 = jnp.pad(z.reshape(batch, seq, z.shape[1]), ((0, 0), (0, SUBLANES - seq), (0, 0)))
    bif = jnp.pad(b_if, (0, LANES - 2 * H_C)).reshape(1, LANES)
    m0r = jnp.broadcast_to(m0[:, :, None, None], (batch, H_C, 1, LANES))

    def col(off, w):
        return pl.BlockSpec((nb, rows, w), functools.partial(lambda g, c, cb: (g, c, cb), cb=off // w))

    st4 = lambda g, c: (g, 0, 0, 0)
    in_specs = [col(ZO_Q, W_C), col(ZO_K, W_C), col(ZO_V, W_C), col(ZO_O, W_C), col(ZO_IF, LANES),
                pl.BlockSpec((1, LANES), lambda g, c: (0, 0)),
                pl.BlockSpec((nb, H_C, DH_C, DH_C), st4),
                pl.BlockSpec((nb, H_C, 1, DH_C), st4),
                pl.BlockSpec((nb, H_C, 1, LANES), st4)]
    out_specs = [pl.BlockSpec((nb, rows, W_C), lambda g, c: (g, c, 0)),
                 pl.BlockSpec((nb, H_C, DH_C, DH_C), st4),
                 pl.BlockSpec((nb, H_C, 1, DH_C), st4),
                 pl.BlockSpec((nb, H_C, 1, LANES), st4)]
    scratch = [pltpu.VMEM((nb, W_C, W_C), F32), pltpu.VMEM((nb, 1, W_C), F32),
               pltpu.VMEM((nb, H_C, 1, LANES), F32)]
    if not long:
        scratch += [pltpu.VMEM((nb, L, W_C), F32)] * 4 + [pltpu.VMEM((nb, L, LANES), F32)]
    y, c, n, m = pl.pallas_call(
        functools.partial(_mlstm_body, nb=nb, n_valid=n_valid),
        grid=(batch // nb, nc),
        in_specs=in_specs,
        out_specs=out_specs,
        out_shape=[jax.ShapeDtypeStruct((batch, rows * nc, W_C), F32),
                   jax.ShapeDtypeStruct((batch, H_C, DH_C, DH_C), F32),
                   jax.ShapeDtypeStruct((batch, H_C, 1, DH_C), F32),
                   jax.ShapeDtypeStruct((batch, H_C, 1, LANES), F32)],
        scratch_shapes=scratch,
        compiler_params=_params(("arbitrary", "arbitrary")),
        name="mlstm_chunks",
    )(z3, z3, z3, z3, z3, bif, c0, n0[:, :, None, :], m0r)
    return y[:, :seq].reshape(batch * seq, W_C), c, n[:, :, 0], m[:, :, 0, 0]


def _lru_short_body(x_ref, h0_ref, buf_ref, cw_ref, cb_ref, wa_ref, wx_ref, ba_ref, bx_ref, lam_ref,
                    h_ref, *, batch, seq):
    cw = cw_ref[...]
    xp = jnp.concatenate([buf_ref[...], x_ref[...]], axis=0)
    xc = cb_ref[...] + xp[(CONV_W - 1) * batch:] * cw[CONV_W - 1:CONV_W]
    for j in range(CONV_W - 1):
        xc = xc + xp[j * batch:(j + seq) * batch] * cw[j:j + 1]
    a, b = _lru_coeffs(xc, wa_ref[...], wx_ref[...], ba_ref[...], bx_ref[...], lam_ref[...])
    h = h0_ref[...]
    for t in range(seq):
        sl = slice(t * batch, (t + 1) * batch)
        h = a[sl] * h + b[sl]
        h_ref[sl, :] = h


def _rglru_short(zg, batch, seq, h0, buf, conv_w, conv_b, wa_bd, wx_bd, ba, bx, lam):
    assert batch % SUBLANES == 0
    x = zg[:, ZE_XB:ZE_XB + W_B].reshape(batch, seq, W_B)
    xp = jnp.concatenate([buf, x], axis=1)
    x_tm = jnp.swapaxes(x, 0, 1).reshape(seq * batch, W_B)
    buf_tm = jnp.swapaxes(buf, 0, 1).reshape((CONV_W - 1) * batch, W_B)
    vec = lambda a: a.reshape(1, W_B)
    h_tm = pl.pallas_call(
        functools.partial(_lru_short_body, batch=batch, seq=seq),
        out_shape=jax.ShapeDtypeStruct((seq * batch, W_B), F32),
        compiler_params=pltpu.CompilerParams(vmem_limit_bytes=VMEM_LIMIT),
        name="rglru_short",
    )(x_tm, h0, buf_tm, conv_w, vec(conv_b), wa_bd, wx_bd, vec(ba), vec(bx), vec(lam))
    h = jnp.swapaxes(h_tm.reshape(seq, batch, W_B), 0, 1)
    return h.reshape(batch * seq, W_B), h[:, -1], xp[:, -(CONV_W - 1):]


def _mem_kv(mem, w_mk, w_mv):
    B = mem.shape[0]
    kv = _proj(mem, jnp.concatenate([w_mk, w_mv], axis=1))
    return (kv[..., :W_M].reshape(B, N_MEM, H_M, DH_M), kv[..., W_M:].reshape(B, N_MEM, H_M, DH_M))


def _sample_queries(qp, batch, t_new):
    n = qp.shape[1]
    r1 = jnp.stack([qp[h, :, KV_LORA + h * ROPE_HALF:KV_LORA + (h + 1) * ROPE_HALF] for h in range(H_A)])
    r2 = jnp.stack([qp[h, :, KV_LORA + LANES + h * ROPE_HALF:KV_LORA + LANES + (h + 1) * ROPE_HALF]
                    for h in range(H_A)])
    qr = jnp.concatenate([r1, r2], axis=-1)

    def rows(a):
        w = a.shape[-1]
        return a.reshape(H_A, batch, t_new, w).transpose(1, 0, 2, 3).reshape(batch, H_A * t_new, w)

    return rows(qp[:, :, :KV_LORA]), rows(qr)


def _sample_self_keys(latent, k_rope, batch, t_new):
    kl = jnp.pad(latent.astype(BF16).reshape(batch, t_new, KV_LORA), ((0, 0), (0, PAGE_SIZE - t_new), (0, 0)))
    kr_t = jnp.swapaxes(k_rope.astype(BF16).reshape(batch, t_new, QK_ROPE), 1, 2)
    return kl, jnp.pad(kr_t, ((0, 0), (0, 0), (0, PAGE_SIZE - t_new)))


def _even_layer(x2d, batch, seq, tables, mem, lru_h0, lru_buf, paged, e, weights,
                q_norm, kv_norm, conv_w, conv_b, wa, ba, wx, bx, lam, w_out, ln_g, ln_b):
    w_in_r, w_uq_r, w_uk_r, w_uv_bd = weights
    zg, latent, k_rope, kp, qp = _even_in(x2d, tables, w_in_r, q_norm, kv_norm, w_uq_r, w_uk_r)
    lru_args = (lru_h0, lru_buf, conv_w, conv_b, _block_diag(wa).astype(BF16), _block_diag(wx).astype(BF16),
                ba, bx, lam)
    if paged is None:
        y_a = _mla_prompt(qp, kp, w_uv_bd, batch, seq)
        h_b, h_last, new_buf = _rglru_seq(zg, batch, seq, *lru_args)
        y_m = _mem_attend_long(zg, ZE_QM, batch, seq, *mem)
    else:
        lat_pool, rope_pool, page_table = paged
        o = _mla_sample(*_sample_queries(qp, batch, seq), *_sample_self_keys(latent, k_rope, batch, seq),
                        lat_pool, jnp.swapaxes(rope_pool, 2, 3), page_table, e, seq)
        o = o.reshape(batch, H_A, seq, KV_LORA).transpose(0, 2, 1, 3).reshape(batch * seq, H_A * KV_LORA)
        y_a = _matmul(o, w_uv_bd)
        h_b, h_last, new_buf = _rglru_short(zg, batch, seq, *lru_args)
        y_m = _mem_attend_short(zg, ZE_QM, batch, seq, *mem)
    x_new = _out_proj_norm([y_a, h_b, y_m], zg, (ZE_GA, ZE_GB, ZE_GM), x2d, w_out, ln_g, ln_b)
    return (x_new, latent.reshape(batch, seq, KV_LORA), k_rope.reshape(batch, seq, QK_ROPE), h_last, new_buf)


def _odd_weights(w_in):
    q, k, v, i_pre, f_pre, o_pre, g_c, u_d, v_d, g_d, q_m, g_m = _split_cols(w_in, ODD_SPLITS)
    gates = jnp.pad(jnp.concatenate([i_pre, f_pre], axis=1), ((0, 0), (0, LANES - 2 * H_C)))
    return jnp.concatenate([q, k, v, o_pre, g_c, u_d, v_d, g_d, q_m, g_m, gates], axis=1)


def _odd_layer(x2d, batch, seq, mem, c0, n0, m0, w_in_r, b_if, ln_g_d, ln_b_d, sg_w, sg_b,
               w_out, ln_g, ln_b):
    z = _matmul(x2d, w_in_r)
    y_c, c, n, m = _mlstm_call(z, batch, seq, b_if, c0, n0, m0)
    if seq % CHUNK_D == 0:
        y_d, vn = _chunk_mlp_call(z, CHUNK_D, ln_g_d, ln_b_d, *_chunk_mlp_weights(sg_w, sg_b))
        y_m = _mem_attend_long(z, ZO_QM, batch, seq, *mem)
    else:
        y_d, vn = _chunk_mlp_short(z, batch, seq, ln_g_d, ln_b_d, sg_w, sg_b)
        y_m = _mem_attend_short(z, ZO_QM, batch, seq, *mem)
    x_new = _out_proj_norm([y_c, y_d, y_m], z, (ZO_GC, ZO_GD, ZO_GM), x2d, w_out, ln_g, ln_b)
    return x_new, vn.reshape(batch, seq, W_D), c, n, m


def kernel(x_prompt, x_sample, cache_mla_latent, cache_mla_krope, state_lru_h, state_lru_conv,
           state_mlstm_c, state_mlstm_n, state_mlstm_m, cache_mem_k, cache_mem_v, page_table,
           mem_prompt, w_in_even, mla_q_norm, mla_kv_norm, w_uq, w_uk, w_uv,
           lru_conv_w, lru_conv_b, lru_wa, lru_ba, lru_wx, lru_bx, lru_lambda, w_out_even,
           w_in_odd, mlstm_b_if, sg_ln_g, sg_ln_b, sg_w, sg_b, w_out_odd,
           w_mem_k, w_mem_v, ln_g, ln_b):
    Bp, Tp, _ = x_prompt.shape
    Bs, Ts, _ = x_sample.shape
    past_len = page_table.shape[1] * PAGE_SIZE
    tables_p = _rope_tables(jnp.arange(Tp, dtype=F32))
    tables_s = tuple(jnp.tile(t, (Bs, 1)) for t in _rope_tables(past_len + jnp.arange(Ts, dtype=F32)))

    h0_p = jnp.zeros((Bp, W_B), F32)
    buf0_p = jnp.zeros((Bp, CONV_W - 1, W_B), F32)
    c0_p = jnp.zeros((Bp, H_C, DH_C, DH_C), F32)
    n0_p = jnp.zeros((Bp, H_C, DH_C), F32)
    m0_p = jnp.zeros((Bp, H_C), F32)

    lat_p, kr_p, h_p, conv_p, c_p, n_p, m_p, mk_p, mv_p = [], [], [], [], [], [], [], [], []
    lat_s, kr_s, h_s, conv_s, c_s, n_s, m_s, v_s = [], [], [], [], [], [], [], []

    xp = x_prompt.reshape(Bp * Tp, D_MODEL)
    xs = x_sample.reshape(Bs * Ts, D_MODEL)
    mem_s = (_mem_transposed(cache_mem_k), _mem_transposed(cache_mem_v))
    for l in range(DEPTH):
        mk_l, mv_l = _mem_kv(mem_prompt, w_mem_k[l], w_mem_v[l])
        mk_p.append(mk_l)
        mv_p.append(mv_l)
        mem_p = (_mem_transposed(mk_l), _mem_transposed(mv_l), 0)
        if l % 2 == 0:
            e = l // 2
            weights = _even_weights(w_in_even[e], w_uq[e], w_uk[e], w_uv[e])
            rest = (mla_q_norm[e], mla_kv_norm[e], lru_conv_w[e], lru_conv_b[e], lru_wa[e], lru_ba[e],
                    lru_wx[e], lru_bx[e], lru_lambda[e], w_out_even[e], ln_g[l], ln_b[l])
            xp, la, kr, hl, cb = _even_layer(xp, Bp, Tp, tables_p, mem_p, h0_p, buf0_p, None, e,
                                             weights, *rest)
            lat_p.append(la); kr_p.append(kr); h_p.append(hl); conv_p.append(cb)
            xs, la, kr, hl, cb = _even_layer(xs, Bs, Ts, tables_s, mem_s + (l,),
                                             state_lru_h[e], state_lru_conv[e],
                                             (cache_mla_latent, cache_mla_krope, page_table), e, weights, *rest)
            lat_s.append(la); kr_s.append(kr); h_s.append(hl); conv_s.append(cb)
        else:
            o = l // 2
            ow = (_odd_weights(w_in_odd[o]), mlstm_b_if[o], sg_ln_g[o], sg_ln_b[o], sg_w[o], sg_b[o],
                  w_out_odd[o], ln_g[l], ln_b[l])
            xp, _, cc, nn, mm = _odd_layer(xp, Bp, Tp, mem_p, c0_p, n0_p, m0_p, *ow)
            c_p.append(cc); n_p.append(nn); m_p.append(mm)
            xs, vn, cc, nn, mm = _odd_layer(xs, Bs, Ts, mem_s + (l,),
                                            state_mlstm_c[o], state_mlstm_n[o], state_mlstm_m[o], *ow)
            c_s.append(cc); n_s.append(nn); m_s.append(mm); v_s.append(vn)

    return (xp.reshape(Bp, Tp, D_MODEL), xs.reshape(Bs, Ts, D_MODEL),
            jnp.stack(lat_p), jnp.stack(kr_p), jnp.stack(h_p), jnp.stack(conv_p),
            jnp.stack(c_p), jnp.stack(n_p), jnp.stack(m_p), jnp.stack(mk_p), jnp.stack(mv_p),
            jnp.stack(lat_s), jnp.stack(kr_s), jnp.stack(h_s), jnp.stack(conv_s),
            jnp.stack(c_s), jnp.stack(n_s), jnp.stack(m_s), jnp.stack(v_s))
```
